```python
import math
import jax, jax.numpy as jnp
from jax import lax
import numpy as np

D_MODEL = 2048
BATCH = 2
SEQ = 8192
DEPTH = 2

HY_CH = D_MODEL // 2
HY_ORDER = 2
HY_SHORT = 3
HF_BANDS = 16
HF_EMB = 2 * HF_BANDS + 1
HF_HIDDEN = 64
HF_TARGET = 1e-2
HF_FAST = 0.3
HF_SLOW = 1.5
DA_HEADS = 8
DA_HEAD_DIM = 64
DA_V_DIM = 2 * DA_HEAD_DIM
DA_QK = DA_HEADS * 2 * DA_HEAD_DIM
DA_WIDTH = DA_HEADS * DA_V_DIM
Q_BLOCK = 128
EV_IN = 3 * HY_CH + 2 * DA_QK + DA_WIDTH

RG_WIDTH = D_MODEL
RG_BLOCKS = 8
RG_BLOCK_W = RG_WIDTH // RG_BLOCKS
RG_CONV = 4
RG_C = 8.0

N_EXPERTS = 32
N_GROUPS = 4
EXPERTS_PER_GROUP = N_EXPERTS // N_GROUPS
TOP_K = 2
D_FF = 512
MOE_BLOCK = 128

N_EVEN = (DEPTH + 1) // 2
N_ODD = DEPTH // 2
DN_ALPHA = (2 * DEPTH) ** 0.25
DN_BETA = (8 * DEPTH) ** -0.25
LN_EPS = 1e-5

kernel_name = "hyena_diffattn_rglru_grouped_moe_deepnorm"

F32 = jnp.float32


def layer_norm(x, g, b):
    xf = x.astype(F32)
    mu = xf.mean(-1, keepdims=True)
    var = jnp.square(xf - mu).mean(-1, keepdims=True)
    return ((xf - mu) * lax.rsqrt(var + LN_EPS) * g.astype(F32) + b.astype(F32)).astype(x.dtype)


def depthwise_conv(x, w, b, left):
    k, s = w.shape[0], x.shape[1]
    xp = jnp.pad(x, ((0, 0), (left, k - 1 - left), (0, 0)))
    return sum(xp[:, j:j + s] * w[j] for j in range(k)) + b


def hyena_filters(L, w1, b1, w2, b2, w3, b3, freq, w4):
    t = jnp.linspace(0.0, 1.0, L, dtype=F32)[:, None]
    omega = 2.0 * math.pi * jnp.arange(L, dtype=F32)[:, None] / L
    bands = jnp.linspace(1e-4, HF_BANDS - 1, HF_BANDS, dtype=F32)[None, :]
    ang = omega * bands
    z = jnp.concatenate([t, jnp.cos(ang), -jnp.sin(ang)], -1)
    fr = freq.astype(F32)
    h = jnp.sin(fr * (z @ w1.astype(F32) + b1.astype(F32)))
    h = jnp.sin(fr * (h @ w2.astype(F32) + b2.astype(F32)))
    h = jnp.sin(fr * (h @ w3.astype(F32) + b3.astype(F32)))
    h = (h @ w4.astype(F32)).reshape(L, 2, HY_ORDER, HY_CH)
    max_decay = math.log(HF_TARGET) / HF_FAST
    min_decay = math.log(HF_TARGET) / HF_SLOW
    deltas = jnp.abs(jnp.linspace(min_decay, max_decay, HY_CH, dtype=F32))
    h = h * jnp.exp(-t[:, :, None, None] * deltas)
    k = jnp.concatenate([h[:, 0], jnp.zeros((1, HY_ORDER, HY_CH), F32), h[:0:-1, 1]], 0)
    k = k * lax.rsqrt(jnp.sum(k * k, 0, keepdims=True))
    return jnp.fft.rfft(k, axis=0)


def hyena_mixer(v, x1, x2, k_f, skip):
    L = v.shape[1]
    z = v.astype(F32)
    for o, g in enumerate((x1, x2)):
        zf = jnp.fft.rfft(z, n=2 * L, axis=1)
        y = jnp.fft.irfft(zf * k_f[None, :, o], n=2 * L, axis=1)[:, :L]
        z = g.astype(F32) * (y + skip[o].astype(F32) * z)
    return z.astype(v.dtype)


def diff_attention(q, k, v, lam, subln_g, lambda_init):
    B, S, _ = q.shape
    q = q.reshape(B, S, DA_HEADS, 2, DA_HEAD_DIM).transpose(0, 2, 3, 1, 4) * (DA_HEAD_DIM ** -0.5)
    k = k.reshape(B, S, DA_HEADS, 2, DA_HEAD_DIM).transpose(0, 2, 3, 1, 4)
    v = v.reshape(B, S, DA_HEADS, DA_V_DIM).transpose(0, 2, 1, 3)
    nqb = S // Q_BLOCK
    qb = q.reshape(B, DA_HEADS, 2, nqb, Q_BLOCK, DA_HEAD_DIM).transpose(3, 0, 1, 2, 4, 5)
    slopes = 2.0 ** (-(8.0 / DA_HEADS) * jnp.arange(1, DA_HEADS + 1, dtype=F32))
    kpos = jnp.arange(S)

    def block(args):
        qblk, start = args
        s = jnp.einsum('bhcqd,bhckd->bhcqk', qblk, k).astype(F32)
        qpos = start + jnp.arange(Q_BLOCK)
        dist = jnp.abs(qpos[:, None] - kpos[None, :]).astype(F32)
        s = s - slopes[:, None, None, None] * dist
        p = jax.nn.softmax(s, axis=-1)
        a = (p[:, :, 0] - lam * p[:, :, 1]).astype(v.dtype)
        return jnp.einsum('bhqk,bhkv->bhqv', a, v)

    o = lax.map(block, (qb, jnp.arange(nqb) * Q_BLOCK))
    o = o.transpose(1, 0, 3, 2, 4).reshape(B, S, DA_HEADS, DA_V_DIM).astype(F32)
    o = o * lax.rsqrt(jnp.mean(o * o, -1, keepdims=True) + LN_EPS) * subln_g.astype(F32)
    o = o * (1.0 - lambda_init)
    return o.reshape(B, S, DA_WIDTH).astype(q.dtype)


def _lin_rec(left, right):
    a1, b1 = left
    a2, b2 = right
    return a1 * a2, a2 * b1 + b2


def rg_lru_bidir(xr, wa, ba, wx, bx, lam):
    B, S, _ = xr.shape
    xf = xr.astype(F32)
    xb = xf.reshape(B, S, RG_BLOCKS, RG_BLOCK_W)
    y = 0
    for d in range(2):
        r = jax.nn.sigmoid(jnp.einsum('bsni,nij->bsnj', xb, wa[d].astype(F32)).reshape(B, S, RG_WIDTH) + ba[d].astype(F32))
        i = jax.nn.sigmoid(jnp.einsum('bsni,nij->bsnj', xb, wx[d].astype(F32)).reshape(B, S, RG_WIDTH) + bx[d].astype(F32))
        log_a = -RG_C * r * jax.nn.softplus(-lam[d].astype(F32))
        a = jnp.exp(log_a)
        b = jnp.sqrt(-jnp.expm1(2.0 * log_a)) * (i * xf)
        _, h = lax.associative_scan(_lin_rec, (a, b), axis=1, reverse=(d == 1))
        y = y + h
    return y


def route(xf, router_w, router_b):
    N = xf.shape[0]
    s = jax.nn.sigmoid((xf @ router_w).astype(F32))
    sel = (s + router_b.astype(F32)).reshape(N, N_GROUPS, EXPERTS_PER_GROUP)
    group_score = lax.top_k(sel, TOP_K)[0].sum(-1)
    g = jnp.argmax(group_score, -1).astype(jnp.int32)
    sel_g = jnp.take_along_axis(sel, g[:, None, None], 1)[:, 0]
    _, local = lax.top_k(sel_g, TOP_K)
    s_g = jnp.take_along_axis(s.reshape(N, N_GROUPS, EXPERTS_PER_GROUP), g[:, None, None], 1)[:, 0]
    w = jnp.take_along_axis(s_g, local, -1)
    w = w / w.sum(-1, keepdims=True)
    return g[:, None] * EXPERTS_PER_GROUP + local.astype(jnp.int32), w


def moe_ffn(x, router_w, router_b, w_gate, w_up, w_down):
    B, S, D = x.shape
    N = B * S
    A = N * TOP_K
    xf = x.reshape(N, D)
    e_idx, gate = route(xf, router_w, router_b)
    e_flat = e_idx.reshape(A)
    g_flat = gate.reshape(A)
    tok = jnp.arange(A, dtype=jnp.int32) // TOP_K
    order = jnp.argsort(e_flat)
    e_sorted = e_flat[order]
    counts = jax.ops.segment_sum(jnp.ones((A,), jnp.int32), e_flat, num_segments=N_EXPERTS)
    padded = (counts + MOE_BLOCK - 1) // MOE_BLOCK * MOE_BLOCK
    pad_end = jnp.cumsum(padded)
    pad_start = pad_end - padded
    start = jnp.cumsum(counts) - counts
    dest = pad_start[e_sorted] + jnp.arange(A, dtype=jnp.int32) - start[e_sorted]
    n_blocks = -(-A // MOE_BLOCK) + N_EXPERTS
    P = n_blocks * MOE_BLOCK
    row_tok = jnp.full((P,), N, jnp.int32).at[dest].set(tok[order])
    row_gate = jnp.zeros((P,), F32).at[dest].set(g_flat[order])
    blk_exp = jnp.minimum(jnp.searchsorted(pad_end, jnp.arange(n_blocks) * MOE_BLOCK, side='right'), N_EXPERTS - 1)
    x_pad = jnp.concatenate([xf, jnp.zeros((1, D), xf.dtype)], 0)
    xb = x_pad[row_tok].reshape(n_blocks, MOE_BLOCK, D)

    def expert_block(args):
        xblk, e = args
        h = jax.nn.silu(xblk @ w_gate[e]) * (xblk @ w_up[e])
        return h @ w_down[e]

    yb = lax.map(expert_block, (xb, blk_exp)).reshape(P, D)
    y = jax.ops.segment_sum(yb * row_gate[:, None].astype(yb.dtype), row_tok, num_segments=N + 1)[:N]
    return y.reshape(B, S, D)


def setup_inputs(seed: int = 0) -> dict:
    key = jax.random.key(seed)
    keys = iter(jax.random.split(key, 48))

    def nrm(shape, scale):
        return scale * jax.random.normal(next(keys), shape, F32)

    D = D_MODEL
    x = nrm((BATCH, SEQ, D), 1.0)
    ev_w_in = nrm((N_EVEN, D, EV_IN), D ** -0.5)
    ev_hy_conv_w = nrm((N_EVEN, HY_SHORT, 3 * HY_CH), HY_SHORT ** -0.5)
    ev_hy_conv_b = nrm((N_EVEN, 3 * HY_CH), 0.02)
    ev_hf_w1 = nrm((N_EVEN, HF_EMB, HF_HIDDEN), HF_EMB ** -0.5)
    ev_hf_b1 = nrm((N_EVEN, HF_HIDDEN), 0.1)
    ev_hf_w2 = nrm((N_EVEN, HF_HIDDEN, HF_HIDDEN), HF_HIDDEN ** -0.5)
    ev_hf_b2 = nrm((N_EVEN, HF_HIDDEN), 0.1)
    ev_hf_w3 = nrm((N_EVEN, HF_HIDDEN, HF_HIDDEN), HF_HIDDEN ** -0.5)
    ev_hf_b3 = nrm((N_EVEN, HF_HIDDEN), 0.1)
    ev_hf_freq = 1.0 + nrm((N_EVEN, HF_HIDDEN), 0.1)
    ev_hf_w4 = nrm((N_EVEN, HF_HIDDEN, 2 * HY_ORDER * HY_CH), HF_HIDDEN ** -0.5)
    ev_hy_skip = nrm((N_EVEN, HY_ORDER, HY_CH), 0.5)
    ev_lam_q1 = nrm((N_EVEN, DA_HEAD_DIM), 0.1)
    ev_lam_k1 = nrm((N_EVEN, DA_HEAD_DIM), 0.1)
    ev_lam_q2 = nrm((N_EVEN, DA_HEAD_DIM), 0.1)
    ev_lam_k2 = nrm((N_EVEN, DA_HEAD_DIM), 0.1)
    ev_subln_g = 1.0 + nrm((N_EVEN, DA_V_DIM), 0.01)
    ev_w_out = nrm((N_EVEN, HY_CH + DA_WIDTH, D), DN_BETA * (HY_CH + DA_WIDTH) ** -0.5)
    od_w_in = nrm((N_ODD, D, 2 * RG_WIDTH), D ** -0.5)
    od_conv_w = nrm((N_ODD, RG_CONV, RG_WIDTH), RG_CONV ** -0.5)
    od_conv_b = nrm((N_ODD, RG_WIDTH), 0.02)
    od_wa = nrm((N_ODD, 2, RG_BLOCKS, RG_BLOCK_W, RG_BLOCK_W), RG_BLOCK_W ** -0.5)
    od_ba = nrm((N_ODD, 2, RG_WIDTH), 0.02)
    od_wx = nrm((N_ODD, 2, RG_BLOCKS, RG_BLOCK_W, RG_BLOCK_W), RG_BLOCK_W ** -0.5)
    od_bx = nrm((N_ODD, 2, RG_WIDTH), 0.02)
    u = jax.random.uniform(next(keys), (N_ODD, 2, RG_WIDTH), F32, 0.9, 0.999)
    s = u ** (1.0 / RG_C)
    od_lam = jnp.log(s) - jnp.log1p(-s)
    od_w_out = nrm((N_ODD, RG_WIDTH, D), DN_BETA * RG_WIDTH ** -0.5)
    ln1_g = 1.0 + nrm((DEPTH, D), 0.01)
    ln1_b = nrm((DEPTH, D), 0.01)
    ln2_g = 1.0 + nrm((DEPTH, D), 0.01)
    ln2_b = nrm((DEPTH, D), 0.01)
    router_w = nrm((D, N_EXPERTS), D ** -0.5)
    router_b = nrm((N_EXPERTS,), 0.01)
    ex_w_gate = nrm((DEPTH, N_EXPERTS, D, D_FF), D ** -0.5)
    ex_w_up = nrm((DEPTH, N_EXPERTS, D, D_FF), D ** -0.5)
    ex_w_down = nrm((DEPTH, N_EXPERTS, D_FF, D), DN_BETA * D_FF ** -0.5)
    return {"x": x, "ev_w_in": ev_w_in, "ev_hy_conv_w": ev_hy_conv_w, "ev_hy_conv_b": ev_hy_conv_b,
            "ev_hf_w1": ev_hf_w1, "ev_hf_b1": ev_hf_b1, "ev_hf_w2": ev_hf_w2, "ev_hf_b2": ev_hf_b2,
            "ev_hf_w3": ev_hf_w3, "ev_hf_b3": ev_hf_b3, "ev_hf_freq": ev_hf_freq, "ev_hf_w4": ev_hf_w4,
            "ev_hy_skip": ev_hy_skip, "ev_lam_q1": ev_lam_q1, "ev_lam_k1": ev_lam_k1,
            "ev_lam_q2": ev_lam_q2, "ev_lam_k2": ev_lam_k2, "ev_subln_g": ev_subln_g, "ev_w_out": ev_w_out,
            "od_w_in": od_w_in, "od_conv_w": od_conv_w, "od_conv_b": od_conv_b, "od_wa": od_wa,
            "od_ba": od_ba, "od_wx": od_wx, "od_bx": od_bx, "od_lam": od_lam, "od_w_out": od_w_out,
            "ln1_g": ln1_g, "ln1_b": ln1_b, "ln2_g": ln2_g, "ln2_b": ln2_b,
            "router_w": router_w, "router_b": router_b,
            "ex_w_gate": ex_w_gate, "ex_w_up": ex_w_up, "ex_w_down": ex_w_down}


def reference(x, ev_w_in, ev_hy_conv_w, ev_hy_conv_b, ev_hf_w1, ev_hf_b1, ev_hf_w2, ev_hf_b2,
              ev_hf_w3, ev_hf_b3, ev_hf_freq, ev_hf_w4, ev_hy_skip, ev_lam_q1, ev_lam_k1,
              ev_lam_q2, ev_lam_k2, ev_subln_g, ev_w_out, od_w_in, od_conv_w, od_conv_b, od_wa,
              od_ba, od_wx, od_bx, od_lam, od_w_out, ln1_g, ln1_b, ln2_g, ln2_b,
              router_w, router_b, ex_w_gate, ex_w_up, ex_w_down):
    S = x.shape[1]
    h = x
    for layer in range(DEPTH):
        i = layer // 2
        if layer % 2 == 0:
            u = h @ ev_w_in[i]
            hy = depthwise_conv(u[..., :3 * HY_CH], ev_hy_conv_w[i], ev_hy_conv_b[i], 1)
            v_h, x1, x2 = jnp.split(hy, 3, axis=-1)
            q = u[..., 3 * HY_CH:3 * HY_CH + DA_QK]
            k = u[..., 3 * HY_CH + DA_QK:3 * HY_CH + 2 * DA_QK]
            v = u[..., 3 * HY_CH + 2 * DA_QK:]
            k_f = hyena_filters(S, ev_hf_w1[i], ev_hf_b1[i], ev_hf_w2[i], ev_hf_b2[i],
                                ev_hf_w3[i], ev_hf_b3[i], ev_hf_freq[i], ev_hf_w4[i])
            y_hy = hyena_mixer(v_h, x1, x2, k_f, ev_hy_skip[i])
            lambda_init = 0.8 - 0.6 * math.exp(-0.3 * layer)
            lam = (jnp.exp(jnp.sum(ev_lam_q1[i].astype(F32) * ev_lam_k1[i].astype(F32)))
                   - jnp.exp(jnp.sum(ev_lam_q2[i].astype(F32) * ev_lam_k2[i].astype(F32))) + lambda_init)
            y_da = diff_attention(q, k, v, lam, ev_subln_g[i], lambda_init)
            mix = jnp.concatenate([y_hy, y_da], axis=-1) @ ev_w_out[i]
        else:
            u = h @ od_w_in[i]
            gate_b, xr = jnp.split(u, 2, axis=-1)
            xr = depthwise_conv(xr, od_conv_w[i], od_conv_b[i], 2)
            y = rg_lru_bidir(xr, od_wa[i], od_ba[i], od_wx[i], od_bx[i], od_lam[i])
            mix = (jax.nn.gelu(gate_b.astype(F32)) * y).astype(h.dtype) @ od_w_out[i]
        h = layer_norm(DN_ALPHA * h + mix, ln1_g[layer], ln1_b[layer])
        ffn = moe_ffn(h, router_w, router_b, ex_w_gate[layer], ex_w_up[layer], ex_w_down[layer])
        h = layer_norm(DN_ALPHA * h + ffn, ln2_g[layer], ln2_b[layer])
    return h
```

```python
import functools
import math

import jax
import jax.numpy as jnp
from jax import lax
from jax.experimental import pallas as pl
from jax.experimental.pallas import tpu as pltpu

F32 = jnp.float32
BF16 = jnp.bfloat16

D_MODEL = 2048
DEPTH = 2
HY_CH = D_MODEL // 2
HY_ORDER = 2
HF_BANDS = 16
HF_TARGET = 1e-2
HF_FAST = 0.3
HF_SLOW = 1.5
DA_HEADS = 8
DA_HEAD_DIM = 64
DA_V_DIM = 2 * DA_HEAD_DIM
DA_QK = DA_HEADS * 2 * DA_HEAD_DIM
DA_WIDTH = DA_HEADS * DA_V_DIM
RG_WIDTH = D_MODEL
RG_BLOCKS = 8
RG_BLOCK_W = RG_WIDTH // RG_BLOCKS
RG_C = 8.0
N_EXPERTS = 32
N_GROUPS = 4
EXPERTS_PER_GROUP = N_EXPERTS // N_GROUPS
TOP_K = 2
D_FF = 512
DN_ALPHA = (2 * DEPTH) ** 0.25
LN_EPS = 1e-5

VMEM_LIMIT_BYTES = 56 * 1024 * 1024
MOE_ROWS = 256


def _params(*sem):
    return pltpu.CompilerParams(dimension_semantics=sem, vmem_limit_bytes=VMEM_LIMIT_BYTES)


def _mm_kernel(x_ref, w_ref, o_ref):
    o_ref[...] = jnp.dot(x_ref[...], w_ref[...], preferred_element_type=F32).astype(o_ref.dtype)


def matmul(x, w, out_dtype, tm=512, tn=1024):
    m, k = x.shape
    n = w.shape[1]
    tm, tn = min(tm, m), min(tn, n)
    return pl.pallas_call(
        _mm_kernel,
        grid=(n // tn, m // tm),
        in_specs=[pl.BlockSpec((tm, k), lambda j, i: (i, 0)),
                  pl.BlockSpec((k, tn), lambda j, i: (0, j))],
        out_specs=pl.BlockSpec((tm, tn), lambda j, i: (i, j)),
        out_shape=jax.ShapeDtypeStruct((m, n), out_dtype),
        compiler_params=_params("arbitrary", "arbitrary"),
        name="matmul",
    )(x, w)


def _ln_rows(z, g, b):
    mu = jnp.mean(z, axis=-1, keepdims=True)
    zc = z - mu
    var = jnp.mean(zc * zc, axis=-1, keepdims=True)
    return zc * lax.rsqrt(var + LN_EPS) * g + b


def _even_out_kernel(xa_ref, xb_ref, wa_ref, wb_ref, r_ref, g_ref, b_ref, o_ref, ob_ref):
    acc = jnp.dot(xa_ref[...], wa_ref[...], preferred_element_type=F32)
    acc += jnp.dot(xb_ref[...], wb_ref[...], preferred_element_type=F32)
    y = _ln_rows(DN_ALPHA * r_ref[...] + acc, g_ref[...], b_ref[...])
    o_ref[...] = y
    ob_ref[...] = y.astype(BF16)


def even_out_proj(y_hy, y_da, w_out, resid, g, b, tm=256):
    m, ka = y_hy.shape
    kb = y_da.shape[1]
    d = w_out.shape[1]
    row = lambda i: (i, 0)
    fixed = lambda i: (0, 0)
    return pl.pallas_call(
        _even_out_kernel,
        grid=(m // tm,),
        in_specs=[pl.BlockSpec((tm, ka), row), pl.BlockSpec((tm, kb), row),
                  pl.BlockSpec((ka, d), fixed), pl.BlockSpec((kb, d), fixed),
                  pl.BlockSpec((tm, d), row), pl.BlockSpec((1, d), fixed), pl.BlockSpec((1, d), fixed)],
        out_specs=[pl.BlockSpec((tm, d), row), pl.BlockSpec((tm, d), row)],
        out_shape=[jax.ShapeDtypeStruct((m, d), F32), jax.ShapeDtypeStruct((m, d), BF16)],
        compiler_params=_params("arbitrary"),
        name="even_out_proj",
    )(y_hy, y_da, w_out[:ka], w_out[ka:], resid, g.reshape(1, d), b.reshape(1, d))


def _odd_out_kernel(gate_ref, y0_ref, y1_ref, w_ref, r_ref, g_ref, b_ref, o_ref, ob_ref):
    x = jax.nn.gelu(gate_ref[...]) * (y0_ref[0] + y1_ref[0])
    acc = jnp.dot(x.astype(BF16), w_ref[...], preferred_element_type=F32)
    y = _ln_rows(DN_ALPHA * r_ref[...] + acc, g_ref[...], b_ref[...])
    o_ref[...] = y
    ob_ref[...] = y.astype(BF16)


def odd_out_proj(gate, yd, w_out, resid, g, b, tm=256):
    m, k = gate.shape
    d = w_out.shape[1]
    row = lambda i: (i, 0)
    fixed = lambda i: (0, 0)
    return pl.pallas_call(
        _odd_out_kernel,
        grid=(m // tm,),
        in_specs=[pl.BlockSpec((tm, k), row),
                  pl.BlockSpec((1, tm, k), lambda i: (0, i, 0)),
                  pl.BlockSpec((1, tm, k), lambda i: (1, i, 0)),
                  pl.BlockSpec((k, d), fixed),
                  pl.BlockSpec((tm, d), row), pl.BlockSpec((1, d), fixed), pl.BlockSpec((1, d), fixed)],
        out_specs=[pl.BlockSpec((tm, d), row), pl.BlockSpec((tm, d), row)],
        out_shape=[jax.ShapeDtypeStruct((m, d), F32), jax.ShapeDtypeStruct((m, d), BF16)],
        compiler_params=_params("arbitrary"),
        name="odd_out_proj",
    )(gate, yd, yd, w_out, resid, g.reshape(1, d), b.reshape(1, d))


def _combine_ln_kernel(h_ref, y0_ref, y1_ref, gate_ref, g_ref, b_ref, o_ref, ob_ref):
    gate = gate_ref[...]
    ffn = gate[:, 0:1] * y0_ref[...].astype(F32) + gate[:, 1:2] * y1_ref[...].astype(F32)
    y = _ln_rows(DN_ALPHA * h_ref[...] + ffn, g_ref[...], b_ref[...])
    o_ref[...] = y
    ob_ref[...] = y.astype(BF16)


def combine_ln(h, y0, y1, gate, g, b, tm=512):
    m, d = h.shape
    row = lambda i: (i, 0)
    fixed = lambda i: (0, 0)
    return pl.pallas_call(
        _combine_ln_kernel,
        grid=(m // tm,),
        in_specs=[pl.BlockSpec((tm, d), row), pl.BlockSpec((tm, d), row), pl.BlockSpec((tm, d), row),
                  pl.BlockSpec((tm, TOP_K), row), pl.BlockSpec((1, d), fixed), pl.BlockSpec((1, d), fixed)],
        out_specs=[pl.BlockSpec((tm, d), row), pl.BlockSpec((tm, d), row)],
        out_shape=[jax.ShapeDtypeStruct((m, d), F32), jax.ShapeDtypeStruct((m, d), BF16)],
        compiler_params=_params("arbitrary"),
        name="combine_ln",
    )(h, y0, y1, gate, g.reshape(1, d), b.reshape(1, d))


def _attn_kernel(slopes_ref, lam_ref, q_ref, kt_ref, v_ref, g_ref, o_ref,
                 m1, l1, a1, m2, l2, a2, *, tq, tk, seq, out_scale):
    head = pl.program_id(1)
    qi = pl.program_id(2)
    slope = slopes_ref[head]
    lam = lam_ref[0]
    q = q_ref[0]
    lane = lax.broadcasted_iota(jnp.int32, q.shape, 1)
    zero = jnp.zeros_like(q)
    qa = jnp.where(lane < DA_HEAD_DIM, q, zero)
    qb = jnp.where(lane >= DA_HEAD_DIM, q, zero)
    rel = (lax.broadcasted_iota(jnp.int32, (tq, tk), 0)
           - lax.broadcasted_iota(jnp.int32, (tq, tk), 1)).astype(F32)
    for m, l, a in ((m1, l1, a1), (m2, l2, a2)):
        m[...] = jnp.full(m.shape, -jnp.inf, F32)
        l[...] = jnp.zeros(l.shape, F32)
        a[...] = jnp.zeros(a.shape, F32)

    def body(j, carry):
        off = pl.multiple_of(j * tk, tk)
        kt = kt_ref[0, :, pl.ds(off, tk)]
        vv = v_ref[0, pl.ds(off, tk), :]
        bias = slope * jnp.abs(rel + (qi * tq - j * tk).astype(F32))
        for qm, m, l, a in ((qa, m1, l1, a1), (qb, m2, l2, a2)):
            s = jnp.dot(qm, kt, preferred_element_type=F32) - bias
            m_prev = m[...]
            m_new = jnp.maximum(m_prev, jnp.max(s, axis=-1, keepdims=True))
            p = jnp.exp(s - m_new)
            alpha = jnp.exp(m_prev - m_new)
            l[...] = alpha * l[...] + jnp.sum(p, axis=-1, keepdims=True)
            a[...] = alpha * a[...] + jnp.dot(p.astype(BF16), vv, preferred_element_type=F32)
            m[...] = m_new
        return carry

    lax.fori_loop(0, seq // tk, body, 0)
    o = a1[...] / l1[...] - lam * (a2[...] / l2[...])
    o = o * lax.rsqrt(jnp.mean(o * o, axis=-1, keepdims=True) + LN_EPS) * g_ref[...]
    o_ref[0] = (o * out_scale).astype(o_ref.dtype)


def diff_attention(q, kt, v, lam, subln_g, lambda_init, tq=256, tk=512):
    bsz, seq, _ = q.shape
    slopes = 2.0 ** (-(8.0 / DA_HEADS) * jnp.arange(1, DA_HEADS + 1, dtype=F32))
    kern = functools.partial(_attn_kernel, tq=tq, tk=tk, seq=seq, out_scale=1.0 - lambda_init)
    smem = pl.BlockSpec(memory_space=pltpu.SMEM)
    return pl.pallas_call(
        kern,
        grid=(bsz, DA_HEADS, seq // tq),
        in_specs=[smem, smem,
                  pl.BlockSpec((1, tq, DA_V_DIM), lambda b, h, i: (b, i, h)),
                  pl.BlockSpec((1, DA_V_DIM, seq), lambda b, h, i: (b, h, 0)),
                  pl.BlockSpec((1, seq, DA_V_DIM), lambda b, h, i: (b, 0, h)),
                  pl.BlockSpec((1, DA_V_DIM), lambda b, h, i: (0, 0))],
        out_specs=pl.BlockSpec((1, tq, DA_V_DIM), lambda b, h, i: (b, i, h)),
        out_shape=jax.ShapeDtypeStruct(q.shape, BF16),
        scratch_shapes=[pltpu.VMEM((tq, 1), F32), pltpu.VMEM((tq, 1), F32), pltpu.VMEM((tq, DA_V_DIM), F32),
                        pltpu.VMEM((tq, 1), F32), pltpu.VMEM((tq, 1), F32), pltpu.VMEM((tq, DA_V_DIM), F32)],
        compiler_params=_params("arbitrary", "arbitrary", "arbitrary"),
        name="diff_attention",
    )(slopes, lam.reshape(1).astype(F32), q, kt, v, subln_g.reshape(1, DA_V_DIM).astype(F32))


def _rglru_kernel(xr_ref, wa_ref, wx_ref, ba_ref, bx_ref, sp_ref, y_ref, a_s, b_s, h_s, *, ts):
    direction = pl.program_id(0)

    @pl.when(pl.program_id(2) == 0)
    def _():
        h_s[...] = jnp.zeros(h_s.shape, F32)

    x = xr_ref[0]
    xb = x.astype(BF16)
    for n in range(RG_BLOCKS):
        cols = slice(n * RG_BLOCK_W, (n + 1) * RG_BLOCK_W)
        xs = xb[:, cols]
        r = jax.nn.sigmoid(jnp.dot(xs, wa_ref[0, n], preferred_element_type=F32) + ba_ref[0, :, cols])
        i = jax.nn.sigmoid(jnp.dot(xs, wx_ref[0, n], preferred_element_type=F32) + bx_ref[0, :, cols])
        log_a = -RG_C * r * sp_ref[0, :, cols]
        a = jnp.exp(log_a)
        a_s[:, cols] = a
        b_s[:, cols] = jnp.sqrt(1.0 - a * a) * (i * x[:, cols])

    def step(k, h):
        row = jnp.where(direction == 0, k, ts - 1 - k)
        h = a_s[pl.ds(row, 1), :] * h + b_s[pl.ds(row, 1), :]
        y_ref[0, 0, pl.ds(row, 1), :] = h
        return h

    h_s[...] = lax.fori_loop(0, ts, step, h_s[...], unroll=8)


def rglru_bidir(xr, wa, ba, wx, bx, lam, ts=256):
    bsz, seq, w = xr.shape
    nt = seq // ts
    sp = jax.nn.softplus(-lam.astype(F32)).reshape(2, 1, w)
    tmap = lambda d, b, t: (b, t + d * (nt - 1 - 2 * t), 0)
    dmap4 = lambda d, b, t: (d, 0, 0, 0)
    dmap3 = lambda d, b, t: (d, 0, 0)
    return pl.pallas_call(
        functools.partial(_rglru_kernel, ts=ts),
        grid=(2, bsz, nt),
        in_specs=[pl.BlockSpec((1, ts, w), tmap),
                  pl.BlockSpec((1, RG_BLOCKS, RG_BLOCK_W, RG_BLOCK_W), dmap4),
                  pl.BlockSpec((1, RG_BLOCKS, RG_BLOCK_W, RG_BLOCK_W), dmap4),
                  pl.BlockSpec((1, 1, w), dmap3), pl.BlockSpec((1, 1, w), dmap3), pl.BlockSpec((1, 1, w), dmap3)],
        out_specs=pl.BlockSpec((1, 1, ts, w), lambda d, b, t: (d, b, t + d * (nt - 1 - 2 * t), 0)),
        out_shape=jax.ShapeDtypeStruct((2, bsz, seq, w), F32),
        scratch_shapes=[pltpu.VMEM((ts, w), F32), pltpu.VMEM((ts, w), F32), pltpu.VMEM((1, w), F32)],
        compiler_params=_params("arbitrary", "arbitrary", "arbitrary"),
        name="rglru_bidir",
    )(xr, wa.astype(BF16), wx.astype(BF16), ba.astype(F32).reshape(2, 1, w), bx.astype(F32).reshape(2, 1, w), sp)


def _router_kernel(h_ref, wt_ref, b_ref, e_ref, g_ref):
    logits = lax.dot_general(wt_ref[...], h_ref[...], (((1,), (1,)), ((), ())),
                             precision=lax.Precision.HIGHEST, preferred_element_type=F32)
    tm = logits.shape[1]
    s = jax.nn.sigmoid(logits).reshape(N_GROUPS, EXPERTS_PER_GROUP, tm)
    sel = s + b_ref[...]
    idx = lax.broadcasted_iota(jnp.int32, sel.shape, 1)
    big = jnp.int32(EXPERTS_PER_GROUP)
    v1 = jnp.max(sel, axis=1, keepdims=True)
    i1 = jnp.min(jnp.where(sel == v1, idx, big), axis=1, keepdims=True)
    rest = jnp.where(idx == i1, -jnp.inf, sel)
    v2 = jnp.max(rest, axis=1, keepdims=True)
    i2 = jnp.min(jnp.where(rest == v2, idx, big), axis=1, keepdims=True)
    score = v1 + v2
    gidx = lax.broadcasted_iota(jnp.int32, score.shape, 0)
    best = jnp.max(score, axis=0, keepdims=True)
    grp = jnp.min(jnp.where(score == best, gidx, jnp.int32(N_GROUPS)), axis=0, keepdims=True)
    pick = gidx == grp
    l1 = jnp.sum(jnp.where(pick, i1, 0), axis=0)
    l2 = jnp.sum(jnp.where(pick, i2, 0), axis=0)
    s_g = jnp.sum(jnp.where(pick, s, 0.0), axis=0)
    eidx = lax.broadcasted_iota(jnp.int32, s_g.shape, 0)
    w1 = jnp.sum(jnp.where(eidx == l1, s_g, 0.0), axis=0, keepdims=True)
    w2 = jnp.sum(jnp.where(eidx == l2, s_g, 0.0), axis=0, keepdims=True)
    tot = w1 + w2
    base = grp[0] * EXPERTS_PER_GROUP
    e_ref[...] = jnp.concatenate([base + l1, base + l2], axis=0)
    g_ref[...] = jnp.concatenate([w1 / tot, w2 / tot], axis=0)


def route(h, router_w, router_b, tm=1024):
    n, d = h.shape
    return pl.pallas_call(
        _router_kernel,
        grid=(n // tm,),
        in_specs=[pl.BlockSpec((tm, d), lambda i: (i, 0)),
                  pl.BlockSpec((N_EXPERTS, d), lambda i: (0, 0)),
                  pl.BlockSpec((N_GROUPS, EXPERTS_PER_GROUP, 1), lambda i: (0, 0, 0))],
        out_specs=[pl.BlockSpec((TOP_K, tm), lambda i: (0, i)), pl.BlockSpec((TOP_K, tm), lambda i: (0, i))],
        out_shape=[jax.ShapeDtypeStruct((TOP_K, n), jnp.int32), jax.ShapeDtypeStruct((TOP_K, n), F32)],
        compiler_params=_params("arbitrary"),
        name="router",
    )(h, router_w.astype(F32).T, router_b.astype(F32).reshape(N_GROUPS, EXPERTS_PER_GROUP, 1))


def _experts_kernel(blk_exp_ref, n_used_ref, x_ref, wg_ref, wu_ref, wd_ref, o_ref):
    i = pl.program_id(0)

    @pl.when(i < n_used_ref[0])
    def _():
        x = x_ref[...]
        hg = jnp.dot(x, wg_ref[0], preferred_element_type=F32)
        hu = jnp.dot(x, wu_ref[0], preferred_element_type=F32)
        hidden = (jax.nn.silu(hg) * hu).astype(BF16)
        o_ref[...] = jnp.dot(hidden, wd_ref[0], preferred_element_type=F32).astype(o_ref.dtype)

    @pl.when(i >= n_used_ref[0])
    def _():
        o_ref[...] = jnp.zeros(o_ref.shape, o_ref.dtype)


def experts(xs, blk_exp, n_used, w_gate, w_up, w_down):
    p, d = xs.shape
    f = w_gate.shape[2]
    grid_spec = pltpu.PrefetchScalarGridSpec(
        num_scalar_prefetch=2,
        grid=(p // MOE_ROWS,),
        in_specs=[pl.BlockSpec((MOE_ROWS, d), lambda i, be, nu: (i, 0)),
                  pl.BlockSpec((1, d, f), lambda i, be, nu: (be[i], 0, 0)),
                  pl.BlockSpec((1, d, f), lambda i, be, nu: (be[i], 0, 0)),
                  pl.BlockSpec((1, f, d), lambda i, be, nu: (be[i], 0, 0))],
        out_specs=pl.BlockSpec((MOE_ROWS, d), lambda i, be, nu: (i, 0)),
    )
    return pl.pallas_call(
        _experts_kernel,
        grid_spec=grid_spec,
        out_shape=jax.ShapeDtypeStruct((p, d), BF16),
        compiler_params=_params("arbitrary"),
        name="experts",
    )(blk_exp, n_used, xs, w_gate, w_up, w_down)


def moe_ffn(h, hb, router_w, router_b, w_gate, w_up, w_down, ln_g, ln_b):
    n, d = h.shape
    a = n * TOP_K
    e_idx, gate = route(h, router_w, router_b)
    e_flat = e_idx.T.reshape(a)
    onehot = (e_flat[:, None] == jnp.arange(N_EXPERTS, dtype=jnp.int32)[None, :]).astype(jnp.int32)
    csum = jnp.cumsum(onehot, axis=0)
    rank = jnp.sum(csum * onehot, axis=1) - 1
    counts = csum[-1]
    padded = (counts + MOE_ROWS - 1) // MOE_ROWS * MOE_ROWS
    pad_end = jnp.cumsum(padded)
    pad_start = pad_end - padded
    dest = pad_start[e_flat] + rank
    n_blocks = a // MOE_ROWS + N_EXPERTS
    p = n_blocks * MOE_ROWS
    row_tok = jnp.zeros((p,), jnp.int32).at[dest].set(jnp.arange(a, dtype=jnp.int32) // TOP_K)
    blk_exp = jnp.minimum(jnp.searchsorted(pad_end, jnp.arange(n_blocks, dtype=jnp.int32) * MOE_ROWS, side='right'),
                          N_EXPERTS - 1).astype(jnp.int32)
    n_used = (pad_end[-1:] // MOE_ROWS).astype(jnp.int32)
    xs = jnp.take(hb, row_tok, axis=0)
    yb = experts(xs, blk_exp, n_used, w_gate, w_up, w_down)
    dest2 = dest.reshape(n, TOP_K)
    y0 = jnp.take(yb, dest2[:, 0], axis=0)
    y1 = jnp.take(yb, dest2[:, 1], axis=0)
    return combine_ln(h, y0, y1, gate.T, ln_g, ln_b)


def _depthwise_conv(x, w, b, left):
    k, s = w.shape[0], x.shape[1]
    xp = jnp.pad(x, ((0, 0), (left, k - 1 - left), (0, 0)))
    return sum(xp[:, j:j + s] * w[j] for j in range(k)) + b


def _hyena_filters(L, w1, b1, w2, b2, w3, b3, freq, w4):
    t = jnp.linspace(0.0, 1.0, L, dtype=F32)[:, None]
    omega = 2.0 * math.pi * jnp.arange(L, dtype=F32)[:, None] / L
    bands = jnp.linspace(1e-4, HF_BANDS - 1, HF_BANDS, dtype=F32)[None, :]
    ang = omega * bands
    z = jnp.concatenate([t, jnp.cos(ang), -jnp.sin(ang)], -1)
    fr = freq.astype(F32)
    h = jnp.sin(fr * (z @ w1.astype(F32) + b1.astype(F32)))
    h = jnp.sin(fr * (h @ w2.astype(F32) + b2.astype(F32)))
    h = jnp.sin(fr * (h @ w3.astype(F32) + b3.astype(F32)))
    h = (h @ w4.astype(F32)).reshape(L, 2, HY_ORDER, HY_CH)
    max_decay = math.log(HF_TARGET) / HF_FAST
    min_decay = math.log(HF_TARGET) / HF_SLOW
    deltas = jnp.abs(jnp.linspace(min_decay, max_decay, HY_CH, dtype=F32))
    h = h * jnp.exp(-t[:, :, None, None] * deltas)
    k = jnp.concatenate([h[:, 0], jnp.zeros((1, HY_ORDER, HY_CH), F32), h[:0:-1, 1]], 0)
    k = k * lax.rsqrt(jnp.sum(k * k, 0, keepdims=True))
    return jnp.fft.rfft(k, axis=0)


def _hyena_mixer(v, x1, x2, k_f, skip):
    L = v.shape[1]
    z = v.astype(F32)
    for o, g in enumerate((x1, x2)):
        zf = jnp.fft.rfft(z, n=2 * L, axis=1)
        y = jnp.fft.irfft(zf * k_f[None, :, o], n=2 * L, axis=1)[:, :L]
        z = g.astype(F32) * (y + skip[o].astype(F32) * z)
    return z


def kernel(x, ev_w_in, ev_hy_conv_w, ev_hy_conv_b, ev_hf_w1, ev_hf_b1, ev_hf_w2, ev_hf_b2, ev_hf_w3, ev_hf_b3,
           ev_hf_freq, ev_hf_w4, ev_hy_skip, ev_lam_q1, ev_lam_k1, ev_lam_q2, ev_lam_k2, ev_subln_g, ev_w_out,
           od_w_in, od_conv_w, od_conv_b, od_wa, od_ba, od_wx, od_bx, od_lam, od_w_out, ln1_g, ln1_b, ln2_g,
           ln2_b, router_w, router_b, ex_w_gate, ex_w_up, ex_w_down):
    bsz, seq, d = x.shape
    n = bsz * seq
    h = x.reshape(n, d).astype(F32)
    hb = h.astype(BF16)
    for layer in range(DEPTH):
        i = layer // 2
        if layer % 2 == 0:
            w_in = ev_w_in[i]
            c0 = 3 * HY_CH
            u_hy = matmul(hb, w_in[:, :c0].astype(BF16), F32).reshape(bsz, seq, c0)
            q = matmul(hb, (w_in[:, c0:c0 + DA_QK] * (DA_HEAD_DIM ** -0.5)).astype(BF16), BF16)
            k = matmul(hb, w_in[:, c0 + DA_QK:c0 + 2 * DA_QK].astype(BF16), BF16)
            v = matmul(hb, w_in[:, c0 + 2 * DA_QK:].astype(BF16), BF16)
            hy = _depthwise_conv(u_hy, ev_hy_conv_w[i], ev_hy_conv_b[i], 1)
            v_h, x1, x2 = jnp.split(hy, 3, axis=-1)
            k_f = _hyena_filters(seq, ev_hf_w1[i], ev_hf_b1[i], ev_hf_w2[i], ev_hf_b2[i],
                                 ev_hf_w3[i], ev_hf_b3[i], ev_hf_freq[i], ev_hf_w4[i])
            y_hy = _hyena_mixer(v_h, x1, x2, k_f, ev_hy_skip[i]).reshape(n, HY_CH).astype(BF16)
            lambda_init = 0.8 - 0.6 * math.exp(-0.3 * layer)
            lam = (jnp.exp(jnp.sum(ev_lam_q1[i].astype(F32) * ev_lam_k1[i].astype(F32)))
                   - jnp.exp(jnp.sum(ev_lam_q2[i].astype(F32) * ev_lam_k2[i].astype(F32))) + lambda_init)
            kt = jnp.swapaxes(k.reshape(bsz, seq, DA_QK), 1, 2)
            y_da = diff_attention(q.reshape(bsz, seq, DA_QK), kt, v.reshape(bsz, seq, DA_WIDTH),
                                  lam, ev_subln_g[i], lambda_init).reshape(n, DA_WIDTH)
            h, hb = even_out_proj(y_hy, y_da, ev_w_out[i].astype(BF16), h, ln1_g[layer], ln1_b[layer])
        else:
            u = matmul(hb, od_w_in[i].astype(BF16), F32)
            gate_b = u[:, :RG_WIDTH]
            xr = _depthwise_conv(u[:, RG_WIDTH:].reshape(bsz, seq, RG_WIDTH), od_conv_w[i], od_conv_b[i], 2)
            yd = rglru_bidir(xr, od_wa[i], od_ba[i], od_wx[i], od_bx[i], od_lam[i]).reshape(2, n, RG_WIDTH)
            h, hb = odd_out_proj(gate_b, yd, od_w_out[i].astype(BF16), h, ln1_g[layer], ln1_b[layer])
        h, hb = moe_ffn(h, hb, router_w, router_b, ex_w_gate[layer].astype(BF16), ex_w_up[layer].astype(BF16),
                        ex_w_down[layer].astype(BF16), ln2_g[layer], ln2_b[layer])
    return h.reshape(bsz, seq, d).astype(x.dtype)
```

```python
import functools
import math

import jax
import jax.numpy as jnp
from jax import lax
from jax.experimental import pallas as pl
from jax.experimental.pallas import tpu as pltpu

F32 = jnp.float32
BF16 = jnp.bfloat16

D_MODEL = 2048
DEPTH = 2
HY_CH = D_MODEL // 2
HY_ORDER = 2
HF_BANDS = 16
HF_TARGET = 1e-2
HF_FAST = 0.3
HF_SLOW = 1.5
DA_HEADS = 8
DA_HEAD_DIM = 64
DA_V_DIM = 2 * DA_HEAD_DIM
DA_QK = DA_HEADS * 2 * DA_HEAD_DIM
DA_WIDTH = DA_HEADS * DA_V_DIM
RG_WIDTH = D_MODEL
RG_BLOCKS = 8
RG_BLOCK_W = RG_WIDTH // RG_BLOCKS
RG_C = 8.0
N_EXPERTS = 32
N_GROUPS = 4
EXPERTS_PER_GROUP = N_EXPERTS // N_GROUPS
TOP_K = 2
D_FF = 512
DN_ALPHA = (2 * DEPTH) ** 0.25
LN_EPS = 1e-5

VMEM_LIMIT_BYTES = 56 * 1024 * 1024
MOE_ROWS = 256


def _params(*sem):
    return pltpu.CompilerParams(dimension_semantics=sem, vmem_limit_bytes=VMEM_LIMIT_BYTES)


def _mm_kernel(x_ref, w_ref, o_ref):
    o_ref[...] = jnp.dot(x_ref[...], w_ref[...], preferred_element_type=F32).astype(o_ref.dtype)


def matmul(x, w, out_dtype, tm=512, tn=1024):
    m, k = x.shape
    n = w.shape[1]
    tm, tn = min(tm, m), min(tn, n)
    return pl.pallas_call(
        _mm_kernel,
        grid=(n // tn, m // tm),
        in_specs=[pl.BlockSpec((tm, k), lambda j, i: (i, 0)),
                  pl.BlockSpec((k, tn), lambda j, i: (0, j))],
        out_specs=pl.BlockSpec((tm, tn), lambda j, i: (i, j)),
        out_shape=jax.ShapeDtypeStruct((m, n), out_dtype),
        compiler_params=_params("arbitrary", "arbitrary"),
        name="matmul",
    )(x, w)


def _ln_rows(z, g, b):
    mu = jnp.mean(z, axis=-1, keepdims=True)
    zc = z - mu
    var = jnp.mean(zc * zc, axis=-1, keepdims=True)
    return zc * lax.rsqrt(var + LN_EPS) * g + b


def _even_out_kernel(xa_ref, xb_ref, wa_ref, wb_ref, r_ref, g_ref, b_ref, o_ref, ob_ref):
    acc = jnp.dot(xa_ref[...], wa_ref[...], preferred_element_type=F32)
    acc += jnp.dot(xb_ref[...], wb_ref[...], preferred_element_type=F32)
    y = _ln_rows(DN_ALPHA * r_ref[...] + acc, g_ref[...], b_ref[...])
    o_ref[...] = y
    ob_ref[...] = y.astype(BF16)


def even_out_proj(y_hy, y_da, w_out, resid, g, b, tm=256):
    m, ka = y_hy.shape
    kb = y_da.shape[1]
    d = w_out.shape[1]
    row = lambda i: (i, 0)
    fixed = lambda i: (0, 0)
    return pl.pallas_call(
        _even_out_kernel,
        grid=(m // tm,),
        in_specs=[pl.BlockSpec((tm, ka), row), pl.BlockSpec((tm, kb), row),
                  pl.BlockSpec((ka, d), fixed), pl.BlockSpec((kb, d), fixed),
                  pl.BlockSpec((tm, d), row), pl.BlockSpec((1, d), fixed), pl.BlockSpec((1, d), fixed)],
        out_specs=[pl.BlockSpec((tm, d), row), pl.BlockSpec((tm, d), row)],
        out_shape=[jax.ShapeDtypeStruct((m, d), F32), jax.ShapeDtypeStruct((m, d), BF16)],
        compiler_params=_params("arbitrary"),
        name="even_out_proj",
    )(y_hy, y_da, w_out[:ka], w_out[ka:], resid, g.reshape(1, d), b.reshape(1, d))


def _odd_out_kernel(gate_ref, y0_ref, y1_ref, w_ref, r_ref, g_ref, b_ref, o_ref, ob_ref):
    x = jax.nn.gelu(gate_ref[...]) * (y0_ref[0] + y1_ref[0])
    acc = jnp.dot(x.astype(BF16), w_ref[...], preferred_element_type=F32)
    y = _ln_rows(DN_ALPHA * r_ref[...] + acc, g_ref[...], b_ref[...])
    o_ref[...] = y
    ob_ref[...] = y.astype(BF16)


def odd_out_proj(gate, yd, w_out, resid, g, b, tm=256):
    m, k = gate.shape
    d = w_out.shape[1]
    row = lambda i: (i, 0)
    fixed = lambda i: (0, 0)
    return pl.pallas_call(
        _odd_out_kernel,
        grid=(m // tm,),
        in_specs=[pl.BlockSpec((tm, k), row),
                  pl.BlockSpec((1, tm, k), lambda i: (0, i, 0)),
                  pl.BlockSpec((1, tm, k), lambda i: (1, i, 0)),
                  pl.BlockSpec((k, d), fixed),
                  pl.BlockSpec((tm, d), row), pl.BlockSpec((1, d), fixed), pl.BlockSpec((1, d), fixed)],
        out_specs=[pl.BlockSpec((tm, d), row), pl.BlockSpec((tm, d), row)],
        out_shape=[jax.ShapeDtypeStruct((m, d), F32), jax.ShapeDtypeStruct((m, d), BF16)],
        compiler_params=_params("arbitrary"),
        name="odd_out_proj",
    )(gate, yd, yd, w_out, resid, g.reshape(1, d), b.reshape(1, d))


def _combine_ln_kernel(h_ref, y0_ref, y1_ref, gate_ref, g_ref, b_ref, o_ref, ob_ref):
    gate = gate_ref[...]
    ffn = gate[:, 0:1] * y0_ref[...].astype(F32) + gate[:, 1:2] * y1_ref[...].astype(F32)
    y = _ln_rows(DN_ALPHA * h_ref[...] + ffn, g_ref[...], b_ref[...])
    o_ref[...] = y
    ob_ref[...] = y.astype(BF16)


def combine_ln(h, y0, y1, gate, g, b, tm=512):
    m, d = h.shape
    row = lambda i: (i, 0)
    fixed = lambda i: (0, 0)
    return pl.pallas_call(
        _combine_ln_kernel,
        grid=(m // tm,),
        in_specs=[pl.BlockSpec((tm, d), row), pl.BlockSpec((tm, d), row), pl.BlockSpec((tm, d), row),
                  pl.BlockSpec((tm, TOP_K), row), pl.BlockSpec((1, d), fixed), pl.BlockSpec((1, d), fixed)],
        out_specs=[pl.BlockSpec((tm, d), row), pl.BlockSpec((tm, d), row)],
        out_shape=[jax.ShapeDtypeStruct((m, d), F32), jax.ShapeDtypeStruct((m, d), BF16)],
        compiler_params=_params("arbitrary"),
        name="combine_ln",
    )(h, y0, y1, gate, g.reshape(1, d), b.reshape(1, d))


ONES_ROWS = 16
POS_SPLIT = 16


def _attn_kernel(slopes_ref, lam_ref, qt_ref, k_ref, vt_ref, fq_ref, fk_ref, g_ref, o_ref, m_s, acc_s, sa_s, sb_s,
                 *, tq, tk, seq, out_scale):
    head = pl.program_id(1)
    i0 = pl.program_id(2) * tq
    slope = slopes_ref[head]
    lam = lam_ref[0]
    qt = qt_ref[0]
    row = lax.broadcasted_iota(jnp.int32, qt.shape, 0)
    zero = jnp.zeros_like(qt)
    q2 = jnp.concatenate([jnp.where(row < DA_HEAD_DIM, qt, zero),
                          jnp.where(row >= DA_HEAD_DIM, qt, zero)], axis=1)
    fq = fq_ref[0]
    fq2 = jnp.concatenate([fq, fq], axis=1)
    q_aug = jnp.concatenate([q2, fq2], axis=0)
    fk = fk_ref[0]
    fk_neg = -fk
    ones = jnp.ones((ONES_ROWS, tk), BF16)
    m_s[...] = jnp.full(m_s.shape, -jnp.inf, F32)
    acc_s[...] = jnp.zeros(acc_s.shape, F32)
    j_diag = i0 // tk

    def update(j, s, shift):
        off = pl.multiple_of(j * tk, tk)
        va = jnp.concatenate([vt_ref[0, :, pl.ds(off, tk)], ones], axis=0)
        m_prev = m_s[...]
        m_new = jnp.maximum(m_prev, jnp.max(s, axis=0, keepdims=True) - shift)
        p = jnp.exp(s - (m_new + shift))
        alpha = jnp.exp(m_prev - m_new)
        acc_s[...] = alpha * acc_s[...] + jnp.dot(va, p.astype(BF16), preferred_element_type=F32)
        m_s[...] = m_new

    def chunk_of(t):
        return t + (t >= j_diag).astype(jnp.int32)

    def scores(t):
        j = chunk_of(t)
        off = pl.multiple_of(j * tk, tk)
        ka = jnp.concatenate([k_ref[0, pl.ds(off, tk), :], jnp.where(j < j_diag, fk, fk_neg)], axis=1)
        return jnp.dot(ka, q_aug, preferred_element_type=F32)

    def consume(s_ref, t):
        j = chunk_of(t)
        update(j, s_ref[...], slope * jnp.abs(i0 - j * tk).astype(F32))

    j0 = pl.multiple_of(j_diag * tk, tk)
    dist = jnp.abs(lax.broadcasted_iota(jnp.int32, (tk, tq), 0) + (j0 - i0)
                   - lax.broadcasted_iota(jnp.int32, (tk, tq), 1)).astype(F32)
    bias = slope * dist
    s_diag = jnp.dot(k_ref[0, pl.ds(j0, tk), :], q2, preferred_element_type=F32)
    sa_s[...] = s_diag - jnp.concatenate([bias, bias], axis=1)
    sb_s[...] = scores(jnp.int32(0))
    update(j_diag, sa_s[...], 0.0)
    n_off = seq // tk - 1

    def pair(t, carry):
        sa_s[...] = scores(2 * t + 1)
        consume(sb_s, 2 * t)
        sb_s[...] = scores(2 * t + 2)
        consume(sa_s, 2 * t + 1)
        return carry

    lax.fori_loop(0, n_off // 2, pair, 0)
    consume(sb_s, jnp.int32(n_off - 1))

    acc = acc_s[...]
    o1 = acc[:DA_V_DIM, :tq] / acc[DA_V_DIM:DA_V_DIM + 1, :tq]
    o2 = acc[:DA_V_DIM, tq:] / acc[DA_V_DIM:DA_V_DIM + 1, tq:]
    o = o1 - lam * o2
    o = o * lax.rsqrt(jnp.mean(o * o, axis=0, keepdims=True) + LN_EPS) * g_ref[...]
    o_ref[0] = (o * out_scale).T.astype(o_ref.dtype)


def diff_attention(qt, k, vt, lam, subln_g, lambda_init, tq=256, tk=512):
    bsz, seq, width = k.shape
    slopes = 2.0 ** (-(8.0 / DA_HEADS) * jnp.arange(1, DA_HEADS + 1, dtype=F32))

    def split(n):
        pos = jnp.arange(n, dtype=jnp.int32)
        return (pos // POS_SPLIT * POS_SPLIT).astype(F32), (pos % POS_SPLIT).astype(F32)

    qhi, qlo = split(tq)
    khi, klo = split(tk)
    sl = slopes[:, None]
    fq = jnp.zeros((DA_HEADS, DA_V_DIM, tq), F32)
    fq = fq.at[:, 0].set(1.0).at[:, 1].set(1.0).at[:, 2].set(-sl * qhi).at[:, 3].set(-sl * qlo)
    fk = jnp.zeros((DA_HEADS, tk, DA_V_DIM), F32)
    fk = fk.at[:, :, 0].set(sl * khi).at[:, :, 1].set(sl * klo).at[:, :, 2].set(1.0).at[:, :, 3].set(1.0)
    kern = functools.partial(_attn_kernel, tq=tq, tk=tk, seq=seq, out_scale=1.0 - lambda_init)
    smem = pl.BlockSpec(memory_space=pltpu.SMEM)
    return pl.pallas_call(
        kern,
        grid=(bsz, DA_HEADS, seq // tq),
        in_specs=[smem, smem,
                  pl.BlockSpec((1, DA_V_DIM, tq), lambda b, h, i: (b, h, i)),
                  pl.BlockSpec((1, seq, DA_V_DIM), lambda b, h, i: (b, 0, h)),
                  pl.BlockSpec((1, DA_V_DIM, seq), lambda b, h, i: (b, h, 0)),
                  pl.BlockSpec((1, DA_V_DIM, tq), lambda b, h, i: (h, 0, 0)),
                  pl.BlockSpec((1, tk, DA_V_DIM), lambda b, h, i: (h, 0, 0)),
                  pl.BlockSpec((DA_V_DIM, 1), lambda b, h, i: (0, 0))],
        out_specs=pl.BlockSpec((1, tq, DA_V_DIM), lambda b, h, i: (b, i, h)),
        out_shape=jax.ShapeDtypeStruct((bsz, seq, width), BF16),
        scratch_shapes=[pltpu.VMEM((1, 2 * tq), F32), pltpu.VMEM((DA_V_DIM + ONES_ROWS, 2 * tq), F32),
                        pltpu.VMEM((tk, 2 * tq), F32), pltpu.VMEM((tk, 2 * tq), F32)],
        compiler_params=_params("arbitrary", "arbitrary", "arbitrary"),
        name="diff_attention",
    )(slopes, lam.reshape(1).astype(F32), qt, k, vt, fq.astype(BF16), fk.astype(BF16),
      subln_g.reshape(DA_V_DIM, 1).astype(F32))


def _rglru_kernel(xr_ref, wa_ref, wx_ref, ba_ref, bx_ref, sp_ref, y_ref, a_s, b_s, h_s, *, ts):
    direction = pl.program_id(0)

    @pl.when(pl.program_id(2) == 0)
    def _():
        h_s[...] = jnp.zeros(h_s.shape, F32)

    x = xr_ref[0]
    xb = x.astype(BF16)
    for n in range(RG_BLOCKS):
        cols = slice(n * RG_BLOCK_W, (n + 1) * RG_BLOCK_W)
        xs = xb[:, cols]
        r = jax.nn.sigmoid(jnp.dot(xs, wa_ref[0, n], preferred_element_type=F32) + ba_ref[0, :, cols])
        i = jax.nn.sigmoid(jnp.dot(xs, wx_ref[0, n], preferred_element_type=F32) + bx_ref[0, :, cols])
        log_a = -RG_C * r * sp_ref[0, :, cols]
        a = jnp.exp(log_a)
        a_s[:, cols] = a
        b_s[:, cols] = jnp.sqrt(1.0 - a * a) * (i * x[:, cols])

    def step(k, h):
        row = jnp.where(direction == 0, k, ts - 1 - k)
        h = a_s[pl.ds(row, 1), :] * h + b_s[pl.ds(row, 1), :]
        y_ref[0, 0, pl.ds(row, 1), :] = h
        return h

    h_s[...] = lax.fori_loop(0, ts, step, h_s[...], unroll=8)


def rglru_bidir(xr, wa, ba, wx, bx, lam, ts=256):
    bsz, seq, w = xr.shape
    nt = seq // ts
    sp = jax.nn.softplus(-lam.astype(F32)).reshape(2, 1, w)
    tmap = lambda d, b, t: (b, t + d * (nt - 1 - 2 * t), 0)
    dmap4 = lambda d, b, t: (d, 0, 0, 0)
    dmap3 = lambda d, b, t: (d, 0, 0)
    return pl.pallas_call(
        functools.partial(_rglru_kernel, ts=ts),
        grid=(2, bsz, nt),
        in_specs=[pl.BlockSpec((1, ts, w), tmap),
                  pl.BlockSpec((1, RG_BLOCKS, RG_BLOCK_W, RG_BLOCK_W), dmap4),
                  pl.BlockSpec((1, RG_BLOCKS, RG_BLOCK_W, RG_BLOCK_W), dmap4),
                  pl.BlockSpec((1, 1, w), dmap3), pl.BlockSpec((1, 1, w), dmap3), pl.BlockSpec((1, 1, w), dmap3)],
        out_specs=pl.BlockSpec((1, 1, ts, w), lambda d, b, t: (d, b, t + d * (nt - 1 - 2 * t), 0)),
        out_shape=jax.ShapeDtypeStruct((2, bsz, seq, w), F32),
        scratch_shapes=[pltpu.VMEM((ts, w), F32), pltpu.VMEM((ts, w), F32), pltpu.VMEM((1, w), F32)],
        compiler_params=_params("arbitrary", "arbitrary", "arbitrary"),
        name="rglru_bidir",
    )(xr, wa.astype(BF16), wx.astype(BF16), ba.astype(F32).reshape(2, 1, w), bx.astype(F32).reshape(2, 1, w), sp)


def _router_kernel(h_ref, wt_ref, b_ref, e_ref, g_ref):
    logits = lax.dot_general(wt_ref[...], h_ref[...], (((1,), (1,)), ((), ())),
                             precision=lax.Precision.HIGHEST, preferred_element_type=F32)
    tm = logits.shape[1]
    s = jax.nn.sigmoid(logits).reshape(N_GROUPS, EXPERTS_PER_GROUP, tm)
    sel = s + b_ref[...]
    idx = lax.broadcasted_iota(jnp.int32, sel.shape, 1)
    big = jnp.int32(EXPERTS_PER_GROUP)
    v1 = jnp.max(sel, axis=1, keepdims=True)
    i1 = jnp.min(jnp.where(sel == v1, idx, big), axis=1, keepdims=True)
    rest = jnp.where(idx == i1, -jnp.inf, sel)
    v2 = jnp.max(rest, axis=1, keepdims=True)
    i2 = jnp.min(jnp.where(rest == v2, idx, big), axis=1, keepdims=True)
    score = v1 + v2
    gidx = lax.broadcasted_iota(jnp.int32, score.shape, 0)
    best = jnp.max(score, axis=0, keepdims=True)
    grp = jnp.min(jnp.where(score == best, gidx, jnp.int32(N_GROUPS)), axis=0, keepdims=True)
    pick = gidx == grp
    l1 = jnp.sum(jnp.where(pick, i1, 0), axis=0)
    l2 = jnp.sum(jnp.where(pick, i2, 0), axis=0)
    s_g = jnp.sum(jnp.where(pick, s, 0.0), axis=0)
    eidx = lax.broadcasted_iota(jnp.int32, s_g.shape, 0)
    w1 = jnp.sum(jnp.where(eidx == l1, s_g, 0.0), axis=0, keepdims=True)
    w2 = jnp.sum(jnp.where(eidx == l2, s_g, 0.0), axis=0, keepdims=True)
    tot = w1 + w2
    base = grp[0] * EXPERTS_PER_GROUP
    e_ref[...] = jnp.concatenate([base + l1, base + l2], axis=0)
    g_ref[...] = jnp.concatenate([w1 / tot, w2 / tot], axis=0)


def route(h, router_w, router_b, tm=1024):
    n, d = h.shape
    return pl.pallas_call(
        _router_kernel,
        grid=(n // tm,),
        in_specs=[pl.BlockSpec((tm, d), lambda i: (i, 0)),
                  pl.BlockSpec((N_EXPERTS, d), lambda i: (0, 0)),
                  pl.BlockSpec((N_GROUPS, EXPERTS_PER_GROUP, 1), lambda i: (0, 0, 0))],
        out_specs=[pl.BlockSpec((TOP_K, tm), lambda i: (0, i)), pl.BlockSpec((TOP_K, tm), lambda i: (0, i))],
        out_shape=[jax.ShapeDtypeStruct((TOP_K, n), jnp.int32), jax.ShapeDtypeStruct((TOP_K, n), F32)],
        compiler_params=_params("arbitrary"),
        name="router",
    )(h, router_w.astype(F32).T, router_b.astype(F32).reshape(N_GROUPS, EXPERTS_PER_GROUP, 1))


def _experts_kernel(blk_exp_ref, n_used_ref, x_ref, wg_ref, wu_ref, wd_ref, o_ref):
    i = pl.program_id(0)

    @pl.when(i < n_used_ref[0])
    def _():
        x = x_ref[...]
        hg = jnp.dot(x, wg_ref[0], preferred_element_type=F32)
        hu = jnp.dot(x, wu_ref[0], preferred_element_type=F32)
        hidden = (jax.nn.silu(hg) * hu).astype(BF16)
        o_ref[...] = jnp.dot(hidden, wd_ref[0], preferred_element_type=F32).astype(o_ref.dtype)

    @pl.when(i >= n_used_ref[0])
    def _():
        o_ref[...] = jnp.zeros(o_ref.shape, o_ref.dtype)


def experts(xs, blk_exp, n_used, w_gate, w_up, w_down):
    p, d = xs.shape
    f = w_gate.shape[2]
    grid_spec = pltpu.PrefetchScalarGridSpec(
        num_scalar_prefetch=2,
        grid=(p // MOE_ROWS,),
        in_specs=[pl.BlockSpec((MOE_ROWS, d), lambda i, be, nu: (i, 0)),
                  pl.BlockSpec((1, d, f), lambda i, be, nu: (be[i], 0, 0)),
                  pl.BlockSpec((1, d, f), lambda i, be, nu: (be[i], 0, 0)),
                  pl.BlockSpec((1, f, d), lambda i, be, nu: (be[i], 0, 0))],
        out_specs=pl.BlockSpec((MOE_ROWS, d), lambda i, be, nu: (i, 0)),
    )
    return pl.pallas_call(
        _experts_kernel,
        grid_spec=grid_spec,
        out_shape=jax.ShapeDtypeStruct((p, d), BF16),
        compiler_params=_params("arbitrary"),
        name="experts",
    )(blk_exp, n_used, xs, w_gate, w_up, w_down)


def moe_ffn(h, hb, router_w, router_b, w_gate, w_up, w_down, ln_g, ln_b):
    n, d = h.shape
    a = n * TOP_K
    e_idx, gate = route(h, router_w, router_b)
    e_flat = e_idx.T.reshape(a)
    onehot = (e_flat[:, None] == jnp.arange(N_EXPERTS, dtype=jnp.int32)[None, :]).astype(jnp.int32)
    csum = jnp.cumsum(onehot, axis=0)
    rank = jnp.sum(csum * onehot, axis=1) - 1
    counts = csum[-1]
    padded = (counts + MOE_ROWS - 1) // MOE_ROWS * MOE_ROWS
    pad_end = jnp.cumsum(padded)
    pad_start = pad_end - padded
    dest = pad_start[e_flat] + rank
    n_blocks = a // MOE_ROWS + N_EXPERTS
    p = n_blocks * MOE_ROWS
    row_tok = jnp.zeros((p,), jnp.int32).at[dest].set(jnp.arange(a, dtype=jnp.int32) // TOP_K)
    blk_exp = jnp.minimum(jnp.searchsorted(pad_end, jnp.arange(n_blocks, dtype=jnp.int32) * MOE_ROWS, side='right'),
                          N_EXPERTS - 1).astype(jnp.int32)
    n_used = (pad_end[-1:] // MOE_ROWS).astype(jnp.int32)
    xs = jnp.take(hb, row_tok, axis=0)
    yb = experts(xs, blk_exp, n_used, w_gate, w_up, w_down)
    dest2 = dest.reshape(n, TOP_K)
    y0 = jnp.take(yb, dest2[:, 0], axis=0)
    y1 = jnp.take(yb, dest2[:, 1], axis=0)
    return combine_ln(h, y0, y1, gate.T, ln_g, ln_b)


def _depthwise_conv(x, w, b, left):
    k, s = w.shape[0], x.shape[1]
    xp = jnp.pad(x, ((0, 0), (left, k - 1 - left), (0, 0)))
    return sum(xp[:, j:j + s] * w[j] for j in range(k)) + b


def _hyena_filters(L, w1, b1, w2, b2, w3, b3, freq, w4):
    t = jnp.linspace(0.0, 1.0, L, dtype=F32)[:, None]
    omega = 2.0 * math.pi * jnp.arange(L, dtype=F32)[:, None] / L
    bands = jnp.linspace(1e-4, HF_BANDS - 1, HF_BANDS, dtype=F32)[None, :]
    ang = omega * bands
    z = jnp.concatenate([t, jnp.cos(ang), -jnp.sin(ang)], -1)
    fr = freq.astype(F32)
    h = jnp.sin(fr * (z @ w1.astype(F32) + b1.astype(F32)))
    h = jnp.sin(fr * (h @ w2.astype(F32) + b2.astype(F32)))
    h = jnp.sin(fr * (h @ w3.astype(F32) + b3.astype(F32)))
    h = (h @ w4.astype(F32)).reshape(L, 2, HY_ORDER, HY_CH)
    max_decay = math.log(HF_TARGET) / HF_FAST
    min_decay = math.log(HF_TARGET) / HF_SLOW
    deltas = jnp.abs(jnp.linspace(min_decay, max_decay, HY_CH, dtype=F32))
    h = h * jnp.exp(-t[:, :, None, None] * deltas)
    k = jnp.concatenate([h[:, 0], jnp.zeros((1, HY_ORDER, HY_CH), F32), h[:0:-1, 1]], 0)
    k = k * lax.rsqrt(jnp.sum(k * k, 0, keepdims=True))
    return jnp.fft.rfft(k, axis=0)


def _hyena_mixer(v, x1, x2, k_f, skip):
    L = v.shape[1]
    z = v.astype(F32)
    for o, g in enumerate((x1, x2)):
        zf = jnp.fft.rfft(z, n=2 * L, axis=1)
        y = jnp.fft.irfft(zf * k_f[None, :, o], n=2 * L, axis=1)[:, :L]
        z = g.astype(F32) * (y + skip[o].astype(F32) * z)
    return z


def kernel(x, ev_w_in, ev_hy_conv_w, ev_hy_conv_b, ev_hf_w1, ev_hf_b1, ev_hf_w2, ev_hf_b2, ev_hf_w3, ev_hf_b3,
           ev_hf_freq, ev_hf_w4, ev_hy_skip, ev_lam_q1, ev_lam_k1, ev_lam_q2, ev_lam_k2, ev_subln_g, ev_w_out,
           od_w_in, od_conv_w, od_conv_b, od_wa, od_ba, od_wx, od_bx, od_lam, od_w_out, ln1_g, ln1_b, ln2_g,
           ln2_b, router_w, router_b, ex_w_gate, ex_w_up, ex_w_down):
    bsz, seq, d = x.shape
    n = bsz * seq
    h = x.reshape(n, d).astype(F32)
    hb = h.astype(BF16)
    for layer in range(DEPTH):
        i = layer // 2
        if layer % 2 == 0:
            w_in = ev_w_in[i]
            c0 = 3 * HY_CH
            u_hy = matmul(hb, w_in[:, :c0].astype(BF16), F32).reshape(bsz, seq, c0)
            q = matmul(hb, (w_in[:, c0:c0 + DA_QK] * (DA_HEAD_DIM ** -0.5)).astype(BF16), BF16)
            k = matmul(hb, w_in[:, c0 + DA_QK:c0 + 2 * DA_QK].astype(BF16), BF16)
            v = matmul(hb, w_in[:, c0 + 2 * DA_QK:].astype(BF16), BF16)
            hy = _depthwise_conv(u_hy, ev_hy_conv_w[i], ev_hy_conv_b[i], 1)
            v_h, x1, x2 = jnp.split(hy, 3, axis=-1)
            k_f = _hyena_filters(seq, ev_hf_w1[i], ev_hf_b1[i], ev_hf_w2[i], ev_hf_b2[i],
                                 ev_hf_w3[i], ev_hf_b3[i], ev_hf_freq[i], ev_hf_w4[i])
            y_hy = _hyena_mixer(v_h, x1, x2, k_f, ev_hy_skip[i]).reshape(n, HY_CH).astype(BF16)
            lambda_init = 0.8 - 0.6 * math.exp(-0.3 * layer)
            lam = (jnp.exp(jnp.sum(ev_lam_q1[i].astype(F32) * ev_lam_k1[i].astype(F32)))
                   - jnp.exp(jnp.sum(ev_lam_q2[i].astype(F32) * ev_lam_k2[i].astype(F32))) + lambda_init)
            qt = jnp.swapaxes(q.reshape(bsz, seq, DA_QK), 1, 2)
            vt = jnp.swapaxes(v.reshape(bsz, seq, DA_WIDTH), 1, 2)
            y_da = diff_attention(qt, k.reshape(bsz, seq, DA_QK), vt,
                                  lam, ev_subln_g[i], lambda_init).reshape(n, DA_WIDTH)
            h, hb = even_out_proj(y_hy, y_da, ev_w_out[i].astype(BF16), h, ln1_g[layer], ln1_b[layer])
        else:
            u = matmul(hb, od_w_in[i].astype(BF16), F32)
            gate_b = u[:, :RG_WIDTH]
            xr = _depthwise_conv(u[:, RG_WIDTH:].reshape(bsz, seq, RG_WIDTH), od_conv_w[i], od_conv_b[i], 2)
            yd = rglru_bidir(xr, od_wa[i], od_ba[i], od_wx[i], od_bx[i], od_lam[i]).reshape(2, n, RG_WIDTH)
            h, hb = odd_out_proj(gate_b, yd, od_w_out[i].astype(BF16), h, ln1_g[layer], ln1_b[layer])
        h, hb = moe_ffn(h, hb, router_w, router_b, ex_w_gate[layer].astype(BF16), ex_w_up[layer].astype(BF16),
                        ex_w_down[layer].astype(BF16), ln2_g[layer], ln2_b[layer])
    return h.reshape(bsz, seq, d).astype(x.dtype)
```

```python
import functools
import math

import jax
import jax.numpy as jnp
from jax import lax
from jax.experimental import pallas as pl
from jax.experimental.pallas import tpu as pltpu

F32 = jnp.float32
BF16 = jnp.bfloat16

D_MODEL = 2048
DEPTH = 2
HY_CH = D_MODEL // 2
HY_ORDER = 2
HF_BANDS = 16
HF_TARGET = 1e-2
HF_FAST = 0.3
HF_SLOW = 1.5
DA_HEADS = 8
DA_HEAD_DIM = 64
DA_V_DIM = 2 * DA_HEAD_DIM
DA_QK = DA_HEADS * 2 * DA_HEAD_DIM
DA_WIDTH = DA_HEADS * DA_V_DIM
RG_WIDTH = D_MODEL
RG_BLOCKS = 8
RG_BLOCK_W = RG_WIDTH // RG_BLOCKS
RG_C = 8.0
N_EXPERTS = 32
N_GROUPS = 4
EXPERTS_PER_GROUP = N_EXPERTS // N_GROUPS
TOP_K = 2
D_FF = 512
DN_ALPHA = (2 * DEPTH) ** 0.25
LN_EPS = 1e-5

VMEM_LIMIT_BYTES = 56 * 1024 * 1024
MOE_ROWS = 256


def _params(*sem):
    return pltpu.CompilerParams(dimension_semantics=sem, vmem_limit_bytes=VMEM_LIMIT_BYTES)


def _mm_kernel(x_ref, w_ref, o_ref):
    o_ref[...] = jnp.dot(x_ref[...], w_ref[...], preferred_element_type=F32).astype(o_ref.dtype)


def matmul(x, w, out_dtype, tm=512, tn=1024):
    m, k = x.shape
    n = w.shape[1]
    tm, tn = min(tm, m), min(tn, n)
    return pl.pallas_call(
        _mm_kernel,
        grid=(n // tn, m // tm),
        in_specs=[pl.BlockSpec((tm, k), lambda j, i: (i, 0)),
                  pl.BlockSpec((k, tn), lambda j, i: (0, j))],
        out_specs=pl.BlockSpec((tm, tn), lambda j, i: (i, j)),
        out_shape=jax.ShapeDtypeStruct((m, n), out_dtype),
        compiler_params=_params("arbitrary", "arbitrary"),
        name="matmul",
    )(x, w)


def _ln_rows(z, g, b):
    mu = jnp.mean(z, axis=-1, keepdims=True)
    zc = z - mu
    var = jnp.mean(zc * zc, axis=-1, keepdims=True)
    return zc * lax.rsqrt(var + LN_EPS) * g + b


def _even_out_kernel(xa_ref, xb_ref, wa_ref, wb_ref, r_ref, g_ref, b_ref, o_ref, ob_ref):
    acc = jnp.dot(xa_ref[...], wa_ref[...], preferred_element_type=F32)
    acc += jnp.dot(xb_ref[...], wb_ref[...], preferred_element_type=F32)
    y = _ln_rows(DN_ALPHA * r_ref[...] + acc, g_ref[...], b_ref[...])
    o_ref[...] = y
    ob_ref[...] = y.astype(BF16)


def even_out_proj(y_hy, y_da, w_out, resid, g, b, tm=256):
    m, ka = y_hy.shape
    kb = y_da.shape[1]
    d = w_out.shape[1]
    row = lambda i: (i, 0)
    fixed = lambda i: (0, 0)
    return pl.pallas_call(
        _even_out_kernel,
        grid=(m // tm,),
        in_specs=[pl.BlockSpec((tm, ka), row), pl.BlockSpec((tm, kb), row),
                  pl.BlockSpec((ka, d), fixed), pl.BlockSpec((kb, d), fixed),
                  pl.BlockSpec((tm, d), row), pl.BlockSpec((1, d), fixed), pl.BlockSpec((1, d), fixed)],
        out_specs=[pl.BlockSpec((tm, d), row), pl.BlockSpec((tm, d), row)],
        out_shape=[jax.ShapeDtypeStruct((m, d), F32), jax.ShapeDtypeStruct((m, d), BF16)],
        compiler_params=_params("arbitrary"),
        name="even_out_proj",
    )(y_hy, y_da, w_out[:ka], w_out[ka:], resid, g.reshape(1, d), b.reshape(1, d))


def _odd_out_kernel(gate_ref, y0_ref, y1_ref, w_ref, r_ref, g_ref, b_ref, o_ref, ob_ref):
    x = jax.nn.gelu(gate_ref[...]) * (y0_ref[0] + y1_ref[0])
    acc = jnp.dot(x.astype(BF16), w_ref[...], preferred_element_type=F32)
    y = _ln_rows(DN_ALPHA * r_ref[...] + acc, g_ref[...], b_ref[...])
    o_ref[...] = y
    ob_ref[...] = y.astype(BF16)


def odd_out_proj(gate, yd, w_out, resid, g, b, tm=256):
    m, k = gate.shape
    d = w_out.shape[1]
    row = lambda i: (i, 0)
    fixed = lambda i: (0, 0)
    return pl.pallas_call(
        _odd_out_kernel,
        grid=(m // tm,),
        in_specs=[pl.BlockSpec((tm, k), row),
                  pl.BlockSpec((1, tm, k), lambda i: (0, i, 0)),
                  pl.BlockSpec((1, tm, k), lambda i: (1, i, 0)),
                  pl.BlockSpec((k, d), fixed),
                  pl.BlockSpec((tm, d), row), pl.BlockSpec((1, d), fixed), pl.BlockSpec((1, d), fixed)],
        out_specs=[pl.BlockSpec((tm, d), row), pl.BlockSpec((tm, d), row)],
        out_shape=[jax.ShapeDtypeStruct((m, d), F32), jax.ShapeDtypeStruct((m, d), BF16)],
        compiler_params=_params("arbitrary"),
        name="odd_out_proj",
    )(gate, yd, yd, w_out, resid, g.reshape(1, d), b.reshape(1, d))


def _combine_ln_kernel(h_ref, y0_ref, y1_ref, gate_ref, g_ref, b_ref, o_ref, ob_ref):
    gate = gate_ref[...]
    ffn = gate[:, 0:1] * y0_ref[...].astype(F32) + gate[:, 1:2] * y1_ref[...].astype(F32)
    y = _ln_rows(DN_ALPHA * h_ref[...] + ffn, g_ref[...], b_ref[...])
    o_ref[...] = y
    ob_ref[...] = y.astype(BF16)


def combine_ln(h, y0, y1, gate, g, b, tm=512):
    m, d = h.shape
    row = lambda i: (i, 0)
    fixed = lambda i: (0, 0)
    return pl.pallas_call(
        _combine_ln_kernel,
        grid=(m // tm,),
        in_specs=[pl.BlockSpec((tm, d), row), pl.BlockSpec((tm, d), row), pl.BlockSpec((tm, d), row),
                  pl.BlockSpec((tm, TOP_K), row), pl.BlockSpec((1, d), fixed), pl.BlockSpec((1, d), fixed)],
        out_specs=[pl.BlockSpec((tm, d), row), pl.BlockSpec((tm, d), row)],
        out_shape=[jax.ShapeDtypeStruct((m, d), F32), jax.ShapeDtypeStruct((m, d), BF16)],
        compiler_params=_params("arbitrary"),
        name="combine_ln",
    )(h, y0, y1, gate, g.reshape(1, d), b.reshape(1, d))


ONES_ROWS = 16
POS_SPLIT = 16


def _attn_kernel(slopes_ref, lam_ref, qt_ref, k_ref, vt_ref, fq_ref, fk_ref, g_ref, o_ref, m_s, acc_s, sa_s, sb_s,
                 *, tq, tk, seq, out_scale):
    head = pl.program_id(1)
    i0 = pl.program_id(2) * tq
    slope = slopes_ref[head]
    lam = lam_ref[0]
    qt = qt_ref[...]
    row = lax.broadcasted_iota(jnp.int32, qt.shape, 0)
    zero = jnp.zeros_like(qt)
    q2 = jnp.concatenate([jnp.where(row < DA_HEAD_DIM, qt, zero),
                          jnp.where(row >= DA_HEAD_DIM, qt, zero)], axis=1)
    fq = fq_ref[0]
    fq2 = jnp.concatenate([fq, fq], axis=1)
    q_aug = jnp.concatenate([q2, fq2], axis=0)
    fk = fk_ref[0]
    fk_neg = -fk
    ones = jnp.ones((ONES_ROWS, tk), BF16)
    m_s[...] = jnp.full(m_s.shape, -jnp.inf, F32)
    acc_s[...] = jnp.zeros(acc_s.shape, F32)
    j_diag = i0 // tk

    def update(j, s, shift):
        off = pl.multiple_of(j * tk, tk)
        va = jnp.concatenate([vt_ref[:, pl.ds(off, tk)], ones], axis=0)
        m_prev = m_s[...]
        m_new = jnp.maximum(m_prev, jnp.max(s, axis=0, keepdims=True) - shift)
        p = jnp.exp(s - (m_new + shift))
        alpha = jnp.exp(m_prev - m_new)
        acc_s[...] = alpha * acc_s[...] + jnp.dot(va, p.astype(BF16), preferred_element_type=F32)
        m_s[...] = m_new

    def chunk_of(t):
        return t + (t >= j_diag).astype(jnp.int32)

    def scores(t):
        j = chunk_of(t)
        off = pl.multiple_of(j * tk, tk)
        ka = jnp.concatenate([k_ref[0, pl.ds(off, tk), :], jnp.where(j < j_diag, fk, fk_neg)], axis=1)
        return jnp.dot(ka, q_aug, preferred_element_type=F32)

    def consume(s_ref, t):
        j = chunk_of(t)
        update(j, s_ref[...], slope * jnp.abs(i0 - j * tk).astype(F32))

    j0 = pl.multiple_of(j_diag * tk, tk)
    dist = jnp.abs(lax.broadcasted_iota(jnp.int32, (tk, tq), 0) + (j0 - i0)
                   - lax.broadcasted_iota(jnp.int32, (tk, tq), 1)).astype(F32)
    bias = slope * dist
    s_diag = jnp.dot(k_ref[0, pl.ds(j0, tk), :], q2, preferred_element_type=F32)
    sa_s[...] = s_diag - jnp.concatenate([bias, bias], axis=1)
    sb_s[...] = scores(jnp.int32(0))
    update(j_diag, sa_s[...], 0.0)
    n_off = seq // tk - 1

    def pair(t, carry):
        sa_s[...] = scores(2 * t + 1)
        consume(sb_s, 2 * t)
        sb_s[...] = scores(2 * t + 2)
        consume(sa_s, 2 * t + 1)
        return carry

    lax.fori_loop(0, n_off // 2, pair, 0)
    consume(sb_s, jnp.int32(n_off - 1))

    acc = acc_s[...]
    o1 = acc[:DA_V_DIM, :tq] / acc[DA_V_DIM:DA_V_DIM + 1, :tq]
    o2 = acc[:DA_V_DIM, tq:] / acc[DA_V_DIM:DA_V_DIM + 1, tq:]
    o = o1 - lam * o2
    o = o * lax.rsqrt(jnp.mean(o * o, axis=0, keepdims=True) + LN_EPS) * g_ref[...]
    o_ref[0] = (o * out_scale).T.astype(o_ref.dtype)


def diff_attention(qt, k, vt, lam, subln_g, lambda_init, tq=256, tk=512):
    bsz, seq, width = k.shape
    nq = seq // tq
    slopes = 2.0 ** (-(8.0 / DA_HEADS) * jnp.arange(1, DA_HEADS + 1, dtype=F32))

    def split(n):
        pos = jnp.arange(n, dtype=jnp.int32)
        return (pos // POS_SPLIT * POS_SPLIT).astype(F32), (pos % POS_SPLIT).astype(F32)

    qhi, qlo = split(tq)
    khi, klo = split(tk)
    sl = slopes[:, None]
    fq = jnp.zeros((DA_HEADS, DA_V_DIM, tq), F32)
    fq = fq.at[:, 0].set(1.0).at[:, 1].set(1.0).at[:, 2].set(-sl * qhi).at[:, 3].set(-sl * qlo)
    fk = jnp.zeros((DA_HEADS, tk, DA_V_DIM), F32)
    fk = fk.at[:, :, 0].set(sl * khi).at[:, :, 1].set(sl * klo).at[:, :, 2].set(1.0).at[:, :, 3].set(1.0)
    kern = functools.partial(_attn_kernel, tq=tq, tk=tk, seq=seq, out_scale=1.0 - lambda_init)
    smem = pl.BlockSpec(memory_space=pltpu.SMEM)
    return pl.pallas_call(
        kern,
        grid=(bsz, DA_HEADS, seq // tq),
        in_specs=[smem, smem,
                  pl.BlockSpec((DA_V_DIM, tq), lambda b, h, i: (h, b * nq + i)),
                  pl.BlockSpec((1, seq, DA_V_DIM), lambda b, h, i: (b, 0, h)),
                  pl.BlockSpec((DA_V_DIM, seq), lambda b, h, i: (h, b)),
                  pl.BlockSpec((1, DA_V_DIM, tq), lambda b, h, i: (h, 0, 0)),
                  pl.BlockSpec((1, tk, DA_V_DIM), lambda b, h, i: (h, 0, 0)),
                  pl.BlockSpec((DA_V_DIM, 1), lambda b, h, i: (0, 0))],
        out_specs=pl.BlockSpec((1, tq, DA_V_DIM), lambda b, h, i: (b, i, h)),
        out_shape=jax.ShapeDtypeStruct((bsz, seq, width), BF16),
        scratch_shapes=[pltpu.VMEM((1, 2 * tq), F32), pltpu.VMEM((DA_V_DIM + ONES_ROWS, 2 * tq), F32),
                        pltpu.VMEM((tk, 2 * tq), F32), pltpu.VMEM((tk, 2 * tq), F32)],
        compiler_params=_params("arbitrary", "arbitrary", "arbitrary"),
        name="diff_attention",
    )(slopes, lam.reshape(1).astype(F32), qt, k, vt, fq.astype(BF16), fk.astype(BF16),
      subln_g.reshape(DA_V_DIM, 1).astype(F32))


def _rglru_kernel(xr_ref, wa_ref, wx_ref, ba_ref, bx_ref, sp_ref, y_ref, a_s, b_s, h_s, *, ts):
    direction = pl.program_id(0)

    @pl.when(pl.program_id(2) == 0)
    def _():
        h_s[...] = jnp.zeros(h_s.shape, F32)

    x = xr_ref[0]
    xb = x.astype(BF16)
    for n in range(RG_BLOCKS):
        cols = slice(n * RG_BLOCK_W, (n + 1) * RG_BLOCK_W)
        xs = xb[:, cols]
        r = jax.nn.sigmoid(jnp.dot(xs, wa_ref[0, n], preferred_element_type=F32) + ba_ref[0, :, cols])
        i = jax.nn.sigmoid(jnp.dot(xs, wx_ref[0, n], preferred_element_type=F32) + bx_ref[0, :, cols])
        log_a = -RG_C * r * sp_ref[0, :, cols]
        a = jnp.exp(log_a)
        a_s[:, cols] = a
        b_s[:, cols] = jnp.sqrt(1.0 - a * a) * (i * x[:, cols])

    def step(k, h):
        row = jnp.where(direction == 0, k, ts - 1 - k)
        h = a_s[pl.ds(row, 1), :] * h + b_s[pl.ds(row, 1), :]
        y_ref[0, 0, pl.ds(row, 1), :] = h
        return h

    h_s[...] = lax.fori_loop(0, ts, step, h_s[...], unroll=8)


def rglru_bidir(xr, wa, ba, wx, bx, lam, ts=256):
    bsz, seq, w = xr.shape
    nt = seq // ts
    sp = jax.nn.softplus(-lam.astype(F32)).reshape(2, 1, w)
    tmap = lambda d, b, t: (b, t + d * (nt - 1 - 2 * t), 0)
    dmap4 = lambda d, b, t: (d, 0, 0, 0)
    dmap3 = lambda d, b, t: (d, 0, 0)
    return pl.pallas_call(
        functools.partial(_rglru_kernel, ts=ts),
        grid=(2, bsz, nt),
        in_specs=[pl.BlockSpec((1, ts, w), tmap),
                  pl.BlockSpec((1, RG_BLOCKS, RG_BLOCK_W, RG_BLOCK_W), dmap4),
                  pl.BlockSpec((1, RG_BLOCKS, RG_BLOCK_W, RG_BLOCK_W), dmap4),
                  pl.BlockSpec((1, 1, w), dmap3), pl.BlockSpec((1, 1, w), dmap3), pl.BlockSpec((1, 1, w), dmap3)],
        out_specs=pl.BlockSpec((1, 1, ts, w), lambda d, b, t: (d, b, t + d * (nt - 1 - 2 * t), 0)),
        out_shape=jax.ShapeDtypeStruct((2, bsz, seq, w), F32),
        scratch_shapes=[pltpu.VMEM((ts, w), F32), pltpu.VMEM((ts, w), F32), pltpu.VMEM((1, w), F32)],
        compiler_params=_params("arbitrary", "arbitrary", "arbitrary"),
        name="rglru_bidir",
    )(xr, wa.astype(BF16), wx.astype(BF16), ba.astype(F32).reshape(2, 1, w), bx.astype(F32).reshape(2, 1, w), sp)


def _router_kernel(h_ref, wt_ref, b_ref, e_ref, g_ref):
    logits = lax.dot_general(wt_ref[...], h_ref[...], (((1,), (1,)), ((), ())),
                             precision=lax.Precision.HIGHEST, preferred_element_type=F32)
    tm = logits.shape[1]
    s = jax.nn.sigmoid(logits).reshape(N_GROUPS, EXPERTS_PER_GROUP, tm)
    sel = s + b_ref[...]
    idx = lax.broadcasted_iota(jnp.int32, sel.shape, 1)
    big = jnp.int32(EXPERTS_PER_GROUP)
    v1 = jnp.max(sel, axis=1, keepdims=True)
    i1 = jnp.min(jnp.where(sel == v1, idx, big), axis=1, keepdims=True)
    rest = jnp.where(idx == i1, -jnp.inf, sel)
    v2 = jnp.max(rest, axis=1, keepdims=True)
    i2 = jnp.min(jnp.where(rest == v2, idx, big), axis=1, keepdims=True)
    score = v1 + v2
    gidx = lax.broadcasted_iota(jnp.int32, score.shape, 0)
    best = jnp.max(score, axis=0, keepdims=True)
    grp = jnp.min(jnp.where(score == best, gidx, jnp.int32(N_GROUPS)), axis=0, keepdims=True)
    pick = gidx == grp
    l1 = jnp.sum(jnp.where(pick, i1, 0), axis=0)
    l2 = jnp.sum(jnp.where(pick, i2, 0), axis=0)
    s_g = jnp.sum(jnp.where(pick, s, 0.0), axis=0)
    eidx = lax.broadcasted_iota(jnp.int32, s_g.shape, 0)
    w1 = jnp.sum(jnp.where(eidx == l1, s_g, 0.0), axis=0, keepdims=True)
    w2 = jnp.sum(jnp.where(eidx == l2, s_g, 0.0), axis=0, keepdims=True)
    tot = w1 + w2
    base = grp[0] * EXPERTS_PER_GROUP
    e_ref[...] = jnp.concatenate([base + l1, base + l2], axis=0)
    g_ref[...] = jnp.concatenate([w1 / tot, w2 / tot], axis=0)


def route(h, router_w, router_b, tm=1024):
    n, d = h.shape
    return pl.pallas_call(
        _router_kernel,
        grid=(n // tm,),
        in_specs=[pl.BlockSpec((tm, d), lambda i: (i, 0)),
                  pl.BlockSpec((N_EXPERTS, d), lambda i: (0, 0)),
                  pl.BlockSpec((N_GROUPS, EXPERTS_PER_GROUP, 1), lambda i: (0, 0, 0))],
        out_specs=[pl.BlockSpec((TOP_K, tm), lambda i: (0, i)), pl.BlockSpec((TOP_K, tm), lambda i: (0, i))],
        out_shape=[jax.ShapeDtypeStruct((TOP_K, n), jnp.int32), jax.ShapeDtypeStruct((TOP_K, n), F32)],
        compiler_params=_params("arbitrary"),
        name="router",
    )(h, router_w.astype(F32).T, router_b.astype(F32).reshape(N_GROUPS, EXPERTS_PER_GROUP, 1))


def _experts_kernel(blk_exp_ref, n_used_ref, x_ref, wg_ref, wu_ref, wd_ref, o_ref):
    i = pl.program_id(0)

    @pl.when(i < n_used_ref[0])
    def _():
        x = x_ref[...]
        hg = jnp.dot(x, wg_ref[0], preferred_element_type=F32)
        hu = jnp.dot(x, wu_ref[0], preferred_element_type=F32)
        hidden = (jax.nn.silu(hg) * hu).astype(BF16)
        o_ref[...] = jnp.dot(hidden, wd_ref[0], preferred_element_type=F32).astype(o_ref.dtype)

    @pl.when(i >= n_used_ref[0])
    def _():
        o_ref[...] = jnp.zeros(o_ref.shape, o_ref.dtype)


def experts(xs, blk_exp, n_used, w_gate, w_up, w_down):
    p, d = xs.shape
    f = w_gate.shape[2]
    grid_spec = pltpu.PrefetchScalarGridSpec(
        num_scalar_prefetch=2,
        grid=(p // MOE_ROWS,),
        in_specs=[pl.BlockSpec((MOE_ROWS, d), lambda i, be, nu: (i, 0)),
                  pl.BlockSpec((1, d, f), lambda i, be, nu: (be[i], 0, 0)),
                  pl.BlockSpec((1, d, f), lambda i, be, nu: (be[i], 0, 0)),
                  pl.BlockSpec((1, f, d), lambda i, be, nu: (be[i], 0, 0))],
        out_specs=pl.BlockSpec((MOE_ROWS, d), lambda i, be, nu: (i, 0)),
    )
    return pl.pallas_call(
        _experts_kernel,
        grid_spec=grid_spec,
        out_shape=jax.ShapeDtypeStruct((p, d), BF16),
        compiler_params=_params("arbitrary"),
        name="experts",
    )(blk_exp, n_used, xs, w_gate, w_up, w_down)


def moe_ffn(h, hb, router_w, router_b, w_gate, w_up, w_down, ln_g, ln_b):
    n, d = h.shape
    a = n * TOP_K
    e_idx, gate = route(h, router_w, router_b)
    e_flat = e_idx.T.reshape(a)
    onehot = (e_flat[:, None] == jnp.arange(N_EXPERTS, dtype=jnp.int32)[None, :]).astype(jnp.int32)
    csum = jnp.cumsum(onehot, axis=0)
    rank = jnp.sum(csum * onehot, axis=1) - 1
    counts = csum[-1]
    padded = (counts + MOE_ROWS - 1) // MOE_ROWS * MOE_ROWS
    pad_end = jnp.cumsum(padded)
    pad_start = pad_end - padded
    dest = pad_start[e_flat] + rank
    n_blocks = a // MOE_ROWS + N_EXPERTS
    p = n_blocks * MOE_ROWS
    row_tok = jnp.zeros((p,), jnp.int32).at[dest].set(jnp.arange(a, dtype=jnp.int32) // TOP_K)
    blk_exp = jnp.minimum(jnp.searchsorted(pad_end, jnp.arange(n_blocks, dtype=jnp.int32) * MOE_ROWS, side='right'),
                          N_EXPERTS - 1).astype(jnp.int32)
    n_used = (pad_end[-1:] // MOE_ROWS).astype(jnp.int32)
    xs = jnp.take(hb, row_tok, axis=0)
    yb = experts(xs, blk_exp, n_used, w_gate, w_up, w_down)
    dest2 = dest.reshape(n, TOP_K)
    y0 = jnp.take(yb, dest2[:, 0], axis=0)
    y1 = jnp.take(yb, dest2[:, 1], axis=0)
    return combine_ln(h, y0, y1, gate.T, ln_g, ln_b)


FFT_R = 128
HALF_R = FFT_R // 2


def _matmul_nt_kernel(a_ref, b_ref, o_ref):
    o_ref[...] = lax.dot_general(a_ref[...], b_ref[...], (((1,), (1,)), ((), ())),
                                 preferred_element_type=F32).astype(o_ref.dtype)


def matmul_nt(a, b, out_dtype, tm=512, tn=1024):
    m, k = a.shape
    n = b.shape[0]
    tm, tn = min(tm, m), min(tn, n)
    return pl.pallas_call(
        _matmul_nt_kernel,
        grid=(m // tm, n // tn),
        in_specs=[pl.BlockSpec((tm, k), lambda i, j: (i, 0)),
                  pl.BlockSpec((tn, k), lambda i, j: (j, 0))],
        out_specs=pl.BlockSpec((tm, tn), lambda i, j: (i, j)),
        out_shape=jax.ShapeDtypeStruct((m, n), out_dtype),
        compiler_params=_params("arbitrary", "arbitrary"),
        name="matmul_nt",
    )(a, b)


def _dft_constants():
    idx = jnp.arange(FFT_R, dtype=jnp.int32)
    prod = idx[:, None] * idx[None, :]
    ang = (prod % FFT_R).astype(F32) * (2.0 * math.pi / FFT_R)
    f_r, f_i = jnp.cos(ang), -jnp.sin(ang)
    ang_t = prod.astype(F32) * (2.0 * math.pi / (FFT_R * FFT_R))
    g_r, g_i = f_r[:HALF_R], -f_i[:HALF_R]
    return dict(
        la=jnp.block([[f_r[:, :HALF_R], -f_i[:, :HALF_R]], [f_i[:, :HALF_R], f_r[:, :HALF_R]]]).astype(BF16),
        la_real=jnp.concatenate([f_r, f_i], axis=0).astype(BF16),
        rb=jnp.block([[f_r, f_i], [-f_i, f_r]]).astype(BF16),
        rc=jnp.block([[f_r, -f_i], [f_i, f_r]]).astype(BF16),
        ld=(jnp.block([[g_r, -g_i], [g_i, g_r]]) / (FFT_R * FFT_R)).astype(BF16),
        t_r=jnp.cos(ang_t), t_i=-jnp.sin(ang_t))


def _rows_to_lanes(x):
    return jnp.concatenate([x[:FFT_R], x[FFT_R:]], axis=1)


def _fwd_lane_stage(a, t_r, t_i, rb):
    a_r, a_i = a[:FFT_R], a[FFT_R:]
    t2_r = jnp.concatenate([t_r, t_r], axis=1)
    t2_i = jnp.concatenate([t_i, t_i], axis=1)
    b_r = a_r * t2_r - a_i * t2_i
    b_i = a_r * t2_i + a_i * t2_r
    lhs = jnp.concatenate([jnp.concatenate([b_r[:, :FFT_R], b_i[:, :FFT_R]], axis=1),
                           jnp.concatenate([b_r[:, FFT_R:], b_i[:, FFT_R:]], axis=1)], axis=0)
    return jnp.dot(lhs.astype(BF16), rb, preferred_element_type=F32)


def _conv_pair(m_r, m_i, kf, la, rb, rc, ld, t_r, t_i):
    a = jnp.dot(la, jnp.concatenate([m_r, m_i], axis=0).astype(BF16), preferred_element_type=F32)
    z = _fwd_lane_stage(a, t_r, t_i, rb)
    z_r, z_i, k_r, k_i = z[:, :FFT_R], z[:, FFT_R:], kf[:, :FFT_R], kf[:, FFT_R:]
    y = jnp.concatenate([z_r * k_r - z_i * k_i, z_r * k_i + z_i * k_r], axis=1)
    c = jnp.dot(y.astype(BF16), rc, preferred_element_type=F32)
    c_r, c_i = c[:, :FFT_R], c[:, FFT_R:]
    t2_r = jnp.concatenate([t_r, t_r], axis=0)
    t2_i = jnp.concatenate([t_i, t_i], axis=0)
    d_r = c_r * t2_r + c_i * t2_i
    d_i = c_i * t2_r - c_r * t2_i
    rhs = jnp.concatenate([_rows_to_lanes(d_r), _rows_to_lanes(d_i)], axis=0)
    x = jnp.dot(ld, rhs.astype(BF16), preferred_element_type=F32)
    return x[:HALF_R], x[HALF_R:]


def _short_conv_tile(x, w0, w1, w2, b):
    rows = x.shape[0]
    lane = lax.broadcasted_iota(jnp.int32, x.shape, 1)
    row = lax.broadcasted_iota(jnp.int32, x.shape, 0)
    r = pltpu.roll(x, 1, 1)
    prev = jnp.where(lane == 0, jnp.where(row == 0, 0.0, pltpu.roll(r, 1, 0)), r)
    r = pltpu.roll(x, FFT_R - 1, 1)
    nxt = jnp.where(lane == FFT_R - 1, jnp.where(row == rows - 1, 0.0, pltpu.roll(r, rows - 1, 0)), r)
    return w0 * prev + w1 * x + w2 * nxt + b


def _hyena_kernel(cw_ref, cb_ref, skip_ref, v_ref, x1_ref, x2_ref, kf_ref,
                  la_ref, rb_ref, rc_ref, ld_ref, tr_ref, ti_ref, o_ref, *, tc):
    c_base = pl.program_id(0) * tc
    lane = lax.broadcasted_iota(jnp.int32, (1, 2 * FFT_R), 1)

    def pair(p, carry):
        c0 = 2 * p

        def conv_in(ref, part, b):
            tiles = []
            for cc in range(2):
                ch = part * HY_CH + c_base + c0 + cc
                tiles.append(_short_conv_tile(ref[c0 + cc, b], cw_ref[0, ch], cw_ref[1, ch], cw_ref[2, ch],
                                              cb_ref[ch]))
            return jnp.concatenate(tiles, axis=1)

        def per_channel(o):
            return jnp.where(lane < FFT_R, skip_ref[o, c_base + c0], skip_ref[o, c_base + c0 + 1])

        consts = (la_ref[...], rb_ref[...], rc_ref[...], ld_ref[...], tr_ref[...], ti_ref[...])
        z_r, z_i = conv_in(v_ref, 0, 0), conv_in(v_ref, 0, 1)
        for o, g_ref in enumerate((x1_ref, x2_ref)):
            kf = kf_ref[o, pl.ds(c0, 2)].reshape(2 * FFT_R, 2 * FFT_R)
            y_r, y_i = _conv_pair(z_r, z_i, kf, *consts)
            sk = per_channel(o)
            z_r = conv_in(g_ref, o + 1, 0) * (y_r + sk * z_r)
            z_i = conv_in(g_ref, o + 1, 1) * (y_i + sk * z_i)
        o_ref[c0, 0] = z_r[:, :FFT_R].astype(o_ref.dtype)
        o_ref[c0 + 1, 0] = z_r[:, FFT_R:].astype(o_ref.dtype)
        o_ref[c0, 1] = z_i[:, :FFT_R].astype(o_ref.dtype)
        o_ref[c0 + 1, 1] = z_i[:, FFT_R:].astype(o_ref.dtype)
        return carry

    lax.fori_loop(0, tc // 2, pair, 0)


def hyena_mix(u_t, kf, conv_w, conv_b, skip, consts, tc=16):
    rows, n = u_t.shape
    ch = rows // 3
    u5 = u_t.reshape(rows, 2, HALF_R, FFT_R)
    nct = ch // tc
    smem = pl.BlockSpec(memory_space=pltpu.SMEM)
    part = lambda k: pl.BlockSpec((tc, 2, HALF_R, FFT_R), lambda i: (k * nct + i, 0, 0, 0))
    full2 = lambda a: pl.BlockSpec(a.shape, lambda i: (0, 0))
    mats = [consts[k] for k in ("la", "rb", "rc", "ld", "t_r", "t_i")]
    out = pl.pallas_call(
        functools.partial(_hyena_kernel, tc=tc),
        grid=(nct,),
        in_specs=[smem, smem, smem, part(0), part(1), part(2),
                  pl.BlockSpec((2, tc, FFT_R, 2 * FFT_R), lambda i: (0, i, 0, 0))] + [full2(a) for a in mats],
        out_specs=pl.BlockSpec((tc, 2, HALF_R, FFT_R), lambda i: (i, 0, 0, 0)),
        out_shape=jax.ShapeDtypeStruct((ch, 2, HALF_R, FFT_R), BF16),
        compiler_params=_params("arbitrary"),
        name="hyena_mix",
    )(conv_w.astype(F32), conv_b.astype(F32), skip.astype(F32), u5, u5, u5, kf, *mats)
    return out.reshape(ch, n)


def _hyena_filter_kernel(delta_ref, hf_ref, hb_ref, tpos_ref, la_ref, rb_ref, tr_ref, ti_ref, kf_ref, *, tc):
    c_base = pl.program_id(1) * tc

    def pair(p, carry):
        c0 = 2 * p
        taps = []
        for cc in range(2):
            k = jnp.concatenate([hf_ref[0, 0, c0 + cc], hb_ref[0, 0, c0 + cc]], axis=0)
            k = k * jnp.exp(-tpos_ref[...] * delta_ref[c_base + c0 + cc])
            taps.append(k * lax.rsqrt(jnp.sum(k * k, keepdims=True)))
        a = jnp.dot(la_ref[...], jnp.concatenate(taps, axis=1).astype(BF16), preferred_element_type=F32)
        z = _fwd_lane_stage(a, tr_ref[...], ti_ref[...], rb_ref[...])
        kf_ref[0, pl.ds(c0, 2)] = z.reshape(2, FFT_R, 2 * FFT_R)
        return carry

    lax.fori_loop(0, tc // 2, pair, 0)


def hyena_filter_spectra(seq, w1, b1, w2, b2, w3, b3, freq, w4, consts, tc=16):
    t = jnp.linspace(0.0, 1.0, seq, dtype=F32)[:, None]
    omega = 2.0 * math.pi * jnp.arange(seq, dtype=F32)[:, None] / seq
    bands = jnp.linspace(1e-4, HF_BANDS - 1, HF_BANDS, dtype=F32)[None, :]
    ang = omega * bands
    z = jnp.concatenate([t, jnp.cos(ang), -jnp.sin(ang)], -1)
    fr = freq.astype(F32)
    hid = jnp.sin(fr * (z @ w1.astype(F32) + b1.astype(F32)))
    hid = jnp.sin(fr * (hid @ w2.astype(F32) + b2.astype(F32)))
    hid = jnp.sin(fr * (hid @ w3.astype(F32) + b3.astype(F32)))
    back = lambda a: jnp.concatenate([jnp.zeros_like(a[:1]), a[:0:-1]], axis=0)
    w4t = w4.astype(BF16).T
    half = HY_ORDER * HY_CH
    h_fwd = matmul(w4t[:half], hid.T.astype(BF16), F32)
    h_bwd = matmul(w4t[half:], back(hid).T.astype(BF16), F32)
    shape5 = (HY_ORDER, HY_CH, HALF_R, FFT_R)
    tpos = jnp.concatenate([t, back(t)], axis=0).reshape(FFT_R, FFT_R)
    max_decay = math.log(HF_TARGET) / HF_FAST
    min_decay = math.log(HF_TARGET) / HF_SLOW
    deltas = jnp.abs(jnp.linspace(min_decay, max_decay, HY_CH, dtype=F32))
    taps = pl.BlockSpec((1, 1, tc, HALF_R, FFT_R), lambda o, i: (0, o, i, 0, 0))
    full2 = lambda a: pl.BlockSpec(a.shape, lambda o, i: (0, 0))
    mats = [consts[k] for k in ("la_real", "rb", "t_r", "t_i")]
    return pl.pallas_call(
        functools.partial(_hyena_filter_kernel, tc=tc),
        grid=(HY_ORDER, HY_CH // tc),
        in_specs=[pl.BlockSpec(memory_space=pltpu.SMEM), taps, taps, full2(tpos)] + [full2(a) for a in mats],
        out_specs=pl.BlockSpec((1, tc, FFT_R, 2 * FFT_R), lambda o, i: (o, i, 0, 0)),
        out_shape=jax.ShapeDtypeStruct((HY_ORDER, HY_CH, FFT_R, 2 * FFT_R), F32),
        compiler_params=_params("arbitrary", "arbitrary"),
        name="hyena_filter_spectra",
    )(deltas, h_fwd.reshape((1,) + shape5), h_bwd.reshape((1,) + shape5), tpos, *mats)


def _depthwise_conv(x, w, b, left):
    k, s = w.shape[0], x.shape[1]
    xp = jnp.pad(x, ((0, 0), (left, k - 1 - left), (0, 0)))
    return sum(xp[:, j:j + s] * w[j] for j in range(k)) + b


def kernel(x, ev_w_in, ev_hy_conv_w, ev_hy_conv_b, ev_hf_w1, ev_hf_b1, ev_hf_w2, ev_hf_b2, ev_hf_w3, ev_hf_b3,
           ev_hf_freq, ev_hf_w4, ev_hy_skip, ev_lam_q1, ev_lam_k1, ev_lam_q2, ev_lam_k2, ev_subln_g, ev_w_out,
           od_w_in, od_conv_w, od_conv_b, od_wa, od_ba, od_wx, od_bx, od_lam, od_w_out, ln1_g, ln1_b, ln2_g,
           ln2_b, router_w, router_b, ex_w_gate, ex_w_up, ex_w_down):
    bsz, seq, d = x.shape
    n = bsz * seq
    h = x.reshape(n, d).astype(F32)
    hb = h.astype(BF16)
    for layer in range(DEPTH):
        i = layer // 2
        if layer % 2 == 0:
            w_in = ev_w_in[i]
            c0 = 3 * HY_CH
            w_t = w_in.T
            u_t = matmul_nt(w_t[:c0].astype(BF16), hb, F32)
            qt = matmul_nt((w_t[c0:c0 + DA_QK] * (DA_HEAD_DIM ** -0.5)).astype(BF16), hb, BF16)
            k = matmul(hb, w_in[:, c0 + DA_QK:c0 + 2 * DA_QK].astype(BF16), BF16)
            vt = matmul_nt(w_t[c0 + 2 * DA_QK:].astype(BF16), hb, BF16)
            consts = _dft_constants()
            kf = hyena_filter_spectra(seq, ev_hf_w1[i], ev_hf_b1[i], ev_hf_w2[i], ev_hf_b2[i],
                                      ev_hf_w3[i], ev_hf_b3[i], ev_hf_freq[i], ev_hf_w4[i], consts)
            y_hy = hyena_mix(u_t, kf, ev_hy_conv_w[i], ev_hy_conv_b[i], ev_hy_skip[i], consts).T
            lambda_init = 0.8 - 0.6 * math.exp(-0.3 * layer)
            lam = (jnp.exp(jnp.sum(ev_lam_q1[i].astype(F32) * ev_lam_k1[i].astype(F32)))
                   - jnp.exp(jnp.sum(ev_lam_q2[i].astype(F32) * ev_lam_k2[i].astype(F32))) + lambda_init)
            y_da = diff_attention(qt, k.reshape(bsz, seq, DA_QK), vt,
                                  lam, ev_subln_g[i], lambda_init).reshape(n, DA_WIDTH)
            h, hb = even_out_proj(y_hy, y_da, ev_w_out[i].astype(BF16), h, ln1_g[layer], ln1_b[layer])
        else:
            u = matmul(hb, od_w_in[i].astype(BF16), F32)
            gate_b = u[:, :RG_WIDTH]
            xr = _depthwise_conv(u[:, RG_WIDTH:].reshape(bsz, seq, RG_WIDTH), od_conv_w[i], od_conv_b[i], 2)
            yd = rglru_bidir(xr, od_wa[i], od_ba[i], od_wx[i], od_bx[i], od_lam[i]).reshape(2, n, RG_WIDTH)
            h, hb = odd_out_proj(gate_b, yd, od_w_out[i].astype(BF16), h, ln1_g[layer], ln1_b[layer])
        h, hb = moe_ffn(h, hb, router_w, router_b, ex_w_gate[layer].astype(BF16), ex_w_up[layer].astype(BF16),
                        ex_w_down[layer].astype(BF16), ln2_g[layer], ln2_b[layer])
    return h.reshape(bsz, seq, d).astype(x.dtype)
```

```python
import functools
import math

import jax
import jax.numpy as jnp
from jax import lax
from jax.experimental import pallas as pl
from jax.experimental.pallas import tpu as pltpu

F32 = jnp.float32
BF16 = jnp.bfloat16

D_MODEL = 2048
DEPTH = 2
HY_CH = D_MODEL // 2
HY_ORDER = 2
HF_BANDS = 16
HF_TARGET = 1e-2
HF_FAST = 0.3
HF_SLOW = 1.5
DA_HEADS = 8
DA_HEAD_DIM = 64
DA_V_DIM = 2 * DA_HEAD_DIM
DA_QK = DA_HEADS * 2 * DA_HEAD_DIM
DA_WIDTH = DA_HEADS * DA_V_DIM
RG_WIDTH = D_MODEL
RG_BLOCKS = 8
RG_BLOCK_W = RG_WIDTH // RG_BLOCKS
RG_C = 8.0
RG_CONV = 4
N_EXPERTS = 32
N_GROUPS = 4
EXPERTS_PER_GROUP = N_EXPERTS // N_GROUPS
TOP_K = 2
D_FF = 512
DN_ALPHA = (2 * DEPTH) ** 0.25
LN_EPS = 1e-5

VMEM_LIMIT_BYTES = 56 * 1024 * 1024
MOE_ROWS = 256


def _params(*sem):
    return pltpu.CompilerParams(dimension_semantics=sem, vmem_limit_bytes=VMEM_LIMIT_BYTES)


def _mm_kernel(x_ref, w_ref, o_ref):
    o_ref[...] = jnp.dot(x_ref[...], w_ref[...], preferred_element_type=F32).astype(o_ref.dtype)


def matmul(x, w, out_dtype, tm=512, tn=1024):
    m, k = x.shape
    n = w.shape[1]
    tm, tn = min(tm, m), min(tn, n)
    return pl.pallas_call(
        _mm_kernel,
        grid=(n // tn, m // tm),
        in_specs=[pl.BlockSpec((tm, k), lambda j, i: (i, 0)),
                  pl.BlockSpec((k, tn), lambda j, i: (0, j))],
        out_specs=pl.BlockSpec((tm, tn), lambda j, i: (i, j)),
        out_shape=jax.ShapeDtypeStruct((m, n), out_dtype),
        compiler_params=_params("arbitrary", "arbitrary"),
        name="matmul",
    )(x, w)


def _ln_rows(z, g, b):
    mu = jnp.mean(z, axis=-1, keepdims=True)
    zc = z - mu
    var = jnp.mean(zc * zc, axis=-1, keepdims=True)
    return zc * lax.rsqrt(var + LN_EPS) * g + b


def _even_out_kernel(xa_ref, xb_ref, wa_ref, wb_ref, r_ref, g_ref, b_ref, o_ref, ob_ref):
    acc = jnp.dot(xa_ref[...], wa_ref[...], preferred_element_type=F32)
    acc += jnp.dot(xb_ref[...], wb_ref[...], preferred_element_type=F32)
    y = _ln_rows(DN_ALPHA * r_ref[...] + acc, g_ref[...], b_ref[...])
    o_ref[...] = y
    ob_ref[...] = y.astype(BF16)


def even_out_proj(y_hy, y_da, w_out, resid, g, b, tm=256):
    m, ka = y_hy.shape
    kb = y_da.shape[1]
    d = w_out.shape[1]
    row = lambda i: (i, 0)
    fixed = lambda i: (0, 0)
    return pl.pallas_call(
        _even_out_kernel,
        grid=(m // tm,),
        in_specs=[pl.BlockSpec((tm, ka), row), pl.BlockSpec((tm, kb), row),
                  pl.BlockSpec((ka, d), fixed), pl.BlockSpec((kb, d), fixed),
                  pl.BlockSpec((tm, d), row), pl.BlockSpec((1, d), fixed), pl.BlockSpec((1, d), fixed)],
        out_specs=[pl.BlockSpec((tm, d), row), pl.BlockSpec((tm, d), row)],
        out_shape=[jax.ShapeDtypeStruct((m, d), F32), jax.ShapeDtypeStruct((m, d), BF16)],
        compiler_params=_params("arbitrary"),
        name="even_out_proj",
    )(y_hy, y_da, w_out[:ka], w_out[ka:], resid, g.reshape(1, d), b.reshape(1, d))


def _odd_out_kernel(gate_ref, y0_ref, y1_ref, w_ref, r_ref, g_ref, b_ref, o_ref, ob_ref):
    x = jax.nn.gelu(gate_ref[...]) * (y0_ref[0] + y1_ref[0])
    acc = jnp.dot(x.astype(BF16), w_ref[...], preferred_element_type=F32)
    y = _ln_rows(DN_ALPHA * r_ref[...] + acc, g_ref[...], b_ref[...])
    o_ref[...] = y
    ob_ref[...] = y.astype(BF16)


def odd_out_proj(u, yd, w_out, resid, g, b, tm=256):
    m = u.shape[0]
    k, d = w_out.shape
    row = lambda i: (i, 0)
    fixed = lambda i: (0, 0)
    return pl.pallas_call(
        _odd_out_kernel,
        grid=(m // tm,),
        in_specs=[pl.BlockSpec((tm, k), row),
                  pl.BlockSpec((1, tm, k), lambda i: (0, i, 0)),
                  pl.BlockSpec((1, tm, k), lambda i: (1, i, 0)),
                  pl.BlockSpec((k, d), fixed),
                  pl.BlockSpec((tm, d), row), pl.BlockSpec((1, d), fixed), pl.BlockSpec((1, d), fixed)],
        out_specs=[pl.BlockSpec((tm, d), row), pl.BlockSpec((tm, d), row)],
        out_shape=[jax.ShapeDtypeStruct((m, d), F32), jax.ShapeDtypeStruct((m, d), BF16)],
        compiler_params=_params("arbitrary"),
        name="odd_out_proj",
    )(u, yd, yd, w_out, resid, g.reshape(1, d), b.reshape(1, d))


def _combine_ln_kernel(h_ref, y0_ref, y1_ref, gate_ref, g_ref, b_ref, o_ref, ob_ref):
    gate = gate_ref[...]
    ffn = gate[:, 0:1] * y0_ref[...].astype(F32) + gate[:, 1:2] * y1_ref[...].astype(F32)
    y = _ln_rows(DN_ALPHA * h_ref[...] + ffn, g_ref[...], b_ref[...])
    o_ref[...] = y
    ob_ref[...] = y.astype(BF16)


def combine_ln(h, y0, y1, gate, g, b, tm=512):
    m, d = h.shape
    row = lambda i: (i, 0)
    fixed = lambda i: (0, 0)
    return pl.pallas_call(
        _combine_ln_kernel,
        grid=(m // tm,),
        in_specs=[pl.BlockSpec((tm, d), row), pl.BlockSpec((tm, d), row), pl.BlockSpec((tm, d), row),
                  pl.BlockSpec((tm, TOP_K), row), pl.BlockSpec((1, d), fixed), pl.BlockSpec((1, d), fixed)],
        out_specs=[pl.BlockSpec((tm, d), row), pl.BlockSpec((tm, d), row)],
        out_shape=[jax.ShapeDtypeStruct((m, d), F32), jax.ShapeDtypeStruct((m, d), BF16)],
        compiler_params=_params("arbitrary"),
        name="combine_ln",
    )(h, y0, y1, gate, g.reshape(1, d), b.reshape(1, d))


ONES_ROWS = 16
POS_SPLIT = 16


def _attn_kernel(slopes_ref, lam_ref, qt_ref, k_ref, vt_ref, fq_ref, fk_ref, g_ref, o_ref, m_s, acc_s, sa_s, sb_s,
                 *, tq, tk, seq, out_scale):
    head = pl.program_id(1)
    i0 = pl.program_id(2) * tq
    slope = slopes_ref[head]
    lam = lam_ref[0]
    qt = qt_ref[...]
    row = lax.broadcasted_iota(jnp.int32, qt.shape, 0)
    zero = jnp.zeros_like(qt)
    q2 = jnp.concatenate([jnp.where(row < DA_HEAD_DIM, qt, zero),
                          jnp.where(row >= DA_HEAD_DIM, qt, zero)], axis=1)
    fq = fq_ref[0]
    fq2 = jnp.concatenate([fq, fq], axis=1)
    q_aug = jnp.concatenate([q2, fq2], axis=0)
    fk = fk_ref[0]
    fk_neg = -fk
    ones = jnp.ones((ONES_ROWS, tk), BF16)
    m_s[...] = jnp.full(m_s.shape, -jnp.inf, F32)
    acc_s[...] = jnp.zeros(acc_s.shape, F32)
    j_diag = i0 // tk

    def update(j, s, shift):
        off = pl.multiple_of(j * tk, tk)
        va = jnp.concatenate([vt_ref[:, pl.ds(off, tk)], ones], axis=0)
        m_prev = m_s[...]
        m_new = jnp.maximum(m_prev, jnp.max(s, axis=0, keepdims=True) - shift)
        p = jnp.exp(s - (m_new + shift))
        alpha = jnp.exp(m_prev - m_new)
        acc_s[...] = alpha * acc_s[...] + jnp.dot(va, p.astype(BF16), preferred_element_type=F32)
        m_s[...] = m_new

    def chunk_of(t):
        return t + (t >= j_diag).astype(jnp.int32)

    def scores(t):
        j = chunk_of(t)
        off = pl.multiple_of(j * tk, tk)
        ka = jnp.concatenate([k_ref[0, pl.ds(off, tk), :], jnp.where(j < j_diag, fk, fk_neg)], axis=1)
        return jnp.dot(ka, q_aug, preferred_element_type=F32)

    def consume(s_ref, t):
        j = chunk_of(t)
        update(j, s_ref[...], slope * jnp.abs(i0 - j * tk).astype(F32))

    j0 = pl.multiple_of(j_diag * tk, tk)
    dist = jnp.abs(lax.broadcasted_iota(jnp.int32, (tk, tq), 0) + (j0 - i0)
                   - lax.broadcasted_iota(jnp.int32, (tk, tq), 1)).astype(F32)
    bias = slope * dist
    s_diag = jnp.dot(k_ref[0, pl.ds(j0, tk), :], q2, preferred_element_type=F32)
    sa_s[...] = s_diag - jnp.concatenate([bias, bias], axis=1)
    sb_s[...] = scores(jnp.int32(0))
    update(j_diag, sa_s[...], 0.0)
    n_off = seq // tk - 1

    def pair(t, carry):
        sa_s[...] = scores(2 * t + 1)
        consume(sb_s, 2 * t)
        sb_s[...] = scores(2 * t + 2)
        consume(sa_s, 2 * t + 1)
        return carry

    lax.fori_loop(0, n_off // 2, pair, 0)
    consume(sb_s, jnp.int32(n_off - 1))

    acc = acc_s[...]
    o1 = acc[:DA_V_DIM, :tq] / acc[DA_V_DIM:DA_V_DIM + 1, :tq]
    o2 = acc[:DA_V_DIM, tq:] / acc[DA_V_DIM:DA_V_DIM + 1, tq:]
    o = o1 - lam * o2
    o = o * lax.rsqrt(jnp.mean(o * o, axis=0, keepdims=True) + LN_EPS) * g_ref[...]
    o_ref[0] = (o * out_scale).T.astype(o_ref.dtype)


def diff_attention(qt, k, vt, lam, subln_g, lambda_init, tq=256, tk=512):
    bsz, seq, width = k.shape
    nq = seq // tq
    slopes = 2.0 ** (-(8.0 / DA_HEADS) * jnp.arange(1, DA_HEADS + 1, dtype=F32))

    def split(n):
        pos = jnp.arange(n, dtype=jnp.int32)
        return (pos // POS_SPLIT * POS_SPLIT).astype(F32), (pos % POS_SPLIT).astype(F32)

    qhi, qlo = split(tq)
    khi, klo = split(tk)
    sl = slopes[:, None]
    fq = jnp.zeros((DA_HEADS, DA_V_DIM, tq), F32)
    fq = fq.at[:, 0].set(1.0).at[:, 1].set(1.0).at[:, 2].set(-sl * qhi).at[:, 3].set(-sl * qlo)
    fk = jnp.zeros((DA_HEADS, tk, DA_V_DIM), F32)
    fk = fk.at[:, :, 0].set(sl * khi).at[:, :, 1].set(sl * klo).at[:, :, 2].set(1.0).at[:, :, 3].set(1.0)
    kern = functools.partial(_attn_kernel, tq=tq, tk=tk, seq=seq, out_scale=1.0 - lambda_init)
    smem = pl.BlockSpec(memory_space=pltpu.SMEM)
    return pl.pallas_call(
        kern,
        grid=(bsz, DA_HEADS, seq // tq),
        in_specs=[smem, smem,
                  pl.BlockSpec((DA_V_DIM, tq), lambda b, h, i: (h, b * nq + i)),
                  pl.BlockSpec((1, seq, DA_V_DIM), lambda b, h, i: (b, 0, h)),
                  pl.BlockSpec((DA_V_DIM, seq), lambda b, h, i: (h, b)),
                  pl.BlockSpec((1, DA_V_DIM, tq), lambda b, h, i: (h, 0, 0)),
                  pl.BlockSpec((1, tk, DA_V_DIM), lambda b, h, i: (h, 0, 0)),
                  pl.BlockSpec((DA_V_DIM, 1), lambda b, h, i: (0, 0))],
        out_specs=pl.BlockSpec((1, tq, DA_V_DIM), lambda b, h, i: (b, i, h)),
        out_shape=jax.ShapeDtypeStruct((bsz, seq, width), BF16),
        scratch_shapes=[pltpu.VMEM((1, 2 * tq), F32), pltpu.VMEM((DA_V_DIM + ONES_ROWS, 2 * tq), F32),
                        pltpu.VMEM((tk, 2 * tq), F32), pltpu.VMEM((tk, 2 * tq), F32)],
        compiler_params=_params("arbitrary", "arbitrary", "arbitrary"),
        name="diff_attention",
    )(slopes, lam.reshape(1).astype(F32), qt, k, vt, fq.astype(BF16), fk.astype(BF16),
      subln_g.reshape(DA_V_DIM, 1).astype(F32))


HALO = 8


def _sigmoid(x):
    return 0.5 * jnp.tanh(0.5 * x) + 0.5


def _rglru_kernel(xr_ref, prev_ref, next_ref, cw_ref, cb_ref, wa_ref, wx_ref, ba_ref, bx_ref, sp_ref, y_ref,
                  xf_s, a_s, b_s, h_s, *, ts, nt):
    direction = pl.program_id(0)
    t = pl.program_id(2)
    chunk = t + direction * (nt - 1 - 2 * t)

    @pl.when(t == 0)
    def _():
        h_s[...] = jnp.zeros(h_s.shape, F32)

    xf_s[0:HALO, :] = jnp.where(chunk == 0, 0.0, prev_ref[0])
    xf_s[HALO:HALO + ts, :] = xr_ref[0]
    xf_s[HALO + ts:, :] = jnp.where(chunk == nt - 1, 0.0, next_ref[0])
    x = cb_ref[...] + sum(cw_ref[j:j + 1, :] * xf_s[pl.ds(HALO - 2 + j, ts), :] for j in range(RG_CONV))
    xb = x.astype(BF16)
    for n in range(RG_BLOCKS):
        cols = slice(n * RG_BLOCK_W, (n + 1) * RG_BLOCK_W)
        xs = xb[:, cols]
        r = _sigmoid(jnp.dot(xs, wa_ref[0, n], preferred_element_type=F32) + ba_ref[0, :, cols])
        i = _sigmoid(jnp.dot(xs, wx_ref[0, n], preferred_element_type=F32) + bx_ref[0, :, cols])
        log_a = -RG_C * r * sp_ref[0, :, cols]
        a = jnp.exp(log_a)
        a_s[:, cols] = a
        b_s[:, cols] = jnp.sqrt(1.0 - a * a) * (i * x[:, cols])

    def step(k, h):
        row = jnp.where(direction == 0, k, ts - 1 - k)
        h = a_s[pl.ds(row, 1), :] * h + b_s[pl.ds(row, 1), :]
        y_ref[0, 0, pl.ds(row, 1), :] = h
        return h

    h_s[...] = lax.fori_loop(0, ts, step, h_s[...], unroll=8)


def rglru_bidir(u, conv_w, conv_b, wa, ba, wx, bx, lam, ts=256):
    bsz, seq, w2 = u.shape
    w = w2 // 2
    nt = seq // ts
    per = ts // HALO
    sp = jax.nn.softplus(-lam.astype(F32)).reshape(2, 1, w)
    chunk = lambda d, t: t + d * (nt - 1 - 2 * t)
    dmap4 = lambda d, b, t: (d, 0, 0, 0)
    dmap3 = lambda d, b, t: (d, 0, 0)
    fixed = lambda d, b, t: (0, 0)
    return pl.pallas_call(
        functools.partial(_rglru_kernel, ts=ts, nt=nt),
        grid=(2, bsz, nt),
        in_specs=[pl.BlockSpec((1, ts, w), lambda d, b, t: (b, chunk(d, t), 1)),
                  pl.BlockSpec((1, HALO, w), lambda d, b, t: (b, jnp.maximum(chunk(d, t) * per - 1, 0), 1)),
                  pl.BlockSpec((1, HALO, w),
                               lambda d, b, t: (b, jnp.minimum((chunk(d, t) + 1) * per, seq // HALO - 1), 1)),
                  pl.BlockSpec((RG_CONV, w), fixed), pl.BlockSpec((1, w), fixed),
                  pl.BlockSpec((1, RG_BLOCKS, RG_BLOCK_W, RG_BLOCK_W), dmap4),
                  pl.BlockSpec((1, RG_BLOCKS, RG_BLOCK_W, RG_BLOCK_W), dmap4),
                  pl.BlockSpec((1, 1, w), dmap3), pl.BlockSpec((1, 1, w), dmap3), pl.BlockSpec((1, 1, w), dmap3)],
        out_specs=pl.BlockSpec((1, 1, ts, w), lambda d, b, t: (d, b, chunk(d, t), 0)),
        out_shape=jax.ShapeDtypeStruct((2, bsz, seq, w), F32),
        scratch_shapes=[pltpu.VMEM((ts + 2 * HALO, w), F32), pltpu.VMEM((ts, w), F32), pltpu.VMEM((ts, w), F32),
                        pltpu.VMEM((1, w), F32)],
        compiler_params=_params("arbitrary", "arbitrary", "arbitrary"),
        name="rglru_bidir",
    )(u, u, u, conv_w.astype(F32), conv_b.astype(F32).reshape(1, w), wa.astype(BF16), wx.astype(BF16),
      ba.astype(F32).reshape(2, 1, w), bx.astype(F32).reshape(2, 1, w), sp)


def _router_kernel(h_ref, wt_ref, b_ref, e_ref, g_ref):
    logits = lax.dot_general(wt_ref[...], h_ref[...], (((1,), (1,)), ((), ())),
                             precision=lax.Precision.HIGHEST, preferred_element_type=F32)
    tm = logits.shape[1]
    s = jax.nn.sigmoid(logits).reshape(N_GROUPS, EXPERTS_PER_GROUP, tm)
    sel = s + b_ref[...]
    idx = lax.broadcasted_iota(jnp.int32, sel.shape, 1)
    big = jnp.int32(EXPERTS_PER_GROUP)
    v1 = jnp.max(sel, axis=1, keepdims=True)
    i1 = jnp.min(jnp.where(sel == v1, idx, big), axis=1, keepdims=True)
    rest = jnp.where(idx == i1, -jnp.inf, sel)
    v2 = jnp.max(rest, axis=1, keepdims=True)
    i2 = jnp.min(jnp.where(rest == v2, idx, big), axis=1, keepdims=True)
    score = v1 + v2
    gidx = lax.broadcasted_iota(jnp.int32, score.shape, 0)
    best = jnp.max(score, axis=0, keepdims=True)
    grp = jnp.min(jnp.where(score == best, gidx, jnp.int32(N_GROUPS)), axis=0, keepdims=True)
    pick = gidx == grp
    l1 = jnp.sum(jnp.where(pick, i1, 0), axis=0)
    l2 = jnp.sum(jnp.where(pick, i2, 0), axis=0)
    s_g = jnp.sum(jnp.where(pick, s, 0.0), axis=0)
    eidx = lax.broadcasted_iota(jnp.int32, s_g.shape, 0)
    w1 = jnp.sum(jnp.where(eidx == l1, s_g, 0.0), axis=0, keepdims=True)
    w2 = jnp.sum(jnp.where(eidx == l2, s_g, 0.0), axis=0, keepdims=True)
    tot = w1 + w2
    base = grp[0] * EXPERTS_PER_GROUP
    e_ref[...] = jnp.concatenate([base + l1, base + l2], axis=0)
    g_ref[...] = jnp.concatenate([w1 / tot, w2 / tot], axis=0)


def route(h, router_w, router_b, tm=1024):
    n, d = h.shape
    return pl.pallas_call(
        _router_kernel,
        grid=(n // tm,),
        in_specs=[pl.BlockSpec((tm, d), lambda i: (i, 0)),
                  pl.BlockSpec((N_EXPERTS, d), lambda i: (0, 0)),
                  pl.BlockSpec((N_GROUPS, EXPERTS_PER_GROUP, 1), lambda i: (0, 0, 0))],
        out_specs=[pl.BlockSpec((TOP_K, tm), lambda i: (0, i)), pl.BlockSpec((TOP_K, tm), lambda i: (0, i))],
        out_shape=[jax.ShapeDtypeStruct((TOP_K, n), jnp.int32), jax.ShapeDtypeStruct((TOP_K, n), F32)],
        compiler_params=_params("arbitrary"),
        name="router",
    )(h, router_w.astype(F32).T, router_b.astype(F32).reshape(N_GROUPS, EXPERTS_PER_GROUP, 1))


def _experts_kernel(blk_exp_ref, n_used_ref, x_ref, wg_ref, wu_ref, wd_ref, o_ref, wg_s, wu_s, wd_s):
    i = pl.program_id(0)
    expert = blk_exp_ref[i]
    prev = blk_exp_ref[jnp.maximum(i - 1, 0)]

    @pl.when((i == 0) | (expert != prev))
    def _():
        wg_s[...] = wg_ref[0, 0].astype(BF16)
        wu_s[...] = wu_ref[0, 0].astype(BF16)
        wd_s[...] = wd_ref[0, 0].astype(BF16)

    @pl.when(i < n_used_ref[0])
    def _():
        x = x_ref[...]
        hg = jnp.dot(x, wg_s[...], preferred_element_type=F32)
        hu = jnp.dot(x, wu_s[...], preferred_element_type=F32)
        hidden = (jax.nn.silu(hg) * hu).astype(BF16)
        o_ref[...] = jnp.dot(hidden, wd_s[...], preferred_element_type=F32).astype(o_ref.dtype)

    @pl.when(i >= n_used_ref[0])
    def _():
        o_ref[...] = jnp.zeros(o_ref.shape, o_ref.dtype)


def experts(xs, blk_exp, n_used, w_gate, w_up, w_down, layer):
    p, d = xs.shape
    f = w_gate.shape[3]
    grid_spec = pltpu.PrefetchScalarGridSpec(
        num_scalar_prefetch=2,
        grid=(p // MOE_ROWS,),
        in_specs=[pl.BlockSpec((MOE_ROWS, d), lambda i, be, nu: (i, 0)),
                  pl.BlockSpec((1, 1, d, f), lambda i, be, nu: (layer, be[i], 0, 0)),
                  pl.BlockSpec((1, 1, d, f), lambda i, be, nu: (layer, be[i], 0, 0)),
                  pl.BlockSpec((1, 1, f, d), lambda i, be, nu: (layer, be[i], 0, 0))],
        out_specs=pl.BlockSpec((MOE_ROWS, d), lambda i, be, nu: (i, 0)),
        scratch_shapes=[pltpu.VMEM((d, f), BF16), pltpu.VMEM((d, f), BF16), pltpu.VMEM((f, d), BF16)],
    )
    return pl.pallas_call(
        _experts_kernel,
        grid_spec=grid_spec,
        out_shape=jax.ShapeDtypeStruct((p, d), BF16),
        compiler_params=_params("arbitrary"),
        name="experts",
    )(blk_exp, n_used, xs, w_gate, w_up, w_down)


def moe_ffn(h, hb, router_w, router_b, w_gate, w_up, w_down, layer, ln_g, ln_b):
    n, d = h.shape
    a = n * TOP_K
    e_idx, gate = route(h, router_w, router_b)
    e_flat = e_idx.T.reshape(a)
    onehot = (e_flat[:, None] == jnp.arange(N_EXPERTS, dtype=jnp.int32)[None, :]).astype(jnp.int32)
    csum = jnp.cumsum(onehot, axis=0)
    rank = jnp.sum(csum * onehot, axis=1) - 1
    counts = csum[-1]
    padded = (counts + MOE_ROWS - 1) // MOE_ROWS * MOE_ROWS
    pad_end = jnp.cumsum(padded)
    pad_start = pad_end - padded
    dest = pad_start[e_flat] + rank
    n_blocks = a // MOE_ROWS + N_EXPERTS
    p = n_blocks * MOE_ROWS
    row_tok = jnp.zeros((p,), jnp.int32).at[dest].set(jnp.arange(a, dtype=jnp.int32) // TOP_K)
    blk_exp = jnp.minimum(jnp.searchsorted(pad_end, jnp.arange(n_blocks, dtype=jnp.int32) * MOE_ROWS, side='right'),
                          N_EXPERTS - 1).astype(jnp.int32)
    n_used = (pad_end[-1:] // MOE_ROWS).astype(jnp.int32)
    xs = jnp.take(hb, row_tok, axis=0)
    yb = experts(xs, blk_exp, n_used, w_gate, w_up, w_down, layer)
    dest2 = dest.reshape(n, TOP_K)
    y0 = jnp.take(yb, dest2[:, 0], axis=0)
    y1 = jnp.take(yb, dest2[:, 1], axis=0)
    return combine_ln(h, y0, y1, gate.T, ln_g, ln_b)


FFT_R = 128
HALF_R = FFT_R // 2
HY_GROUP = 4


def _matmul_nt_kernel(a_ref, b_ref, o_ref):
    o_ref[...] = lax.dot_general(a_ref[...], b_ref[...], (((1,), (1,)), ((), ())),
                                 preferred_element_type=F32).astype(o_ref.dtype)


def matmul_nt(a, b, out_dtype, tm=512, tn=1024):
    m, k = a.shape
    n = b.shape[0]
    tm, tn = min(tm, m), min(tn, n)
    return pl.pallas_call(
        _matmul_nt_kernel,
        grid=(m // tm, n // tn),
        in_specs=[pl.BlockSpec((tm, k), lambda i, j: (i, 0)),
                  pl.BlockSpec((tn, k), lambda i, j: (j, 0))],
        out_specs=pl.BlockSpec((tm, tn), lambda i, j: (i, j)),
        out_shape=jax.ShapeDtypeStruct((m, n), out_dtype),
        compiler_params=_params("arbitrary", "arbitrary"),
        name="matmul_nt",
    )(a, b)


def _dft_constants():
    idx = jnp.arange(FFT_R, dtype=jnp.int32)
    prod = idx[:, None] * idx[None, :]
    ang = (prod % FFT_R).astype(F32) * (2.0 * math.pi / FFT_R)
    f_r, f_i = jnp.cos(ang), -jnp.sin(ang)
    ang_t = prod.astype(F32) * (2.0 * math.pi / (FFT_R * FFT_R))
    g_r, g_i = f_r[:HALF_R], -f_i[:HALF_R]
    return dict(
        la=jnp.block([[f_r[:, :HALF_R], -f_i[:, :HALF_R]], [f_i[:, :HALF_R], f_r[:, :HALF_R]]]).astype(BF16),
        la_real=jnp.concatenate([f_r, f_i], axis=0).astype(BF16),
        rb=jnp.block([[f_r, f_i], [-f_i, f_r]]).astype(BF16),
        rc=jnp.block([[f_r, -f_i], [f_i, f_r]]).astype(BF16),
        ld=(jnp.block([[g_r, -g_i], [g_i, g_r]]) / (FFT_R * FFT_R)).astype(BF16),
        t_r=jnp.cos(ang_t), t_i=-jnp.sin(ang_t))


def _rows_to_lanes(x):
    return jnp.concatenate([x[:FFT_R], x[FFT_R:]], axis=1)


def _fwd_lane_stage(a, t_r, t_i, rb):
    a_r, a_i = a[:FFT_R], a[FFT_R:]
    t2_r = jnp.concatenate([t_r, t_r], axis=1)
    t2_i = jnp.concatenate([t_i, t_i], axis=1)
    b_r = a_r * t2_r - a_i * t2_i
    b_i = a_r * t2_i + a_i * t2_r
    lhs = jnp.concatenate([jnp.concatenate([b_r[:, :FFT_R], b_i[:, :FFT_R]], axis=1),
                           jnp.concatenate([b_r[:, FFT_R:], b_i[:, FFT_R:]], axis=1)], axis=0)
    return jnp.dot(lhs.astype(BF16), rb, preferred_element_type=F32)


def _conv_pairs(ms, kfs, la, rb, rc, ld, t_r, t_i):
    a = [jnp.dot(la, jnp.concatenate([m_r, m_i], axis=0).astype(BF16), preferred_element_type=F32)
         for m_r, m_i in ms]
    z = [_fwd_lane_stage(ai, t_r, t_i, rb) for ai in a]
    c = []
    for zi, kf in zip(z, kfs):
        z_r, z_i, k_r, k_i = zi[:, :FFT_R], zi[:, FFT_R:], kf[:, :FFT_R], kf[:, FFT_R:]
        y = jnp.concatenate([z_r * k_r - z_i * k_i, z_r * k_i + z_i * k_r], axis=1)
        c.append(jnp.dot(y.astype(BF16), rc, preferred_element_type=F32))
    t2_r = jnp.concatenate([t_r, t_r], axis=0)
    t2_i = jnp.concatenate([t_i, t_i], axis=0)
    out = []
    for ci in c:
        c_r, c_i = ci[:, :FFT_R], ci[:, FFT_R:]
        d_r = c_r * t2_r + c_i * t2_i
        d_i = c_i * t2_r - c_r * t2_i
        rhs = jnp.concatenate([_rows_to_lanes(d_r), _rows_to_lanes(d_i)], axis=0)
        x = jnp.dot(ld, rhs.astype(BF16), preferred_element_type=F32)
        out.append((x[:HALF_R], x[HALF_R:]))
    return out


def _short_conv_tile(x, w0, w1, w2, b):
    rows = x.shape[0]
    lane = lax.broadcasted_iota(jnp.int32, x.shape, 1)
    row = lax.broadcasted_iota(jnp.int32, x.shape, 0)
    r = pltpu.roll(x, 1, 1)
    prev = jnp.where(lane == 0, jnp.where(row == 0, 0.0, pltpu.roll(r, 1, 0)), r)
    r = pltpu.roll(x, FFT_R - 1, 1)
    nxt = jnp.where(lane == FFT_R - 1, jnp.where(row == rows - 1, 0.0, pltpu.roll(r, rows - 1, 0)), r)
    return w0 * prev + w1 * x + w2 * nxt + b


def _hyena_kernel(cw_ref, cb_ref, skip_ref, v_ref, x1_ref, x2_ref, kf_ref,
                  la_ref, rb_ref, rc_ref, ld_ref, tr_ref, ti_ref, o_ref, *, tc):
    c_base = pl.program_id(0) * tc
    lane = lax.broadcasted_iota(jnp.int32, (1, 2 * FFT_R), 1)

    def conv_in(ref, part, b, c0):
        tiles = []
        for cc in range(2):
            ch = part * HY_CH + c_base + c0 + cc
            tiles.append(_short_conv_tile(ref[c0 + cc, b], cw_ref[0, ch], cw_ref[1, ch], cw_ref[2, ch], cb_ref[ch]))
        return jnp.concatenate(tiles, axis=1)

    def group(g, carry):
        starts = [2 * (HY_GROUP * g + j) for j in range(HY_GROUP)]
        consts = (la_ref[...], rb_ref[...], rc_ref[...], ld_ref[...], tr_ref[...], ti_ref[...])
        zs = [(conv_in(v_ref, 0, 0, c0), conv_in(v_ref, 0, 1, c0)) for c0 in starts]
        for o, g_ref in enumerate((x1_ref, x2_ref)):
            kfs = [kf_ref[o, pl.ds(c0, 2)].reshape(2 * FFT_R, 2 * FFT_R) for c0 in starts]
            ys = _conv_pairs(zs, kfs, *consts)
            nxt = []
            for c0, (z_r, z_i), (y_r, y_i) in zip(starts, zs, ys):
                sk = jnp.where(lane < FFT_R, skip_ref[o, c_base + c0], skip_ref[o, c_base + c0 + 1])
                nxt.append((conv_in(g_ref, o + 1, 0, c0) * (y_r + sk * z_r),
                            conv_in(g_ref, o + 1, 1, c0) * (y_i + sk * z_i)))
            zs = nxt
        for c0, (z_r, z_i) in zip(starts, zs):
            o_ref[c0, 0] = z_r[:, :FFT_R].astype(o_ref.dtype)
            o_ref[c0 + 1, 0] = z_r[:, FFT_R:].astype(o_ref.dtype)
            o_ref[c0, 1] = z_i[:, :FFT_R].astype(o_ref.dtype)
            o_ref[c0 + 1, 1] = z_i[:, FFT_R:].astype(o_ref.dtype)
        return carry

    lax.fori_loop(0, tc // (2 * HY_GROUP), group, 0)


def hyena_mix(u_t, kf, conv_w, conv_b, skip, consts, tc=16):
    rows, n = u_t.shape
    ch = rows // 3
    u5 = u_t.reshape(rows, 2, HALF_R, FFT_R)
    nct = ch // tc
    smem = pl.BlockSpec(memory_space=pltpu.SMEM)
    part = lambda k: pl.BlockSpec((tc, 2, HALF_R, FFT_R), lambda i: (k * nct + i, 0, 0, 0))
    full2 = lambda a: pl.BlockSpec(a.shape, lambda i: (0, 0))
    mats = [consts[k] for k in ("la", "rb", "rc", "ld", "t_r", "t_i")]
    out = pl.pallas_call(
        functools.partial(_hyena_kernel, tc=tc),
        grid=(nct,),
        in_specs=[smem, smem, smem, part(0), part(1), part(2),
                  pl.BlockSpec((2, tc, FFT_R, 2 * FFT_R), lambda i: (0, i, 0, 0))] + [full2(a) for a in mats],
        out_specs=pl.BlockSpec((tc, 2, HALF_R, FFT_R), lambda i: (i, 0, 0, 0)),
        out_shape=jax.ShapeDtypeStruct((ch, 2, HALF_R, FFT_R), BF16),
        compiler_params=_params("arbitrary"),
        name="hyena_mix",
    )(conv_w.astype(F32), conv_b.astype(F32), skip.astype(F32), u5, u5, u5, kf, *mats)
    return out.reshape(ch, n)


def _hyena_filter_kernel(delta_ref, hf_ref, hb_ref, tpos_ref, la_ref, rb_ref, tr_ref, ti_ref, kf_ref, *, tc):
    c_base = pl.program_id(1) * tc

    def taps_of(c0):
        taps = []
        for cc in range(2):
            k = jnp.concatenate([hf_ref[0, 0, c0 + cc], hb_ref[0, 0, c0 + cc]], axis=0)
            k = k * jnp.exp(-tpos_ref[...] * delta_ref[c_base + c0 + cc])
            taps.append(k * lax.rsqrt(jnp.sum(k * k, keepdims=True)))
        return jnp.concatenate(taps, axis=1).astype(BF16)

    def group(g, carry):
        starts = [2 * (HY_GROUP * g + j) for j in range(HY_GROUP)]
        a = [jnp.dot(la_ref[...], taps_of(c0), preferred_element_type=F32) for c0 in starts]
        z = [_fwd_lane_stage(ai, tr_ref[...], ti_ref[...], rb_ref[...]) for ai in a]
        for c0, zi in zip(starts, z):
            kf_ref[0, pl.ds(c0, 2)] = zi.reshape(2, FFT_R, 2 * FFT_R)
        return carry

    lax.fori_loop(0, tc // (2 * HY_GROUP), group, 0)


def hyena_filter_spectra(seq, w1, b1, w2, b2, w3, b3, freq, w4, consts, tc=16):
    t = jnp.linspace(0.0, 1.0, seq, dtype=F32)[:, None]
    omega = 2.0 * math.pi * jnp.arange(seq, dtype=F32)[:, None] / seq
    bands = jnp.linspace(1e-4, HF_BANDS - 1, HF_BANDS, dtype=F32)[None, :]
    ang = omega * bands
    z = jnp.concatenate([t, jnp.cos(ang), -jnp.sin(ang)], -1)
    fr = freq.astype(F32)
    hid = jnp.sin(fr * (z @ w1.astype(F32) + b1.astype(F32)))
    hid = jnp.sin(fr * (hid @ w2.astype(F32) + b2.astype(F32)))
    hid = jnp.sin(fr * (hid @ w3.astype(F32) + b3.astype(F32)))
    back = lambda a: jnp.concatenate([jnp.zeros_like(a[:1]), a[:0:-1]], axis=0)
    w4t = w4.astype(BF16).T
    half = HY_ORDER * HY_CH
    h_fwd = matmul(w4t[:half], hid.T.astype(BF16), F32)
    h_bwd = matmul(w4t[half:], back(hid).T.astype(BF16), F32)
    shape5 = (HY_ORDER, HY_CH, HALF_R, FFT_R)
    tpos = jnp.concatenate([t, back(t)], axis=0).reshape(FFT_R, FFT_R)
    max_decay = math.log(HF_TARGET) / HF_FAST
    min_decay = math.log(HF_TARGET) / HF_SLOW
    deltas = jnp.abs(jnp.linspace(min_decay, max_decay, HY_CH, dtype=F32))
    taps = pl.BlockSpec((1, 1, tc, HALF_R, FFT_R), lambda o, i: (0, o, i, 0, 0))
    full2 = lambda a: pl.BlockSpec(a.shape, lambda o, i: (0, 0))
    mats = [consts[k] for k in ("la_real", "rb", "t_r", "t_i")]
    return pl.pallas_call(
        functools.partial(_hyena_filter_kernel, tc=tc),
        grid=(HY_ORDER, HY_CH // tc),
        in_specs=[pl.BlockSpec(memory_space=pltpu.SMEM), taps, taps, full2(tpos)] + [full2(a) for a in mats],
        out_specs=pl.BlockSpec((1, tc, FFT_R, 2 * FFT_R), lambda o, i: (o, i, 0, 0)),
        out_shape=jax.ShapeDtypeStruct((HY_ORDER, HY_CH, FFT_R, 2 * FFT_R), F32),
        compiler_params=_params("arbitrary", "arbitrary"),
        name="hyena_filter_spectra",
    )(deltas, h_fwd.reshape((1,) + shape5), h_bwd.reshape((1,) + shape5), tpos, *mats)


def kernel(x, ev_w_in, ev_hy_conv_w, ev_hy_conv_b, ev_hf_w1, ev_hf_b1, ev_hf_w2, ev_hf_b2, ev_hf_w3, ev_hf_b3,
           ev_hf_freq, ev_hf_w4, ev_hy_skip, ev_lam_q1, ev_lam_k1, ev_lam_q2, ev_lam_k2, ev_subln_g, ev_w_out,
           od_w_in, od_conv_w, od_conv_b, od_wa, od_ba, od_wx, od_bx, od_lam, od_w_out, ln1_g, ln1_b, ln2_g,
           ln2_b, router_w, router_b, ex_w_gate, ex_w_up, ex_w_down):
    bsz, seq, d = x.shape
    n = bsz * seq
    h = x.reshape(n, d).astype(F32)
    hb = h.astype(BF16)
    for layer in range(DEPTH):
        i = layer // 2
        if layer % 2 == 0:
            w_in = ev_w_in[i]
            c0 = 3 * HY_CH
            w_t = w_in.T
            u_t = matmul_nt(w_t[:c0].astype(BF16), hb, F32)
            qt = matmul_nt((w_t[c0:c0 + DA_QK] * (DA_HEAD_DIM ** -0.5)).astype(BF16), hb, BF16)
            k = matmul(hb, w_in[:, c0 + DA_QK:c0 + 2 * DA_QK].astype(BF16), BF16)
            vt = matmul_nt(w_t[c0 + 2 * DA_QK:].astype(BF16), hb, BF16)
            consts = _dft_constants()
            kf = hyena_filter_spectra(seq, ev_hf_w1[i], ev_hf_b1[i], ev_hf_w2[i], ev_hf_b2[i],
                                      ev_hf_w3[i], ev_hf_b3[i], ev_hf_freq[i], ev_hf_w4[i], consts)
            y_hy = hyena_mix(u_t, kf, ev_hy_conv_w[i], ev_hy_conv_b[i], ev_hy_skip[i], consts).T
            lambda_init = 0.8 - 0.6 * math.exp(-0.3 * layer)
            lam = (jnp.exp(jnp.sum(ev_lam_q1[i].astype(F32) * ev_lam_k1[i].astype(F32)))
                   - jnp.exp(jnp.sum(ev_lam_q2[i].astype(F32) * ev_lam_k2[i].astype(F32))) + lambda_init)
            y_da = diff_attention(qt, k.reshape(bsz, seq, DA_QK), vt,
                                  lam, ev_subln_g[i], lambda_init).reshape(n, DA_WIDTH)
            h, hb = even_out_proj(y_hy, y_da, ev_w_out[i].astype(BF16), h, ln1_g[layer], ln1_b[layer])
        else:
            u = matmul(hb, od_w_in[i].astype(BF16), F32)
            yd = rglru_bidir(u.reshape(bsz, seq, 2 * RG_WIDTH), od_conv_w[i], od_conv_b[i], od_wa[i], od_ba[i],
                             od_wx[i], od_bx[i], od_lam[i]).reshape(2, n, RG_WIDTH)
            h, hb = odd_out_proj(u, yd, od_w_out[i].astype(BF16), h, ln1_g[layer], ln1_b[layer])
        h, hb = moe_ffn(h, hb, router_w, router_b, ex_w_gate, ex_w_up, ex_w_down, layer,
                        ln2_g[layer], ln2_b[layer])
    return h.reshape(bsz, seq, d).astype(x.dtype)
```

```python
import functools
import math

import jax
import jax.numpy as jnp
from jax import lax
from jax.experimental import pallas as pl
from jax.experimental.pallas import tpu as pltpu

F32 = jnp.float32
BF16 = jnp.bfloat16

D_MODEL = 2048
DEPTH = 2
HY_CH = D_MODEL // 2
HY_ORDER = 2
HF_BANDS = 16
HF_TARGET = 1e-2
HF_FAST = 0.3
HF_SLOW = 1.5
DA_HEADS = 8
DA_HEAD_DIM = 64
DA_V_DIM = 2 * DA_HEAD_DIM
DA_QK = DA_HEADS * 2 * DA_HEAD_DIM
DA_WIDTH = DA_HEADS * DA_V_DIM
RG_WIDTH = D_MODEL
RG_BLOCKS = 8
RG_BLOCK_W = RG_WIDTH // RG_BLOCKS
RG_C = 8.0
RG_CONV = 4
N_EXPERTS = 32
N_GROUPS = 4
EXPERTS_PER_GROUP = N_EXPERTS // N_GROUPS
TOP_K = 2
D_FF = 512
DN_ALPHA = (2 * DEPTH) ** 0.25
LN_EPS = 1e-5

VMEM_LIMIT_BYTES = 56 * 1024 * 1024
MOE_ROWS = 256


def _params(*sem):
    return pltpu.CompilerParams(dimension_semantics=sem, vmem_limit_bytes=VMEM_LIMIT_BYTES)


def _mm_kernel(x_ref, w_ref, o_ref):
    o_ref[...] = jnp.dot(x_ref[...], w_ref[...], preferred_element_type=F32).astype(o_ref.dtype)


def matmul(x, w, out_dtype, tm=512, tn=1024):
    m, k = x.shape
    n = w.shape[1]
    tm, tn = min(tm, m), min(tn, n)
    return pl.pallas_call(
        _mm_kernel,
        grid=(n // tn, m // tm),
        in_specs=[pl.BlockSpec((tm, k), lambda j, i: (i, 0)),
                  pl.BlockSpec((k, tn), lambda j, i: (0, j))],
        out_specs=pl.BlockSpec((tm, tn), lambda j, i: (i, j)),
        out_shape=jax.ShapeDtypeStruct((m, n), out_dtype),
        compiler_params=_params("arbitrary", "arbitrary"),
        name="matmul",
    )(x, w)


def _ln_rows(z, g, b):
    mu = jnp.mean(z, axis=-1, keepdims=True)
    zc = z - mu
    var = jnp.mean(zc * zc, axis=-1, keepdims=True)
    return zc * lax.rsqrt(var + LN_EPS) * g + b


def _even_out_kernel(xa_ref, xb_ref, wa_ref, wb_ref, r_ref, g_ref, b_ref, o_ref, ob_ref):
    acc = jnp.dot(xa_ref[...], wa_ref[...], preferred_element_type=F32)
    acc += jnp.dot(xb_ref[...], wb_ref[...], preferred_element_type=F32)
    y = _ln_rows(DN_ALPHA * r_ref[...] + acc, g_ref[...], b_ref[...])
    o_ref[...] = y
    ob_ref[...] = y.astype(BF16)


def even_out_proj(y_hy, y_da, w_out, resid, g, b, tm=256):
    m, ka = y_hy.shape
    kb = y_da.shape[1]
    d = w_out.shape[1]
    row = lambda i: (i, 0)
    fixed = lambda i: (0, 0)
    return pl.pallas_call(
        _even_out_kernel,
        grid=(m // tm,),
        in_specs=[pl.BlockSpec((tm, ka), row), pl.BlockSpec((tm, kb), row),
                  pl.BlockSpec((ka, d), fixed), pl.BlockSpec((kb, d), fixed),
                  pl.BlockSpec((tm, d), row), pl.BlockSpec((1, d), fixed), pl.BlockSpec((1, d), fixed)],
        out_specs=[pl.BlockSpec((tm, d), row), pl.BlockSpec((tm, d), row)],
        out_shape=[jax.ShapeDtypeStruct((m, d), F32), jax.ShapeDtypeStruct((m, d), BF16)],
        compiler_params=_params("arbitrary"),
        name="even_out_proj",
    )(y_hy, y_da, w_out[:ka], w_out[ka:], resid, g.reshape(1, d), b.reshape(1, d))


def _odd_out_kernel(gate_ref, y0_ref, y1_ref, w_ref, r_ref, g_ref, b_ref, o_ref, ob_ref):
    x = jax.nn.gelu(gate_ref[...]) * (y0_ref[0] + y1_ref[0])
    acc = jnp.dot(x.astype(BF16), w_ref[...], preferred_element_type=F32)
    y = _ln_rows(DN_ALPHA * r_ref[...] + acc, g_ref[...], b_ref[...])
    o_ref[...] = y
    ob_ref[...] = y.astype(BF16)


def odd_out_proj(u, yd, w_out, resid, g, b, tm=256):
    m = u.shape[0]
    k, d = w_out.shape
    row = lambda i: (i, 0)
    fixed = lambda i: (0, 0)
    return pl.pallas_call(
        _odd_out_kernel,
        grid=(m // tm,),
        in_specs=[pl.BlockSpec((tm, k), row),
                  pl.BlockSpec((1, tm, k), lambda i: (0, i, 0)),
                  pl.BlockSpec((1, tm, k), lambda i: (1, i, 0)),
                  pl.BlockSpec((k, d), fixed),
                  pl.BlockSpec((tm, d), row), pl.BlockSpec((1, d), fixed), pl.BlockSpec((1, d), fixed)],
        out_specs=[pl.BlockSpec((tm, d), row), pl.BlockSpec((tm, d), row)],
        out_shape=[jax.ShapeDtypeStruct((m, d), F32), jax.ShapeDtypeStruct((m, d), BF16)],
        compiler_params=_params("arbitrary"),
        name="odd_out_proj",
    )(u, yd, yd, w_out, resid, g.reshape(1, d), b.reshape(1, d))


def _combine_ln_kernel(h_ref, y0_ref, y1_ref, gate_ref, g_ref, b_ref, o_ref, ob_ref):
    gate = gate_ref[...]
    ffn = gate[:, 0:1] * y0_ref[...].astype(F32) + gate[:, 1:2] * y1_ref[...].astype(F32)
    y = _ln_rows(DN_ALPHA * h_ref[...] + ffn, g_ref[...], b_ref[...])
    o_ref[...] = y
    ob_ref[...] = y.astype(BF16)


def combine_ln(h, y0, y1, gate, g, b, tm=512):
    m, d = h.shape
    row = lambda i: (i, 0)
    fixed = lambda i: (0, 0)
    return pl.pallas_call(
        _combine_ln_kernel,
        grid=(m // tm,),
        in_specs=[pl.BlockSpec((tm, d), row), pl.BlockSpec((tm, d), row), pl.BlockSpec((tm, d), row),
                  pl.BlockSpec((tm, TOP_K), row), pl.BlockSpec((1, d), fixed), pl.BlockSpec((1, d), fixed)],
        out_specs=[pl.BlockSpec((tm, d), row), pl.BlockSpec((tm, d), row)],
        out_shape=[jax.ShapeDtypeStruct((m, d), F32), jax.ShapeDtypeStruct((m, d), BF16)],
        compiler_params=_params("arbitrary"),
        name="combine_ln",
    )(h, y0, y1, gate, g.reshape(1, d), b.reshape(1, d))


ONES_ROWS = 16
POS_SPLIT = 16


def _attn_kernel(slopes_ref, lam_ref, qt_ref, k_ref, vt_ref, fq_ref, fk_ref, g_ref, o_ref, m_s, acc_s, sa_s, sb_s,
                 *, tq, tk, seq, out_scale):
    head = pl.program_id(1)
    i0 = pl.program_id(2) * tq
    slope = slopes_ref[head]
    lam = lam_ref[0]
    qt = qt_ref[...]
    row = lax.broadcasted_iota(jnp.int32, qt.shape, 0)
    zero = jnp.zeros_like(qt)
    q2 = jnp.concatenate([jnp.where(row < DA_HEAD_DIM, qt, zero),
                          jnp.where(row >= DA_HEAD_DIM, qt, zero)], axis=1)
    fq = fq_ref[0]
    fq2 = jnp.concatenate([fq, fq], axis=1)
    q_aug = jnp.concatenate([q2, fq2], axis=0)
    fk = fk_ref[0]
    fk_neg = -fk
    ones = jnp.ones((ONES_ROWS, tk), BF16)
    m_s[...] = jnp.full(m_s.shape, -jnp.inf, F32)
    acc_s[...] = jnp.zeros(acc_s.shape, F32)
    j_diag = i0 // tk

    def update(j, s, shift):
        off = pl.multiple_of(j * tk, tk)
        va = jnp.concatenate([vt_ref[:, pl.ds(off, tk)], ones], axis=0)
        m_prev = m_s[...]
        m_new = jnp.maximum(m_prev, jnp.max(s, axis=0, keepdims=True) - shift)
        p = jnp.exp(s - (m_new + shift))
        alpha = jnp.exp(m_prev - m_new)
        acc_s[...] = alpha * acc_s[...] + jnp.dot(va, p.astype(BF16), preferred_element_type=F32)
        m_s[...] = m_new

    def chunk_of(t):
        return t + (t >= j_diag).astype(jnp.int32)

    def scores(t):
        j = chunk_of(t)
        off = pl.multiple_of(j * tk, tk)
        ka = jnp.concatenate([k_ref[0, pl.ds(off, tk), :], jnp.where(j < j_diag, fk, fk_neg)], axis=1)
        return jnp.dot(ka, q_aug, preferred_element_type=F32)

    def consume(s_ref, t):
        j = chunk_of(t)
        update(j, s_ref[...], slope * jnp.abs(i0 - j * tk).astype(F32))

    j0 = pl.multiple_of(j_diag * tk, tk)
    dist = jnp.abs(lax.broadcasted_iota(jnp.int32, (tk, tq), 0) + (j0 - i0)
                   - lax.broadcasted_iota(jnp.int32, (tk, tq), 1)).astype(F32)
    bias = slope * dist
    s_diag = jnp.dot(k_ref[0, pl.ds(j0, tk), :], q2, preferred_element_type=F32)
    sa_s[...] = s_diag - jnp.concatenate([bias, bias], axis=1)
    sb_s[...] = scores(jnp.int32(0))
    update(j_diag, sa_s[...], 0.0)
    n_off = seq // tk - 1

    def pair(t, carry):
        sa_s[...] = scores(2 * t + 1)
        consume(sb_s, 2 * t)
        sb_s[...] = scores(2 * t + 2)
        consume(sa_s, 2 * t + 1)
        return carry

    lax.fori_loop(0, n_off // 2, pair, 0)
    consume(sb_s, jnp.int32(n_off - 1))

    acc = acc_s[...]
    o1 = acc[:DA_V_DIM, :tq] / acc[DA_V_DIM:DA_V_DIM + 1, :tq]
    o2 = acc[:DA_V_DIM, tq:] / acc[DA_V_DIM:DA_V_DIM + 1, tq:]
    o = o1 - lam * o2
    o = o * lax.rsqrt(jnp.mean(o * o, axis=0, keepdims=True) + LN_EPS) * g_ref[...]
    o_ref[0] = (o * out_scale).T.astype(o_ref.dtype)


def diff_attention(qt, k, vt, lam, subln_g, lambda_init, tq=512, tk=512):
    bsz, seq, width = k.shape
    nq = seq // tq
    slopes = 2.0 ** (-(8.0 / DA_HEADS) * jnp.arange(1, DA_HEADS + 1, dtype=F32))

    def split(n):
        pos = jnp.arange(n, dtype=jnp.int32)
        return (pos // POS_SPLIT * POS_SPLIT).astype(F32), (pos % POS_SPLIT).astype(F32)

    qhi, qlo = split(tq)
    khi, klo = split(tk)
    sl = slopes[:, None]
    fq = jnp.zeros((DA_HEADS, DA_V_DIM, tq), F32)
    fq = fq.at[:, 0].set(1.0).at[:, 1].set(1.0).at[:, 2].set(-sl * qhi).at[:, 3].set(-sl * qlo)
    fk = jnp.zeros((DA_HEADS, tk, DA_V_DIM), F32)
    fk = fk.at[:, :, 0].set(sl * khi).at[:, :, 1].set(sl * klo).at[:, :, 2].set(1.0).at[:, :, 3].set(1.0)
    kern = functools.partial(_attn_kernel, tq=tq, tk=tk, seq=seq, out_scale=1.0 - lambda_init)
    smem = pl.BlockSpec(memory_space=pltpu.SMEM)
    return pl.pallas_call(
        kern,
        grid=(bsz, DA_HEADS, seq // tq),
        in_specs=[smem, smem,
                  pl.BlockSpec((DA_V_DIM, tq), lambda b, h, i: (h, b * nq + i)),
                  pl.BlockSpec((1, seq, DA_V_DIM), lambda b, h, i: (b, 0, h)),
                  pl.BlockSpec((DA_V_DIM, seq), lambda b, h, i: (h, b)),
                  pl.BlockSpec((1, DA_V_DIM, tq), lambda b, h, i: (h, 0, 0)),
                  pl.BlockSpec((1, tk, DA_V_DIM), lambda b, h, i: (h, 0, 0)),
                  pl.BlockSpec((DA_V_DIM, 1), lambda b, h, i: (0, 0))],
        out_specs=pl.BlockSpec((1, tq, DA_V_DIM), lambda b, h, i: (b, i, h)),
        out_shape=jax.ShapeDtypeStruct((bsz, seq, width), BF16),
        scratch_shapes=[pltpu.VMEM((1, 2 * tq), F32), pltpu.VMEM((DA_V_DIM + ONES_ROWS, 2 * tq), F32),
                        pltpu.VMEM((tk, 2 * tq), F32), pltpu.VMEM((tk, 2 * tq), F32)],
        compiler_params=_params("arbitrary", "arbitrary", "arbitrary"),
        name="diff_attention",
    )(slopes, lam.reshape(1).astype(F32), qt, k, vt, fq.astype(BF16), fk.astype(BF16),
      subln_g.reshape(DA_V_DIM, 1).astype(F32))


HALO = 8


def _sigmoid(x):
    return 0.5 * jnp.tanh(0.5 * x) + 0.5


def _rglru_kernel(xr_ref, prev_ref, next_ref, cw_ref, cb_ref, wa_ref, wx_ref, ba_ref, bx_ref, sp_ref, y_ref,
                  xf_s, a_s, b_s, h_s, *, ts, nt):
    direction = pl.program_id(0)
    t = pl.program_id(2)
    chunk = t + direction * (nt - 1 - 2 * t)

    @pl.when(t == 0)
    def _():
        h_s[...] = jnp.zeros(h_s.shape, F32)

    xf_s[0:HALO, :] = jnp.where(chunk == 0, 0.0, prev_ref[0])
    xf_s[HALO:HALO + ts, :] = xr_ref[0]
    xf_s[HALO + ts:, :] = jnp.where(chunk == nt - 1, 0.0, next_ref[0])
    x = cb_ref[...] + sum(cw_ref[j:j + 1, :] * xf_s[pl.ds(HALO - 2 + j, ts), :] for j in range(RG_CONV))
    xb = x.astype(BF16)
    for n in range(RG_BLOCKS):
        cols = slice(n * RG_BLOCK_W, (n + 1) * RG_BLOCK_W)
        xs = xb[:, cols]
        r = _sigmoid(jnp.dot(xs, wa_ref[0, n], preferred_element_type=F32) + ba_ref[0, :, cols])
        i = _sigmoid(jnp.dot(xs, wx_ref[0, n], preferred_element_type=F32) + bx_ref[0, :, cols])
        log_a = -RG_C * r * sp_ref[0, :, cols]
        a = jnp.exp(log_a)
        a_s[:, cols] = a
        b_s[:, cols] = jnp.sqrt(1.0 - a * a) * (i * x[:, cols])

    def step(k, h):
        row = jnp.where(direction == 0, k, ts - 1 - k)
        h = a_s[pl.ds(row, 1), :] * h + b_s[pl.ds(row, 1), :]
        y_ref[0, 0, pl.ds(row, 1), :] = h
        return h

    h_s[...] = lax.fori_loop(0, ts, step, h_s[...], unroll=8)


def rglru_bidir(u, conv_w, conv_b, wa, ba, wx, bx, lam, ts=256):
    bsz, seq, w2 = u.shape
    w = w2 // 2
    nt = seq // ts
    per = ts // HALO
    sp = jax.nn.softplus(-lam.astype(F32)).reshape(2, 1, w)
    chunk = lambda d, t: t + d * (nt - 1 - 2 * t)
    dmap4 = lambda d, b, t: (d, 0, 0, 0)
    dmap3 = lambda d, b, t: (d, 0, 0)
    fixed = lambda d, b, t: (0, 0)
    return pl.pallas_call(
        functools.partial(_rglru_kernel, ts=ts, nt=nt),
        grid=(2, bsz, nt),
        in_specs=[pl.BlockSpec((1, ts, w), lambda d, b, t: (b, chunk(d, t), 1)),
                  pl.BlockSpec((1, HALO, w), lambda d, b, t: (b, jnp.maximum(chunk(d, t) * per - 1, 0), 1)),
                  pl.BlockSpec((1, HALO, w),
                               lambda d, b, t: (b, jnp.minimum((chunk(d, t) + 1) * per, seq // HALO - 1), 1)),
                  pl.BlockSpec((RG_CONV, w), fixed), pl.BlockSpec((1, w), fixed),
                  pl.BlockSpec((1, RG_BLOCKS, RG_BLOCK_W, RG_BLOCK_W), dmap4),
                  pl.BlockSpec((1, RG_BLOCKS, RG_BLOCK_W, RG_BLOCK_W), dmap4),
                  pl.BlockSpec((1, 1, w), dmap3), pl.BlockSpec((1, 1, w), dmap3), pl.BlockSpec((1, 1, w), dmap3)],
        out_specs=pl.BlockSpec((1, 1, ts, w), lambda d, b, t: (d, b, chunk(d, t), 0)),
        out_shape=jax.ShapeDtypeStruct((2, bsz, seq, w), F32),
        scratch_shapes=[pltpu.VMEM((ts + 2 * HALO, w), F32), pltpu.VMEM((ts, w), F32), pltpu.VMEM((ts, w), F32),
                        pltpu.VMEM((1, w), F32)],
        compiler_params=_params("arbitrary", "arbitrary", "arbitrary"),
        name="rglru_bidir",
    )(u, u, u, conv_w.astype(F32), conv_b.astype(F32).reshape(1, w), wa.astype(BF16), wx.astype(BF16),
      ba.astype(F32).reshape(2, 1, w), bx.astype(F32).reshape(2, 1, w), sp)


def _router_kernel(h_ref, wt_ref, b_ref, tri_ref, e_ref, g_ref, r_ref, cnt_ref, seen_s):
    logits = lax.dot_general(wt_ref[...], h_ref[...], (((1,), (1,)), ((), ())),
                             precision=lax.Precision.HIGHEST, preferred_element_type=F32)
    tm = logits.shape[1]
    s = jax.nn.sigmoid(logits).reshape(N_GROUPS, EXPERTS_PER_GROUP, tm)
    sel = s + b_ref[...]
    idx = lax.broadcasted_iota(jnp.int32, sel.shape, 1)
    big = jnp.int32(EXPERTS_PER_GROUP)
    v1 = jnp.max(sel, axis=1, keepdims=True)
    i1 = jnp.min(jnp.where(sel == v1, idx, big), axis=1, keepdims=True)
    rest = jnp.where(idx == i1, -jnp.inf, sel)
    v2 = jnp.max(rest, axis=1, keepdims=True)
    i2 = jnp.min(jnp.where(rest == v2, idx, big), axis=1, keepdims=True)
    score = v1 + v2
    gidx = lax.broadcasted_iota(jnp.int32, score.shape, 0)
    best = jnp.max(score, axis=0, keepdims=True)
    grp = jnp.min(jnp.where(score == best, gidx, jnp.int32(N_GROUPS)), axis=0, keepdims=True)
    pick = gidx == grp
    l1 = jnp.sum(jnp.where(pick, i1, 0), axis=0)
    l2 = jnp.sum(jnp.where(pick, i2, 0), axis=0)
    s_g = jnp.sum(jnp.where(pick, s, 0.0), axis=0)
    eidx = lax.broadcasted_iota(jnp.int32, s_g.shape, 0)
    w1 = jnp.sum(jnp.where(eidx == l1, s_g, 0.0), axis=0, keepdims=True)
    w2 = jnp.sum(jnp.where(eidx == l2, s_g, 0.0), axis=0, keepdims=True)
    tot = w1 + w2
    base = grp[0] * EXPERTS_PER_GROUP
    e1, e2 = base + l1, base + l2
    e_ref[...] = jnp.concatenate([e1, e2], axis=0)
    g_ref[...] = jnp.concatenate([w1 / tot, w2 / tot], axis=0)

    @pl.when(pl.program_id(0) == 0)
    def _():
        seen_s[...] = jnp.zeros(seen_s.shape, F32)

    eall = lax.broadcasted_iota(jnp.int32, (N_EXPERTS, tm), 0)
    hit1, hit2 = eall == e1, eall == e2
    both = jnp.where(hit1, 1.0, jnp.where(hit2, 1.0, 0.0))
    incl = jnp.dot(both.astype(BF16), tri_ref[...], preferred_element_type=F32)
    before = incl - both + seen_s[...]
    r_ref[...] = jnp.concatenate([jnp.sum(jnp.where(hit1, before, 0.0), axis=0, keepdims=True),
                                  jnp.sum(jnp.where(hit2, before, 0.0), axis=0, keepdims=True)],
                                 axis=0).astype(jnp.int32)
    seen_s[...] = seen_s[...] + incl[:, tm - 1:tm]
    cnt_ref[...] = seen_s[...].astype(jnp.int32)


def route(h, router_w, router_b, tm=1024):
    n, d = h.shape
    tri = (jnp.arange(tm)[:, None] <= jnp.arange(tm)[None, :]).astype(BF16)
    tile = pl.BlockSpec((TOP_K, tm), lambda i: (0, i))
    return pl.pallas_call(
        _router_kernel,
        grid=(n // tm,),
        in_specs=[pl.BlockSpec((tm, d), lambda i: (i, 0)),
                  pl.BlockSpec((N_EXPERTS, d), lambda i: (0, 0)),
                  pl.BlockSpec((N_GROUPS, EXPERTS_PER_GROUP, 1), lambda i: (0, 0, 0)),
                  pl.BlockSpec((tm, tm), lambda i: (0, 0))],
        out_specs=[tile, tile, tile, pl.BlockSpec((N_EXPERTS, 1), lambda i: (0, 0))],
        out_shape=[jax.ShapeDtypeStruct((TOP_K, n), jnp.int32), jax.ShapeDtypeStruct((TOP_K, n), F32),
                   jax.ShapeDtypeStruct((TOP_K, n), jnp.int32), jax.ShapeDtypeStruct((N_EXPERTS, 1), jnp.int32)],
        scratch_shapes=[pltpu.VMEM((N_EXPERTS, 1), F32)],
        compiler_params=_params("arbitrary"),
        name="router",
    )(h, router_w.astype(F32).T, router_b.astype(F32).reshape(N_GROUPS, EXPERTS_PER_GROUP, 1), tri)


def _experts_kernel(blk_exp_ref, n_used_ref, x_ref, wg_ref, wu_ref, wd_ref, o_ref, wg_s, wu_s, wd_s):
    i = pl.program_id(0)
    expert = blk_exp_ref[i]
    prev = blk_exp_ref[jnp.maximum(i - 1, 0)]

    @pl.when((i == 0) | (expert != prev))
    def _():
        wg_s[...] = wg_ref[0, 0].astype(BF16)
        wu_s[...] = wu_ref[0, 0].astype(BF16)
        wd_s[...] = wd_ref[0, 0].astype(BF16)

    @pl.when(i < n_used_ref[0])
    def _():
        x = x_ref[...]
        hg = jnp.dot(x, wg_s[...], preferred_element_type=F32)
        hu = jnp.dot(x, wu_s[...], preferred_element_type=F32)
        hidden = (jax.nn.silu(hg) * hu).astype(BF16)
        o_ref[...] = jnp.dot(hidden, wd_s[...], preferred_element_type=F32).astype(o_ref.dtype)

    @pl.when(i >= n_used_ref[0])
    def _():
        o_ref[...] = jnp.zeros(o_ref.shape, o_ref.dtype)


def experts(xs, blk_exp, n_used, w_gate, w_up, w_down, layer):
    p, d = xs.shape
    f = w_gate.shape[3]
    grid_spec = pltpu.PrefetchScalarGridSpec(
        num_scalar_prefetch=2,
        grid=(p // MOE_ROWS,),
        in_specs=[pl.BlockSpec((MOE_ROWS, d), lambda i, be, nu: (i, 0)),
                  pl.BlockSpec((1, 1, d, f), lambda i, be, nu: (layer, be[i], 0, 0)),
                  pl.BlockSpec((1, 1, d, f), lambda i, be, nu: (layer, be[i], 0, 0)),
                  pl.BlockSpec((1, 1, f, d), lambda i, be, nu: (layer, be[i], 0, 0))],
        out_specs=pl.BlockSpec((MOE_ROWS, d), lambda i, be, nu: (i, 0)),
        scratch_shapes=[pltpu.VMEM((d, f), BF16), pltpu.VMEM((d, f), BF16), pltpu.VMEM((f, d), BF16)],
    )
    return pl.pallas_call(
        _experts_kernel,
        grid_spec=grid_spec,
        out_shape=jax.ShapeDtypeStruct((p, d), BF16),
        compiler_params=_params("arbitrary"),
        name="experts",
    )(blk_exp, n_used, xs, w_gate, w_up, w_down)


def moe_ffn(h, hb, router_w, router_b, w_gate, w_up, w_down, layer, ln_g, ln_b):
    n, d = h.shape
    a = n * TOP_K
    e_idx, gate, rank, counts = route(h, router_w, router_b)
    counts = counts[:, 0]
    padded = (counts + MOE_ROWS - 1) // MOE_ROWS * MOE_ROWS
    pad_end = jnp.cumsum(padded)
    pad_start = pad_end - padded
    experts_iota = jnp.arange(N_EXPERTS, dtype=jnp.int32)[:, None, None]
    dest = jnp.sum(jnp.where(e_idx[None] == experts_iota, pad_start[:, None, None], 0), axis=0) + rank
    n_blocks = a // MOE_ROWS + N_EXPERTS
    p = n_blocks * MOE_ROWS
    tok = jnp.broadcast_to(jnp.arange(n, dtype=jnp.int32)[None, :], (TOP_K, n))
    row_tok = jnp.zeros((p,), jnp.int32).at[dest.reshape(a)].set(tok.reshape(a))
    blk_exp = jnp.minimum(jnp.searchsorted(pad_end, jnp.arange(n_blocks, dtype=jnp.int32) * MOE_ROWS, side='right'),
                          N_EXPERTS - 1).astype(jnp.int32)
    n_used = (pad_end[-1:] // MOE_ROWS).astype(jnp.int32)
    xs = jnp.take(hb, row_tok, axis=0)
    yb = experts(xs, blk_exp, n_used, w_gate, w_up, w_down, layer)
    y0 = jnp.take(yb, dest[0], axis=0)
    y1 = jnp.take(yb, dest[1], axis=0)
    return combine_ln(h, y0, y1, gate.T, ln_g, ln_b)


FFT_R = 128
HALF_R = FFT_R // 2
HY_GROUP = 4


def _matmul_nt_kernel(a_ref, b_ref, o_ref):
    o_ref[...] = lax.dot_general(a_ref[...], b_ref[...], (((1,), (1,)), ((), ())),
                                 preferred_element_type=F32).astype(o_ref.dtype)


def matmul_nt(a, b, out_dtype, tm=512, tn=1024):
    m, k = a.shape
    n = b.shape[0]
    tm, tn = min(tm, m), min(tn, n)
    return pl.pallas_call(
        _matmul_nt_kernel,
        grid=(m // tm, n // tn),
        in_specs=[pl.BlockSpec((tm, k), lambda i, j: (i, 0)),
                  pl.BlockSpec((tn, k), lambda i, j: (j, 0))],
        out_specs=pl.BlockSpec((tm, tn), lambda i, j: (i, j)),
        out_shape=jax.ShapeDtypeStruct((m, n), out_dtype),
        compiler_params=_params("arbitrary", "arbitrary"),
        name="matmul_nt",
    )(a, b)


def _dft_constants():
    idx = jnp.arange(FFT_R, dtype=jnp.int32)
    prod = idx[:, None] * idx[None, :]
    ang = (prod % FFT_R).astype(F32) * (2.0 * math.pi / FFT_R)
    f_r, f_i = jnp.cos(ang), -jnp.sin(ang)
    ang_t = prod.astype(F32) * (2.0 * math.pi / (FFT_R * FFT_R))
    g_r, g_i = f_r[:HALF_R], -f_i[:HALF_R]
    return dict(
        la=jnp.block([[f_r[:, :HALF_R], -f_i[:, :HALF_R]], [f_i[:, :HALF_R], f_r[:, :HALF_R]]]).astype(BF16),
        la_real=jnp.concatenate([f_r, f_i], axis=0).astype(BF16),
        rb=jnp.block([[f_r, f_i], [-f_i, f_r]]).astype(BF16),
        rc=jnp.block([[f_r, -f_i], [f_i, f_r]]).astype(BF16),
        ld=(jnp.block([[g_r, -g_i], [g_i, g_r]]) / (FFT_R * FFT_R)).astype(BF16),
        t_r=jnp.cos(ang_t), t_i=-jnp.sin(ang_t))


def _rows_to_lanes(x):
    return jnp.concatenate([x[:FFT_R], x[FFT_R:]], axis=1)


def _fwd_lane_stage(a, t_r, t_i, rb):
    a_r, a_i = a[:FFT_R], a[FFT_R:]
    t2_r = jnp.concatenate([t_r, t_r], axis=1)
    t2_i = jnp.concatenate([t_i, t_i], axis=1)
    b_r = a_r * t2_r - a_i * t2_i
    b_i = a_r * t2_i + a_i * t2_r
    lhs = jnp.concatenate([jnp.concatenate([b_r[:, :FFT_R], b_i[:, :FFT_R]], axis=1),
                           jnp.concatenate([b_r[:, FFT_R:], b_i[:, FFT_R:]], axis=1)], axis=0)
    return jnp.dot(lhs.astype(BF16), rb, preferred_element_type=F32)


def _conv_pairs(ms, kfs, la, rb, rc, ld, t_r, t_i):
    a = [jnp.dot(la, jnp.concatenate([m_r, m_i], axis=0).astype(BF16), preferred_element_type=F32)
         for m_r, m_i in ms]
    z = [_fwd_lane_stage(ai, t_r, t_i, rb) for ai in a]
    c = []
    for zi, kf in zip(z, kfs):
        z_r, z_i, k_r, k_i = zi[:, :FFT_R], zi[:, FFT_R:], kf[:, :FFT_R], kf[:, FFT_R:]
        y = jnp.concatenate([z_r * k_r - z_i * k_i, z_r * k_i + z_i * k_r], axis=1)
        c.append(jnp.dot(y.astype(BF16), rc, preferred_element_type=F32))
    t2_r = jnp.concatenate([t_r, t_r], axis=0)
    t2_i = jnp.concatenate([t_i, t_i], axis=0)
    out = []
    for ci in c:
        c_r, c_i = ci[:, :FFT_R], ci[:, FFT_R:]
        d_r = c_r * t2_r + c_i * t2_i
        d_i = c_i * t2_r - c_r * t2_i
        rhs = jnp.concatenate([_rows_to_lanes(d_r), _rows_to_lanes(d_i)], axis=0)
        x = jnp.dot(ld, rhs.astype(BF16), preferred_element_type=F32)
        out.append((x[:HALF_R], x[HALF_R:]))
    return out


def _short_conv_tile(x, w0, w1, w2, b):
    rows = x.shape[0]
    lane = lax.broadcasted_iota(jnp.int32, x.shape, 1)
    row = lax.broadcasted_iota(jnp.int32, x.shape, 0)
    r = pltpu.roll(x, 1, 1)
    prev = jnp.where(lane == 0, jnp.where(row == 0, 0.0, pltpu.roll(r, 1, 0)), r)
    r = pltpu.roll(x, FFT_R - 1, 1)
    nxt = jnp.where(lane == FFT_R - 1, jnp.where(row == rows - 1, 0.0, pltpu.roll(r, rows - 1, 0)), r)
    return w0 * prev + w1 * x + w2 * nxt + b


def _hyena_kernel(cw_ref, cb_ref, skip_ref, v_ref, x1_ref, x2_ref, kf_ref,
                  la_ref, rb_ref, rc_ref, ld_ref, tr_ref, ti_ref, o_ref, *, tc):
    c_base = pl.program_id(0) * tc
    lane = lax.broadcasted_iota(jnp.int32, (1, 2 * FFT_R), 1)

    def conv_in(ref, part, b, c0):
        tiles = []
        for cc in range(2):
            ch = part * HY_CH + c_base + c0 + cc
            tiles.append(_short_conv_tile(ref[c0 + cc, b], cw_ref[0, ch], cw_ref[1, ch], cw_ref[2, ch], cb_ref[ch]))
        return jnp.concatenate(tiles, axis=1)

    def group(g, carry):
        starts = [2 * (HY_GROUP * g + j) for j in range(HY_GROUP)]
        consts = (la_ref[...], rb_ref[...], rc_ref[...], ld_ref[...], tr_ref[...], ti_ref[...])
        zs = [(conv_in(v_ref, 0, 0, c0), conv_in(v_ref, 0, 1, c0)) for c0 in starts]
        for o, g_ref in enumerate((x1_ref, x2_ref)):
            kfs = [kf_ref[o, pl.ds(c0, 2)].reshape(2 * FFT_R, 2 * FFT_R) for c0 in starts]
            ys = _conv_pairs(zs, kfs, *consts)
            nxt = []
            for c0, (z_r, z_i), (y_r, y_i) in zip(starts, zs, ys):
                sk = jnp.where(lane < FFT_R, skip_ref[o, c_base + c0], skip_ref[o, c_base + c0 + 1])
                nxt.append((conv_in(g_ref, o + 1, 0, c0) * (y_r + sk * z_r),
                            conv_in(g_ref, o + 1, 1, c0) * (y_i + sk * z_i)))
            zs = nxt
        for c0, (z_r, z_i) in zip(starts, zs):
            o_ref[c0, 0] = z_r[:, :FFT_R].astype(o_ref.dtype)
            o_ref[c0 + 1, 0] = z_r[:, FFT_R:].astype(o_ref.dtype)
            o_ref[c0, 1] = z_i[:, :FFT_R].astype(o_ref.dtype)
            o_ref[c0 + 1, 1] = z_i[:, FFT_R:].astype(o_ref.dtype)
        return carry

    lax.fori_loop(0, tc // (2 * HY_GROUP), group, 0)


def hyena_mix(u_t, kf, conv_w, conv_b, skip, consts, tc=16):
    rows, n = u_t.shape
    ch = rows // 3
    u5 = u_t.reshape(rows, 2, HALF_R, FFT_R)
    nct = ch // tc
    smem = pl.BlockSpec(memory_space=pltpu.SMEM)
    part = lambda k: pl.BlockSpec((tc, 2, HALF_R, FFT_R), lambda i: (k * nct + i, 0, 0, 0))
    full2 = lambda a: pl.BlockSpec(a.shape, lambda i: (0, 0))
    mats = [consts[k] for k in ("la", "rb", "rc", "ld", "t_r", "t_i")]
    out = pl.pallas_call(
        functools.partial(_hyena_kernel, tc=tc),
        grid=(nct,),
        in_specs=[smem, smem, smem, part(0), part(1), part(2),
                  pl.BlockSpec((2, tc, FFT_R, 2 * FFT_R), lambda i: (0, i, 0, 0))] + [full2(a) for a in mats],
        out_specs=pl.BlockSpec((tc, 2, HALF_R, FFT_R), lambda i: (i, 0, 0, 0)),
        out_shape=jax.ShapeDtypeStruct((ch, 2, HALF_R, FFT_R), BF16),
        compiler_params=_params("arbitrary"),
        name="hyena_mix",
    )(conv_w.astype(F32), conv_b.astype(F32), skip.astype(F32), u5, u5, u5, kf, *mats)
    return out.reshape(ch, n)


def _hyena_filter_kernel(delta_ref, hf_ref, hb_ref, tpos_ref, la_ref, rb_ref, tr_ref, ti_ref, kf_ref, *, tc):
    c_base = pl.program_id(1) * tc

    def taps_of(c0):
        taps = []
        for cc in range(2):
            k = jnp.concatenate([hf_ref[0, 0, c0 + cc], hb_ref[0, 0, c0 + cc]], axis=0)
            k = k * jnp.exp(-tpos_ref[...] * delta_ref[c_base + c0 + cc])
            taps.append(k * lax.rsqrt(jnp.sum(k * k, keepdims=True)))
        return jnp.concatenate(taps, axis=1).astype(BF16)

    def group(g, carry):
        starts = [2 * (HY_GROUP * g + j) for j in range(HY_GROUP)]
        a = [jnp.dot(la_ref[...], taps_of(c0), preferred_element_type=F32) for c0 in starts]
        z = [_fwd_lane_stage(ai, tr_ref[...], ti_ref[...], rb_ref[...]) for ai in a]
        for c0, zi in zip(starts, z):
            kf_ref[0, pl.ds(c0, 2)] = zi.reshape(2, FFT_R, 2 * FFT_R)
        return carry

    lax.fori_loop(0, tc // (2 * HY_GROUP), group, 0)


def hyena_filter_spectra(seq, w1, b1, w2, b2, w3, b3, freq, w4, consts, tc=16):
    t = jnp.linspace(0.0, 1.0, seq, dtype=F32)[:, None]
    omega = 2.0 * math.pi * jnp.arange(seq, dtype=F32)[:, None] / seq
    bands = jnp.linspace(1e-4, HF_BANDS - 1, HF_BANDS, dtype=F32)[None, :]
    ang = omega * bands
    z = jnp.concatenate([t, jnp.cos(ang), -jnp.sin(ang)], -1)
    fr = freq.astype(F32)
    hid = jnp.sin(fr * (z @ w1.astype(F32) + b1.astype(F32)))
    hid = jnp.sin(fr * (hid @ w2.astype(F32) + b2.astype(F32)))
    hid = jnp.sin(fr * (hid @ w3.astype(F32) + b3.astype(F32)))
    back = lambda a: jnp.concatenate([jnp.zeros_like(a[:1]), a[:0:-1]], axis=0)
    w4t = w4.astype(BF16).T
    half = HY_ORDER * HY_CH
    h_fwd = matmul(w4t[:half], hid.T.astype(BF16), F32)
    h_bwd = matmul(w4t[half:], back(hid).T.astype(BF16), F32)
    shape5 = (HY_ORDER, HY_CH, HALF_R, FFT_R)
    tpos = jnp.concatenate([t, back(t)], axis=0).reshape(FFT_R, FFT_R)
    max_decay = math.log(HF_TARGET) / HF_FAST
    min_decay = math.log(HF_TARGET) / HF_SLOW
    deltas = jnp.abs(jnp.linspace(min_decay, max_decay, HY_CH, dtype=F32))
    taps = pl.BlockSpec((1, 1, tc, HALF_R, FFT_R), lambda o, i: (0, o, i, 0, 0))
    full2 = lambda a: pl.BlockSpec(a.shape, lambda o, i: (0, 0))
    mats = [consts[k] for k in ("la_real", "rb", "t_r", "t_i")]
    return pl.pallas_call(
        functools.partial(_hyena_filter_kernel, tc=tc),
        grid=(HY_ORDER, HY_CH // tc),
        in_specs=[pl.BlockSpec(memory_space=pltpu.SMEM), taps, taps, full2(tpos)] + [full2(a) for a in mats],
        out_specs=pl.BlockSpec((1, tc, FFT_R, 2 * FFT_R), lambda o, i: (o, i, 0, 0)),
        out_shape=jax.ShapeDtypeStruct((HY_ORDER, HY_CH, FFT_R, 2 * FFT_R), F32),
        compiler_params=_params("arbitrary", "arbitrary"),
        name="hyena_filter_spectra",
    )(deltas, h_fwd.reshape((1,) + shape5), h_bwd.reshape((1,) + shape5), tpos, *mats)


def kernel(x, ev_w_in, ev_hy_conv_w, ev_hy_conv_b, ev_hf_w1, ev_hf_b1, ev_hf_w2, ev_hf_b2, ev_hf_w3, ev_hf_b3,
           ev_hf_freq, ev_hf_w4, ev_hy_skip, ev_lam_q1, ev_lam_k1, ev_lam_q2, ev_lam_k2, ev_subln_g, ev_w_out,
           od_w_in, od_conv_w, od_conv_b, od_wa, od_ba, od_wx, od_bx, od_lam, od_w_out, ln1_g, ln1_b, ln2_g,
           ln2_b, router_w, router_b, ex_w_gate, ex_w_up, ex_w_down):
    bsz, seq, d = x.shape
    n = bsz * seq
    h = x.reshape(n, d).astype(F32)
    hb = h.astype(BF16)
    for layer in range(DEPTH):
        i = layer // 2
        if layer % 2 == 0:
            w_in = ev_w_in[i]
            c0 = 3 * HY_CH
            w_t = w_in.T
            u_t = matmul_nt(w_t[:c0].astype(BF16), hb, F32)
            qt = matmul_nt((w_t[c0:c0 + DA_QK] * (DA_HEAD_DIM ** -0.5)).astype(BF16), hb, BF16)
            k = matmul(hb, w_in[:, c0 + DA_QK:c0 + 2 * DA_QK].astype(BF16), BF16)
            vt = matmul_nt(w_t[c0 + 2 * DA_QK:].astype(BF16), hb, BF16)
            consts = _dft_constants()
            kf = hyena_filter_spectra(seq, ev_hf_w1[i], ev_hf_b1[i], ev_hf_w2[i], ev_hf_b2[i],
                                      ev_hf_w3[i], ev_hf_b3[i], ev_hf_freq[i], ev_hf_w4[i], consts)
            y_hy = hyena_mix(u_t, kf, ev_hy_conv_w[i], ev_hy_conv_b[i], ev_hy_skip[i], consts).T
            lambda_init = 0.8 - 0.6 * math.exp(-0.3 * layer)
            lam = (jnp.exp(jnp.sum(ev_lam_q1[i].astype(F32) * ev_lam_k1[i].astype(F32)))
                   - jnp.exp(jnp.sum(ev_lam_q2[i].astype(F32) * ev_lam_k2[i].astype(F32))) + lambda_init)
            y_da = diff_attention(qt, k.reshape(bsz, seq, DA_QK), vt,
                                  lam, ev_subln_g[i], lambda_init).reshape(n, DA_WIDTH)
            h, hb = even_out_proj(y_hy, y_da, ev_w_out[i].astype(BF16), h, ln1_g[layer], ln1_b[layer])
        else:
            u = matmul(hb, od_w_in[i].astype(BF16), F32)
            yd = rglru_bidir(u.reshape(bsz, seq, 2 * RG_WIDTH), od_conv_w[i], od_conv_b[i], od_wa[i], od_ba[i],
                             od_wx[i], od_bx[i], od_lam[i]).reshape(2, n, RG_WIDTH)
            h, hb = odd_out_proj(u, yd, od_w_out[i].astype(BF16), h, ln1_g[layer], ln1_b[layer])
        h, hb = moe_ffn(h, hb, router_w, router_b, ex_w_gate, ex_w_up, ex_w_down, layer,
                        ln2_g[layer], ln2_b[layer])
    return h.reshape(bsz, seq, d).astype(x.dtype)
```

```python
import functools
import math

import jax
import jax.numpy as jnp
from jax import lax
from jax.experimental import pallas as pl
from jax.experimental.pallas import tpu as pltpu

F32 = jnp.float32
BF16 = jnp.bfloat16

D_MODEL = 2048
DEPTH = 2
HY_CH = D_MODEL // 2
HY_ORDER = 2
HF_BANDS = 16
HF_TARGET = 1e-2
HF_FAST = 0.3
HF_SLOW = 1.5
DA_HEADS = 8
DA_HEAD_DIM = 64
DA_V_DIM = 2 * DA_HEAD_DIM
DA_QK = DA_HEADS * 2 * DA_HEAD_DIM
DA_WIDTH = DA_HEADS * DA_V_DIM
RG_WIDTH = D_MODEL
RG_BLOCKS = 8
RG_BLOCK_W = RG_WIDTH // RG_BLOCKS
RG_C = 8.0
RG_CONV = 4
N_EXPERTS = 32
N_GROUPS = 4
EXPERTS_PER_GROUP = N_EXPERTS // N_GROUPS
TOP_K = 2
D_FF = 512
DN_ALPHA = (2 * DEPTH) ** 0.25
LN_EPS = 1e-5

VMEM_LIMIT_BYTES = 56 * 1024 * 1024
MOE_ROWS = 256


def _params(*sem):
    return pltpu.CompilerParams(dimension_semantics=sem, vmem_limit_bytes=VMEM_LIMIT_BYTES)


def _mm_kernel(x_ref, w_ref, o_ref):
    o_ref[...] = jnp.dot(x_ref[...], w_ref[...], preferred_element_type=F32).astype(o_ref.dtype)


def matmul(x, w, out_dtype, tm=512, tn=1024):
    m, k = x.shape
    n = w.shape[1]
    tm, tn = min(tm, m), min(tn, n)
    return pl.pallas_call(
        _mm_kernel,
        grid=(n // tn, m // tm),
        in_specs=[pl.BlockSpec((tm, k), lambda j, i: (i, 0)),
                  pl.BlockSpec((k, tn), lambda j, i: (0, j))],
        out_specs=pl.BlockSpec((tm, tn), lambda j, i: (i, j)),
        out_shape=jax.ShapeDtypeStruct((m, n), out_dtype),
        compiler_params=_params("arbitrary", "arbitrary"),
        name="matmul",
    )(x, w)


def _ln_rows(z, g, b):
    mu = jnp.mean(z, axis=-1, keepdims=True)
    zc = z - mu
    var = jnp.mean(zc * zc, axis=-1, keepdims=True)
    return zc * lax.rsqrt(var + LN_EPS) * g + b


def _even_out_kernel(xa_ref, xb_ref, wa_ref, wb_ref, r_ref, g_ref, b_ref, o_ref, ob_ref):
    acc = lax.dot_general(xa_ref[...], wa_ref[...], (((0,), (0,)), ((), ())), preferred_element_type=F32)
    acc += jnp.dot(xb_ref[...], wb_ref[...], preferred_element_type=F32)
    y = _ln_rows(DN_ALPHA * r_ref[...] + acc, g_ref[...], b_ref[...])
    o_ref[...] = y
    ob_ref[...] = y.astype(BF16)


def even_out_proj(y_hy_t, y_da, w_out, resid, g, b, tm=512):
    ka, m = y_hy_t.shape
    kb = y_da.shape[1]
    d = w_out.shape[1]
    row = lambda i: (i, 0)
    fixed = lambda i: (0, 0)
    return pl.pallas_call(
        _even_out_kernel,
        grid=(m // tm,),
        in_specs=[pl.BlockSpec((ka, tm), lambda i: (0, i)), pl.BlockSpec((tm, kb), row),
                  pl.BlockSpec((ka, d), fixed), pl.BlockSpec((kb, d), fixed),
                  pl.BlockSpec((tm, d), row), pl.BlockSpec((1, d), fixed), pl.BlockSpec((1, d), fixed)],
        out_specs=[pl.BlockSpec((tm, d), row), pl.BlockSpec((tm, d), row)],
        out_shape=[jax.ShapeDtypeStruct((m, d), F32), jax.ShapeDtypeStruct((m, d), BF16)],
        compiler_params=_params("arbitrary"),
        name="even_out_proj",
    )(y_hy_t, y_da, w_out[:ka], w_out[ka:], resid, g.reshape(1, d), b.reshape(1, d))


def _odd_out_kernel(gate_ref, y0_ref, y1_ref, w_ref, r_ref, g_ref, b_ref, o_ref, ob_ref):
    x = jax.nn.gelu(gate_ref[...]) * (y0_ref[0] + y1_ref[0])
    acc = jnp.dot(x.astype(BF16), w_ref[...], preferred_element_type=F32)
    y = _ln_rows(DN_ALPHA * r_ref[...] + acc, g_ref[...], b_ref[...])
    o_ref[...] = y
    ob_ref[...] = y.astype(BF16)


def odd_out_proj(u, yd, w_out, resid, g, b, tm=256):
    m = u.shape[0]
    k, d = w_out.shape
    row = lambda i: (i, 0)
    fixed = lambda i: (0, 0)
    return pl.pallas_call(
        _odd_out_kernel,
        grid=(m // tm,),
        in_specs=[pl.BlockSpec((tm, k), row),
                  pl.BlockSpec((1, tm, k), lambda i: (0, i, 0)),
                  pl.BlockSpec((1, tm, k), lambda i: (1, i, 0)),
                  pl.BlockSpec((k, d), fixed),
                  pl.BlockSpec((tm, d), row), pl.BlockSpec((1, d), fixed), pl.BlockSpec((1, d), fixed)],
        out_specs=[pl.BlockSpec((tm, d), row), pl.BlockSpec((tm, d), row)],
        out_shape=[jax.ShapeDtypeStruct((m, d), F32), jax.ShapeDtypeStruct((m, d), BF16)],
        compiler_params=_params("arbitrary"),
        name="odd_out_proj",
    )(u, yd, yd, w_out, resid, g.reshape(1, d), b.reshape(1, d))


def _combine_ln_kernel(h_ref, y0_ref, y1_ref, gate_ref, g_ref, b_ref, o_ref, ob_ref):
    gate = gate_ref[...]
    ffn = gate[:, 0:1] * y0_ref[...].astype(F32) + gate[:, 1:2] * y1_ref[...].astype(F32)
    y = _ln_rows(DN_ALPHA * h_ref[...] + ffn, g_ref[...], b_ref[...])
    o_ref[...] = y
    ob_ref[...] = y.astype(BF16)


def combine_ln(h, y0, y1, gate, g, b, tm=512):
    m, d = h.shape
    row = lambda i: (i, 0)
    fixed = lambda i: (0, 0)
    return pl.pallas_call(
        _combine_ln_kernel,
        grid=(m // tm,),
        in_specs=[pl.BlockSpec((tm, d), row), pl.BlockSpec((tm, d), row), pl.BlockSpec((tm, d), row),
                  pl.BlockSpec((tm, TOP_K), row), pl.BlockSpec((1, d), fixed), pl.BlockSpec((1, d), fixed)],
        out_specs=[pl.BlockSpec((tm, d), row), pl.BlockSpec((tm, d), row)],
        out_shape=[jax.ShapeDtypeStruct((m, d), F32), jax.ShapeDtypeStruct((m, d), BF16)],
        compiler_params=_params("arbitrary"),
        name="combine_ln",
    )(h, y0, y1, gate, g.reshape(1, d), b.reshape(1, d))


ONES_ROWS = 16
POS_SPLIT = 16


def _attn_kernel(slopes_ref, lam_ref, qt_ref, k_ref, vt_ref, fq_ref, fk_ref, g_ref, o_ref, m_s, acc_s, sa_s, sb_s,
                 *, tq, tk, seq, out_scale):
    head = pl.program_id(1)
    i0 = pl.program_id(2) * tq
    slope = slopes_ref[head]
    lam = lam_ref[0]
    qt = qt_ref[...]
    row = lax.broadcasted_iota(jnp.int32, qt.shape, 0)
    zero = jnp.zeros_like(qt)
    q2 = jnp.concatenate([jnp.where(row < DA_HEAD_DIM, qt, zero),
                          jnp.where(row >= DA_HEAD_DIM, qt, zero)], axis=1)
    fq = fq_ref[0]
    fq2 = jnp.concatenate([fq, fq], axis=1)
    q_aug = jnp.concatenate([q2, fq2], axis=0)
    fk = fk_ref[0]
    fk_neg = -fk
    ones = jnp.ones((ONES_ROWS, tk), BF16)
    m_s[...] = jnp.full(m_s.shape, -jnp.inf, F32)
    acc_s[...] = jnp.zeros(acc_s.shape, F32)
    j_diag = i0 // tk

    def update(j, s, shift):
        off = pl.multiple_of(j * tk, tk)
        va = jnp.concatenate([vt_ref[:, pl.ds(off, tk)], ones], axis=0)
        m_prev = m_s[...]
        m_new = jnp.maximum(m_prev, jnp.max(s, axis=0, keepdims=True) - shift)
        p = jnp.exp(s - (m_new + shift))
        alpha = jnp.exp(m_prev - m_new)
        acc_s[...] = alpha * acc_s[...] + jnp.dot(va, p.astype(BF16), preferred_element_type=F32)
        m_s[...] = m_new

    def chunk_of(t):
        return t + (t >= j_diag).astype(jnp.int32)

    def scores(t):
        j = chunk_of(t)
        off = pl.multiple_of(j * tk, tk)
        ka = jnp.concatenate([k_ref[0, pl.ds(off, tk), :], jnp.where(j < j_diag, fk, fk_neg)], axis=1)
        return jnp.dot(ka, q_aug, preferred_element_type=F32)

    def consume(s_ref, t):
        j = chunk_of(t)
        update(j, s_ref[...], slope * jnp.abs(i0 - j * tk).astype(F32))

    j0 = pl.multiple_of(j_diag * tk, tk)
    dist = jnp.abs(lax.broadcasted_iota(jnp.int32, (tk, tq), 0) + (j0 - i0)
                   - lax.broadcasted_iota(jnp.int32, (tk, tq), 1)).astype(F32)
    bias = slope * dist
    s_diag = jnp.dot(k_ref[0, pl.ds(j0, tk), :], q2, preferred_element_type=F32)
    sa_s[...] = s_diag - jnp.concatenate([bias, bias], axis=1)
    sb_s[...] = scores(jnp.int32(0))
    update(j_diag, sa_s[...], 0.0)
    n_off = seq // tk - 1

    def pair(t, carry):
        sa_s[...] = scores(2 * t + 1)
        consume(sb_s, 2 * t)
        sb_s[...] = scores(2 * t + 2)
        consume(sa_s, 2 * t + 1)
        return carry

    lax.fori_loop(0, n_off // 2, pair, 0)
    consume(sb_s, jnp.int32(n_off - 1))

    acc = acc_s[...]
    o1 = acc[:DA_V_DIM, :tq] / acc[DA_V_DIM:DA_V_DIM + 1, :tq]
    o2 = acc[:DA_V_DIM, tq:] / acc[DA_V_DIM:DA_V_DIM + 1, tq:]
    o = o1 - lam * o2
    o = o * lax.rsqrt(jnp.mean(o * o, axis=0, keepdims=True) + LN_EPS) * g_ref[...]
    o_ref[0] = (o * out_scale).T.astype(o_ref.dtype)


def diff_attention(qt, k, vt, lam, subln_g, lambda_init, tq=512, tk=512):
    bsz, seq, width = k.shape
    nq = seq // tq
    slopes = 2.0 ** (-(8.0 / DA_HEADS) * jnp.arange(1, DA_HEADS + 1, dtype=F32))

    def split(n):
        pos = jnp.arange(n, dtype=jnp.int32)
        return (pos // POS_SPLIT * POS_SPLIT).astype(F32), (pos % POS_SPLIT).astype(F32)

    qhi, qlo = split(tq)
    khi, klo = split(tk)
    sl = slopes[:, None]
    fq = jnp.zeros((DA_HEADS, DA_V_DIM, tq), F32)
    fq = fq.at[:, 0].set(1.0).at[:, 1].set(1.0).at[:, 2].set(-sl * qhi).at[:, 3].set(-sl * qlo)
    fk = jnp.zeros((DA_HEADS, tk, DA_V_DIM), F32)
    fk = fk.at[:, :, 0].set(sl * khi).at[:, :, 1].set(sl * klo).at[:, :, 2].set(1.0).at[:, :, 3].set(1.0)
    kern = functools.partial(_attn_kernel, tq=tq, tk=tk, seq=seq, out_scale=1.0 - lambda_init)
    smem = pl.BlockSpec(memory_space=pltpu.SMEM)
    return pl.pallas_call(
        kern,
        grid=(bsz, DA_HEADS, seq // tq),
        in_specs=[smem, smem,
                  pl.BlockSpec((DA_V_DIM, tq), lambda b, h, i: (h, b * nq + i)),
                  pl.BlockSpec((1, seq, DA_V_DIM), lambda b, h, i: (b, 0, h)),
                  pl.BlockSpec((DA_V_DIM, seq), lambda b, h, i: (h, b)),
                  pl.BlockSpec((1, DA_V_DIM, tq), lambda b, h, i: (h, 0, 0)),
                  pl.BlockSpec((1, tk, DA_V_DIM), lambda b, h, i: (h, 0, 0)),
                  pl.BlockSpec((DA_V_DIM, 1), lambda b, h, i: (0, 0))],
        out_specs=pl.BlockSpec((1, tq, DA_V_DIM), lambda b, h, i: (b, i, h)),
        out_shape=jax.ShapeDtypeStruct((bsz, seq, width), BF16),
        scratch_shapes=[pltpu.VMEM((1, 2 * tq), F32), pltpu.VMEM((DA_V_DIM + ONES_ROWS, 2 * tq), F32),
                        pltpu.VMEM((tk, 2 * tq), F32), pltpu.VMEM((tk, 2 * tq), F32)],
        compiler_params=_params("arbitrary", "arbitrary", "arbitrary"),
        name="diff_attention",
    )(slopes, lam.reshape(1).astype(F32), qt, k, vt, fq.astype(BF16), fk.astype(BF16),
      subln_g.reshape(DA_V_DIM, 1).astype(F32))


HALO = 8


def _sigmoid(x):
    return 0.5 * jnp.tanh(0.5 * x) + 0.5


def _rglru_kernel(xr_ref, prev_ref, next_ref, cw_ref, cb_ref, wa_ref, wx_ref, ba_ref, bx_ref, sp_ref, y_ref,
                  xf_s, a_s, b_s, h_s, *, ts, nt):
    direction = pl.program_id(0)
    t = pl.program_id(2)
    chunk = t + direction * (nt - 1 - 2 * t)

    @pl.when(t == 0)
    def _():
        h_s[...] = jnp.zeros(h_s.shape, F32)

    xf_s[0:HALO, :] = jnp.where(chunk == 0, 0.0, prev_ref[0])
    xf_s[HALO:HALO + ts, :] = xr_ref[0]
    xf_s[HALO + ts:, :] = jnp.where(chunk == nt - 1, 0.0, next_ref[0])
    x = cb_ref[...] + sum(cw_ref[j:j + 1, :] * xf_s[pl.ds(HALO - 2 + j, ts), :] for j in range(RG_CONV))
    xb = x.astype(BF16)
    for n in range(RG_BLOCKS):
        cols = slice(n * RG_BLOCK_W, (n + 1) * RG_BLOCK_W)
        xs = xb[:, cols]
        r = _sigmoid(jnp.dot(xs, wa_ref[0, n], preferred_element_type=F32) + ba_ref[0, :, cols])
        i = _sigmoid(jnp.dot(xs, wx_ref[0, n], preferred_element_type=F32) + bx_ref[0, :, cols])
        log_a = -RG_C * r * sp_ref[0, :, cols]
        a = jnp.exp(log_a)
        a_s[:, cols] = a
        b_s[:, cols] = jnp.sqrt(1.0 - a * a) * (i * x[:, cols])

    def step(k, h):
        row = jnp.where(direction == 0, k, ts - 1 - k)
        h = a_s[pl.ds(row, 1), :] * h + b_s[pl.ds(row, 1), :]
        y_ref[0, 0, pl.ds(row, 1), :] = h
        return h

    h_s[...] = lax.fori_loop(0, ts, step, h_s[...], unroll=8)


def rglru_bidir(u, conv_w, conv_b, wa, ba, wx, bx, lam, ts=256):
    bsz, seq, w2 = u.shape
    w = w2 // 2
    nt = seq // ts
    per = ts // HALO
    sp = jax.nn.softplus(-lam.astype(F32)).reshape(2, 1, w)
    chunk = lambda d, t: t + d * (nt - 1 - 2 * t)
    dmap4 = lambda d, b, t: (d, 0, 0, 0)
    dmap3 = lambda d, b, t: (d, 0, 0)
    fixed = lambda d, b, t: (0, 0)
    return pl.pallas_call(
        functools.partial(_rglru_kernel, ts=ts, nt=nt),
        grid=(2, bsz, nt),
        in_specs=[pl.BlockSpec((1, ts, w), lambda d, b, t: (b, chunk(d, t), 1)),
                  pl.BlockSpec((1, HALO, w), lambda d, b, t: (b, jnp.maximum(chunk(d, t) * per - 1, 0), 1)),
                  pl.BlockSpec((1, HALO, w),
                               lambda d, b, t: (b, jnp.minimum((chunk(d, t) + 1) * per, seq // HALO - 1), 1)),
                  pl.BlockSpec((RG_CONV, w), fixed), pl.BlockSpec((1, w), fixed),
                  pl.BlockSpec((1, RG_BLOCKS, RG_BLOCK_W, RG_BLOCK_W), dmap4),
                  pl.BlockSpec((1, RG_BLOCKS, RG_BLOCK_W, RG_BLOCK_W), dmap4),
                  pl.BlockSpec((1, 1, w), dmap3), pl.BlockSpec((1, 1, w), dmap3), pl.BlockSpec((1, 1, w), dmap3)],
        out_specs=pl.BlockSpec((1, 1, ts, w), lambda d, b, t: (d, b, chunk(d, t), 0)),
        out_shape=jax.ShapeDtypeStruct((2, bsz, seq, w), F32),
        scratch_shapes=[pltpu.VMEM((ts + 2 * HALO, w), F32), pltpu.VMEM((ts, w), F32), pltpu.VMEM((ts, w), F32),
                        pltpu.VMEM((1, w), F32)],
        compiler_params=_params("arbitrary", "arbitrary", "arbitrary"),
        name="rglru_bidir",
    )(u, u, u, conv_w.astype(F32), conv_b.astype(F32).reshape(1, w), wa.astype(BF16), wx.astype(BF16),
      ba.astype(F32).reshape(2, 1, w), bx.astype(F32).reshape(2, 1, w), sp)


def _router_kernel(h_ref, wt_ref, b_ref, tri_ref, e_ref, g_ref, r_ref, cnt_ref, seen_s):
    logits = lax.dot_general(wt_ref[...], h_ref[...], (((1,), (1,)), ((), ())),
                             precision=lax.Precision.HIGHEST, preferred_element_type=F32)
    tm = logits.shape[1]
    s = jax.nn.sigmoid(logits).reshape(N_GROUPS, EXPERTS_PER_GROUP, tm)
    sel = s + b_ref[...]
    idx = lax.broadcasted_iota(jnp.int32, sel.shape, 1)
    big = jnp.int32(EXPERTS_PER_GROUP)
    v1 = jnp.max(sel, axis=1, keepdims=True)
    i1 = jnp.min(jnp.where(sel == v1, idx, big), axis=1, keepdims=True)
    rest = jnp.where(idx == i1, -jnp.inf, sel)
    v2 = jnp.max(rest, axis=1, keepdims=True)
    i2 = jnp.min(jnp.where(rest == v2, idx, big), axis=1, keepdims=True)
    score = v1 + v2
    gidx = lax.broadcasted_iota(jnp.int32, score.shape, 0)
    best = jnp.max(score, axis=0, keepdims=True)
    grp = jnp.min(jnp.where(score == best, gidx, jnp.int32(N_GROUPS)), axis=0, keepdims=True)
    pick = gidx == grp
    l1 = jnp.sum(jnp.where(pick, i1, 0), axis=0)
    l2 = jnp.sum(jnp.where(pick, i2, 0), axis=0)
    s_g = jnp.sum(jnp.where(pick, s, 0.0), axis=0)
    eidx = lax.broadcasted_iota(jnp.int32, s_g.shape, 0)
    w1 = jnp.sum(jnp.where(eidx == l1, s_g, 0.0), axis=0, keepdims=True)
    w2 = jnp.sum(jnp.where(eidx == l2, s_g, 0.0), axis=0, keepdims=True)
    tot = w1 + w2
    base = grp[0] * EXPERTS_PER_GROUP
    e1, e2 = base + l1, base + l2
    e_ref[...] = jnp.concatenate([e1, e2], axis=0)
    g_ref[...] = jnp.concatenate([w1 / tot, w2 / tot], axis=0)

    @pl.when(pl.program_id(0) == 0)
    def _():
        seen_s[...] = jnp.zeros(seen_s.shape, F32)

    eall = lax.broadcasted_iota(jnp.int32, (N_EXPERTS, tm), 0)
    hit1, hit2 = eall == e1, eall == e2
    both = jnp.where(hit1, 1.0, jnp.where(hit2, 1.0, 0.0))
    incl = jnp.dot(both.astype(BF16), tri_ref[...], preferred_element_type=F32)
    before = incl - both + seen_s[...]
    r_ref[...] = jnp.concatenate([jnp.sum(jnp.where(hit1, before, 0.0), axis=0, keepdims=True),
                                  jnp.sum(jnp.where(hit2, before, 0.0), axis=0, keepdims=True)],
                                 axis=0).astype(jnp.int32)
    seen_s[...] = seen_s[...] + incl[:, tm - 1:tm]
    cnt_ref[...] = seen_s[...].astype(jnp.int32)


def route(h, router_w, router_b, tm=1024):
    n, d = h.shape
    tri = (jnp.arange(tm)[:, None] <= jnp.arange(tm)[None, :]).astype(BF16)
    tile = pl.BlockSpec((TOP_K, tm), lambda i: (0, i))
    return pl.pallas_call(
        _router_kernel,
        grid=(n // tm,),
        in_specs=[pl.BlockSpec((tm, d), lambda i: (i, 0)),
                  pl.BlockSpec((N_EXPERTS, d), lambda i: (0, 0)),
                  pl.BlockSpec((N_GROUPS, EXPERTS_PER_GROUP, 1), lambda i: (0, 0, 0)),
                  pl.BlockSpec((tm, tm), lambda i: (0, 0))],
        out_specs=[tile, tile, tile, pl.BlockSpec((N_EXPERTS, 1), lambda i: (0, 0))],
        out_shape=[jax.ShapeDtypeStruct((TOP_K, n), jnp.int32), jax.ShapeDtypeStruct((TOP_K, n), F32),
                   jax.ShapeDtypeStruct((TOP_K, n), jnp.int32), jax.ShapeDtypeStruct((N_EXPERTS, 1), jnp.int32)],
        scratch_shapes=[pltpu.VMEM((N_EXPERTS, 1), F32)],
        compiler_params=_params("arbitrary"),
        name="router",
    )(h, router_w.astype(F32).T, router_b.astype(F32).reshape(N_GROUPS, EXPERTS_PER_GROUP, 1), tri)


def _experts_kernel(blk_exp_ref, n_used_ref, x_ref, wg_ref, wu_ref, wd_ref, o_ref, wg_s, wu_s, wd_s):
    i = pl.program_id(0)
    expert = blk_exp_ref[i]
    prev = blk_exp_ref[jnp.maximum(i - 1, 0)]

    @pl.when((i == 0) | (expert != prev))
    def _():
        wg_s[...] = wg_ref[0, 0].astype(BF16)
        wu_s[...] = wu_ref[0, 0].astype(BF16)
        wd_s[...] = wd_ref[0, 0].astype(BF16)

    @pl.when(i < n_used_ref[0])
    def _():
        x = x_ref[...]
        hg = jnp.dot(x, wg_s[...], preferred_element_type=F32)
        hu = jnp.dot(x, wu_s[...], preferred_element_type=F32)
        hidden = (jax.nn.silu(hg) * hu).astype(BF16)
        o_ref[...] = jnp.dot(hidden, wd_s[...], preferred_element_type=F32).astype(o_ref.dtype)

    @pl.when(i >= n_used_ref[0])
    def _():
        o_ref[...] = jnp.zeros(o_ref.shape, o_ref.dtype)


def experts(xs, blk_exp, n_used, w_gate, w_up, w_down, layer):
    p, d = xs.shape
    f = w_gate.shape[3]
    grid_spec = pltpu.PrefetchScalarGridSpec(
        num_scalar_prefetch=2,
        grid=(p // MOE_ROWS,),
        in_specs=[pl.BlockSpec((MOE_ROWS, d), lambda i, be, nu: (i, 0)),
                  pl.BlockSpec((1, 1, d, f), lambda i, be, nu: (layer, be[i], 0, 0)),
                  pl.BlockSpec((1, 1, d, f), lambda i, be, nu: (layer, be[i], 0, 0)),
                  pl.BlockSpec((1, 1, f, d), lambda i, be, nu: (layer, be[i], 0, 0))],
        out_specs=pl.BlockSpec((MOE_ROWS, d), lambda i, be, nu: (i, 0)),
        scratch_shapes=[pltpu.VMEM((d, f), BF16), pltpu.VMEM((d, f), BF16), pltpu.VMEM((f, d), BF16)],
    )
    return pl.pallas_call(
        _experts_kernel,
        grid_spec=grid_spec,
        out_shape=jax.ShapeDtypeStruct((p, d), BF16),
        compiler_params=_params("arbitrary"),
        name="experts",
    )(blk_exp, n_used, xs, w_gate, w_up, w_down)


def moe_ffn(h, hb, router_w, router_b, w_gate, w_up, w_down, layer, ln_g, ln_b):
    n, d = h.shape
    a = n * TOP_K
    e_idx, gate, rank, counts = route(h, router_w, router_b)
    counts = counts[:, 0]
    padded = (counts + MOE_ROWS - 1) // MOE_ROWS * MOE_ROWS
    pad_end = jnp.cumsum(padded)
    pad_start = pad_end - padded
    experts_iota = jnp.arange(N_EXPERTS, dtype=jnp.int32)[:, None, None]
    dest = jnp.sum(jnp.where(e_idx[None] == experts_iota, pad_start[:, None, None], 0), axis=0) + rank
    n_blocks = a // MOE_ROWS + N_EXPERTS
    p = n_blocks * MOE_ROWS
    tok = jnp.broadcast_to(jnp.arange(n, dtype=jnp.int32)[None, :], (TOP_K, n))
    row_tok = jnp.zeros((p,), jnp.int32).at[dest.reshape(a)].set(tok.reshape(a))
    blk_start = jnp.arange(n_blocks, dtype=jnp.int32)[:, None] * MOE_ROWS
    blk_exp = jnp.minimum(jnp.sum((pad_end[None, :] <= blk_start).astype(jnp.int32), axis=1), N_EXPERTS - 1)
    n_used = (pad_end[-1:] // MOE_ROWS).astype(jnp.int32)
    xs = hb.at[row_tok].get(mode="promise_in_bounds")
    yb = experts(xs, blk_exp, n_used, w_gate, w_up, w_down, layer)
    y0 = yb.at[dest[0]].get(mode="promise_in_bounds")
    y1 = yb.at[dest[1]].get(mode="promise_in_bounds")
    return combine_ln(h, y0, y1, gate.T, ln_g, ln_b)


FFT_R = 128
HALF_R = FFT_R // 2
HY_GROUP = 4


def _matmul_nt_kernel(a_ref, b_ref, o_ref):
    o_ref[...] = lax.dot_general(a_ref[...], b_ref[...], (((1,), (1,)), ((), ())),
                                 preferred_element_type=F32).astype(o_ref.dtype)


def matmul_nt(a, b, out_dtype, tm=512, tn=1024):
    m, k = a.shape
    n = b.shape[0]
    tm, tn = min(tm, m), min(tn, n)
    return pl.pallas_call(
        _matmul_nt_kernel,
        grid=(m // tm, n // tn),
        in_specs=[pl.BlockSpec((tm, k), lambda i, j: (i, 0)),
                  pl.BlockSpec((tn, k), lambda i, j: (j, 0))],
        out_specs=pl.BlockSpec((tm, tn), lambda i, j: (i, j)),
        out_shape=jax.ShapeDtypeStruct((m, n), out_dtype),
        compiler_params=_params("arbitrary", "arbitrary"),
        name="matmul_nt",
    )(a, b)


def _dft_constants():
    idx = jnp.arange(FFT_R, dtype=jnp.int32)
    prod = idx[:, None] * idx[None, :]
    ang = (prod % FFT_R).astype(F32) * (2.0 * math.pi / FFT_R)
    f_r, f_i = jnp.cos(ang), -jnp.sin(ang)
    ang_t = prod.astype(F32) * (2.0 * math.pi / (FFT_R * FFT_R))
    g_r, g_i = f_r[:HALF_R], -f_i[:HALF_R]
    return dict(
        la=jnp.block([[f_r[:, :HALF_R], -f_i[:, :HALF_R]], [f_i[:, :HALF_R], f_r[:, :HALF_R]]]).astype(BF16),
        la_real=jnp.concatenate([f_r, f_i], axis=0).astype(BF16),
        rb=jnp.block([[f_r, f_i], [-f_i, f_r]]).astype(BF16),
        rc=jnp.block([[f_r, -f_i], [f_i, f_r]]).astype(BF16),
        ld=(jnp.block([[g_r, -g_i], [g_i, g_r]]) / (FFT_R * FFT_R)).astype(BF16),
        t_r=jnp.cos(ang_t), t_i=-jnp.sin(ang_t))


def _rows_to_lanes(x):
    return jnp.concatenate([x[:FFT_R], x[FFT_R:]], axis=1)


def _fwd_lane_stage(a, t_r, t_i, rb):
    a_r, a_i = a[:FFT_R], a[FFT_R:]
    t2_r = jnp.concatenate([t_r, t_r], axis=1)
    t2_i = jnp.concatenate([t_i, t_i], axis=1)
    b_r = a_r * t2_r - a_i * t2_i
    b_i = a_r * t2_i + a_i * t2_r
    lhs = jnp.concatenate([jnp.concatenate([b_r[:, :FFT_R], b_i[:, :FFT_R]], axis=1),
                           jnp.concatenate([b_r[:, FFT_R:], b_i[:, FFT_R:]], axis=1)], axis=0)
    return jnp.dot(lhs.astype(BF16), rb, preferred_element_type=F32)


def _conv_pairs(ms, kfs, la, rb, rc, ld, t_r, t_i):
    a = [jnp.dot(la, jnp.concatenate([m_r, m_i], axis=0).astype(BF16), preferred_element_type=F32)
         for m_r, m_i in ms]
    z = [_fwd_lane_stage(ai, t_r, t_i, rb) for ai in a]
    c = []
    for zi, kf in zip(z, kfs):
        z_r, z_i, k_r, k_i = zi[:, :FFT_R], zi[:, FFT_R:], kf[:, :FFT_R], kf[:, FFT_R:]
        y = jnp.concatenate([z_r * k_r - z_i * k_i, z_r * k_i + z_i * k_r], axis=1)
        c.append(jnp.dot(y.astype(BF16), rc, preferred_element_type=F32))
    t2_r = jnp.concatenate([t_r, t_r], axis=0)
    t2_i = jnp.concatenate([t_i, t_i], axis=0)
    out = []
    for ci in c:
        c_r, c_i = ci[:, :FFT_R], ci[:, FFT_R:]
        d_r = c_r * t2_r + c_i * t2_i
        d_i = c_i * t2_r - c_r * t2_i
        rhs = jnp.concatenate([_rows_to_lanes(d_r), _rows_to_lanes(d_i)], axis=0)
        x = jnp.dot(ld, rhs.astype(BF16), preferred_element_type=F32)
        out.append((x[:HALF_R], x[HALF_R:]))
    return out


def _short_conv_tile(x, w0, w1, w2, b):
    rows = x.shape[0]
    lane = lax.broadcasted_iota(jnp.int32, x.shape, 1)
    row = lax.broadcasted_iota(jnp.int32, x.shape, 0)
    r = pltpu.roll(x, 1, 1)
    prev = jnp.where(lane == 0, jnp.where(row == 0, 0.0, pltpu.roll(r, 1, 0)), r)
    r = pltpu.roll(x, FFT_R - 1, 1)
    nxt = jnp.where(lane == FFT_R - 1, jnp.where(row == rows - 1, 0.0, pltpu.roll(r, rows - 1, 0)), r)
    return w0 * prev + w1 * x + w2 * nxt + b


def _hyena_kernel(cw_ref, cb_ref, skip_ref, v_ref, x1_ref, x2_ref, kf_ref,
                  la_ref, rb_ref, rc_ref, ld_ref, tr_ref, ti_ref, o_ref, *, tc):
    c_base = pl.program_id(0) * tc
    lane = lax.broadcasted_iota(jnp.int32, (1, 2 * FFT_R), 1)

    def conv_in(ref, part, b, c0):
        tiles = []
        for cc in range(2):
            ch = part * HY_CH + c_base + c0 + cc
            tiles.append(_short_conv_tile(ref[c0 + cc, b], cw_ref[0, ch], cw_ref[1, ch], cw_ref[2, ch], cb_ref[ch]))
        return jnp.concatenate(tiles, axis=1)

    def group(g, carry):
        starts = [2 * (HY_GROUP * g + j) for j in range(HY_GROUP)]
        consts = (la_ref[...], rb_ref[...], rc_ref[...], ld_ref[...], tr_ref[...], ti_ref[...])
        zs = [(conv_in(v_ref, 0, 0, c0), conv_in(v_ref, 0, 1, c0)) for c0 in starts]
        for o, g_ref in enumerate((x1_ref, x2_ref)):
            kfs = [kf_ref[o, pl.ds(c0, 2)].reshape(2 * FFT_R, 2 * FFT_R) for c0 in starts]
            ys = _conv_pairs(zs, kfs, *consts)
            nxt = []
            for c0, (z_r, z_i), (y_r, y_i) in zip(starts, zs, ys):
                sk = jnp.where(lane < FFT_R, skip_ref[o, c_base + c0], skip_ref[o, c_base + c0 + 1])
                nxt.append((conv_in(g_ref, o + 1, 0, c0) * (y_r + sk * z_r),
                            conv_in(g_ref, o + 1, 1, c0) * (y_i + sk * z_i)))
            zs = nxt
        for c0, (z_r, z_i) in zip(starts, zs):
            o_ref[c0, 0] = z_r[:, :FFT_R].astype(o_ref.dtype)
            o_ref[c0 + 1, 0] = z_r[:, FFT_R:].astype(o_ref.dtype)
            o_ref[c0, 1] = z_i[:, :FFT_R].astype(o_ref.dtype)
            o_ref[c0 + 1, 1] = z_i[:, FFT_R:].astype(o_ref.dtype)
        return carry

    lax.fori_loop(0, tc // (2 * HY_GROUP), group, 0)


def hyena_mix(u_t, kf, conv_w, conv_b, skip, consts, tc=16):
    rows, n = u_t.shape
    ch = rows // 3
    u5 = u_t.reshape(rows, 2, HALF_R, FFT_R)
    nct = ch // tc
    smem = pl.BlockSpec(memory_space=pltpu.SMEM)
    part = lambda k: pl.BlockSpec((tc, 2, HALF_R, FFT_R), lambda i: (k * nct + i, 0, 0, 0))
    full2 = lambda a: pl.BlockSpec(a.shape, lambda i: (0, 0))
    mats = [consts[k] for k in ("la", "rb", "rc", "ld", "t_r", "t_i")]
    out = pl.pallas_call(
        functools.partial(_hyena_kernel, tc=tc),
        grid=(nct,),
        in_specs=[smem, smem, smem, part(0), part(1), part(2),
                  pl.BlockSpec((2, tc, FFT_R, 2 * FFT_R), lambda i: (0, i, 0, 0))] + [full2(a) for a in mats],
        out_specs=pl.BlockSpec((tc, 2, HALF_R, FFT_R), lambda i: (i, 0, 0, 0)),
        out_shape=jax.ShapeDtypeStruct((ch, 2, HALF_R, FFT_R), BF16),
        compiler_params=_params("arbitrary"),
        name="hyena_mix",
    )(conv_w.astype(F32), conv_b.astype(F32), skip.astype(F32), u5, u5, u5, kf, *mats)
    return out.reshape(ch, n)


def _hyena_filter_kernel(delta_ref, hf_ref, hb_ref, tpos_ref, la_ref, rb_ref, tr_ref, ti_ref, kf_ref, *, tc):
    c_base = pl.program_id(1) * tc

    def taps_of(c0):
        taps = []
        for cc in range(2):
            k = jnp.concatenate([hf_ref[0, 0, c0 + cc], hb_ref[0, 0, c0 + cc]], axis=0)
            k = k * jnp.exp(-tpos_ref[...] * delta_ref[c_base + c0 + cc])
            taps.append(k * lax.rsqrt(jnp.sum(k * k, keepdims=True)))
        return jnp.concatenate(taps, axis=1).astype(BF16)

    def group(g, carry):
        starts = [2 * (HY_GROUP * g + j) for j in range(HY_GROUP)]
        a = [jnp.dot(la_ref[...], taps_of(c0), preferred_element_type=F32) for c0 in starts]
        z = [_fwd_lane_stage(ai, tr_ref[...], ti_ref[...], rb_ref[...]) for ai in a]
        for c0, zi in zip(starts, z):
            kf_ref[0, pl.ds(c0, 2)] = zi.reshape(2, FFT_R, 2 * FFT_R)
        return carry

    lax.fori_loop(0, tc // (2 * HY_GROUP), group, 0)


def hyena_filter_spectra(seq, w1, b1, w2, b2, w3, b3, freq, w4, consts, tc=16):
    t = jnp.linspace(0.0, 1.0, seq, dtype=F32)[:, None]
    omega = 2.0 * math.pi * jnp.arange(seq, dtype=F32)[:, None] / seq
    bands = jnp.linspace(1e-4, HF_BANDS - 1, HF_BANDS, dtype=F32)[None, :]
    ang = omega * bands
    z = jnp.concatenate([t, jnp.cos(ang), -jnp.sin(ang)], -1)
    fr = freq.astype(F32)
    hid = jnp.sin(fr * (z @ w1.astype(F32) + b1.astype(F32)))
    hid = jnp.sin(fr * (hid @ w2.astype(F32) + b2.astype(F32)))
    hid = jnp.sin(fr * (hid @ w3.astype(F32) + b3.astype(F32)))
    back = lambda a: jnp.concatenate([jnp.zeros_like(a[:1]), a[:0:-1]], axis=0)
    w4t = w4.astype(BF16).T
    half = HY_ORDER * HY_CH
    h_fwd = matmul(w4t[:half], hid.T.astype(BF16), F32)
    h_bwd = matmul(w4t[half:], back(hid).T.astype(BF16), F32)
    shape5 = (HY_ORDER, HY_CH, HALF_R, FFT_R)
    tpos = jnp.concatenate([t, back(t)], axis=0).reshape(FFT_R, FFT_R)
    max_decay = math.log(HF_TARGET) / HF_FAST
    min_decay = math.log(HF_TARGET) / HF_SLOW
    deltas = jnp.abs(jnp.linspace(min_decay, max_decay, HY_CH, dtype=F32))
    taps = pl.BlockSpec((1, 1, tc, HALF_R, FFT_R), lambda o, i: (0, o, i, 0, 0))
    full2 = lambda a: pl.BlockSpec(a.shape, lambda o, i: (0, 0))
    mats = [consts[k] for k in ("la_real", "rb", "t_r", "t_i")]
    return pl.pallas_call(
        functools.partial(_hyena_filter_kernel, tc=tc),
        grid=(HY_ORDER, HY_CH // tc),
        in_specs=[pl.BlockSpec(memory_space=pltpu.SMEM), taps, taps, full2(tpos)] + [full2(a) for a in mats],
        out_specs=pl.BlockSpec((1, tc, FFT_R, 2 * FFT_R), lambda o, i: (o, i, 0, 0)),
        out_shape=jax.ShapeDtypeStruct((HY_ORDER, HY_CH, FFT_R, 2 * FFT_R), F32),
        compiler_params=_params("arbitrary", "arbitrary"),
        name="hyena_filter_spectra",
    )(deltas, h_fwd.reshape((1,) + shape5), h_bwd.reshape((1,) + shape5), tpos, *mats)


def kernel(x, ev_w_in, ev_hy_conv_w, ev_hy_conv_b, ev_hf_w1, ev_hf_b1, ev_hf_w2, ev_hf_b2, ev_hf_w3, ev_hf_b3,
           ev_hf_freq, ev_hf_w4, ev_hy_skip, ev_lam_q1, ev_lam_k1, ev_lam_q2, ev_lam_k2, ev_subln_g, ev_w_out,
           od_w_in, od_conv_w, od_conv_b, od_wa, od_ba, od_wx, od_bx, od_lam, od_w_out, ln1_g, ln1_b, ln2_g,
           ln2_b, router_w, router_b, ex_w_gate, ex_w_up, ex_w_down):
    bsz, seq, d = x.shape
    n = bsz * seq
    h = x.reshape(n, d).astype(F32)
    hb = h.astype(BF16)
    for layer in range(DEPTH):
        i = layer // 2
        if layer % 2 == 0:
            w_in = ev_w_in[i]
            c0 = 3 * HY_CH
            w_t = w_in.T
            u_t = matmul_nt(w_t[:c0].astype(BF16), hb, F32)
            qt = matmul_nt((w_t[c0:c0 + DA_QK] * (DA_HEAD_DIM ** -0.5)).astype(BF16), hb, BF16)
            k = matmul(hb, w_in[:, c0 + DA_QK:c0 + 2 * DA_QK].astype(BF16), BF16)
            vt = matmul_nt(w_t[c0 + 2 * DA_QK:].astype(BF16), hb, BF16)
            consts = _dft_constants()
            kf = hyena_filter_spectra(seq, ev_hf_w1[i], ev_hf_b1[i], ev_hf_w2[i], ev_hf_b2[i],
                                      ev_hf_w3[i], ev_hf_b3[i], ev_hf_freq[i], ev_hf_w4[i], consts)
            y_hy = hyena_mix(u_t, kf, ev_hy_conv_w[i], ev_hy_conv_b[i], ev_hy_skip[i], consts)
            lambda_init = 0.8 - 0.6 * math.exp(-0.3 * layer)
            lam = (jnp.exp(jnp.sum(ev_lam_q1[i].astype(F32) * ev_lam_k1[i].astype(F32)))
                   - jnp.exp(jnp.sum(ev_lam_q2[i].astype(F32) * ev_lam_k2[i].astype(F32))) + lambda_init)
            y_da = diff_attention(qt, k.reshape(bsz, seq, DA_QK), vt,
                                  lam, ev_subln_g[i], lambda_init).reshape(n, DA_WIDTH)
            h, hb = even_out_proj(y_hy, y_da, ev_w_out[i].astype(BF16), h, ln1_g[layer], ln1_b[layer])
        else:
            u = matmul(hb, od_w_in[i].astype(BF16), F32)
            yd = rglru_bidir(u.reshape(bsz, seq, 2 * RG_WIDTH), od_conv_w[i], od_conv_b[i], od_wa[i], od_ba[i],
                             od_wx[i], od_bx[i], od_lam[i]).reshape(2, n, RG_WIDTH)
            h, hb = odd_out_proj(u, yd, od_w_out[i].astype(BF16), h, ln1_g[layer], ln1_b[layer])
        h, hb = moe_ffn(h, hb, router_w, router_b, ex_w_gate, ex_w_up, ex_w_down, layer,
                        ln2_g[layer], ln2_b[layer])
    return h.reshape(bsz, seq, d).astype(x.dtype)
```

```python
import functools
import math

import jax
import jax.numpy as jnp
from jax import lax
from jax.experimental import pallas as pl
from jax.experimental.pallas import tpu as pltpu

F32 = jnp.float32
BF16 = jnp.bfloat16

D_MODEL = 2048
DEPTH = 2
HY_CH = D_MODEL // 2
HY_ORDER = 2
HF_BANDS = 16
HF_TARGET = 1e-2
HF_FAST = 0.3
HF_SLOW = 1.5
DA_HEADS = 8
DA_HEAD_DIM = 64
DA_V_DIM = 2 * DA_HEAD_DIM
DA_QK = DA_HEADS * 2 * DA_HEAD_DIM
DA_WIDTH = DA_HEADS * DA_V_DIM
RG_WIDTH = D_MODEL
RG_BLOCKS = 8
RG_BLOCK_W = RG_WIDTH // RG_BLOCKS
RG_C = 8.0
RG_CONV = 4
N_EXPERTS = 32
N_GROUPS = 4
EXPERTS_PER_GROUP = N_EXPERTS // N_GROUPS
TOP_K = 2
D_FF = 512
DN_ALPHA = (2 * DEPTH) ** 0.25
LN_EPS = 1e-5

VMEM_LIMIT_BYTES = 56 * 1024 * 1024
MOE_ROWS = 256
MOE_CHUNKS = 4


def _params(*sem):
    return pltpu.CompilerParams(dimension_semantics=sem, vmem_limit_bytes=VMEM_LIMIT_BYTES)


def _mm_kernel(x_ref, w_ref, o_ref):
    o_ref[...] = jnp.dot(x_ref[...], w_ref[...].astype(BF16), preferred_element_type=F32).astype(o_ref.dtype)


def matmul(x, w, out_dtype, col0=0, ncols=None, tm=512, tn=1024):
    m, k = x.shape
    n = w.shape[1] if ncols is None else ncols
    tm, tn = min(tm, m), min(tn, n)
    first = col0 // tn
    return pl.pallas_call(
        _mm_kernel,
        grid=(n // tn, m // tm),
        in_specs=[pl.BlockSpec((tm, k), lambda j, i: (i, 0)),
                  pl.BlockSpec((k, tn), lambda j, i: (0, first + j))],
        out_specs=pl.BlockSpec((tm, tn), lambda j, i: (i, j)),
        out_shape=jax.ShapeDtypeStruct((m, n), out_dtype),
        compiler_params=_params("arbitrary", "arbitrary"),
        name="matmul",
    )(x, w)


def _ln_rows(z, g, b):
    mu = jnp.mean(z, axis=-1, keepdims=True)
    zc = z - mu
    var = jnp.mean(zc * zc, axis=-1, keepdims=True)
    return zc * lax.rsqrt(var + LN_EPS) * g + b


def _even_out_kernel(xa_ref, xb_ref, wa_ref, wb_ref, r_ref, g_ref, b_ref, o_ref, ob_ref):
    acc = lax.dot_general(xa_ref[...], wa_ref[...], (((0,), (0,)), ((), ())), preferred_element_type=F32)
    acc += jnp.dot(xb_ref[...], wb_ref[...], preferred_element_type=F32)
    y = _ln_rows(DN_ALPHA * r_ref[...] + acc, g_ref[...], b_ref[...])
    o_ref[...] = y
    ob_ref[...] = y.astype(BF16)


def even_out_proj(y_hy_t, y_da, w_out, resid, g, b, tm=512):
    ka, m = y_hy_t.shape
    kb = y_da.shape[1]
    d = w_out.shape[1]
    row = lambda i: (i, 0)
    fixed = lambda i: (0, 0)
    return pl.pallas_call(
        _even_out_kernel,
        grid=(m // tm,),
        in_specs=[pl.BlockSpec((ka, tm), lambda i: (0, i)), pl.BlockSpec((tm, kb), row),
                  pl.BlockSpec((ka, d), fixed), pl.BlockSpec((kb, d), fixed),
                  pl.BlockSpec((tm, d), row), pl.BlockSpec((1, d), fixed), pl.BlockSpec((1, d), fixed)],
        out_specs=[pl.BlockSpec((tm, d), row), pl.BlockSpec((tm, d), row)],
        out_shape=[jax.ShapeDtypeStruct((m, d), F32), jax.ShapeDtypeStruct((m, d), BF16)],
        compiler_params=_params("arbitrary"),
        name="even_out_proj",
    )(y_hy_t, y_da, w_out[:ka], w_out[ka:], resid, g.reshape(1, d), b.reshape(1, d))


def _odd_out_kernel(gate_ref, y0_ref, y1_ref, w_ref, r_ref, g_ref, b_ref, o_ref, ob_ref):
    x = jax.nn.gelu(gate_ref[...]) * (y0_ref[0] + y1_ref[0])
    acc = jnp.dot(x.astype(BF16), w_ref[...], preferred_element_type=F32)
    y = _ln_rows(DN_ALPHA * r_ref[...] + acc, g_ref[...], b_ref[...])
    o_ref[...] = y
    ob_ref[...] = y.astype(BF16)


def odd_out_proj(u, yd, w_out, resid, g, b, tm=256):
    m = u.shape[0]
    k, d = w_out.shape
    row = lambda i: (i, 0)
    fixed = lambda i: (0, 0)
    return pl.pallas_call(
        _odd_out_kernel,
        grid=(m // tm,),
        in_specs=[pl.BlockSpec((tm, k), row),
                  pl.BlockSpec((1, tm, k), lambda i: (0, i, 0)),
                  pl.BlockSpec((1, tm, k), lambda i: (1, i, 0)),
                  pl.BlockSpec((k, d), fixed),
                  pl.BlockSpec((tm, d), row), pl.BlockSpec((1, d), fixed), pl.BlockSpec((1, d), fixed)],
        out_specs=[pl.BlockSpec((tm, d), row), pl.BlockSpec((tm, d), row)],
        out_shape=[jax.ShapeDtypeStruct((m, d), F32), jax.ShapeDtypeStruct((m, d), BF16)],
        compiler_params=_params("arbitrary"),
        name="odd_out_proj",
    )(u, yd, yd, w_out, resid, g.reshape(1, d), b.reshape(1, d))


def _combine_ln_kernel(h_ref, y0_ref, y1_ref, gate_ref, g_ref, b_ref, o_ref, ob_ref):
    gate = gate_ref[...]
    ffn = gate[:, 0:1] * y0_ref[...].astype(F32) + gate[:, 1:2] * y1_ref[...].astype(F32)
    y = _ln_rows(DN_ALPHA * h_ref[...] + ffn, g_ref[...], b_ref[...])
    o_ref[...] = y
    ob_ref[...] = y.astype(BF16)


def combine_ln(h, y0, y1, gate, g, b, tm=512):
    m, d = h.shape
    row = lambda i: (i, 0)
    fixed = lambda i: (0, 0)
    return pl.pallas_call(
        _combine_ln_kernel,
        grid=(m // tm,),
        in_specs=[pl.BlockSpec((tm, d), row), pl.BlockSpec((tm, d), row), pl.BlockSpec((tm, d), row),
                  pl.BlockSpec((tm, TOP_K), row), pl.BlockSpec((1, d), fixed), pl.BlockSpec((1, d), fixed)],
        out_specs=[pl.BlockSpec((tm, d), row), pl.BlockSpec((tm, d), row)],
        out_shape=[jax.ShapeDtypeStruct((m, d), F32), jax.ShapeDtypeStruct((m, d), BF16)],
        compiler_params=_params("arbitrary"),
        name="combine_ln",
    )(h, y0, y1, gate, g.reshape(1, d), b.reshape(1, d))


ONES_ROWS = 16
POS_SPLIT = 16


def _attn_kernel(slopes_ref, lam_ref, qt_ref, k_ref, vt_ref, fq_ref, fk_ref, g_ref, o_ref, m_s, acc_s, sa_s, sb_s,
                 *, tq, tk, seq, out_scale):
    head = pl.program_id(1)
    i0 = pl.program_id(2) * tq
    slope = slopes_ref[head]
    lam = lam_ref[0]
    qt = qt_ref[...]
    row = lax.broadcasted_iota(jnp.int32, qt.shape, 0)
    zero = jnp.zeros_like(qt)
    q2 = jnp.concatenate([jnp.where(row < DA_HEAD_DIM, qt, zero),
                          jnp.where(row >= DA_HEAD_DIM, qt, zero)], axis=1)
    fq = fq_ref[0]
    fq2 = jnp.concatenate([fq, fq], axis=1)
    q_aug = jnp.concatenate([q2, fq2], axis=0)
    fk = fk_ref[0]
    fk_neg = -fk
    ones = jnp.ones((ONES_ROWS, tk), BF16)
    m_s[...] = jnp.full(m_s.shape, -jnp.inf, F32)
    acc_s[...] = jnp.zeros(acc_s.shape, F32)
    j_diag = i0 // tk

    def update(j, s, shift):
        off = pl.multiple_of(j * tk, tk)
        va = jnp.concatenate([vt_ref[:, pl.ds(off, tk)], ones], axis=0)
        m_prev = m_s[...]
        m_new = jnp.maximum(m_prev, jnp.max(s, axis=0, keepdims=True) - shift)
        p = jnp.exp(s - (m_new + shift))
        alpha = jnp.exp(m_prev - m_new)
        acc_s[...] = alpha * acc_s[...] + jnp.dot(va, p.astype(BF16), preferred_element_type=F32)
        m_s[...] = m_new

    def chunk_of(t):
        return t + (t >= j_diag).astype(jnp.int32)

    def scores(t):
        j = chunk_of(t)
        off = pl.multiple_of(j * tk, tk)
        ka = jnp.concatenate([k_ref[0, pl.ds(off, tk), :], jnp.where(j < j_diag, fk, fk_neg)], axis=1)
        return jnp.dot(ka, q_aug, preferred_element_type=F32)

    def consume(s_ref, t):
        j = chunk_of(t)
        update(j, s_ref[...], slope * jnp.abs(i0 - j * tk).astype(F32))

    j0 = pl.multiple_of(j_diag * tk, tk)
    dist = jnp.abs(lax.broadcasted_iota(jnp.int32, (tk, tq), 0) + (j0 - i0)
                   - lax.broadcasted_iota(jnp.int32, (tk, tq), 1)).astype(F32)
    bias = slope * dist
    s_diag = jnp.dot(k_ref[0, pl.ds(j0, tk), :], q2, preferred_element_type=F32)
    sa_s[...] = s_diag - jnp.concatenate([bias, bias], axis=1)
    sb_s[...] = scores(jnp.int32(0))
    update(j_diag, sa_s[...], 0.0)
    n_off = seq // tk - 1

    def pair(t, carry):
        sa_s[...] = scores(2 * t + 1)
        consume(sb_s, 2 * t)
        sb_s[...] = scores(2 * t + 2)
        consume(sa_s, 2 * t + 1)
        return carry

    lax.fori_loop(0, n_off // 2, pair, 0)
    consume(sb_s, jnp.int32(n_off - 1))

    acc = acc_s[...]
    o1 = acc[:DA_V_DIM, :tq] / acc[DA_V_DIM:DA_V_DIM + 1, :tq]
    o2 = acc[:DA_V_DIM, tq:] / acc[DA_V_DIM:DA_V_DIM + 1, tq:]
    o = o1 - lam * o2
    o = o * lax.rsqrt(jnp.mean(o * o, axis=0, keepdims=True) + LN_EPS) * g_ref[...]
    o_ref[0] = (o * out_scale).T.astype(o_ref.dtype)


def diff_attention(qt, k, vt, lam, subln_g, lambda_init, tq=512, tk=512):
    bsz, seq, width = k.shape
    nq = seq // tq
    slopes = 2.0 ** (-(8.0 / DA_HEADS) * jnp.arange(1, DA_HEADS + 1, dtype=F32))

    def split(n):
        pos = jnp.arange(n, dtype=jnp.int32)
        return (pos // POS_SPLIT * POS_SPLIT).astype(F32), (pos % POS_SPLIT).astype(F32)

    qhi, qlo = split(tq)
    khi, klo = split(tk)
    sl = slopes[:, None]
    fq = jnp.zeros((DA_HEADS, DA_V_DIM, tq), F32)
    fq = fq.at[:, 0].set(1.0).at[:, 1].set(1.0).at[:, 2].set(-sl * qhi).at[:, 3].set(-sl * qlo)
    fk = jnp.zeros((DA_HEADS, tk, DA_V_DIM), F32)
    fk = fk.at[:, :, 0].set(sl * khi).at[:, :, 1].set(sl * klo).at[:, :, 2].set(1.0).at[:, :, 3].set(1.0)
    kern = functools.partial(_attn_kernel, tq=tq, tk=tk, seq=seq, out_scale=1.0 - lambda_init)
    smem = pl.BlockSpec(memory_space=pltpu.SMEM)
    return pl.pallas_call(
        kern,
        grid=(bsz, DA_HEADS, seq // tq),
        in_specs=[smem, smem,
                  pl.BlockSpec((DA_V_DIM, tq), lambda b, h, i: (h, b * nq + i)),
                  pl.BlockSpec((1, seq, DA_V_DIM), lambda b, h, i: (b, 0, h)),
                  pl.BlockSpec((DA_V_DIM, seq), lambda b, h, i: (h, b)),
                  pl.BlockSpec((1, DA_V_DIM, tq), lambda b, h, i: (h, 0, 0)),
                  pl.BlockSpec((1, tk, DA_V_DIM), lambda b, h, i: (h, 0, 0)),
                  pl.BlockSpec((DA_V_DIM, 1), lambda b, h, i: (0, 0))],
        out_specs=pl.BlockSpec((1, tq, DA_V_DIM), lambda b, h, i: (b, i, h)),
        out_shape=jax.ShapeDtypeStruct((bsz, seq, width), BF16),
        scratch_shapes=[pltpu.VMEM((1, 2 * tq), F32), pltpu.VMEM((DA_V_DIM + ONES_ROWS, 2 * tq), F32),
                        pltpu.VMEM((tk, 2 * tq), F32), pltpu.VMEM((tk, 2 * tq), F32)],
        compiler_params=_params("arbitrary", "arbitrary", "arbitrary"),
        name="diff_attention",
    )(slopes, lam.reshape(1).astype(F32), qt, k, vt, fq.astype(BF16), fk.astype(BF16),
      subln_g.reshape(DA_V_DIM, 1).astype(F32))


HALO = 8


def _rglru_kernel(xr_ref, prev_ref, next_ref, cw_ref, cb_ref, wa_ref, wx_ref, ba_ref, bx_ref, kk_ref, y_ref,
                  a_s, b_s, h_s, *, ts, nt):
    direction = pl.program_id(0)
    t = pl.program_id(2)
    chunk = t + direction * (nt - 1 - 2 * t)

    @pl.when(t == 0)
    def _():
        h_s[...] = jnp.zeros(h_s.shape, F32)

    xf = jnp.concatenate([jnp.where(chunk == 0, 0.0, prev_ref[0]), xr_ref[0],
                          jnp.where(chunk == nt - 1, 0.0, next_ref[0])], axis=0)
    rows = ts + 2 * HALO

    def tap(j):
        shifted = xf if j == 2 else pltpu.roll(xf, (2 - j) % rows, 0)
        return cw_ref[j:j + 1, :] * shifted[HALO:HALO + ts]

    x = cb_ref[...] + sum(tap(j) for j in range(RG_CONV))
    xb = x.astype(BF16)
    xh = 0.5 * x
    for n in range(RG_BLOCKS):
        cols = slice(n * RG_BLOCK_W, (n + 1) * RG_BLOCK_W)
        xs = xb[:, cols]
        tr = jnp.tanh(jnp.dot(xs, wa_ref[0, n], preferred_element_type=F32) + ba_ref[0, :, cols])
        ti = jnp.tanh(jnp.dot(xs, wx_ref[0, n], preferred_element_type=F32) + bx_ref[0, :, cols])
        kk = kk_ref[0, :, cols]
        a = jnp.exp2(kk * tr + kk)
        a_s[:, cols] = a
        s = 1.0 - a * a
        root = jnp.where(s > 0.0, s * lax.rsqrt(s), 0.0)
        b_s[:, cols] = root * (ti * xh[:, cols] + xh[:, cols])

    def step(k, h):
        row = jnp.where(direction == 0, k, ts - 1 - k)
        h = a_s[pl.ds(row, 1), :] * h + b_s[pl.ds(row, 1), :]
        y_ref[0, 0, pl.ds(row, 1), :] = h
        return h

    h_s[...] = lax.fori_loop(0, ts, step, h_s[...], unroll=8)


def rglru_bidir(u, conv_w, conv_b, wa, ba, wx, bx, lam, ts=256):
    bsz, seq, w2 = u.shape
    w = w2 // 2
    nt = seq // ts
    per = ts // HALO
    kk = (-0.5 * RG_C * math.log2(math.e)) * jax.nn.softplus(-lam.astype(F32)).reshape(2, 1, w)
    chunk = lambda d, t: t + d * (nt - 1 - 2 * t)
    dmap4 = lambda d, b, t: (d, 0, 0, 0)
    dmap3 = lambda d, b, t: (d, 0, 0)
    fixed = lambda d, b, t: (0, 0)
    return pl.pallas_call(
        functools.partial(_rglru_kernel, ts=ts, nt=nt),
        grid=(2, bsz, nt),
        in_specs=[pl.BlockSpec((1, ts, w), lambda d, b, t: (b, chunk(d, t), 1)),
                  pl.BlockSpec((1, HALO, w), lambda d, b, t: (b, jnp.maximum(chunk(d, t) * per - 1, 0), 1)),
                  pl.BlockSpec((1, HALO, w),
                               lambda d, b, t: (b, jnp.minimum((chunk(d, t) + 1) * per, seq // HALO - 1), 1)),
                  pl.BlockSpec((RG_CONV, w), fixed), pl.BlockSpec((1, w), fixed),
                  pl.BlockSpec((1, RG_BLOCKS, RG_BLOCK_W, RG_BLOCK_W), dmap4),
                  pl.BlockSpec((1, RG_BLOCKS, RG_BLOCK_W, RG_BLOCK_W), dmap4),
                  pl.BlockSpec((1, 1, w), dmap3), pl.BlockSpec((1, 1, w), dmap3), pl.BlockSpec((1, 1, w), dmap3)],
        out_specs=pl.BlockSpec((1, 1, ts, w), lambda d, b, t: (d, b, chunk(d, t), 0)),
        out_shape=jax.ShapeDtypeStruct((2, bsz, seq, w), F32),
        scratch_shapes=[pltpu.VMEM((ts, w), F32), pltpu.VMEM((ts, w), F32),
                        pltpu.VMEM((1, w), F32)],
        compiler_params=_params("arbitrary", "arbitrary", "arbitrary"),
        name="rglru_bidir",
    )(u, u, u, conv_w.astype(F32), conv_b.astype(F32).reshape(1, w), (0.5 * wa).astype(BF16),
      (0.5 * wx).astype(BF16), 0.5 * ba.astype(F32).reshape(2, 1, w), 0.5 * bx.astype(F32).reshape(2, 1, w), kk)


def _router_kernel(h_ref, wt_ref, b_ref, tri_ref, e_ref, g_ref, r_ref, cnt_ref, seen_s):
    logits = lax.dot_general(wt_ref[...], h_ref[...], (((1,), (1,)), ((), ())),
                             precision=lax.Precision.HIGHEST, preferred_element_type=F32)
    tm = logits.shape[1]
    s = jax.nn.sigmoid(logits).reshape(N_GROUPS, EXPERTS_PER_GROUP, tm)
    sel = s + b_ref[...]
    idx = lax.broadcasted_iota(jnp.int32, sel.shape, 1)
    big = jnp.int32(EXPERTS_PER_GROUP)
    v1 = jnp.max(sel, axis=1, keepdims=True)
    i1 = jnp.min(jnp.where(sel == v1, idx, big), axis=1, keepdims=True)
    rest = jnp.where(idx == i1, -jnp.inf, sel)
    v2 = jnp.max(rest, axis=1, keepdims=True)
    i2 = jnp.min(jnp.where(rest == v2, idx, big), axis=1, keepdims=True)
    score = v1 + v2
    gidx = lax.broadcasted_iota(jnp.int32, score.shape, 0)
    best = jnp.max(score, axis=0, keepdims=True)
    grp = jnp.min(jnp.where(score == best, gidx, jnp.int32(N_GROUPS)), axis=0, keepdims=True)
    pick = gidx == grp
    l1 = jnp.sum(jnp.where(pick, i1, 0), axis=0)
    l2 = jnp.sum(jnp.where(pick, i2, 0), axis=0)
    s_g = jnp.sum(jnp.where(pick, s, 0.0), axis=0)
    eidx = lax.broadcasted_iota(jnp.int32, s_g.shape, 0)
    w1 = jnp.sum(jnp.where(eidx == l1, s_g, 0.0), axis=0, keepdims=True)
    w2 = jnp.sum(jnp.where(eidx == l2, s_g, 0.0), axis=0, keepdims=True)
    tot = w1 + w2
    base = grp[0] * EXPERTS_PER_GROUP
    e1, e2 = base + l1, base + l2
    e_ref[...] = jnp.concatenate([e1, e2], axis=0)
    g_ref[...] = jnp.concatenate([w1 / tot, w2 / tot], axis=0)

    @pl.when(pl.program_id(0) == 0)
    def _():
        seen_s[...] = jnp.zeros(seen_s.shape, F32)

    eall = lax.broadcasted_iota(jnp.int32, (N_EXPERTS, tm), 0)
    hit1, hit2 = eall == e1, eall == e2
    both = jnp.where(hit1, 1.0, jnp.where(hit2, 1.0, 0.0))
    incl = jnp.dot(both.astype(BF16), tri_ref[...], preferred_element_type=F32)
    before = incl - both + seen_s[...]
    r_ref[...] = jnp.concatenate([jnp.sum(jnp.where(hit1, before, 0.0), axis=0, keepdims=True),
                                  jnp.sum(jnp.where(hit2, before, 0.0), axis=0, keepdims=True)],
                                 axis=0).astype(jnp.int32)
    seen_s[...] = seen_s[...] + incl[:, tm - 1:tm]
    cnt_ref[...] = seen_s[...].astype(jnp.int32)


def route(h, router_w, router_b, tm=1024):
    n, d = h.shape
    tri = (jnp.arange(tm)[:, None] <= jnp.arange(tm)[None, :]).astype(BF16)
    tile = pl.BlockSpec((TOP_K, tm), lambda i: (0, i))
    return pl.pallas_call(
        _router_kernel,
        grid=(n // tm,),
        in_specs=[pl.BlockSpec((tm, d), lambda i: (i, 0)),
                  pl.BlockSpec((N_EXPERTS, d), lambda i: (0, 0)),
                  pl.BlockSpec((N_GROUPS, EXPERTS_PER_GROUP, 1), lambda i: (0, 0, 0)),
                  pl.BlockSpec((tm, tm), lambda i: (0, 0))],
        out_specs=[tile, tile, tile, pl.BlockSpec((N_EXPERTS, 1), lambda i: (0, 0))],
        out_shape=[jax.ShapeDtypeStruct((TOP_K, n), jnp.int32), jax.ShapeDtypeStruct((TOP_K, n), F32),
                   jax.ShapeDtypeStruct((TOP_K, n), jnp.int32), jax.ShapeDtypeStruct((N_EXPERTS, 1), jnp.int32)],
        scratch_shapes=[pltpu.VMEM((N_EXPERTS, 1), F32)],
        compiler_params=_params("arbitrary"),
        name="router",
    )(h, router_w.astype(F32).T, router_b.astype(F32).reshape(N_GROUPS, EXPERTS_PER_GROUP, 1), tri)


def _experts_kernel(blk_exp_ref, n_used_ref, x_ref, wg_ref, wu_ref, wd_ref, *rest):
    o_ref, wg_s, wu_s, wd_s = rest[-4:]
    i = pl.program_id(0)
    expert = blk_exp_ref[i]
    prev = blk_exp_ref[jnp.maximum(i - 1, 0)]

    @pl.when((i == 0) | (expert != prev))
    def _():
        wg_s[...] = wg_ref[0, 0].astype(BF16)
        wu_s[...] = wu_ref[0, 0].astype(BF16)
        wd_s[...] = wd_ref[0, 0].astype(BF16)

    @pl.when(i < n_used_ref[0])
    def _():
        x = x_ref[...]
        hg = jnp.dot(x, wg_s[...], preferred_element_type=F32)
        hu = jnp.dot(x, wu_s[...], preferred_element_type=F32)
        hidden = (jax.nn.silu(hg) * hu).astype(BF16)
        o_ref[...] = jnp.dot(hidden, wd_s[...], preferred_element_type=F32).astype(o_ref.dtype)

    @pl.when(i >= n_used_ref[0])
    def _():
        o_ref[...] = jnp.zeros(o_ref.shape, o_ref.dtype)


def experts(xs, blk_exp, n_used, w_gate, w_up, w_down, layer, yb, first_block, total_blocks):
    rows, d = xs.shape
    f = w_gate.shape[3]
    in_specs = [pl.BlockSpec((MOE_ROWS, d), lambda i, be, nu: (i, 0)),
                pl.BlockSpec((1, 1, d, f), lambda i, be, nu: (layer, be[i], 0, 0)),
                pl.BlockSpec((1, 1, d, f), lambda i, be, nu: (layer, be[i], 0, 0)),
                pl.BlockSpec((1, 1, f, d), lambda i, be, nu: (layer, be[i], 0, 0))]
    operands = [blk_exp, n_used, xs, w_gate, w_up, w_down]
    aliases = {}
    if yb is not None:
        in_specs.append(pl.BlockSpec(memory_space=pl.ANY))
        aliases = {len(operands): 0}
        operands.append(yb)
    grid_spec = pltpu.PrefetchScalarGridSpec(
        num_scalar_prefetch=2,
        grid=(rows // MOE_ROWS,),
        in_specs=in_specs,
        out_specs=pl.BlockSpec((MOE_ROWS, d), lambda i, be, nu: (first_block + i, 0)),
        scratch_shapes=[pltpu.VMEM((d, f), BF16), pltpu.VMEM((d, f), BF16), pltpu.VMEM((f, d), BF16)],
    )
    return pl.pallas_call(
        _experts_kernel,
        grid_spec=grid_spec,
        out_shape=jax.ShapeDtypeStruct((total_blocks * MOE_ROWS, d), BF16),
        input_output_aliases=aliases,
        compiler_params=_params("arbitrary"),
        name="experts",
    )(*operands)


def moe_ffn(h, hb, router_w, router_b, w_gate, w_up, w_down, layer, ln_g, ln_b):
    n, d = h.shape
    a = n * TOP_K
    e_idx, gate, rank, counts = route(h, router_w, router_b)
    counts = counts[:, 0]
    padded = (counts + MOE_ROWS - 1) // MOE_ROWS * MOE_ROWS
    pad_end = jnp.cumsum(padded)
    pad_start = pad_end - padded
    experts_iota = jnp.arange(N_EXPERTS, dtype=jnp.int32)[:, None, None]
    dest = jnp.sum(jnp.where(e_idx[None] == experts_iota, pad_start[:, None, None], 0), axis=0) + rank
    n_blocks = a // MOE_ROWS + N_EXPERTS
    p = n_blocks * MOE_ROWS
    tok = jnp.broadcast_to(jnp.arange(n, dtype=jnp.int32)[None, :], (TOP_K, n))
    row_tok = jnp.zeros((p,), jnp.int32).at[dest.reshape(a)].set(tok.reshape(a))
    blk_start = jnp.arange(n_blocks, dtype=jnp.int32)[:, None] * MOE_ROWS
    blk_exp = jnp.minimum(jnp.sum((pad_end[None, :] <= blk_start).astype(jnp.int32), axis=1), N_EXPERTS - 1)
    n_used = (pad_end[-1:] // MOE_ROWS).astype(jnp.int32)
    per = n_blocks // MOE_CHUNKS
    yb = None
    for c in range(MOE_CHUNKS):
        xs = hb.at[row_tok[c * per * MOE_ROWS:(c + 1) * per * MOE_ROWS]].get(mode="promise_in_bounds")
        yb = experts(xs, blk_exp[c * per:(c + 1) * per], n_used - c * per, w_gate, w_up, w_down, layer,
                     yb, c * per, n_blocks)
    y0 = yb.at[dest[0]].get(mode="promise_in_bounds")
    y1 = yb.at[dest[1]].get(mode="promise_in_bounds")
    return combine_ln(h, y0, y1, gate.T, ln_g, ln_b)


FFT_R = 128
HALF_R = FFT_R // 2
HY_GROUP = 4


def _proj_t_kernel(w_ref, x_ref, o_ref, *, scale):
    acc = lax.dot_general(w_ref[...].astype(BF16), x_ref[...], (((0,), (1,)), ((), ())),
                          preferred_element_type=F32)
    o_ref[...] = (acc * scale).astype(o_ref.dtype)


def proj_t(w, col0, ncols, x, out_dtype, scale=1.0, tm=512, tn=1024):
    k = w.shape[0]
    n = x.shape[0]
    first = col0 // tm
    return pl.pallas_call(
        functools.partial(_proj_t_kernel, scale=scale),
        grid=(ncols // tm, n // tn),
        in_specs=[pl.BlockSpec((k, tm), lambda i, j: (0, first + i)),
                  pl.BlockSpec((tn, k), lambda i, j: (j, 0))],
        out_specs=pl.BlockSpec((tm, tn), lambda i, j: (i, j)),
        out_shape=jax.ShapeDtypeStruct((ncols, n), out_dtype),
        compiler_params=_params("arbitrary", "arbitrary"),
        name="proj_t",
    )(w, x)


def _dft_constants():
    idx = jnp.arange(FFT_R, dtype=jnp.int32)
    prod = idx[:, None] * idx[None, :]
    ang = (prod % FFT_R).astype(F32) * (2.0 * math.pi / FFT_R)
    f_r, f_i = jnp.cos(ang), -jnp.sin(ang)
    ang_t = prod.astype(F32) * (2.0 * math.pi / (FFT_R * FFT_R))
    g_r, g_i = f_r[:HALF_R], -f_i[:HALF_R]
    return dict(
        la=jnp.block([[f_r[:, :HALF_R], -f_i[:, :HALF_R]], [f_i[:, :HALF_R], f_r[:, :HALF_R]]]).astype(BF16),
        la_real=jnp.concatenate([f_r, f_i], axis=0).astype(BF16),
        rb=jnp.block([[f_r, f_i], [-f_i, f_r]]).astype(BF16),
        rc=jnp.block([[f_r, -f_i], [f_i, f_r]]).astype(BF16),
        ld=(jnp.block([[g_r, -g_i], [g_i, g_r]]) / (FFT_R * FFT_R)).astype(BF16),
        t_r=jnp.cos(ang_t), t_i=-jnp.sin(ang_t))


def _rows_to_lanes(x):
    return jnp.concatenate([x[:FFT_R], x[FFT_R:]], axis=1)


def _fwd_lane_stage(a, t_r, t_i, rb):
    a_r, a_i = a[:FFT_R], a[FFT_R:]
    t2_r = jnp.concatenate([t_r, t_r], axis=1)
    t2_i = jnp.concatenate([t_i, t_i], axis=1)
    b_r = a_r * t2_r - a_i * t2_i
    b_i = a_r * t2_i + a_i * t2_r
    lhs = jnp.concatenate([jnp.concatenate([b_r[:, :FFT_R], b_i[:, :FFT_R]], axis=1),
                           jnp.concatenate([b_r[:, FFT_R:], b_i[:, FFT_R:]], axis=1)], axis=0)
    return jnp.dot(lhs.astype(BF16), rb, preferred_element_type=F32)


def _conv_pairs(ms, kfs, la, rb, rc, ld, t_r, t_i):
    a = [jnp.dot(la, jnp.concatenate([m_r, m_i], axis=0).astype(BF16), preferred_element_type=F32)
         for m_r, m_i in ms]
    z = [_fwd_lane_stage(ai, t_r, t_i, rb) for ai in a]
    c = []
    for zi, kf in zip(z, kfs):
        z_r, z_i, k_r, k_i = zi[:, :FFT_R], zi[:, FFT_R:], kf[:, :FFT_R], kf[:, FFT_R:]
        y = jnp.concatenate([z_r * k_r - z_i * k_i, z_r * k_i + z_i * k_r], axis=1)
        c.append(jnp.dot(y.astype(BF16), rc, preferred_element_type=F32))
    t2_r = jnp.concatenate([t_r, t_r], axis=0)
    t2_i = jnp.concatenate([t_i, t_i], axis=0)
    out = []
    for ci in c:
        c_r, c_i = ci[:, :FFT_R], ci[:, FFT_R:]
        d_r = c_r * t2_r + c_i * t2_i
        d_i = c_i * t2_r - c_r * t2_i
        rhs = jnp.concatenate([_rows_to_lanes(d_r), _rows_to_lanes(d_i)], axis=0)
        x = jnp.dot(ld, rhs.astype(BF16), preferred_element_type=F32)
        out.append((x[:HALF_R], x[HALF_R:]))
    return out


def _short_conv_tile(x, w0, w1, w2, b):
    rows = x.shape[0]
    lane = lax.broadcasted_iota(jnp.int32, x.shape, 1)
    row = lax.broadcasted_iota(jnp.int32, x.shape, 0)
    r = pltpu.roll(x, 1, 1)
    prev = jnp.where(lane == 0, jnp.where(row == 0, 0.0, pltpu.roll(r, 1, 0)), r)
    r = pltpu.roll(x, FFT_R - 1, 1)
    nxt = jnp.where(lane == FFT_R - 1, jnp.where(row == rows - 1, 0.0, pltpu.roll(r, rows - 1, 0)), r)
    return w0 * prev + w1 * x + w2 * nxt + b


def _hyena_kernel(cw_ref, cb_ref, skip_ref, v_ref, x1_ref, x2_ref, kf_ref,
                  la_ref, rb_ref, rc_ref, ld_ref, tr_ref, ti_ref, o_ref, *, tc):
    c_base = pl.program_id(0) * tc
    lane = lax.broadcasted_iota(jnp.int32, (1, 2 * FFT_R), 1)

    def conv_in(ref, part, b, c0):
        tiles = []
        for cc in range(2):
            ch = part * HY_CH + c_base + c0 + cc
            tiles.append(_short_conv_tile(ref[c0 + cc, b], cw_ref[0, ch], cw_ref[1, ch], cw_ref[2, ch], cb_ref[ch]))
        return jnp.concatenate(tiles, axis=1)

    def group(g, carry):
        starts = [2 * (HY_GROUP * g + j) for j in range(HY_GROUP)]
        consts = (la_ref[...], rb_ref[...], rc_ref[...], ld_ref[...], tr_ref[...], ti_ref[...])
        zs = [(conv_in(v_ref, 0, 0, c0), conv_in(v_ref, 0, 1, c0)) for c0 in starts]
        for o, g_ref in enumerate((x1_ref, x2_ref)):
            kfs = [kf_ref[o, pl.ds(c0, 2)].reshape(2 * FFT_R, 2 * FFT_R) for c0 in starts]
            ys = _conv_pairs(zs, kfs, *consts)
            nxt = []
            for c0, (z_r, z_i), (y_r, y_i) in zip(starts, zs, ys):
                sk = jnp.where(lane < FFT_R, skip_ref[o, c_base + c0], skip_ref[o, c_base + c0 + 1])
                nxt.append((conv_in(g_ref, o + 1, 0, c0) * (y_r + sk * z_r),
                            conv_in(g_ref, o + 1, 1, c0) * (y_i + sk * z_i)))
            zs = nxt
        for c0, (z_r, z_i) in zip(starts, zs):
            o_ref[c0, 0] = z_r[:, :FFT_R].astype(o_ref.dtype)
            o_ref[c0 + 1, 0] = z_r[:, FFT_R:].astype(o_ref.dtype)
            o_ref[c0, 1] = z_i[:, :FFT_R].astype(o_ref.dtype)
            o_ref[c0 + 1, 1] = z_i[:, FFT_R:].astype(o_ref.dtype)
        return carry

    lax.fori_loop(0, tc // (2 * HY_GROUP), group, 0)


def hyena_mix(u_t, kf, conv_w, conv_b, skip, consts, tc=16):
    rows, n = u_t.shape
    ch = rows // 3
    u5 = u_t.reshape(rows, 2, HALF_R, FFT_R)
    nct = ch // tc
    smem = pl.BlockSpec(memory_space=pltpu.SMEM)
    part = lambda k: pl.BlockSpec((tc, 2, HALF_R, FFT_R), lambda i: (k * nct + i, 0, 0, 0))
    full2 = lambda a: pl.BlockSpec(a.shape, lambda i: (0, 0))
    mats = [consts[k] for k in ("la", "rb", "rc", "ld", "t_r", "t_i")]
    out = pl.pallas_call(
        functools.partial(_hyena_kernel, tc=tc),
        grid=(nct,),
        in_specs=[smem, smem, smem, part(0), part(1), part(2),
                  pl.BlockSpec((2, tc, FFT_R, 2 * FFT_R), lambda i: (0, i, 0, 0))] + [full2(a) for a in mats],
        out_specs=pl.BlockSpec((tc, 2, HALF_R, FFT_R), lambda i: (i, 0, 0, 0)),
        out_shape=jax.ShapeDtypeStruct((ch, 2, HALF_R, FFT_R), BF16),
        compiler_params=_params("arbitrary"),
        name="hyena_mix",
    )(conv_w.astype(F32), conv_b.astype(F32), skip.astype(F32), u5, u5, u5, kf, *mats)
    return out.reshape(ch, n)


def _hyena_filter_kernel(delta_ref, hf_ref, hb_ref, tpos_ref, la_ref, rb_ref, tr_ref, ti_ref, kf_ref, *, tc):
    c_base = pl.program_id(1) * tc

    def taps_of(c0):
        taps = []
        for cc in range(2):
            k = jnp.concatenate([hf_ref[0, 0, c0 + cc], hb_ref[0, 0, c0 + cc]], axis=0)
            k = k * jnp.exp(-tpos_ref[...] * delta_ref[c_base + c0 + cc])
            taps.append(k * lax.rsqrt(jnp.sum(k * k, keepdims=True)))
        return jnp.concatenate(taps, axis=1).astype(BF16)

    def group(g, carry):
        starts = [2 * (HY_GROUP * g + j) for j in range(HY_GROUP)]
        a = [jnp.dot(la_ref[...], taps_of(c0), preferred_element_type=F32) for c0 in starts]
        z = [_fwd_lane_stage(ai, tr_ref[...], ti_ref[...], rb_ref[...]) for ai in a]
        for c0, zi in zip(starts, z):
            kf_ref[0, pl.ds(c0, 2)] = zi.reshape(2, FFT_R, 2 * FFT_R)
        return carry

    lax.fori_loop(0, tc // (2 * HY_GROUP), group, 0)


def hyena_filter_spectra(seq, w1, b1, w2, b2, w3, b3, freq, w4, consts, tc=16):
    t = jnp.linspace(0.0, 1.0, seq, dtype=F32)[:, None]
    omega = 2.0 * math.pi * jnp.arange(seq, dtype=F32)[:, None] / seq
    bands = jnp.linspace(1e-4, HF_BANDS - 1, HF_BANDS, dtype=F32)[None, :]
    ang = omega * bands
    z = jnp.concatenate([t, jnp.cos(ang), -jnp.sin(ang)], -1)
    fr = freq.astype(F32)
    hid = jnp.sin(fr * (z @ w1.astype(F32) + b1.astype(F32)))
    hid = jnp.sin(fr * (hid @ w2.astype(F32) + b2.astype(F32)))
    hid = jnp.sin(fr * (hid @ w3.astype(F32) + b3.astype(F32)))
    back = lambda a: jnp.concatenate([jnp.zeros_like(a[:1]), a[:0:-1]], axis=0)
    w4t = w4.astype(BF16).T
    half = HY_ORDER * HY_CH
    h_fwd = matmul(w4t[:half], hid.T.astype(BF16), F32)
    h_bwd = matmul(w4t[half:], back(hid).T.astype(BF16), F32)
    shape5 = (HY_ORDER, HY_CH, HALF_R, FFT_R)
    tpos = jnp.concatenate([t, back(t)], axis=0).reshape(FFT_R, FFT_R)
    max_decay = math.log(HF_TARGET) / HF_FAST
    min_decay = math.log(HF_TARGET) / HF_SLOW
    deltas = jnp.abs(jnp.linspace(min_decay, max_decay, HY_CH, dtype=F32))
    taps = pl.BlockSpec((1, 1, tc, HALF_R, FFT_R), lambda o, i: (0, o, i, 0, 0))
    full2 = lambda a: pl.BlockSpec(a.shape, lambda o, i: (0, 0))
    mats = [consts[k] for k in ("la_real", "rb", "t_r", "t_i")]
    return pl.pallas_call(
        functools.partial(_hyena_filter_kernel, tc=tc),
        grid=(HY_ORDER, HY_CH // tc),
        in_specs=[pl.BlockSpec(memory_space=pltpu.SMEM), taps, taps, full2(tpos)] + [full2(a) for a in mats],
        out_specs=pl.BlockSpec((1, tc, FFT_R, 2 * FFT_R), lambda o, i: (o, i, 0, 0)),
        out_shape=jax.ShapeDtypeStruct((HY_ORDER, HY_CH, FFT_R, 2 * FFT_R), F32),
        compiler_params=_params("arbitrary", "arbitrary"),
        name="hyena_filter_spectra",
    )(deltas, h_fwd.reshape((1,) + shape5), h_bwd.reshape((1,) + shape5), tpos, *mats)


def kernel(x, ev_w_in, ev_hy_conv_w, ev_hy_conv_b, ev_hf_w1, ev_hf_b1, ev_hf_w2, ev_hf_b2, ev_hf_w3, ev_hf_b3,
           ev_hf_freq, ev_hf_w4, ev_hy_skip, ev_lam_q1, ev_lam_k1, ev_lam_q2, ev_lam_k2, ev_subln_g, ev_w_out,
           od_w_in, od_conv_w, od_conv_b, od_wa, od_ba, od_wx, od_bx, od_lam, od_w_out, ln1_g, ln1_b, ln2_g,
           ln2_b, router_w, router_b, ex_w_gate, ex_w_up, ex_w_down):
    bsz, seq, d = x.shape
    n = bsz * seq
    h = x.reshape(n, d).astype(F32)
    hb = h.astype(BF16)
    for layer in range(DEPTH):
        i = layer // 2
        if layer % 2 == 0:
            w_in = ev_w_in[i]
            c0 = 3 * HY_CH
            u_t = proj_t(w_in, 0, c0, hb, F32)
            qt = proj_t(w_in, c0, DA_QK, hb, BF16, scale=DA_HEAD_DIM ** -0.5)
            k = matmul(hb, w_in, BF16, col0=c0 + DA_QK, ncols=DA_QK)
            vt = proj_t(w_in, c0 + 2 * DA_QK, DA_WIDTH, hb, BF16)
            consts = _dft_constants()
            kf = hyena_filter_spectra(seq, ev_hf_w1[i], ev_hf_b1[i], ev_hf_w2[i], ev_hf_b2[i],
                                      ev_hf_w3[i], ev_hf_b3[i], ev_hf_freq[i], ev_hf_w4[i], consts)
            y_hy = hyena_mix(u_t, kf, ev_hy_conv_w[i], ev_hy_conv_b[i], ev_hy_skip[i], consts)
            lambda_init = 0.8 - 0.6 * math.exp(-0.3 * layer)
            lam = (jnp.exp(jnp.sum(ev_lam_q1[i].astype(F32) * ev_lam_k1[i].astype(F32)))
                   - jnp.exp(jnp.sum(ev_lam_q2[i].astype(F32) * ev_lam_k2[i].astype(F32))) + lambda_init)
            y_da = diff_attention(qt, k.reshape(bsz, seq, DA_QK), vt,
                                  lam, ev_subln_g[i], lambda_init).reshape(n, DA_WIDTH)
            h, hb = even_out_proj(y_hy, y_da, ev_w_out[i].astype(BF16), h, ln1_g[layer], ln1_b[layer])
        else:
            u = matmul(hb, od_w_in[i], F32)
            yd = rglru_bidir(u.reshape(bsz, seq, 2 * RG_WIDTH), od_conv_w[i], od_conv_b[i], od_wa[i], od_ba[i],
                             od_wx[i], od_bx[i], od_lam[i]).reshape(2, n, RG_WIDTH)
            h, hb = odd_out_proj(u, yd, od_w_out[i].astype(BF16), h, ln1_g[layer], ln1_b[layer])
        h, hb = moe_ffn(h, hb, router_w, router_b, ex_w_gate, ex_w_up, ex_w_down, layer,
                        ln2_g[layer], ln2_b[layer])
    return h.reshape(bsz, seq, d).astype(x.dtype)
```

```python
import functools
import math

import jax
import jax.numpy as jnp
from jax import lax
from jax.experimental import pallas as pl
from jax.experimental.pallas import tpu as pltpu

F32 = jnp.float32
BF16 = jnp.bfloat16

D_MODEL = 2048
DEPTH = 2
HY_CH = D_MODEL // 2
HY_ORDER = 2
HF_BANDS = 16
HF_TARGET = 1e-2
HF_FAST = 0.3
HF_SLOW = 1.5
DA_HEADS = 8
DA_HEAD_DIM = 64
DA_V_DIM = 2 * DA_HEAD_DIM
DA_QK = DA_HEADS * 2 * DA_HEAD_DIM
DA_WIDTH = DA_HEADS * DA_V_DIM
RG_WIDTH = D_MODEL
RG_BLOCKS = 8
RG_BLOCK_W = RG_WIDTH // RG_BLOCKS
RG_C = 8.0
RG_CONV = 4
N_EXPERTS = 32
N_GROUPS = 4
EXPERTS_PER_GROUP = N_EXPERTS // N_GROUPS
TOP_K = 2
D_FF = 512
DN_ALPHA = (2 * DEPTH) ** 0.25
LN_EPS = 1e-5

VMEM_LIMIT_BYTES = 56 * 1024 * 1024
MOE_ROWS = 256


def _params(*sem):
    return pltpu.CompilerParams(dimension_semantics=sem, vmem_limit_bytes=VMEM_LIMIT_BYTES)


def _mm_kernel(x_ref, w_ref, o_ref):
    o_ref[...] = jnp.dot(x_ref[...], w_ref[...].astype(BF16), preferred_element_type=F32).astype(o_ref.dtype)


def matmul(x, w, out_dtype, col0=0, ncols=None, tm=512, tn=1024):
    m, k = x.shape
    n = w.shape[1] if ncols is None else ncols
    tm, tn = min(tm, m), min(tn, n)
    first = col0 // tn
    return pl.pallas_call(
        _mm_kernel,
        grid=(n // tn, m // tm),
        in_specs=[pl.BlockSpec((tm, k), lambda j, i: (i, 0)),
                  pl.BlockSpec((k, tn), lambda j, i: (0, first + j))],
        out_specs=pl.BlockSpec((tm, tn), lambda j, i: (i, j)),
        out_shape=jax.ShapeDtypeStruct((m, n), out_dtype),
        compiler_params=_params("arbitrary", "arbitrary"),
        name="matmul",
    )(x, w)


def _ln_rows(z, g, b):
    mu = jnp.mean(z, axis=-1, keepdims=True)
    zc = z - mu
    var = jnp.mean(zc * zc, axis=-1, keepdims=True)
    return zc * lax.rsqrt(var + LN_EPS) * g + b


def _even_out_kernel(xa_ref, xb_ref, wa_ref, wb_ref, r_ref, g_ref, b_ref, o_ref, ob_ref):
    acc = lax.dot_general(xa_ref[...], wa_ref[...], (((0,), (0,)), ((), ())), preferred_element_type=F32)
    acc += jnp.dot(xb_ref[...], wb_ref[...], preferred_element_type=F32)
    y = _ln_rows(DN_ALPHA * r_ref[...] + acc, g_ref[...], b_ref[...])
    o_ref[...] = y
    ob_ref[...] = y.astype(BF16)


def even_out_proj(y_hy_t, y_da, w_out, resid, g, b, tm=512):
    ka, m = y_hy_t.shape
    kb = y_da.shape[1]
    d = w_out.shape[1]
    row = lambda i: (i, 0)
    fixed = lambda i: (0, 0)
    return pl.pallas_call(
        _even_out_kernel,
        grid=(m // tm,),
        in_specs=[pl.BlockSpec((ka, tm), lambda i: (0, i)), pl.BlockSpec((tm, kb), row),
                  pl.BlockSpec((ka, d), fixed), pl.BlockSpec((kb, d), fixed),
                  pl.BlockSpec((tm, d), row), pl.BlockSpec((1, d), fixed), pl.BlockSpec((1, d), fixed)],
        out_specs=[pl.BlockSpec((tm, d), row), pl.BlockSpec((tm, d), row)],
        out_shape=[jax.ShapeDtypeStruct((m, d), F32), jax.ShapeDtypeStruct((m, d), BF16)],
        compiler_params=_params("arbitrary"),
        name="even_out_proj",
    )(y_hy_t, y_da, w_out[:ka], w_out[ka:], resid, g.reshape(1, d), b.reshape(1, d))


def _odd_out_kernel(gate_ref, y0_ref, y1_ref, w_ref, r_ref, g_ref, b_ref, o_ref, ob_ref):
    x = jax.nn.gelu(gate_ref[...]) * (y0_ref[0] + y1_ref[0])
    acc = jnp.dot(x.astype(BF16), w_ref[...], preferred_element_type=F32)
    y = _ln_rows(DN_ALPHA * r_ref[...] + acc, g_ref[...], b_ref[...])
    o_ref[...] = y
    ob_ref[...] = y.astype(BF16)


def odd_out_proj(u, yd, w_out, resid, g, b, tm=256):
    m = u.shape[0]
    k, d = w_out.shape
    row = lambda i: (i, 0)
    fixed = lambda i: (0, 0)
    return pl.pallas_call(
        _odd_out_kernel,
        grid=(m // tm,),
        in_specs=[pl.BlockSpec((tm, k), row),
                  pl.BlockSpec((1, tm, k), lambda i: (0, i, 0)),
                  pl.BlockSpec((1, tm, k), lambda i: (1, i, 0)),
                  pl.BlockSpec((k, d), fixed),
                  pl.BlockSpec((tm, d), row), pl.BlockSpec((1, d), fixed), pl.BlockSpec((1, d), fixed)],
        out_specs=[pl.BlockSpec((tm, d), row), pl.BlockSpec((tm, d), row)],
        out_shape=[jax.ShapeDtypeStruct((m, d), F32), jax.ShapeDtypeStruct((m, d), BF16)],
        compiler_params=_params("arbitrary"),
        name="odd_out_proj",
    )(u, yd, yd, w_out, resid, g.reshape(1, d), b.reshape(1, d))


def _combine_ln_kernel(h_ref, y0_ref, y1_ref, gate_ref, g_ref, b_ref, o_ref, ob_ref):
    gate = gate_ref[...]
    ffn = gate[:, 0:1] * y0_ref[...].astype(F32) + gate[:, 1:2] * y1_ref[...].astype(F32)
    y = _ln_rows(DN_ALPHA * h_ref[...] + ffn, g_ref[...], b_ref[...])
    o_ref[...] = y
    ob_ref[...] = y.astype(BF16)


def combine_ln(h, y0, y1, gate, g, b, tm=512):
    m, d = h.shape
    row = lambda i: (i, 0)
    fixed = lambda i: (0, 0)
    return pl.pallas_call(
        _combine_ln_kernel,
        grid=(m // tm,),
        in_specs=[pl.BlockSpec((tm, d), row), pl.BlockSpec((tm, d), row), pl.BlockSpec((tm, d), row),
                  pl.BlockSpec((tm, TOP_K), row), pl.BlockSpec((1, d), fixed), pl.BlockSpec((1, d), fixed)],
        out_specs=[pl.BlockSpec((tm, d), row), pl.BlockSpec((tm, d), row)],
        out_shape=[jax.ShapeDtypeStruct((m, d), F32), jax.ShapeDtypeStruct((m, d), BF16)],
        compiler_params=_params("arbitrary"),
        name="combine_ln",
    )(h, y0, y1, gate, g.reshape(1, d), b.reshape(1, d))


ONES_ROWS = 16
POS_SPLIT = 16


def _attn_kernel(slopes_ref, lam_ref, qt_ref, k_ref, vt_ref, fq_ref, fk_ref, g_ref, o_ref, m_s, acc_s, sa_s, sb_s,
                 *, tq, tk, seq, out_scale):
    head = pl.program_id(1)
    i0 = pl.program_id(2) * tq
    slope = slopes_ref[head]
    lam = lam_ref[0]
    qt = qt_ref[...]
    row = lax.broadcasted_iota(jnp.int32, qt.shape, 0)
    zero = jnp.zeros_like(qt)
    q2 = jnp.concatenate([jnp.where(row < DA_HEAD_DIM, qt, zero),
                          jnp.where(row >= DA_HEAD_DIM, qt, zero)], axis=1)
    fq = fq_ref[0]
    fq2 = jnp.concatenate([fq, fq], axis=1)
    q_aug = jnp.concatenate([q2, fq2], axis=0)
    fk = fk_ref[0]
    fk_neg = -fk
    ones = jnp.ones((ONES_ROWS, tk), BF16)
    m_s[...] = jnp.full(m_s.shape, -jnp.inf, F32)
    acc_s[...] = jnp.zeros(acc_s.shape, F32)
    j_diag = i0 // tk

    def update(j, s, shift):
        off = pl.multiple_of(j * tk, tk)
        va = jnp.concatenate([vt_ref[:, pl.ds(off, tk)], ones], axis=0)
        m_prev = m_s[...]
        m_new = jnp.maximum(m_prev, jnp.max(s, axis=0, keepdims=True) - shift)
        p = jnp.exp(s - (m_new + shift))
        alpha = jnp.exp(m_prev - m_new)
        acc_s[...] = alpha * acc_s[...] + jnp.dot(va, p.astype(BF16), preferred_element_type=F32)
        m_s[...] = m_new

    def chunk_of(t):
        return t + (t >= j_diag).astype(jnp.int32)

    def scores(t):
        j = chunk_of(t)
        off = pl.multiple_of(j * tk, tk)
        ka = jnp.concatenate([k_ref[0, pl.ds(off, tk), :], jnp.where(j < j_diag, fk, fk_neg)], axis=1)
        return jnp.dot(ka, q_aug, preferred_element_type=F32)

    def consume(s_ref, t):
        j = chunk_of(t)
        update(j, s_ref[...], slope * jnp.abs(i0 - j * tk).astype(F32))

    j0 = pl.multiple_of(j_diag * tk, tk)
    dist = jnp.abs(lax.broadcasted_iota(jnp.int32, (tk, tq), 0) + (j0 - i0)
                   - lax.broadcasted_iota(jnp.int32, (tk, tq), 1)).astype(F32)
    bias = slope * dist
    s_diag = jnp.dot(k_ref[0, pl.ds(j0, tk), :], q2, preferred_element_type=F32)
    sa_s[...] = s_diag - jnp.concatenate([bias, bias], axis=1)
    sb_s[...] = scores(jnp.int32(0))
    update(j_diag, sa_s[...], 0.0)
    n_off = seq // tk - 1

    def pair(t, carry):
        sa_s[...] = scores(2 * t + 1)
        consume(sb_s, 2 * t)
        sb_s[...] = scores(2 * t + 2)
        consume(sa_s, 2 * t + 1)
        return carry

    lax.fori_loop(0, n_off // 2, pair, 0)
    consume(sb_s, jnp.int32(n_off - 1))

    acc = acc_s[...]
    o1 = acc[:DA_V_DIM, :tq] / acc[DA_V_DIM:DA_V_DIM + 1, :tq]
    o2 = acc[:DA_V_DIM, tq:] / acc[DA_V_DIM:DA_V_DIM + 1, tq:]
    o = o1 - lam * o2
    o = o * lax.rsqrt(jnp.mean(o * o, axis=0, keepdims=True) + LN_EPS) * g_ref[...]
    o_ref[0] = (o * out_scale).T.astype(o_ref.dtype)


def diff_attention(qt, k, vt, lam, subln_g, lambda_init, tq=512, tk=512):
    bsz, seq, width = k.shape
    nq = seq // tq
    slopes = 2.0 ** (-(8.0 / DA_HEADS) * jnp.arange(1, DA_HEADS + 1, dtype=F32))

    def split(n):
        pos = jnp.arange(n, dtype=jnp.int32)
        return (pos // POS_SPLIT * POS_SPLIT).astype(F32), (pos % POS_SPLIT).astype(F32)

    qhi, qlo = split(tq)
    khi, klo = split(tk)
    sl = slopes[:, None]
    fq = jnp.zeros((DA_HEADS, DA_V_DIM, tq), F32)
    fq = fq.at[:, 0].set(1.0).at[:, 1].set(1.0).at[:, 2].set(-sl * qhi).at[:, 3].set(-sl * qlo)
    fk = jnp.zeros((DA_HEADS, tk, DA_V_DIM), F32)
    fk = fk.at[:, :, 0].set(sl * khi).at[:, :, 1].set(sl * klo).at[:, :, 2].set(1.0).at[:, :, 3].set(1.0)
    kern = functools.partial(_attn_kernel, tq=tq, tk=tk, seq=seq, out_scale=1.0 - lambda_init)
    smem = pl.BlockSpec(memory_space=pltpu.SMEM)
    return pl.pallas_call(
        kern,
        grid=(bsz, DA_HEADS, seq // tq),
        in_specs=[smem, smem,
                  pl.BlockSpec((DA_V_DIM, tq), lambda b, h, i: (h, b * nq + i)),
                  pl.BlockSpec((1, seq, DA_V_DIM), lambda b, h, i: (b, 0, h)),
                  pl.BlockSpec((DA_V_DIM, seq), lambda b, h, i: (h, b)),
                  pl.BlockSpec((1, DA_V_DIM, tq), lambda b, h, i: (h, 0, 0)),
                  pl.BlockSpec((1, tk, DA_V_DIM), lambda b, h, i: (h, 0, 0)),
                  pl.BlockSpec((DA_V_DIM, 1), lambda b, h, i: (0, 0))],
        out_specs=pl.BlockSpec((1, tq, DA_V_DIM), lambda b, h, i: (b, i, h)),
        out_shape=jax.ShapeDtypeStruct((bsz, seq, width), BF16),
        scratch_shapes=[pltpu.VMEM((1, 2 * tq), F32), pltpu.VMEM((DA_V_DIM + ONES_ROWS, 2 * tq), F32),
                        pltpu.VMEM((tk, 2 * tq), F32), pltpu.VMEM((tk, 2 * tq), F32)],
        compiler_params=_params("arbitrary", "arbitrary", "arbitrary"),
        name="diff_attention",
    )(slopes, lam.reshape(1).astype(F32), qt, k, vt, fq.astype(BF16), fk.astype(BF16),
      subln_g.reshape(DA_V_DIM, 1).astype(F32))


HALO = 8


def _rglru_kernel(xr_ref, prev_ref, next_ref, cw_ref, cb_ref, wa_ref, wx_ref, ba_ref, bx_ref, kk_ref, y_ref,
                  a_s, b_s, h_s, *, ts, nt):
    direction = pl.program_id(0)
    t = pl.program_id(2)
    chunk = t + direction * (nt - 1 - 2 * t)

    @pl.when(t == 0)
    def _():
        h_s[...] = jnp.zeros(h_s.shape, F32)

    xf = jnp.concatenate([jnp.where(chunk == 0, 0.0, prev_ref[0]), xr_ref[0],
                          jnp.where(chunk == nt - 1, 0.0, next_ref[0])], axis=0)
    rows = ts + 2 * HALO

    def tap(j):
        shifted = xf if j == 2 else pltpu.roll(xf, (2 - j) % rows, 0)
        return cw_ref[j:j + 1, :] * shifted[HALO:HALO + ts]

    x = cb_ref[...] + sum(tap(j) for j in range(RG_CONV))
    xb = x.astype(BF16)
    xh = 0.5 * x
    for n in range(RG_BLOCKS):
        cols = slice(n * RG_BLOCK_W, (n + 1) * RG_BLOCK_W)
        xs = xb[:, cols]
        tr = jnp.tanh(jnp.dot(xs, wa_ref[0, n], preferred_element_type=F32) + ba_ref[0, :, cols])
        ti = jnp.tanh(jnp.dot(xs, wx_ref[0, n], preferred_element_type=F32) + bx_ref[0, :, cols])
        kk = kk_ref[0, :, cols]
        a = jnp.exp2(kk * tr + kk)
        a_s[:, cols] = a
        s = 1.0 - a * a
        root = jnp.where(s > 0.0, s * lax.rsqrt(s), 0.0)
        b_s[:, cols] = root * (ti * xh[:, cols] + xh[:, cols])

    def step(k, h):
        row = jnp.where(direction == 0, k, ts - 1 - k)
        h = a_s[pl.ds(row, 1), :] * h + b_s[pl.ds(row, 1), :]
        y_ref[0, 0, pl.ds(row, 1), :] = h
        return h

    h_s[...] = lax.fori_loop(0, ts, step, h_s[...], unroll=8)


def rglru_bidir(u, conv_w, conv_b, wa, ba, wx, bx, lam, ts=256):
    bsz, seq, w2 = u.shape
    w = w2 // 2
    nt = seq // ts
    per = ts // HALO
    kk = (-0.5 * RG_C * math.log2(math.e)) * jax.nn.softplus(-lam.astype(F32)).reshape(2, 1, w)
    chunk = lambda d, t: t + d * (nt - 1 - 2 * t)
    dmap4 = lambda d, b, t: (d, 0, 0, 0)
    dmap3 = lambda d, b, t: (d, 0, 0)
    fixed = lambda d, b, t: (0, 0)
    return pl.pallas_call(
        functools.partial(_rglru_kernel, ts=ts, nt=nt),
        grid=(2, bsz, nt),
        in_specs=[pl.BlockSpec((1, ts, w), lambda d, b, t: (b, chunk(d, t), 1)),
                  pl.BlockSpec((1, HALO, w), lambda d, b, t: (b, jnp.maximum(chunk(d, t) * per - 1, 0), 1)),
                  pl.BlockSpec((1, HALO, w),
                               lambda d, b, t: (b, jnp.minimum((chunk(d, t) + 1) * per, seq // HALO - 1), 1)),
                  pl.BlockSpec((RG_CONV, w), fixed), pl.BlockSpec((1, w), fixed),
                  pl.BlockSpec((1, RG_BLOCKS, RG_BLOCK_W, RG_BLOCK_W), dmap4),
                  pl.BlockSpec((1, RG_BLOCKS, RG_BLOCK_W, RG_BLOCK_W), dmap4),
                  pl.BlockSpec((1, 1, w), dmap3), pl.BlockSpec((1, 1, w), dmap3), pl.BlockSpec((1, 1, w), dmap3)],
        out_specs=pl.BlockSpec((1, 1, ts, w), lambda d, b, t: (d, b, chunk(d, t), 0)),
        out_shape=jax.ShapeDtypeStruct((2, bsz, seq, w), F32),
        scratch_shapes=[pltpu.VMEM((ts, w), F32), pltpu.VMEM((ts, w), F32),
                        pltpu.VMEM((1, w), F32)],
        compiler_params=_params("arbitrary", "arbitrary", "arbitrary"),
        name="rglru_bidir",
    )(u, u, u, conv_w.astype(F32), conv_b.astype(F32).reshape(1, w), (0.5 * wa).astype(BF16),
      (0.5 * wx).astype(BF16), 0.5 * ba.astype(F32).reshape(2, 1, w), 0.5 * bx.astype(F32).reshape(2, 1, w), kk)


def _router_kernel(h_ref, wt_ref, b_ref, tri_ref, e_ref, g_ref, r_ref, cnt_ref, seen_s):
    logits = lax.dot_general(wt_ref[...], h_ref[...], (((1,), (1,)), ((), ())),
                             precision=lax.Precision.HIGHEST, preferred_element_type=F32)
    tm = logits.shape[1]
    s = jax.nn.sigmoid(logits).reshape(N_GROUPS, EXPERTS_PER_GROUP, tm)
    sel = s + b_ref[...]
    idx = lax.broadcasted_iota(jnp.int32, sel.shape, 1)
    big = jnp.int32(EXPERTS_PER_GROUP)
    v1 = jnp.max(sel, axis=1, keepdims=True)
    i1 = jnp.min(jnp.where(sel == v1, idx, big), axis=1, keepdims=True)
    rest = jnp.where(idx == i1, -jnp.inf, sel)
    v2 = jnp.max(rest, axis=1, keepdims=True)
    i2 = jnp.min(jnp.where(rest == v2, idx, big), axis=1, keepdims=True)
    score = v1 + v2
    gidx = lax.broadcasted_iota(jnp.int32, score.shape, 0)
    best = jnp.max(score, axis=0, keepdims=True)
    grp = jnp.min(jnp.where(score == best, gidx, jnp.int32(N_GROUPS)), axis=0, keepdims=True)
    pick = gidx == grp
    l1 = jnp.sum(jnp.where(pick, i1, 0), axis=0)
    l2 = jnp.sum(jnp.where(pick, i2, 0), axis=0)
    s_g = jnp.sum(jnp.where(pick, s, 0.0), axis=0)
    eidx = lax.broadcasted_iota(jnp.int32, s_g.shape, 0)
    w1 = jnp.sum(jnp.where(eidx == l1, s_g, 0.0), axis=0, keepdims=True)
    w2 = jnp.sum(jnp.where(eidx == l2, s_g, 0.0), axis=0, keepdims=True)
    tot = w1 + w2
    base = grp[0] * EXPERTS_PER_GROUP
    e1, e2 = base + l1, base + l2
    e_ref[...] = jnp.concatenate([e1, e2], axis=0)
    g_ref[...] = jnp.concatenate([w1 / tot, w2 / tot], axis=0)

    @pl.when(pl.program_id(0) == 0)
    def _():
        seen_s[...] = jnp.zeros(seen_s.shape, F32)

    eall = lax.broadcasted_iota(jnp.int32, (N_EXPERTS, tm), 0)
    hit1, hit2 = eall == e1, eall == e2
    both = jnp.where(hit1, 1.0, jnp.where(hit2, 1.0, 0.0))
    incl = jnp.dot(both.astype(BF16), tri_ref[...], preferred_element_type=F32)
    before = incl - both + seen_s[...]
    r_ref[...] = jnp.concatenate([jnp.sum(jnp.where(hit1, before, 0.0), axis=0, keepdims=True),
                                  jnp.sum(jnp.where(hit2, before, 0.0), axis=0, keepdims=True)],
                                 axis=0).astype(jnp.int32)
    seen_s[...] = seen_s[...] + incl[:, tm - 1:tm]
    cnt_ref[...] = seen_s[...].astype(jnp.int32)


def route(h, router_w, router_b, tm=1024):
    n, d = h.shape
    tri = (jnp.arange(tm)[:, None] <= jnp.arange(tm)[None, :]).astype(BF16)
    tile = pl.BlockSpec((TOP_K, tm), lambda i: (0, i))
    return pl.pallas_call(
        _router_kernel,
        grid=(n // tm,),
        in_specs=[pl.BlockSpec((tm, d), lambda i: (i, 0)),
                  pl.BlockSpec((N_EXPERTS, d), lambda i: (0, 0)),
                  pl.BlockSpec((N_GROUPS, EXPERTS_PER_GROUP, 1), lambda i: (0, 0, 0)),
                  pl.BlockSpec((tm, tm), lambda i: (0, 0))],
        out_specs=[tile, tile, tile, pl.BlockSpec((N_EXPERTS, 1), lambda i: (0, 0))],
        out_shape=[jax.ShapeDtypeStruct((TOP_K, n), jnp.int32), jax.ShapeDtypeStruct((TOP_K, n), F32),
                   jax.ShapeDtypeStruct((TOP_K, n), jnp.int32), jax.ShapeDtypeStruct((N_EXPERTS, 1), jnp.int32)],
        scratch_shapes=[pltpu.VMEM((N_EXPERTS, 1), F32)],
        compiler_params=_params("arbitrary"),
        name="router",
    )(h, router_w.astype(F32).T, router_b.astype(F32).reshape(N_GROUPS, EXPERTS_PER_GROUP, 1), tri)


def _experts_kernel(blk_exp_ref, n_used_ref, x_ref, wg_ref, wu_ref, wd_ref, o_ref, wg_s, wu_s, wd_s):
    i = pl.program_id(0)
    expert = blk_exp_ref[i]
    prev = blk_exp_ref[jnp.maximum(i - 1, 0)]

    @pl.when((i == 0) | (expert != prev))
    def _():
        wg_s[...] = wg_ref[0, 0].astype(BF16)
        wu_s[...] = wu_ref[0, 0].astype(BF16)
        wd_s[...] = wd_ref[0, 0].astype(BF16)

    @pl.when(i < n_used_ref[0])
    def _():
        x = x_ref[...].astype(BF16)
        hg = jnp.dot(x, wg_s[...], preferred_element_type=F32)
        hu = jnp.dot(x, wu_s[...], preferred_element_type=F32)
        hidden = (jax.nn.silu(hg) * hu).astype(BF16)
        o_ref[...] = jnp.dot(hidden, wd_s[...], preferred_element_type=F32).astype(o_ref.dtype)

    @pl.when(i >= n_used_ref[0])
    def _():
        o_ref[...] = jnp.zeros(o_ref.shape, o_ref.dtype)


def experts(xs, blk_exp, n_used, w_gate, w_up, w_down, layer):
    p, d = xs.shape
    f = w_gate.shape[3]
    grid_spec = pltpu.PrefetchScalarGridSpec(
        num_scalar_prefetch=2,
        grid=(p // MOE_ROWS,),
        in_specs=[pl.BlockSpec((MOE_ROWS, d), lambda i, be, nu: (i, 0)),
                  pl.BlockSpec((1, 1, d, f), lambda i, be, nu: (layer, be[i], 0, 0)),
                  pl.BlockSpec((1, 1, d, f), lambda i, be, nu: (layer, be[i], 0, 0)),
                  pl.BlockSpec((1, 1, f, d), lambda i, be, nu: (layer, be[i], 0, 0))],
        out_specs=pl.BlockSpec((MOE_ROWS, d), lambda i, be, nu: (i, 0)),
        scratch_shapes=[pltpu.VMEM((d, f), BF16), pltpu.VMEM((d, f), BF16), pltpu.VMEM((f, d), BF16)],
    )
    return pl.pallas_call(
        _experts_kernel,
        grid_spec=grid_spec,
        out_shape=jax.ShapeDtypeStruct((p, d), BF16),
        compiler_params=_params("arbitrary"),
        name="experts",
    )(blk_exp, n_used, xs, w_gate, w_up, w_down)


def _dispatch_kernel(dest_ref, fill_ref, x_ref, xs_ref, zero_s, fill_sem, row_sem, *, tm, n_blocks):
    base = pl.program_id(0) * tm

    @pl.when(pl.program_id(0) == 0)
    def _():
        zero_s[...] = jnp.zeros(zero_s.shape, zero_s.dtype)

        def zero_block(blk):
            return pltpu.make_async_copy(zero_s, xs_ref.at[pl.ds(pl.multiple_of(blk * MOE_ROWS, MOE_ROWS),
                                                                 MOE_ROWS)], fill_sem)

        def start(blk, carry):
            @pl.when(fill_ref[blk] != 0)
            def _():
                zero_block(blk).start()
            return carry

        def wait(blk, carry):
            @pl.when(fill_ref[blk] != 0)
            def _():
                zero_block(blk).wait()
            return carry

        lax.fori_loop(0, n_blocks, start, 0)
        lax.fori_loop(0, n_blocks, wait, 0)

    def issue(r, carry):
        for k in range(TOP_K):
            pltpu.make_async_copy(x_ref.at[pl.ds(r, 1)], xs_ref.at[pl.ds(dest_ref[k, base + r], 1)],
                                  row_sem).start()
        return carry

    lax.fori_loop(0, tm, issue, 0, unroll=4)
    for k in range(TOP_K):
        pltpu.make_async_copy(x_ref, xs_ref.at[pl.ds(0, tm)], row_sem).wait()


def dispatch_rows(x, dest, fill, tm=512):
    n, d = x.shape
    n_blocks = fill.shape[0]
    grid_spec = pltpu.PrefetchScalarGridSpec(
        num_scalar_prefetch=2,
        grid=(n // tm,),
        in_specs=[pl.BlockSpec((tm, d), lambda i, dst, fl: (i, 0))],
        out_specs=pl.BlockSpec(memory_space=pl.ANY),
        scratch_shapes=[pltpu.VMEM((MOE_ROWS, d), x.dtype), pltpu.SemaphoreType.DMA(()),
                        pltpu.SemaphoreType.DMA(())],
    )
    return pl.pallas_call(
        functools.partial(_dispatch_kernel, tm=tm, n_blocks=n_blocks),
        grid_spec=grid_spec,
        out_shape=jax.ShapeDtypeStruct((n_blocks * MOE_ROWS, d), x.dtype),
        compiler_params=_params("arbitrary"),
        name="dispatch_rows",
    )(dest, fill, x)


def moe_ffn(h, router_w, router_b, w_gate, w_up, w_down, layer, ln_g, ln_b):
    n, d = h.shape
    a = n * TOP_K
    e_idx, gate, rank, counts = route(h, router_w, router_b)
    counts = counts[:, 0]
    padded = (counts + MOE_ROWS - 1) // MOE_ROWS * MOE_ROWS
    pad_end = jnp.cumsum(padded)
    pad_start = pad_end - padded
    experts_iota = jnp.arange(N_EXPERTS, dtype=jnp.int32)[:, None, None]
    dest = jnp.sum(jnp.where(e_idx[None] == experts_iota, pad_start[:, None, None], 0), axis=0) + rank
    n_blocks = a // MOE_ROWS + N_EXPERTS
    blk_start = jnp.arange(n_blocks, dtype=jnp.int32)[:, None] * MOE_ROWS
    blk_exp = jnp.minimum(jnp.sum((pad_end[None, :] <= blk_start).astype(jnp.int32), axis=1), N_EXPERTS - 1)
    n_used = (pad_end[-1:] // MOE_ROWS).astype(jnp.int32)
    ends_expert = jnp.any((pad_end[None, :] == blk_start + MOE_ROWS) & (padded[None, :] > 0), axis=1)
    fill = (ends_expert | (blk_start[:, 0] >= pad_end[-1])).astype(jnp.int32)
    xs = dispatch_rows(h, dest, fill)
    yb = experts(xs, blk_exp, n_used, w_gate, w_up, w_down, layer)
    y0 = yb.at[dest[0]].get(mode="promise_in_bounds")
    y1 = yb.at[dest[1]].get(mode="promise_in_bounds")
    return combine_ln(h, y0, y1, gate.T, ln_g, ln_b)


FFT_R = 128
HALF_R = FFT_R // 2
HY_GROUP = 4


def _proj_t_kernel(w_ref, x_ref, o_ref, *, scale):
    acc = lax.dot_general(w_ref[...].astype(BF16), x_ref[...], (((0,), (1,)), ((), ())),
                          preferred_element_type=F32)
    o_ref[...] = (acc * scale).astype(o_ref.dtype)


def proj_t(w, col0, ncols, x, out_dtype, scale=1.0, tm=512, tn=1024):
    k = w.shape[0]
    n = x.shape[0]
    first = col0 // tm
    return pl.pallas_call(
        functools.partial(_proj_t_kernel, scale=scale),
        grid=(ncols // tm, n // tn),
        in_specs=[pl.BlockSpec((k, tm), lambda i, j: (0, first + i)),
                  pl.BlockSpec((tn, k), lambda i, j: (j, 0))],
        out_specs=pl.BlockSpec((tm, tn), lambda i, j: (i, j)),
        out_shape=jax.ShapeDtypeStruct((ncols, n), out_dtype),
        compiler_params=_params("arbitrary", "arbitrary"),
        name="proj_t",
    )(w, x)


def _dft_constants():
    idx = jnp.arange(FFT_R, dtype=jnp.int32)
    prod = idx[:, None] * idx[None, :]
    ang = (prod % FFT_R).astype(F32) * (2.0 * math.pi / FFT_R)
    f_r, f_i = jnp.cos(ang), -jnp.sin(ang)
    ang_t = prod.astype(F32) * (2.0 * math.pi / (FFT_R * FFT_R))
    g_r, g_i = f_r[:HALF_R], -f_i[:HALF_R]
    return dict(
        la=jnp.block([[f_r[:, :HALF_R], -f_i[:, :HALF_R]], [f_i[:, :HALF_R], f_r[:, :HALF_R]]]).astype(BF16),
        la_real=jnp.concatenate([f_r, f_i], axis=0).astype(BF16),
        rb=jnp.block([[f_r, f_i], [-f_i, f_r]]).astype(BF16),
        rc=jnp.block([[f_r, -f_i], [f_i, f_r]]).astype(BF16),
        ld=(jnp.block([[g_r, -g_i], [g_i, g_r]]) / (FFT_R * FFT_R)).astype(BF16),
        t_r=jnp.cos(ang_t), t_i=-jnp.sin(ang_t))


def _rows_to_lanes(x):
    return jnp.concatenate([x[:FFT_R], x[FFT_R:]], axis=1)


def _fwd_lane_stage(a, t_r, t_i, rb):
    a_r, a_i = a[:FFT_R], a[FFT_R:]
    t2_r = jnp.concatenate([t_r, t_r], axis=1)
    t2_i = jnp.concatenate([t_i, t_i], axis=1)
    b_r = a_r * t2_r - a_i * t2_i
    b_i = a_r * t2_i + a_i * t2_r
    lhs = jnp.concatenate([jnp.concatenate([b_r[:, :FFT_R], b_i[:, :FFT_R]], axis=1),
                           jnp.concatenate([b_r[:, FFT_R:], b_i[:, FFT_R:]], axis=1)], axis=0)
    return jnp.dot(lhs.astype(BF16), rb, preferred_element_type=F32)


def _conv_pairs(ms, kfs, la, rb, rc, ld, t_r, t_i):
    a = [jnp.dot(la, jnp.concatenate([m_r, m_i], axis=0).astype(BF16), preferred_element_type=F32)
         for m_r, m_i in ms]
    z = [_fwd_lane_stage(ai, t_r, t_i, rb) for ai in a]
    c = []
    for zi, kf in zip(z, kfs):
        z_r, z_i, k_r, k_i = zi[:, :FFT_R], zi[:, FFT_R:], kf[:, :FFT_R], kf[:, FFT_R:]
        y = jnp.concatenate([z_r * k_r - z_i * k_i, z_r * k_i + z_i * k_r], axis=1)
        c.append(jnp.dot(y.astype(BF16), rc, preferred_element_type=F32))
    t2_r = jnp.concatenate([t_r, t_r], axis=0)
    t2_i = jnp.concatenate([t_i, t_i], axis=0)
    out = []
    for ci in c:
        c_r, c_i = ci[:, :FFT_R], ci[:, FFT_R:]
        d_r = c_r * t2_r + c_i * t2_i
        d_i = c_i * t2_r - c_r * t2_i
        rhs = jnp.concatenate([_rows_to_lanes(d_r), _rows_to_lanes(d_i)], axis=0)
        x = jnp.dot(ld, rhs.astype(BF16), preferred_element_type=F32)
        out.append((x[:HALF_R], x[HALF_R:]))
    return out


def _short_conv_tile(x, w0, w1, w2, b):
    rows = x.shape[0]
    lane = lax.broadcasted_iota(jnp.int32, x.shape, 1)
    row = lax.broadcasted_iota(jnp.int32, x.shape, 0)
    r = pltpu.roll(x, 1, 1)
    prev = jnp.where(lane == 0, jnp.where(row == 0, 0.0, pltpu.roll(r, 1, 0)), r)
    r = pltpu.roll(x, FFT_R - 1, 1)
    nxt = jnp.where(lane == FFT_R - 1, jnp.where(row == rows - 1, 0.0, pltpu.roll(r, rows - 1, 0)), r)
    return w0 * prev + w1 * x + w2 * nxt + b


def _hyena_kernel(cw_ref, cb_ref, skip_ref, v_ref, x1_ref, x2_ref, kf_ref,
                  la_ref, rb_ref, rc_ref, ld_ref, tr_ref, ti_ref, o_ref, *, tc):
    c_base = pl.program_id(0) * tc
    lane = lax.broadcasted_iota(jnp.int32, (1, 2 * FFT_R), 1)

    def conv_in(ref, part, b, c0):
        tiles = []
        for cc in range(2):
            ch = part * HY_CH + c_base + c0 + cc
            tiles.append(_short_conv_tile(ref[c0 + cc, b], cw_ref[0, ch], cw_ref[1, ch], cw_ref[2, ch], cb_ref[ch]))
        return jnp.concatenate(tiles, axis=1)

    def group(g, carry):
        starts = [2 * (HY_GROUP * g + j) for j in range(HY_GROUP)]
        consts = (la_ref[...], rb_ref[...], rc_ref[...], ld_ref[...], tr_ref[...], ti_ref[...])
        zs = [(conv_in(v_ref, 0, 0, c0), conv_in(v_ref, 0, 1, c0)) for c0 in starts]
        for o, g_ref in enumerate((x1_ref, x2_ref)):
            kfs = [kf_ref[o, pl.ds(c0, 2)].reshape(2 * FFT_R, 2 * FFT_R) for c0 in starts]
            ys = _conv_pairs(zs, kfs, *consts)
            nxt = []
            for c0, (z_r, z_i), (y_r, y_i) in zip(starts, zs, ys):
                sk = jnp.where(lane < FFT_R, skip_ref[o, c_base + c0], skip_ref[o, c_base + c0 + 1])
                nxt.append((conv_in(g_ref, o + 1, 0, c0) * (y_r + sk * z_r),
                            conv_in(g_ref, o + 1, 1, c0) * (y_i + sk * z_i)))
            zs = nxt
        for c0, (z_r, z_i) in zip(starts, zs):
            o_ref[c0, 0] = z_r[:, :FFT_R].astype(o_ref.dtype)
            o_ref[c0 + 1, 0] = z_r[:, FFT_R:].astype(o_ref.dtype)
            o_ref[c0, 1] = z_i[:, :FFT_R].astype(o_ref.dtype)
            o_ref[c0 + 1, 1] = z_i[:, FFT_R:].astype(o_ref.dtype)
        return carry

    lax.fori_loop(0, tc // (2 * HY_GROUP), group, 0)


def hyena_mix(u_t, kf, conv_w, conv_b, skip, consts, tc=16):
    rows, n = u_t.shape
    ch = rows // 3
    u5 = u_t.reshape(rows, 2, HALF_R, FFT_R)
    nct = ch // tc
    smem = pl.BlockSpec(memory_space=pltpu.SMEM)
    part = lambda k: pl.BlockSpec((tc, 2, HALF_R, FFT_R), lambda i: (k * nct + i, 0, 0, 0))
    full2 = lambda a: pl.BlockSpec(a.shape, lambda i: (0, 0))
    mats = [consts[k] for k in ("la", "rb", "rc", "ld", "t_r", "t_i")]
    out = pl.pallas_call(
        functools.partial(_hyena_kernel, tc=tc),
        grid=(nct,),
        in_specs=[smem, smem, smem, part(0), part(1), part(2),
                  pl.BlockSpec((2, tc, FFT_R, 2 * FFT_R), lambda i: (0, i, 0, 0))] + [full2(a) for a in mats],
        out_specs=pl.BlockSpec((tc, 2, HALF_R, FFT_R), lambda i: (i, 0, 0, 0)),
        out_shape=jax.ShapeDtypeStruct((ch, 2, HALF_R, FFT_R), BF16),
        compiler_params=_params("arbitrary"),
        name="hyena_mix",
    )(conv_w.astype(F32), conv_b.astype(F32), skip.astype(F32), u5, u5, u5, kf, *mats)
    return out.reshape(ch, n)


def _hyena_filter_kernel(delta_ref, hf_ref, hb_ref, tpos_ref, la_ref, rb_ref, tr_ref, ti_ref, kf_ref, *, tc):
    c_base = pl.program_id(1) * tc

    def taps_of(c0):
        taps = []
        for cc in range(2):
            k = jnp.concatenate([hf_ref[0, 0, c0 + cc], hb_ref[0, 0, c0 + cc]], axis=0)
            k = k * jnp.exp(-tpos_ref[...] * delta_ref[c_base + c0 + cc])
            taps.append(k * lax.rsqrt(jnp.sum(k * k, keepdims=True)))
        return jnp.concatenate(taps, axis=1).astype(BF16)

    def group(g, carry):
        starts = [2 * (HY_GROUP * g + j) for j in range(HY_GROUP)]
        a = [jnp.dot(la_ref[...], taps_of(c0), preferred_element_type=F32) for c0 in starts]
        z = [_fwd_lane_stage(ai, tr_ref[...], ti_ref[...], rb_ref[...]) for ai in a]
        for c0, zi in zip(starts, z):
            kf_ref[0, pl.ds(c0, 2)] = zi.reshape(2, FFT_R, 2 * FFT_R)
        return carry

    lax.fori_loop(0, tc // (2 * HY_GROUP), group, 0)


def hyena_filter_spectra(seq, w1, b1, w2, b2, w3, b3, freq, w4, consts, tc=16):
    t = jnp.linspace(0.0, 1.0, seq, dtype=F32)[:, None]
    omega = 2.0 * math.pi * jnp.arange(seq, dtype=F32)[:, None] / seq
    bands = jnp.linspace(1e-4, HF_BANDS - 1, HF_BANDS, dtype=F32)[None, :]
    ang = omega * bands
    z = jnp.concatenate([t, jnp.cos(ang), -jnp.sin(ang)], -1)
    fr = freq.astype(F32)
    hid = jnp.sin(fr * (z @ w1.astype(F32) + b1.astype(F32)))
    hid = jnp.sin(fr * (hid @ w2.astype(F32) + b2.astype(F32)))
    hid = jnp.sin(fr * (hid @ w3.astype(F32) + b3.astype(F32)))
    back = lambda a: jnp.concatenate([jnp.zeros_like(a[:1]), a[:0:-1]], axis=0)
    w4t = w4.astype(BF16).T
    half = HY_ORDER * HY_CH
    h_fwd = matmul(w4t[:half], hid.T.astype(BF16), F32)
    h_bwd = matmul(w4t[half:], back(hid).T.astype(BF16), F32)
    shape5 = (HY_ORDER, HY_CH, HALF_R, FFT_R)
    tpos = jnp.concatenate([t, back(t)], axis=0).reshape(FFT_R, FFT_R)
    max_decay = math.log(HF_TARGET) / HF_FAST
    min_decay = math.log(HF_TARGET) / HF_SLOW
    deltas = jnp.abs(jnp.linspace(min_decay, max_decay, HY_CH, dtype=F32))
    taps = pl.BlockSpec((1, 1, tc, HALF_R, FFT_R), lambda o, i: (0, o, i, 0, 0))
    full2 = lambda a: pl.BlockSpec(a.shape, lambda o, i: (0, 0))
    mats = [consts[k] for k in ("la_real", "rb", "t_r", "t_i")]
    return pl.pallas_call(
        functools.partial(_hyena_filter_kernel, tc=tc),
        grid=(HY_ORDER, HY_CH // tc),
        in_specs=[pl.BlockSpec(memory_space=pltpu.SMEM), taps, taps, full2(tpos)] + [full2(a) for a in mats],
        out_specs=pl.BlockSpec((1, tc, FFT_R, 2 * FFT_R), lambda o, i: (o, i, 0, 0)),
        out_shape=jax.ShapeDtypeStruct((HY_ORDER, HY_CH, FFT_R, 2 * FFT_R), F32),
        compiler_params=_params("arbitrary", "arbitrary"),
        name="hyena_filter_spectra",
    )(deltas, h_fwd.reshape((1,) + shape5), h_bwd.reshape((1,) + shape5), tpos, *mats)


def kernel(x, ev_w_in, ev_hy_conv_w, ev_hy_conv_b, ev_hf_w1, ev_hf_b1, ev_hf_w2, ev_hf_b2, ev_hf_w3, ev_hf_b3,
           ev_hf_freq, ev_hf_w4, ev_hy_skip, ev_lam_q1, ev_lam_k1, ev_lam_q2, ev_lam_k2, ev_subln_g, ev_w_out,
           od_w_in, od_conv_w, od_conv_b, od_wa, od_ba, od_wx, od_bx, od_lam, od_w_out, ln1_g, ln1_b, ln2_g,
           ln2_b, router_w, router_b, ex_w_gate, ex_w_up, ex_w_down):
    bsz, seq, d = x.shape
    n = bsz * seq
    h = x.reshape(n, d).astype(F32)
    hb = h.astype(BF16)
    for layer in range(DEPTH):
        i = layer // 2
        if layer % 2 == 0:
            w_in = ev_w_in[i]
            c0 = 3 * HY_CH
            u_t = proj_t(w_in, 0, c0, hb, F32)
            qt = proj_t(w_in, c0, DA_QK, hb, BF16, scale=DA_HEAD_DIM ** -0.5)
            k = matmul(hb, w_in, BF16, col0=c0 + DA_QK, ncols=DA_QK)
            vt = proj_t(w_in, c0 + 2 * DA_QK, DA_WIDTH, hb, BF16)
            consts = _dft_constants()
            kf = hyena_filter_spectra(seq, ev_hf_w1[i], ev_hf_b1[i], ev_hf_w2[i], ev_hf_b2[i],
                                      ev_hf_w3[i], ev_hf_b3[i], ev_hf_freq[i], ev_hf_w4[i], consts)
            y_hy = hyena_mix(u_t, kf, ev_hy_conv_w[i], ev_hy_conv_b[i], ev_hy_skip[i], consts)
            lambda_init = 0.8 - 0.6 * math.exp(-0.3 * layer)
            lam = (jnp.exp(jnp.sum(ev_lam_q1[i].astype(F32) * ev_lam_k1[i].astype(F32)))
                   - jnp.exp(jnp.sum(ev_lam_q2[i].astype(F32) * ev_lam_k2[i].astype(F32))) + lambda_init)
            y_da = diff_attention(qt, k.reshape(bsz, seq, DA_QK), vt,
                                  lam, ev_subln_g[i], lambda_init).reshape(n, DA_WIDTH)
            h, hb = even_out_proj(y_hy, y_da, ev_w_out[i].astype(BF16), h, ln1_g[layer], ln1_b[layer])
        else:
            u = matmul(hb, od_w_in[i], F32)
            yd = rglru_bidir(u.reshape(bsz, seq, 2 * RG_WIDTH), od_conv_w[i], od_conv_b[i], od_wa[i], od_ba[i],
                             od_wx[i], od_bx[i], od_lam[i]).reshape(2, n, RG_WIDTH)
            h, hb = odd_out_proj(u, yd, od_w_out[i].astype(BF16), h, ln1_g[layer], ln1_b[layer])
        h, hb = moe_ffn(h, router_w, router_b, ex_w_gate, ex_w_up, ex_w_down, layer,
                        ln2_g[layer], ln2_b[layer])
    return h.reshape(bsz, seq, d).astype(x.dtype)
```

```python
import functools
import math

import jax
import jax.numpy as jnp
from jax import lax
from jax.experimental import pallas as pl
from jax.experimental.pallas import tpu as pltpu

F32 = jnp.float32
BF16 = jnp.bfloat16

D_MODEL = 2048
DEPTH = 2
HY_CH = D_MODEL // 2
HY_ORDER = 2
HF_BANDS = 16
HF_TARGET = 1e-2
HF_FAST = 0.3
HF_SLOW = 1.5
DA_HEADS = 8
DA_HEAD_DIM = 64
DA_V_DIM = 2 * DA_HEAD_DIM
DA_QK = DA_HEADS * 2 * DA_HEAD_DIM
DA_WIDTH = DA_HEADS * DA_V_DIM
RG_WIDTH = D_MODEL
RG_BLOCKS = 8
RG_BLOCK_W = RG_WIDTH // RG_BLOCKS
RG_C = 8.0
RG_CONV = 4
N_EXPERTS = 32
N_GROUPS = 4
EXPERTS_PER_GROUP = N_EXPERTS // N_GROUPS
TOP_K = 2
D_FF = 512
DN_ALPHA = (2 * DEPTH) ** 0.25
LN_EPS = 1e-5

VMEM_LIMIT_BYTES = 56 * 1024 * 1024
MOE_ROWS = 256


def _params(*sem):
    return pltpu.CompilerParams(dimension_semantics=sem, vmem_limit_bytes=VMEM_LIMIT_BYTES)


def _mm_kernel(x_ref, w_ref, o_ref):
    o_ref[...] = jnp.dot(x_ref[...], w_ref[...].astype(BF16), preferred_element_type=F32).astype(o_ref.dtype)


def matmul(x, w, out_dtype, col0=0, ncols=None, tm=512, tn=1024):
    m, k = x.shape
    n = w.shape[1] if ncols is None else ncols
    tm, tn = min(tm, m), min(tn, n)
    first = col0 // tn
    return pl.pallas_call(
        _mm_kernel,
        grid=(n // tn, m // tm),
        in_specs=[pl.BlockSpec((tm, k), lambda j, i: (i, 0)),
                  pl.BlockSpec((k, tn), lambda j, i: (0, first + j))],
        out_specs=pl.BlockSpec((tm, tn), lambda j, i: (i, j)),
        out_shape=jax.ShapeDtypeStruct((m, n), out_dtype),
        compiler_params=_params("arbitrary", "arbitrary"),
        name="matmul",
    )(x, w)


def _ln_rows(z, g, b):
    mu = jnp.mean(z, axis=-1, keepdims=True)
    zc = z - mu
    var = jnp.mean(zc * zc, axis=-1, keepdims=True)
    return zc * lax.rsqrt(var + LN_EPS) * g + b


def _even_out_kernel(xa_ref, xb_ref, wa_ref, wb_ref, r_ref, g_ref, b_ref, o_ref, ob_ref):
    acc = lax.dot_general(xa_ref[...], wa_ref[...], (((0,), (0,)), ((), ())), preferred_element_type=F32)
    acc += jnp.dot(xb_ref[...], wb_ref[...], preferred_element_type=F32)
    y = _ln_rows(DN_ALPHA * r_ref[...] + acc, g_ref[...], b_ref[...])
    o_ref[...] = y
    ob_ref[...] = y.astype(BF16)


def even_out_proj(y_hy_t, y_da, w_out, resid, g, b, tm=512):
    ka, m = y_hy_t.shape
    kb = y_da.shape[1]
    d = w_out.shape[1]
    row = lambda i: (i, 0)
    fixed = lambda i: (0, 0)
    return pl.pallas_call(
        _even_out_kernel,
        grid=(m // tm,),
        in_specs=[pl.BlockSpec((ka, tm), lambda i: (0, i)), pl.BlockSpec((tm, kb), row),
                  pl.BlockSpec((ka, d), fixed), pl.BlockSpec((kb, d), fixed),
                  pl.BlockSpec((tm, d), row), pl.BlockSpec((1, d), fixed), pl.BlockSpec((1, d), fixed)],
        out_specs=[pl.BlockSpec((tm, d), row), pl.BlockSpec((tm, d), row)],
        out_shape=[jax.ShapeDtypeStruct((m, d), F32), jax.ShapeDtypeStruct((m, d), BF16)],
        compiler_params=_params("arbitrary"),
        name="even_out_proj",
    )(y_hy_t, y_da, w_out[:ka], w_out[ka:], resid, g.reshape(1, d), b.reshape(1, d))


def _odd_out_kernel(gate_ref, y0_ref, y1_ref, w_ref, r_ref, g_ref, b_ref, o_ref, ob_ref):
    x = jax.nn.gelu(gate_ref[...].astype(F32)) * (y0_ref[0].astype(F32) + y1_ref[0].astype(F32))
    acc = jnp.dot(x.astype(BF16), w_ref[...], preferred_element_type=F32)
    y = _ln_rows(DN_ALPHA * r_ref[...] + acc, g_ref[...], b_ref[...])
    o_ref[...] = y
    ob_ref[...] = y.astype(BF16)


def odd_out_proj(u, yd, w_out, resid, g, b, tm=256):
    m = u.shape[0]
    k, d = w_out.shape
    row = lambda i: (i, 0)
    fixed = lambda i: (0, 0)
    return pl.pallas_call(
        _odd_out_kernel,
        grid=(m // tm,),
        in_specs=[pl.BlockSpec((tm, k), row),
                  pl.BlockSpec((1, tm, k), lambda i: (0, i, 0)),
                  pl.BlockSpec((1, tm, k), lambda i: (1, i, 0)),
                  pl.BlockSpec((k, d), fixed),
                  pl.BlockSpec((tm, d), row), pl.BlockSpec((1, d), fixed), pl.BlockSpec((1, d), fixed)],
        out_specs=[pl.BlockSpec((tm, d), row), pl.BlockSpec((tm, d), row)],
        out_shape=[jax.ShapeDtypeStruct((m, d), F32), jax.ShapeDtypeStruct((m, d), BF16)],
        compiler_params=_params("arbitrary"),
        name="odd_out_proj",
    )(u, yd, yd, w_out, resid, g.reshape(1, d), b.reshape(1, d))


def _combine_ln_kernel(h_ref, y0_ref, y1_ref, gate_ref, g_ref, b_ref, o_ref, ob_ref):
    gate = gate_ref[...]
    ffn = gate[:, 0:1] * y0_ref[...].astype(F32) + gate[:, 1:2] * y1_ref[...].astype(F32)
    y = _ln_rows(DN_ALPHA * h_ref[...] + ffn, g_ref[...], b_ref[...])
    o_ref[...] = y
    ob_ref[...] = y.astype(BF16)


def combine_ln(h, y0, y1, gate, g, b, tm=512):
    m, d = h.shape
    row = lambda i: (i, 0)
    fixed = lambda i: (0, 0)
    return pl.pallas_call(
        _combine_ln_kernel,
        grid=(m // tm,),
        in_specs=[pl.BlockSpec((tm, d), row), pl.BlockSpec((tm, d), row), pl.BlockSpec((tm, d), row),
                  pl.BlockSpec((tm, TOP_K), row), pl.BlockSpec((1, d), fixed), pl.BlockSpec((1, d), fixed)],
        out_specs=[pl.BlockSpec((tm, d), row), pl.BlockSpec((tm, d), row)],
        out_shape=[jax.ShapeDtypeStruct((m, d), F32), jax.ShapeDtypeStruct((m, d), BF16)],
        compiler_params=_params("arbitrary"),
        name="combine_ln",
    )(h, y0, y1, gate, g.reshape(1, d), b.reshape(1, d))


ONES_ROWS = 16
POS_SPLIT = 16


def _attn_kernel(slopes_ref, lam_ref, qt_ref, k_ref, vt_ref, fq_ref, fk_ref, g_ref, o_ref, m_s, acc_s, sa_s, sb_s,
                 *, tq, tk, seq, out_scale):
    head = pl.program_id(1)
    i0 = pl.program_id(2) * tq
    slope = slopes_ref[head]
    lam = lam_ref[0]
    qt = qt_ref[...]
    row = lax.broadcasted_iota(jnp.int32, qt.shape, 0)
    zero = jnp.zeros_like(qt)
    q2 = jnp.concatenate([jnp.where(row < DA_HEAD_DIM, qt, zero),
                          jnp.where(row >= DA_HEAD_DIM, qt, zero)], axis=1)
    fq = fq_ref[0]
    fq2 = jnp.concatenate([fq, fq], axis=1)
    q_aug = jnp.concatenate([q2, fq2], axis=0)
    fk = fk_ref[0]
    fk_neg = -fk
    ones = jnp.ones((ONES_ROWS, tk), BF16)
    m_s[...] = jnp.full(m_s.shape, -jnp.inf, F32)
    acc_s[...] = jnp.zeros(acc_s.shape, F32)
    j_diag = i0 // tk

    def update(j, s, shift):
        off = pl.multiple_of(j * tk, tk)
        va = jnp.concatenate([vt_ref[:, pl.ds(off, tk)], ones], axis=0)
        m_prev = m_s[...]
        m_new = jnp.maximum(m_prev, jnp.max(s, axis=0, keepdims=True) - shift)
        p = jnp.exp(s - (m_new + shift))
        alpha = jnp.exp(m_prev - m_new)
        acc_s[...] = alpha * acc_s[...] + jnp.dot(va, p.astype(BF16), preferred_element_type=F32)
        m_s[...] = m_new

    def chunk_of(t):
        return t + (t >= j_diag).astype(jnp.int32)

    def scores(t):
        j = chunk_of(t)
        off = pl.multiple_of(j * tk, tk)
        ka = jnp.concatenate([k_ref[0, pl.ds(off, tk), :], jnp.where(j < j_diag, fk, fk_neg)], axis=1)
        return jnp.dot(ka, q_aug, preferred_element_type=F32)

    def consume(s_ref, t):
        j = chunk_of(t)
        update(j, s_ref[...], slope * jnp.abs(i0 - j * tk).astype(F32))

    j0 = pl.multiple_of(j_diag * tk, tk)
    dist = jnp.abs(lax.broadcasted_iota(jnp.int32, (tk, tq), 0) + (j0 - i0)
                   - lax.broadcasted_iota(jnp.int32, (tk, tq), 1)).astype(F32)
    bias = slope * dist
    s_diag = jnp.dot(k_ref[0, pl.ds(j0, tk), :], q2, preferred_element_type=F32)
    sa_s[...] = s_diag - jnp.concatenate([bias, bias], axis=1)
    sb_s[...] = scores(jnp.int32(0))
    update(j_diag, sa_s[...], 0.0)
    n_off = seq // tk - 1

    def pair(t, carry):
        sa_s[...] = scores(2 * t + 1)
        consume(sb_s, 2 * t)
        sb_s[...] = scores(2 * t + 2)
        consume(sa_s, 2 * t + 1)
        return carry

    lax.fori_loop(0, n_off // 2, pair, 0)
    consume(sb_s, jnp.int32(n_off - 1))

    acc = acc_s[...]
    o1 = acc[:DA_V_DIM, :tq] / acc[DA_V_DIM:DA_V_DIM + 1, :tq]
    o2 = acc[:DA_V_DIM, tq:] / acc[DA_V_DIM:DA_V_DIM + 1, tq:]
    o = o1 - lam * o2
    o = o * lax.rsqrt(jnp.mean(o * o, axis=0, keepdims=True) + LN_EPS) * g_ref[...]
    o_ref[0] = (o * out_scale).T.astype(o_ref.dtype)


def diff_attention(qt, k, vt, lam, subln_g, lambda_init, tq=512, tk=512):
    bsz, seq, width = k.shape
    nq = seq // tq
    slopes = 2.0 ** (-(8.0 / DA_HEADS) * jnp.arange(1, DA_HEADS + 1, dtype=F32))

    def split(n):
        pos = jnp.arange(n, dtype=jnp.int32)
        return (pos // POS_SPLIT * POS_SPLIT).astype(F32), (pos % POS_SPLIT).astype(F32)

    qhi, qlo = split(tq)
    khi, klo = split(tk)
    sl = slopes[:, None]
    fq = jnp.zeros((DA_HEADS, DA_V_DIM, tq), F32)
    fq = fq.at[:, 0].set(1.0).at[:, 1].set(1.0).at[:, 2].set(-sl * qhi).at[:, 3].set(-sl * qlo)
    fk = jnp.zeros((DA_HEADS, tk, DA_V_DIM), F32)
    fk = fk.at[:, :, 0].set(sl * khi).at[:, :, 1].set(sl * klo).at[:, :, 2].set(1.0).at[:, :, 3].set(1.0)
    kern = functools.partial(_attn_kernel, tq=tq, tk=tk, seq=seq, out_scale=1.0 - lambda_init)
    smem = pl.BlockSpec(memory_space=pltpu.SMEM)
    return pl.pallas_call(
        kern,
        grid=(bsz, DA_HEADS, seq // tq),
        in_specs=[smem, smem,
                  pl.BlockSpec((DA_V_DIM, tq), lambda b, h, i: (h, b * nq + i)),
                  pl.BlockSpec((1, seq, DA_V_DIM), lambda b, h, i: (b, 0, h)),
                  pl.BlockSpec((DA_V_DIM, seq), lambda b, h, i: (h, b)),
                  pl.BlockSpec((1, DA_V_DIM, tq), lambda b, h, i: (h, 0, 0)),
                  pl.BlockSpec((1, tk, DA_V_DIM), lambda b, h, i: (h, 0, 0)),
                  pl.BlockSpec((DA_V_DIM, 1), lambda b, h, i: (0, 0))],
        out_specs=pl.BlockSpec((1, tq, DA_V_DIM), lambda b, h, i: (b, i, h)),
        out_shape=jax.ShapeDtypeStruct((bsz, seq, width), BF16),
        scratch_shapes=[pltpu.VMEM((1, 2 * tq), F32), pltpu.VMEM((DA_V_DIM + ONES_ROWS, 2 * tq), F32),
                        pltpu.VMEM((tk, 2 * tq), F32), pltpu.VMEM((tk, 2 * tq), F32)],
        compiler_params=_params("arbitrary", "arbitrary", "arbitrary"),
        name="diff_attention",
    )(slopes, lam.reshape(1).astype(F32), qt, k, vt, fq.astype(BF16), fk.astype(BF16),
      subln_g.reshape(DA_V_DIM, 1).astype(F32))


HALO = 16


def _rglru_kernel(xr_ref, prev_ref, next_ref, cw_ref, cb_ref, wa_ref, wx_ref, ba_ref, bx_ref, kk_ref, y_ref,
                  a_s, b_s, y_s, h_s, *, ts, nt):
    direction = pl.program_id(0)
    t = pl.program_id(2)
    chunk = t + direction * (nt - 1 - 2 * t)

    @pl.when(t == 0)
    def _():
        h_s[...] = jnp.zeros(h_s.shape, F32)

    xf = jnp.concatenate([jnp.where(chunk == 0, 0.0, prev_ref[0].astype(F32)), xr_ref[0].astype(F32),
                          jnp.where(chunk == nt - 1, 0.0, next_ref[0].astype(F32))], axis=0)
    rows = ts + 2 * HALO

    def tap(j):
        shifted = xf if j == 2 else pltpu.roll(xf, (2 - j) % rows, 0)
        return cw_ref[j:j + 1, :] * shifted[HALO:HALO + ts]

    x = cb_ref[...] + sum(tap(j) for j in range(RG_CONV))
    xb = x.astype(BF16)
    xh = 0.5 * x
    for n in range(RG_BLOCKS):
        cols = slice(n * RG_BLOCK_W, (n + 1) * RG_BLOCK_W)
        xs = xb[:, cols]
        tr = jnp.tanh(jnp.dot(xs, wa_ref[0, n], preferred_element_type=F32) + ba_ref[0, :, cols])
        ti = jnp.tanh(jnp.dot(xs, wx_ref[0, n], preferred_element_type=F32) + bx_ref[0, :, cols])
        kk = kk_ref[0, :, cols]
        a = jnp.exp2(kk * tr + kk)
        a_s[:, cols] = a
        s = 1.0 - a * a
        root = jnp.where(s > 0.0, s * lax.rsqrt(s), 0.0)
        b_s[:, cols] = root * (ti * xh[:, cols] + xh[:, cols])

    def step(k, h):
        row = jnp.where(direction == 0, k, ts - 1 - k)
        h = a_s[pl.ds(row, 1), :] * h + b_s[pl.ds(row, 1), :]
        y_s[pl.ds(row, 1), :] = h
        return h

    h_s[...] = lax.fori_loop(0, ts, step, h_s[...], unroll=8)
    y_ref[0, 0] = y_s[...].astype(y_ref.dtype)


def rglru_bidir(u, conv_w, conv_b, wa, ba, wx, bx, lam, ts=256):
    bsz, seq, w2 = u.shape
    w = w2 // 2
    nt = seq // ts
    per = ts // HALO
    kk = (-0.5 * RG_C * math.log2(math.e)) * jax.nn.softplus(-lam.astype(F32)).reshape(2, 1, w)
    chunk = lambda d, t: t + d * (nt - 1 - 2 * t)
    dmap4 = lambda d, b, t: (d, 0, 0, 0)
    dmap3 = lambda d, b, t: (d, 0, 0)
    fixed = lambda d, b, t: (0, 0)
    return pl.pallas_call(
        functools.partial(_rglru_kernel, ts=ts, nt=nt),
        grid=(2, bsz, nt),
        in_specs=[pl.BlockSpec((1, ts, w), lambda d, b, t: (b, chunk(d, t), 1)),
                  pl.BlockSpec((1, HALO, w), lambda d, b, t: (b, jnp.maximum(chunk(d, t) * per - 1, 0), 1)),
                  pl.BlockSpec((1, HALO, w),
                               lambda d, b, t: (b, jnp.minimum((chunk(d, t) + 1) * per, seq // HALO - 1), 1)),
                  pl.BlockSpec((RG_CONV, w), fixed), pl.BlockSpec((1, w), fixed),
                  pl.BlockSpec((1, RG_BLOCKS, RG_BLOCK_W, RG_BLOCK_W), dmap4),
                  pl.BlockSpec((1, RG_BLOCKS, RG_BLOCK_W, RG_BLOCK_W), dmap4),
                  pl.BlockSpec((1, 1, w), dmap3), pl.BlockSpec((1, 1, w), dmap3), pl.BlockSpec((1, 1, w), dmap3)],
        out_specs=pl.BlockSpec((1, 1, ts, w), lambda d, b, t: (d, b, chunk(d, t), 0)),
        out_shape=jax.ShapeDtypeStruct((2, bsz, seq, w), BF16),
        scratch_shapes=[pltpu.VMEM((ts, w), F32), pltpu.VMEM((ts, w), F32), pltpu.VMEM((ts, w), F32),
                        pltpu.VMEM((1, w), F32)],
        compiler_params=_params("arbitrary", "arbitrary", "arbitrary"),
        name="rglru_bidir",
    )(u, u, u, conv_w.astype(F32), conv_b.astype(F32).reshape(1, w), (0.5 * wa).astype(BF16),
      (0.5 * wx).astype(BF16), 0.5 * ba.astype(F32).reshape(2, 1, w), 0.5 * bx.astype(F32).reshape(2, 1, w), kk)


def _router_kernel(h_ref, wt_ref, b_ref, tri_ref, e_ref, g_ref, r_ref, cnt_ref, seen_s):
    h = h_ref[...]
    h_hi = h.astype(BF16)
    h_lo = (h - h_hi.astype(F32)).astype(BF16)
    nt_dot = lambda a, b: lax.dot_general(a, b, (((1,), (1,)), ((), ())), preferred_element_type=F32)
    logits = nt_dot(wt_ref[0], h_hi) + (nt_dot(wt_ref[0], h_lo) + nt_dot(wt_ref[1], h_hi))
    tm = logits.shape[1]
    s = jax.nn.sigmoid(logits).reshape(N_GROUPS, EXPERTS_PER_GROUP, tm)
    sel = s + b_ref[...]
    idx = lax.broadcasted_iota(jnp.int32, sel.shape, 1)
    big = jnp.int32(EXPERTS_PER_GROUP)
    v1 = jnp.max(sel, axis=1, keepdims=True)
    i1 = jnp.min(jnp.where(sel == v1, idx, big), axis=1, keepdims=True)
    rest = jnp.where(idx == i1, -jnp.inf, sel)
    v2 = jnp.max(rest, axis=1, keepdims=True)
    i2 = jnp.min(jnp.where(rest == v2, idx, big), axis=1, keepdims=True)
    score = v1 + v2
    gidx = lax.broadcasted_iota(jnp.int32, score.shape, 0)
    best = jnp.max(score, axis=0, keepdims=True)
    grp = jnp.min(jnp.where(score == best, gidx, jnp.int32(N_GROUPS)), axis=0, keepdims=True)
    pick = gidx == grp
    l1 = jnp.sum(jnp.where(pick, i1, 0), axis=0)
    l2 = jnp.sum(jnp.where(pick, i2, 0), axis=0)
    s_g = jnp.sum(jnp.where(pick, s, 0.0), axis=0)
    eidx = lax.broadcasted_iota(jnp.int32, s_g.shape, 0)
    w1 = jnp.sum(jnp.where(eidx == l1, s_g, 0.0), axis=0, keepdims=True)
    w2 = jnp.sum(jnp.where(eidx == l2, s_g, 0.0), axis=0, keepdims=True)
    tot = w1 + w2
    base = grp[0] * EXPERTS_PER_GROUP
    e1, e2 = base + l1, base + l2
    e_ref[...] = jnp.concatenate([e1, e2], axis=0)
    g_ref[...] = jnp.concatenate([w1 / tot, w2 / tot], axis=0)

    @pl.when(pl.program_id(0) == 0)
    def _():
        seen_s[...] = jnp.zeros(seen_s.shape, F32)

    eall = lax.broadcasted_iota(jnp.int32, (N_EXPERTS, tm), 0)
    hit1, hit2 = eall == e1, eall == e2
    both = jnp.where(hit1, 1.0, jnp.where(hit2, 1.0, 0.0))
    incl = jnp.dot(both.astype(BF16), tri_ref[...], preferred_element_type=F32)
    before = incl - both + seen_s[...]
    r_ref[...] = jnp.concatenate([jnp.sum(jnp.where(hit1, before, 0.0), axis=0, keepdims=True),
                                  jnp.sum(jnp.where(hit2, before, 0.0), axis=0, keepdims=True)],
                                 axis=0).astype(jnp.int32)
    seen_s[...] = seen_s[...] + incl[:, tm - 1:tm]
    cnt_ref[...] = seen_s[...].astype(jnp.int32)


def route(h, router_w, router_b, tm=1024):
    n, d = h.shape
    tri = (jnp.arange(tm)[:, None] <= jnp.arange(tm)[None, :]).astype(BF16)
    wt = router_w.astype(F32).T
    wt_hi = wt.astype(BF16)
    wt_lo = (wt - wt_hi.astype(F32)).astype(BF16)
    tile = pl.BlockSpec((TOP_K, tm), lambda i: (0, i))
    return pl.pallas_call(
        _router_kernel,
        grid=(n // tm,),
        in_specs=[pl.BlockSpec((tm, d), lambda i: (i, 0)),
                  pl.BlockSpec((2, N_EXPERTS, d), lambda i: (0, 0, 0)),
                  pl.BlockSpec((N_GROUPS, EXPERTS_PER_GROUP, 1), lambda i: (0, 0, 0)),
                  pl.BlockSpec((tm, tm), lambda i: (0, 0))],
        out_specs=[tile, tile, tile, pl.BlockSpec((N_EXPERTS, 1), lambda i: (0, 0))],
        out_shape=[jax.ShapeDtypeStruct((TOP_K, n), jnp.int32), jax.ShapeDtypeStruct((TOP_K, n), F32),
                   jax.ShapeDtypeStruct((TOP_K, n), jnp.int32), jax.ShapeDtypeStruct((N_EXPERTS, 1), jnp.int32)],
        scratch_shapes=[pltpu.VMEM((N_EXPERTS, 1), F32)],
        compiler_params=_params("arbitrary"),
        name="router",
    )(h, jnp.stack([wt_hi, wt_lo]), router_b.astype(F32).reshape(N_GROUPS, EXPERTS_PER_GROUP, 1), tri)


def _experts_kernel(blk_exp_ref, n_used_ref, x_ref, wg_ref, wu_ref, wd_ref, o_ref, wg_s, wu_s, wd_s):
    i = pl.program_id(0)
    expert = blk_exp_ref[i]
    prev = blk_exp_ref[jnp.maximum(i - 1, 0)]

    @pl.when((i == 0) | (expert != prev))
    def _():
        wg_s[...] = wg_ref[0, 0].astype(BF16)
        wu_s[...] = wu_ref[0, 0].astype(BF16)
        wd_s[...] = wd_ref[0, 0].astype(BF16)

    @pl.when(i < n_used_ref[0])
    def _():
        x = x_ref[...].astype(BF16)
        hg = jnp.dot(x, wg_s[...], preferred_element_type=F32)
        hu = jnp.dot(x, wu_s[...], preferred_element_type=F32)
        hidden = (jax.nn.silu(hg) * hu).astype(BF16)
        o_ref[...] = jnp.dot(hidden, wd_s[...], preferred_element_type=F32).astype(o_ref.dtype)

    @pl.when(i >= n_used_ref[0])
    def _():
        o_ref[...] = jnp.zeros(o_ref.shape, o_ref.dtype)


def experts(xs, blk_exp, n_used, w_gate, w_up, w_down, layer):
    p, d = xs.shape
    f = w_gate.shape[3]
    grid_spec = pltpu.PrefetchScalarGridSpec(
        num_scalar_prefetch=2,
        grid=(p // MOE_ROWS,),
        in_specs=[pl.BlockSpec((MOE_ROWS, d), lambda i, be, nu: (i, 0)),
                  pl.BlockSpec((1, 1, d, f), lambda i, be, nu: (layer, be[i], 0, 0)),
                  pl.BlockSpec((1, 1, d, f), lambda i, be, nu: (layer, be[i], 0, 0)),
                  pl.BlockSpec((1, 1, f, d), lambda i, be, nu: (layer, be[i], 0, 0))],
        out_specs=pl.BlockSpec((MOE_ROWS, d), lambda i, be, nu: (i, 0)),
        scratch_shapes=[pltpu.VMEM((d, f), BF16), pltpu.VMEM((d, f), BF16), pltpu.VMEM((f, d), BF16)],
    )
    return pl.pallas_call(
        _experts_kernel,
        grid_spec=grid_spec,
        out_shape=jax.ShapeDtypeStruct((p, d), BF16),
        compiler_params=_params("arbitrary"),
        name="experts",
    )(blk_exp, n_used, xs, w_gate, w_up, w_down)


def _dispatch_kernel(dest_ref, fill_ref, x_ref, xs_ref, zero_s, fill_sem, row_sem, *, tm, n_blocks):
    base = pl.program_id(0) * tm

    @pl.when(pl.program_id(0) == 0)
    def _():
        zero_s[...] = jnp.zeros(zero_s.shape, zero_s.dtype)

        def zero_block(blk):
            return pltpu.make_async_copy(zero_s, xs_ref.at[pl.ds(pl.multiple_of(blk * MOE_ROWS, MOE_ROWS),
                                                                 MOE_ROWS)], fill_sem)

        def start(blk, carry):
            @pl.when(fill_ref[blk] != 0)
            def _():
                zero_block(blk).start()
            return carry

        def wait(blk, carry):
            @pl.when(fill_ref[blk] != 0)
            def _():
                zero_block(blk).wait()
            return carry

        lax.fori_loop(0, n_blocks, start, 0)
        lax.fori_loop(0, n_blocks, wait, 0)

    def issue(r, carry):
        for k in range(TOP_K):
            pltpu.make_async_copy(x_ref.at[pl.ds(r, 1)], xs_ref.at[pl.ds(dest_ref[k, base + r], 1)],
                                  row_sem).start()
        return carry

    lax.fori_loop(0, tm, issue, 0, unroll=4)
    for k in range(TOP_K):
        pltpu.make_async_copy(x_ref, xs_ref.at[pl.ds(0, tm)], row_sem).wait()


def dispatch_rows(x, dest, fill, tm=512):
    n, d = x.shape
    n_blocks = fill.shape[0]
    grid_spec = pltpu.PrefetchScalarGridSpec(
        num_scalar_prefetch=2,
        grid=(n // tm,),
        in_specs=[pl.BlockSpec((tm, d), lambda i, dst, fl: (i, 0))],
        out_specs=pl.BlockSpec(memory_space=pl.ANY),
        scratch_shapes=[pltpu.VMEM((MOE_ROWS, d), x.dtype), pltpu.SemaphoreType.DMA(()),
                        pltpu.SemaphoreType.DMA(())],
    )
    return pl.pallas_call(
        functools.partial(_dispatch_kernel, tm=tm, n_blocks=n_blocks),
        grid_spec=grid_spec,
        out_shape=jax.ShapeDtypeStruct((n_blocks * MOE_ROWS, d), x.dtype),
        compiler_params=_params("arbitrary"),
        name="dispatch_rows",
    )(dest, fill, x)


def moe_ffn(h, router_w, router_b, w_gate, w_up, w_down, layer, ln_g, ln_b):
    n, d = h.shape
    a = n * TOP_K
    e_idx, gate, rank, counts = route(h, router_w, router_b)
    counts = counts[:, 0]
    padded = (counts + MOE_ROWS - 1) // MOE_ROWS * MOE_ROWS
    pad_end = jnp.cumsum(padded)
    pad_start = pad_end - padded
    experts_iota = jnp.arange(N_EXPERTS, dtype=jnp.int32)[:, None, None]
    dest = jnp.sum(jnp.where(e_idx[None] == experts_iota, pad_start[:, None, None], 0), axis=0) + rank
    n_blocks = a // MOE_ROWS + N_EXPERTS
    blk_start = jnp.arange(n_blocks, dtype=jnp.int32)[:, None] * MOE_ROWS
    blk_exp = jnp.minimum(jnp.sum((pad_end[None, :] <= blk_start).astype(jnp.int32), axis=1), N_EXPERTS - 1)
    n_used = (pad_end[-1:] // MOE_ROWS).astype(jnp.int32)
    ends_expert = jnp.any((pad_end[None, :] == blk_start + MOE_ROWS) & (padded[None, :] > 0), axis=1)
    fill = (ends_expert | (blk_start[:, 0] >= pad_end[-1])).astype(jnp.int32)
    xs = dispatch_rows(h, dest, fill)
    yb = experts(xs, blk_exp, n_used, w_gate, w_up, w_down, layer)
    y0 = yb.at[dest[0]].get(mode="promise_in_bounds")
    y1 = yb.at[dest[1]].get(mode="promise_in_bounds")
    return combine_ln(h, y0, y1, gate.T, ln_g, ln_b)


FFT_R = 128
HALF_R = FFT_R // 2
HY_GROUP = 4


def _proj_t_kernel(w_ref, x_ref, o_ref, *, scale):
    acc = lax.dot_general(w_ref[...].astype(BF16), x_ref[...], (((0,), (1,)), ((), ())),
                          preferred_element_type=F32)
    o_ref[...] = (acc * scale).astype(o_ref.dtype)


def proj_t(w, col0, ncols, x, out_dtype, scale=1.0, tm=512, tn=1024):
    k = w.shape[0]
    n = x.shape[0]
    first = col0 // tm
    return pl.pallas_call(
        functools.partial(_proj_t_kernel, scale=scale),
        grid=(ncols // tm, n // tn),
        in_specs=[pl.BlockSpec((k, tm), lambda i, j: (0, first + i)),
                  pl.BlockSpec((tn, k), lambda i, j: (j, 0))],
        out_specs=pl.BlockSpec((tm, tn), lambda i, j: (i, j)),
        out_shape=jax.ShapeDtypeStruct((ncols, n), out_dtype),
        compiler_params=_params("arbitrary", "arbitrary"),
        name="proj_t",
    )(w, x)


def _dft_constants():
    idx = jnp.arange(FFT_R, dtype=jnp.int32)
    prod = idx[:, None] * idx[None, :]
    ang = (prod % FFT_R).astype(F32) * (2.0 * math.pi / FFT_R)
    f_r, f_i = jnp.cos(ang), -jnp.sin(ang)
    ang_t = prod.astype(F32) * (2.0 * math.pi / (FFT_R * FFT_R))
    g_r, g_i = f_r[:HALF_R], -f_i[:HALF_R]
    return dict(
        la=jnp.block([[f_r[:, :HALF_R], -f_i[:, :HALF_R]], [f_i[:, :HALF_R], f_r[:, :HALF_R]]]).astype(BF16),
        la_real=jnp.concatenate([f_r, f_i], axis=0).astype(BF16),
        rb=jnp.block([[f_r, f_i], [-f_i, f_r]]).astype(BF16),
        rc=jnp.block([[f_r, -f_i], [f_i, f_r]]).astype(BF16),
        ld=(jnp.block([[g_r, -g_i], [g_i, g_r]]) / (FFT_R * FFT_R)).astype(BF16),
        t_r=jnp.cos(ang_t), t_i=-jnp.sin(ang_t))


def _rows_to_lanes(x):
    return jnp.concatenate([x[:FFT_R], x[FFT_R:]], axis=1)


def _fwd_lane_stage(a, t_r, t_i, rb):
    a_r, a_i = a[:FFT_R], a[FFT_R:]
    t2_r = jnp.concatenate([t_r, t_r], axis=1)
    t2_i = jnp.concatenate([t_i, t_i], axis=1)
    b_r = a_r * t2_r - a_i * t2_i
    b_i = a_r * t2_i + a_i * t2_r
    lhs = jnp.concatenate([jnp.concatenate([b_r[:, :FFT_R], b_i[:, :FFT_R]], axis=1),
                           jnp.concatenate([b_r[:, FFT_R:], b_i[:, FFT_R:]], axis=1)], axis=0)
    return jnp.dot(lhs.astype(BF16), rb, preferred_element_type=F32)


def _conv_pairs(ms, kfs, la, rb, rc, ld, t_r, t_i):
    a = [jnp.dot(la, jnp.concatenate([m_r, m_i], axis=0).astype(BF16), preferred_element_type=F32)
         for m_r, m_i in ms]
    z = [_fwd_lane_stage(ai, t_r, t_i, rb) for ai in a]
    c = []
    for zi, kf in zip(z, kfs):
        z_r, z_i, k_r, k_i = zi[:, :FFT_R], zi[:, FFT_R:], kf[:, :FFT_R], kf[:, FFT_R:]
        y = jnp.concatenate([z_r * k_r - z_i * k_i, z_r * k_i + z_i * k_r], axis=1)
        c.append(jnp.dot(y.astype(BF16), rc, preferred_element_type=F32))
    t2_r = jnp.concatenate([t_r, t_r], axis=0)
    t2_i = jnp.concatenate([t_i, t_i], axis=0)
    out = []
    for ci in c:
        c_r, c_i = ci[:, :FFT_R], ci[:, FFT_R:]
        d_r = c_r * t2_r + c_i * t2_i
        d_i = c_i * t2_r - c_r * t2_i
        rhs = jnp.concatenate([_rows_to_lanes(d_r), _rows_to_lanes(d_i)], axis=0)
        x = jnp.dot(ld, rhs.astype(BF16), preferred_element_type=F32)
        out.append((x[:HALF_R], x[HALF_R:]))
    return out


def _short_conv_tile(x, w0, w1, w2, b):
    rows = x.shape[0]
    lane = lax.broadcasted_iota(jnp.int32, x.shape, 1)
    row = lax.broadcasted_iota(jnp.int32, x.shape, 0)
    r = pltpu.roll(x, 1, 1)
    prev = jnp.where(lane == 0, jnp.where(row == 0, 0.0, pltpu.roll(r, 1, 0)), r)
    r = pltpu.roll(x, FFT_R - 1, 1)
    nxt = jnp.where(lane == FFT_R - 1, jnp.where(row == rows - 1, 0.0, pltpu.roll(r, rows - 1, 0)), r)
    return w0 * prev + w1 * x + w2 * nxt + b


def _hyena_kernel(cw_ref, cb_ref, skip_ref, v_ref, x1_ref, x2_ref, kf_ref,
                  la_ref, rb_ref, rc_ref, ld_ref, tr_ref, ti_ref, o_ref, *, tc):
    c_base = pl.program_id(0) * tc
    lane = lax.broadcasted_iota(jnp.int32, (1, 2 * FFT_R), 1)

    def conv_in(ref, part, b, c0):
        tiles = []
        for cc in range(2):
            ch = part * HY_CH + c_base + c0 + cc
            tiles.append(_short_conv_tile(ref[c0 + cc, b], cw_ref[0, ch], cw_ref[1, ch], cw_ref[2, ch], cb_ref[ch]))
        return jnp.concatenate(tiles, axis=1)

    def group(g, carry):
        starts = [2 * (HY_GROUP * g + j) for j in range(HY_GROUP)]
        consts = (la_ref[...], rb_ref[...], rc_ref[...], ld_ref[...], tr_ref[...], ti_ref[...])
        zs = [(conv_in(v_ref, 0, 0, c0), conv_in(v_ref, 0, 1, c0)) for c0 in starts]
        for o, g_ref in enumerate((x1_ref, x2_ref)):
            kfs = [kf_ref[o, pl.ds(c0, 2)].reshape(2 * FFT_R, 2 * FFT_R) for c0 in starts]
            ys = _conv_pairs(zs, kfs, *consts)
            nxt = []
            for c0, (z_r, z_i), (y_r, y_i) in zip(starts, zs, ys):
                sk = jnp.where(lane < FFT_R, skip_ref[o, c_base + c0], skip_ref[o, c_base + c0 + 1])
                nxt.append((conv_in(g_ref, o + 1, 0, c0) * (y_r + sk * z_r),
                            conv_in(g_ref, o + 1, 1, c0) * (y_i + sk * z_i)))
            zs = nxt
        for c0, (z_r, z_i) in zip(starts, zs):
            o_ref[c0, 0] = z_r[:, :FFT_R].astype(o_ref.dtype)
            o_ref[c0 + 1, 0] = z_r[:, FFT_R:].astype(o_ref.dtype)
            o_ref[c0, 1] = z_i[:, :FFT_R].astype(o_ref.dtype)
            o_ref[c0 + 1, 1] = z_i[:, FFT_R:].astype(o_ref.dtype)
        return carry

    lax.fori_loop(0, tc // (2 * HY_GROUP), group, 0)


def hyena_mix(u_t, kf, conv_w, conv_b, skip, consts, tc=16):
    rows, n = u_t.shape
    ch = rows // 3
    u5 = u_t.reshape(rows, 2, HALF_R, FFT_R)
    nct = ch // tc
    smem = pl.BlockSpec(memory_space=pltpu.SMEM)
    part = lambda k: pl.BlockSpec((tc, 2, HALF_R, FFT_R), lambda i: (k * nct + i, 0, 0, 0))
    full2 = lambda a: pl.BlockSpec(a.shape, lambda i: (0, 0))
    mats = [consts[k] for k in ("la", "rb", "rc", "ld", "t_r", "t_i")]
    out = pl.pallas_call(
        functools.partial(_hyena_kernel, tc=tc),
        grid=(nct,),
        in_specs=[smem, smem, smem, part(0), part(1), part(2),
                  pl.BlockSpec((2, tc, FFT_R, 2 * FFT_R), lambda i: (0, i, 0, 0))] + [full2(a) for a in mats],
        out_specs=pl.BlockSpec((tc, 2, HALF_R, FFT_R), lambda i: (i, 0, 0, 0)),
        out_shape=jax.ShapeDtypeStruct((ch, 2, HALF_R, FFT_R), BF16),
        compiler_params=_params("arbitrary"),
        name="hyena_mix",
    )(conv_w.astype(F32), conv_b.astype(F32), skip.astype(F32), u5, u5, u5, kf, *mats)
    return out.reshape(ch, n)


def _hyena_filter_kernel(delta_ref, hf_ref, hb_ref, tpos_ref, la_ref, rb_ref, tr_ref, ti_ref, kf_ref, *, tc):
    c_base = pl.program_id(1) * tc

    def taps_of(c0):
        taps = []
        for cc in range(2):
            k = jnp.concatenate([hf_ref[0, 0, c0 + cc], hb_ref[0, 0, c0 + cc]], axis=0)
            k = k * jnp.exp(-tpos_ref[...] * delta_ref[c_base + c0 + cc])
            taps.append(k * lax.rsqrt(jnp.sum(k * k, keepdims=True)))
        return jnp.concatenate(taps, axis=1).astype(BF16)

    def group(g, carry):
        starts = [2 * (HY_GROUP * g + j) for j in range(HY_GROUP)]
        a = [jnp.dot(la_ref[...], taps_of(c0), preferred_element_type=F32) for c0 in starts]
        z = [_fwd_lane_stage(ai, tr_ref[...], ti_ref[...], rb_ref[...]) for ai in a]
        for c0, zi in zip(starts, z):
            kf_ref[0, pl.ds(c0, 2)] = zi.reshape(2, FFT_R, 2 * FFT_R)
        return carry

    lax.fori_loop(0, tc // (2 * HY_GROUP), group, 0)


def hyena_filter_spectra(seq, w1, b1, w2, b2, w3, b3, freq, w4, consts, tc=16):
    t = jnp.linspace(0.0, 1.0, seq, dtype=F32)[:, None]
    omega = 2.0 * math.pi * jnp.arange(seq, dtype=F32)[:, None] / seq
    bands = jnp.linspace(1e-4, HF_BANDS - 1, HF_BANDS, dtype=F32)[None, :]
    ang = omega * bands
    z = jnp.concatenate([t, jnp.cos(ang), -jnp.sin(ang)], -1)
    fr = freq.astype(F32)
    hid = jnp.sin(fr * (z @ w1.astype(F32) + b1.astype(F32)))
    hid = jnp.sin(fr * (hid @ w2.astype(F32) + b2.astype(F32)))
    hid = jnp.sin(fr * (hid @ w3.astype(F32) + b3.astype(F32)))
    back = lambda a: jnp.concatenate([jnp.zeros_like(a[:1]), a[:0:-1]], axis=0)
    w4t = w4.astype(BF16).T
    half = HY_ORDER * HY_CH
    h_fwd = matmul(w4t[:half], hid.T.astype(BF16), F32)
    h_bwd = matmul(w4t[half:], back(hid).T.astype(BF16), F32)
    shape5 = (HY_ORDER, HY_CH, HALF_R, FFT_R)
    tpos = jnp.concatenate([t, back(t)], axis=0).reshape(FFT_R, FFT_R)
    max_decay = math.log(HF_TARGET) / HF_FAST
    min_decay = math.log(HF_TARGET) / HF_SLOW
    deltas = jnp.abs(jnp.linspace(min_decay, max_decay, HY_CH, dtype=F32))
    taps = pl.BlockSpec((1, 1, tc, HALF_R, FFT_R), lambda o, i: (0, o, i, 0, 0))
    full2 = lambda a: pl.BlockSpec(a.shape, lambda o, i: (0, 0))
    mats = [consts[k] for k in ("la_real", "rb", "t_r", "t_i")]
    return pl.pallas_call(
        functools.partial(_hyena_filter_kernel, tc=tc),
        grid=(HY_ORDER, HY_CH // tc),
        in_specs=[pl.BlockSpec(memory_space=pltpu.SMEM), taps, taps, full2(tpos)] + [full2(a) for a in mats],
        out_specs=pl.BlockSpec((1, tc, FFT_R, 2 * FFT_R), lambda o, i: (o, i, 0, 0)),
        out_shape=jax.ShapeDtypeStruct((HY_ORDER, HY_CH, FFT_R, 2 * FFT_R), F32),
        compiler_params=_params("arbitrary", "arbitrary"),
        name="hyena_filter_spectra",
    )(deltas, h_fwd.reshape((1,) + shape5), h_bwd.reshape((1,) + shape5), tpos, *mats)


def kernel(x, ev_w_in, ev_hy_conv_w, ev_hy_conv_b, ev_hf_w1, ev_hf_b1, ev_hf_w2, ev_hf_b2, ev_hf_w3, ev_hf_b3,
           ev_hf_freq, ev_hf_w4, ev_hy_skip, ev_lam_q1, ev_lam_k1, ev_lam_q2, ev_lam_k2, ev_subln_g, ev_w_out,
           od_w_in, od_conv_w, od_conv_b, od_wa, od_ba, od_wx, od_bx, od_lam, od_w_out, ln1_g, ln1_b, ln2_g,
           ln2_b, router_w, router_b, ex_w_gate, ex_w_up, ex_w_down):
    bsz, seq, d = x.shape
    n = bsz * seq
    h = x.reshape(n, d).astype(F32)
    hb = h.astype(BF16)
    for layer in range(DEPTH):
        i = layer // 2
        if layer % 2 == 0:
            w_in = ev_w_in[i]
            c0 = 3 * HY_CH
            u_t = proj_t(w_in, 0, c0, hb, F32)
            qt = proj_t(w_in, c0, DA_QK, hb, BF16, scale=DA_HEAD_DIM ** -0.5)
            k = matmul(hb, w_in, BF16, col0=c0 + DA_QK, ncols=DA_QK)
            vt = proj_t(w_in, c0 + 2 * DA_QK, DA_WIDTH, hb, BF16)
            consts = _dft_constants()
            kf = hyena_filter_spectra(seq, ev_hf_w1[i], ev_hf_b1[i], ev_hf_w2[i], ev_hf_b2[i],
                                      ev_hf_w3[i], ev_hf_b3[i], ev_hf_freq[i], ev_hf_w4[i], consts)
            y_hy = hyena_mix(u_t, kf, ev_hy_conv_w[i], ev_hy_conv_b[i], ev_hy_skip[i], consts)
            lambda_init = 0.8 - 0.6 * math.exp(-0.3 * layer)
            lam = (jnp.exp(jnp.sum(ev_lam_q1[i].astype(F32) * ev_lam_k1[i].astype(F32)))
                   - jnp.exp(jnp.sum(ev_lam_q2[i].astype(F32) * ev_lam_k2[i].astype(F32))) + lambda_init)
            y_da = diff_attention(qt, k.reshape(bsz, seq, DA_QK), vt,
                                  lam, ev_subln_g[i], lambda_init).reshape(n, DA_WIDTH)
            h, hb = even_out_proj(y_hy, y_da, ev_w_out[i].astype(BF16), h, ln1_g[layer], ln1_b[layer])
        else:
            u = matmul(hb, od_w_in[i], BF16)
            yd = rglru_bidir(u.reshape(bsz, seq, 2 * RG_WIDTH), od_conv_w[i], od_conv_b[i], od_wa[i], od_ba[i],
                             od_wx[i], od_bx[i], od_lam[i]).reshape(2, n, RG_WIDTH)
            h, hb = odd_out_proj(u, yd, od_w_out[i].astype(BF16), h, ln1_g[layer], ln1_b[layer])
        h, hb = moe_ffn(h, router_w, router_b, ex_w_gate, ex_w_up, ex_w_down, layer,
                        ln2_g[layer], ln2_b[layer])
    return h.reshape(bsz, seq, d).astype(x.dtype)
```

```python
import functools
import math

import jax
import jax.numpy as jnp
from jax import lax
from jax.experimental import pallas as pl
from jax.experimental.pallas import tpu as pltpu

F32 = jnp.float32
BF16 = jnp.bfloat16

D_MODEL = 2048
DEPTH = 2
HY_CH = D_MODEL // 2
HY_ORDER = 2
HF_BANDS = 16
HF_TARGET = 1e-2
HF_FAST = 0.3
HF_SLOW = 1.5
DA_HEADS = 8
DA_HEAD_DIM = 64
DA_V_DIM = 2 * DA_HEAD_DIM
DA_QK = DA_HEADS * 2 * DA_HEAD_DIM
DA_WIDTH = DA_HEADS * DA_V_DIM
RG_WIDTH = D_MODEL
RG_BLOCKS = 8
RG_BLOCK_W = RG_WIDTH // RG_BLOCKS
RG_C = 8.0
RG_CONV = 4
N_EXPERTS = 32
N_GROUPS = 4
EXPERTS_PER_GROUP = N_EXPERTS // N_GROUPS
TOP_K = 2
D_FF = 512
DN_ALPHA = (2 * DEPTH) ** 0.25
LN_EPS = 1e-5

VMEM_LIMIT_BYTES = 56 * 1024 * 1024
MOE_ROWS = 256


def _params(*sem):
    return pltpu.CompilerParams(dimension_semantics=sem, vmem_limit_bytes=VMEM_LIMIT_BYTES)


def _mm_kernel(x_ref, w_ref, o_ref):
    o_ref[...] = jnp.dot(x_ref[...], w_ref[...].astype(BF16), preferred_element_type=F32).astype(o_ref.dtype)


def matmul(x, w, out_dtype, col0=0, ncols=None, tm=512, tn=1024):
    m, k = x.shape
    n = w.shape[1] if ncols is None else ncols
    tm, tn = min(tm, m), min(tn, n)
    first = col0 // tn
    return pl.pallas_call(
        _mm_kernel,
        grid=(n // tn, m // tm),
        in_specs=[pl.BlockSpec((tm, k), lambda j, i: (i, 0)),
                  pl.BlockSpec((k, tn), lambda j, i: (0, first + j))],
        out_specs=pl.BlockSpec((tm, tn), lambda j, i: (i, j)),
        out_shape=jax.ShapeDtypeStruct((m, n), out_dtype),
        compiler_params=_params("arbitrary", "arbitrary"),
        name="matmul",
    )(x, w)


def _ln_rows(z, g, b):
    mu = jnp.mean(z, axis=-1, keepdims=True)
    zc = z - mu
    var = jnp.mean(zc * zc, axis=-1, keepdims=True)
    return zc * lax.rsqrt(var + LN_EPS) * g + b


def _pack_halves(y):
    half = y.shape[1] // 2
    hi = pltpu.bitcast(y[:, :half].astype(BF16).astype(F32), jnp.uint32)
    lo = pltpu.bitcast(y[:, half:].astype(BF16).astype(F32), jnp.uint32)
    return hi | (lo >> 16)


def _unpack_halves(p):
    hi = pltpu.bitcast(p & jnp.uint32(0xFFFF0000), F32)
    lo = pltpu.bitcast(p << 16, F32)
    return jnp.concatenate([hi, lo], axis=1).astype(BF16)


def _even_out_kernel(xa_ref, xb_ref, wa_ref, wb_ref, r_ref, g_ref, b_ref, o_ref, op_ref):
    acc = lax.dot_general(xa_ref[...], wa_ref[...], (((0,), (0,)), ((), ())), preferred_element_type=F32)
    acc += jnp.dot(xb_ref[...], wb_ref[...], preferred_element_type=F32)
    y = _ln_rows(DN_ALPHA * r_ref[...] + acc, g_ref[...], b_ref[...])
    o_ref[...] = y
    op_ref[...] = _pack_halves(y)


def even_out_proj(y_hy_t, y_da, w_out, resid, g, b, tm=512):
    ka, m = y_hy_t.shape
    kb = y_da.shape[1]
    d = w_out.shape[1]
    row = lambda i: (i, 0)
    fixed = lambda i: (0, 0)
    return pl.pallas_call(
        _even_out_kernel,
        grid=(m // tm,),
        in_specs=[pl.BlockSpec((ka, tm), lambda i: (0, i)), pl.BlockSpec((tm, kb), row),
                  pl.BlockSpec((ka, d), fixed), pl.BlockSpec((kb, d), fixed),
                  pl.BlockSpec((tm, d), row), pl.BlockSpec((1, d), fixed), pl.BlockSpec((1, d), fixed)],
        out_specs=[pl.BlockSpec((tm, d), row), pl.BlockSpec((tm, d // 2), row)],
        out_shape=[jax.ShapeDtypeStruct((m, d), F32), jax.ShapeDtypeStruct((m, d // 2), jnp.uint32)],
        compiler_params=_params("arbitrary"),
        name="even_out_proj",
    )(y_hy_t, y_da, w_out[:ka], w_out[ka:], resid, g.reshape(1, d), b.reshape(1, d))


def _odd_out_kernel(gate_ref, y0_ref, y1_ref, w_ref, r_ref, g_ref, b_ref, o_ref, op_ref):
    x = jax.nn.gelu(gate_ref[...].astype(F32)) * (y0_ref[0].astype(F32) + y1_ref[0].astype(F32))
    acc = jnp.dot(x.astype(BF16), w_ref[...], preferred_element_type=F32)
    y = _ln_rows(DN_ALPHA * r_ref[...] + acc, g_ref[...], b_ref[...])
    o_ref[...] = y
    op_ref[...] = _pack_halves(y)


def odd_out_proj(u, yd, w_out, resid, g, b, tm=256):
    m = u.shape[0]
    k, d = w_out.shape
    row = lambda i: (i, 0)
    fixed = lambda i: (0, 0)
    return pl.pallas_call(
        _odd_out_kernel,
        grid=(m // tm,),
        in_specs=[pl.BlockSpec((tm, k), row),
                  pl.BlockSpec((1, tm, k), lambda i: (0, i, 0)),
                  pl.BlockSpec((1, tm, k), lambda i: (1, i, 0)),
                  pl.BlockSpec((k, d), fixed),
                  pl.BlockSpec((tm, d), row), pl.BlockSpec((1, d), fixed), pl.BlockSpec((1, d), fixed)],
        out_specs=[pl.BlockSpec((tm, d), row), pl.BlockSpec((tm, d // 2), row)],
        out_shape=[jax.ShapeDtypeStruct((m, d), F32), jax.ShapeDtypeStruct((m, d // 2), jnp.uint32)],
        compiler_params=_params("arbitrary"),
        name="odd_out_proj",
    )(u, yd, yd, w_out, resid, g.reshape(1, d), b.reshape(1, d))


def _combine_ln_kernel(h_ref, y0_ref, y1_ref, gate_ref, g_ref, b_ref, o_ref, ob_ref):
    gate = gate_ref[...]
    ffn = gate[:, 0:1] * y0_ref[...].astype(F32) + gate[:, 1:2] * y1_ref[...].astype(F32)
    y = _ln_rows(DN_ALPHA * h_ref[...] + ffn, g_ref[...], b_ref[...])
    o_ref[...] = y
    ob_ref[...] = y.astype(BF16)


def combine_ln(h, y0, y1, gate, g, b, tm=512):
    m, d = h.shape
    row = lambda i: (i, 0)
    fixed = lambda i: (0, 0)
    return pl.pallas_call(
        _combine_ln_kernel,
        grid=(m // tm,),
        in_specs=[pl.BlockSpec((tm, d), row), pl.BlockSpec((tm, d), row), pl.BlockSpec((tm, d), row),
                  pl.BlockSpec((tm, TOP_K), row), pl.BlockSpec((1, d), fixed), pl.BlockSpec((1, d), fixed)],
        out_specs=[pl.BlockSpec((tm, d), row), pl.BlockSpec((tm, d), row)],
        out_shape=[jax.ShapeDtypeStruct((m, d), F32), jax.ShapeDtypeStruct((m, d), BF16)],
        compiler_params=_params("arbitrary"),
        name="combine_ln",
    )(h, y0, y1, gate, g.reshape(1, d), b.reshape(1, d))


ONES_ROWS = 16
POS_SPLIT = 16


def _attn_kernel(slopes_ref, lam_ref, qt_ref, k_ref, vt_ref, fq_ref, fk_ref, g_ref, o_ref, m_s, acc_s, sa_s, sb_s,
                 *, tq, tk, seq, out_scale):
    head = pl.program_id(1)
    i0 = pl.program_id(2) * tq
    slope = slopes_ref[head]
    lam = lam_ref[0]
    qt = qt_ref[...]
    row = lax.broadcasted_iota(jnp.int32, qt.shape, 0)
    zero = jnp.zeros_like(qt)
    q2 = jnp.concatenate([jnp.where(row < DA_HEAD_DIM, qt, zero),
                          jnp.where(row >= DA_HEAD_DIM, qt, zero)], axis=1)
    fq = fq_ref[0]
    fq2 = jnp.concatenate([fq, fq], axis=1)
    q_aug = jnp.concatenate([q2, fq2], axis=0)
    fk = fk_ref[0]
    fk_neg = -fk
    ones = jnp.ones((ONES_ROWS, tk), BF16)
    m_s[...] = jnp.full(m_s.shape, -jnp.inf, F32)
    acc_s[...] = jnp.zeros(acc_s.shape, F32)
    j_diag = i0 // tk

    def update(j, s, shift):
        off = pl.multiple_of(j * tk, tk)
        va = jnp.concatenate([vt_ref[:, pl.ds(off, tk)], ones], axis=0)
        m_prev = m_s[...]
        m_new = jnp.maximum(m_prev, jnp.max(s, axis=0, keepdims=True) - shift)
        p = jnp.exp(s - (m_new + shift))
        alpha = jnp.exp(m_prev - m_new)
        acc_s[...] = alpha * acc_s[...] + jnp.dot(va, p.astype(BF16), preferred_element_type=F32)
        m_s[...] = m_new

    def chunk_of(t):
        return t + (t >= j_diag).astype(jnp.int32)

    def scores(t):
        j = chunk_of(t)
        off = pl.multiple_of(j * tk, tk)
        ka = jnp.concatenate([k_ref[0, pl.ds(off, tk), :], jnp.where(j < j_diag, fk, fk_neg)], axis=1)
        return jnp.dot(ka, q_aug, preferred_element_type=F32)

    def consume(s_ref, t):
        j = chunk_of(t)
        update(j, s_ref[...], slope * jnp.abs(i0 - j * tk).astype(F32))

    j0 = pl.multiple_of(j_diag * tk, tk)
    dist = jnp.abs(lax.broadcasted_iota(jnp.int32, (tk, tq), 0) + (j0 - i0)
                   - lax.broadcasted_iota(jnp.int32, (tk, tq), 1)).astype(F32)
    bias = slope * dist
    s_diag = jnp.dot(k_ref[0, pl.ds(j0, tk), :], q2, preferred_element_type=F32)
    sa_s[...] = s_diag - jnp.concatenate([bias, bias], axis=1)
    sb_s[...] = scores(jnp.int32(0))
    update(j_diag, sa_s[...], 0.0)
    n_off = seq // tk - 1

    def pair(t, carry):
        sa_s[...] = scores(2 * t + 1)
        consume(sb_s, 2 * t)
        sb_s[...] = scores(2 * t + 2)
        consume(sa_s, 2 * t + 1)
        return carry

    lax.fori_loop(0, n_off // 2, pair, 0)
    consume(sb_s, jnp.int32(n_off - 1))

    acc = acc_s[...]
    o1 = acc[:DA_V_DIM, :tq] / acc[DA_V_DIM:DA_V_DIM + 1, :tq]
    o2 = acc[:DA_V_DIM, tq:] / acc[DA_V_DIM:DA_V_DIM + 1, tq:]
    o = o1 - lam * o2
    o = o * lax.rsqrt(jnp.mean(o * o, axis=0, keepdims=True) + LN_EPS) * g_ref[...]
    o_ref[0] = (o * out_scale).T.astype(o_ref.dtype)


def diff_attention(qt, k, vt, lam, subln_g, lambda_init, tq=512, tk=512):
    bsz, seq, width = k.shape
    nq = seq // tq
    slopes = 2.0 ** (-(8.0 / DA_HEADS) * jnp.arange(1, DA_HEADS + 1, dtype=F32))

    def split(n):
        pos = jnp.arange(n, dtype=jnp.int32)
        return (pos // POS_SPLIT * POS_SPLIT).astype(F32), (pos % POS_SPLIT).astype(F32)

    qhi, qlo = split(tq)
    khi, klo = split(tk)
    sl = slopes[:, None]
    fq = jnp.zeros((DA_HEADS, DA_V_DIM, tq), F32)
    fq = fq.at[:, 0].set(1.0).at[:, 1].set(1.0).at[:, 2].set(-sl * qhi).at[:, 3].set(-sl * qlo)
    fk = jnp.zeros((DA_HEADS, tk, DA_V_DIM), F32)
    fk = fk.at[:, :, 0].set(sl * khi).at[:, :, 1].set(sl * klo).at[:, :, 2].set(1.0).at[:, :, 3].set(1.0)
    kern = functools.partial(_attn_kernel, tq=tq, tk=tk, seq=seq, out_scale=1.0 - lambda_init)
    smem = pl.BlockSpec(memory_space=pltpu.SMEM)
    return pl.pallas_call(
        kern,
        grid=(bsz, DA_HEADS, seq // tq),
        in_specs=[smem, smem,
                  pl.BlockSpec((DA_V_DIM, tq), lambda b, h, i: (h, b * nq + i)),
                  pl.BlockSpec((1, seq, DA_V_DIM), lambda b, h, i: (b, 0, h)),
                  pl.BlockSpec((DA_V_DIM, seq), lambda b, h, i: (h, b)),
                  pl.BlockSpec((1, DA_V_DIM, tq), lambda b, h, i: (h, 0, 0)),
                  pl.BlockSpec((1, tk, DA_V_DIM), lambda b, h, i: (h, 0, 0)),
                  pl.BlockSpec((DA_V_DIM, 1), lambda b, h, i: (0, 0))],
        out_specs=pl.BlockSpec((1, tq, DA_V_DIM), lambda b, h, i: (b, i, h)),
        out_shape=jax.ShapeDtypeStruct((bsz, seq, width), BF16),
        scratch_shapes=[pltpu.VMEM((1, 2 * tq), F32), pltpu.VMEM((DA_V_DIM + ONES_ROWS, 2 * tq), F32),
                        pltpu.VMEM((tk, 2 * tq), F32), pltpu.VMEM((tk, 2 * tq), F32)],
        compiler_params=_params("arbitrary", "arbitrary", "arbitrary"),
        name="diff_attention",
    )(slopes, lam.reshape(1).astype(F32), qt, k, vt, fq.astype(BF16), fk.astype(BF16),
      subln_g.reshape(DA_V_DIM, 1).astype(F32))


HALO = 16


def _rglru_kernel(xr_ref, prev_ref, next_ref, cw_ref, cb_ref, wa_ref, wx_ref, ba_ref, bx_ref, kk_ref, y_ref,
                  a_s, b_s, y_s, h_s, *, ts, nt):
    direction = pl.program_id(0)
    t = pl.program_id(2)
    chunk = t + direction * (nt - 1 - 2 * t)

    @pl.when(t == 0)
    def _():
        h_s[...] = jnp.zeros(h_s.shape, F32)

    xf = jnp.concatenate([jnp.where(chunk == 0, 0.0, prev_ref[0].astype(F32)), xr_ref[0].astype(F32),
                          jnp.where(chunk == nt - 1, 0.0, next_ref[0].astype(F32))], axis=0)
    rows = ts + 2 * HALO

    def tap(j):
        shifted = xf if j == 2 else pltpu.roll(xf, (2 - j) % rows, 0)
        return cw_ref[j:j + 1, :] * shifted[HALO:HALO + ts]

    x = cb_ref[...] + sum(tap(j) for j in range(RG_CONV))
    xb = x.astype(BF16)
    xh = 0.5 * x
    for n in range(RG_BLOCKS):
        cols = slice(n * RG_BLOCK_W, (n + 1) * RG_BLOCK_W)
        xs = xb[:, cols]
        tr = jnp.tanh(jnp.dot(xs, wa_ref[0, n], preferred_element_type=F32) + ba_ref[0, :, cols])
        ti = jnp.tanh(jnp.dot(xs, wx_ref[0, n], preferred_element_type=F32) + bx_ref[0, :, cols])
        kk = kk_ref[0, :, cols]
        a = jnp.exp2(kk * tr + kk)
        a_s[:, cols] = a
        s = 1.0 - a * a
        root = jnp.where(s > 0.0, s * lax.rsqrt(s), 0.0)
        b_s[:, cols] = root * (ti * xh[:, cols] + xh[:, cols])

    def step(k, h):
        row = jnp.where(direction == 0, k, ts - 1 - k)
        h = a_s[pl.ds(row, 1), :] * h + b_s[pl.ds(row, 1), :]
        y_s[pl.ds(row, 1), :] = h
        return h

    h_s[...] = lax.fori_loop(0, ts, step, h_s[...], unroll=8)
    y_ref[0, 0] = y_s[...].astype(y_ref.dtype)


def rglru_bidir(u, conv_w, conv_b, wa, ba, wx, bx, lam, ts=256):
    bsz, seq, w2 = u.shape
    w = w2 // 2
    nt = seq // ts
    per = ts // HALO
    kk = (-0.5 * RG_C * math.log2(math.e)) * jax.nn.softplus(-lam.astype(F32)).reshape(2, 1, w)
    chunk = lambda d, t: t + d * (nt - 1 - 2 * t)
    dmap4 = lambda d, b, t: (d, 0, 0, 0)
    dmap3 = lambda d, b, t: (d, 0, 0)
    fixed = lambda d, b, t: (0, 0)
    return pl.pallas_call(
        functools.partial(_rglru_kernel, ts=ts, nt=nt),
        grid=(2, bsz, nt),
        in_specs=[pl.BlockSpec((1, ts, w), lambda d, b, t: (b, chunk(d, t), 1)),
                  pl.BlockSpec((1, HALO, w), lambda d, b, t: (b, jnp.maximum(chunk(d, t) * per - 1, 0), 1)),
                  pl.BlockSpec((1, HALO, w),
                               lambda d, b, t: (b, jnp.minimum((chunk(d, t) + 1) * per, seq // HALO - 1), 1)),
                  pl.BlockSpec((RG_CONV, w), fixed), pl.BlockSpec((1, w), fixed),
                  pl.BlockSpec((1, RG_BLOCKS, RG_BLOCK_W, RG_BLOCK_W), dmap4),
                  pl.BlockSpec((1, RG_BLOCKS, RG_BLOCK_W, RG_BLOCK_W), dmap4),
                  pl.BlockSpec((1, 1, w), dmap3), pl.BlockSpec((1, 1, w), dmap3), pl.BlockSpec((1, 1, w), dmap3)],
        out_specs=pl.BlockSpec((1, 1, ts, w), lambda d, b, t: (d, b, chunk(d, t), 0)),
        out_shape=jax.ShapeDtypeStruct((2, bsz, seq, w), BF16),
        scratch_shapes=[pltpu.VMEM((ts, w), F32), pltpu.VMEM((ts, w), F32), pltpu.VMEM((ts, w), F32),
                        pltpu.VMEM((1, w), F32)],
        compiler_params=_params("arbitrary", "arbitrary", "arbitrary"),
        name="rglru_bidir",
    )(u, u, u, conv_w.astype(F32), conv_b.astype(F32).reshape(1, w), (0.5 * wa).astype(BF16),
      (0.5 * wx).astype(BF16), 0.5 * ba.astype(F32).reshape(2, 1, w), 0.5 * bx.astype(F32).reshape(2, 1, w), kk)


def _router_kernel(h_ref, wt_ref, b_ref, tri_ref, e_ref, g_ref, r_ref, cnt_ref, seen_s):
    h = h_ref[...]
    h_hi = h.astype(BF16)
    h_lo = (h - h_hi.astype(F32)).astype(BF16)
    nt_dot = lambda a, b: lax.dot_general(a, b, (((1,), (1,)), ((), ())), preferred_element_type=F32)
    logits = nt_dot(wt_ref[0], h_hi) + (nt_dot(wt_ref[0], h_lo) + nt_dot(wt_ref[1], h_hi))
    tm = logits.shape[1]
    s = jax.nn.sigmoid(logits).reshape(N_GROUPS, EXPERTS_PER_GROUP, tm)
    sel = s + b_ref[...]
    idx = lax.broadcasted_iota(jnp.int32, sel.shape, 1)
    big = jnp.int32(EXPERTS_PER_GROUP)
    v1 = jnp.max(sel, axis=1, keepdims=True)
    i1 = jnp.min(jnp.where(sel == v1, idx, big), axis=1, keepdims=True)
    rest = jnp.where(idx == i1, -jnp.inf, sel)
    v2 = jnp.max(rest, axis=1, keepdims=True)
    i2 = jnp.min(jnp.where(rest == v2, idx, big), axis=1, keepdims=True)
    score = v1 + v2
    gidx = lax.broadcasted_iota(jnp.int32, score.shape, 0)
    best = jnp.max(score, axis=0, keepdims=True)
    grp = jnp.min(jnp.where(score == best, gidx, jnp.int32(N_GROUPS)), axis=0, keepdims=True)
    pick = gidx == grp
    l1 = jnp.sum(jnp.where(pick, i1, 0), axis=0)
    l2 = jnp.sum(jnp.where(pick, i2, 0), axis=0)
    s_g = jnp.sum(jnp.where(pick, s, 0.0), axis=0)
    eidx = lax.broadcasted_iota(jnp.int32, s_g.shape, 0)
    w1 = jnp.sum(jnp.where(eidx == l1, s_g, 0.0), axis=0, keepdims=True)
    w2 = jnp.sum(jnp.where(eidx == l2, s_g, 0.0), axis=0, keepdims=True)
    tot = w1 + w2
    base = grp[0] * EXPERTS_PER_GROUP
    e1, e2 = base + l1, base + l2
    e_ref[...] = jnp.concatenate([e1, e2], axis=0)
    g_ref[...] = jnp.concatenate([w1 / tot, w2 / tot], axis=0)

    @pl.when(pl.program_id(0) == 0)
    def _():
        seen_s[...] = jnp.zeros(seen_s.shape, F32)

    eall = lax.broadcasted_iota(jnp.int32, (N_EXPERTS, tm), 0)
    hit1, hit2 = eall == e1, eall == e2
    both = jnp.where(hit1, 1.0, jnp.where(hit2, 1.0, 0.0))
    incl = jnp.dot(both.astype(BF16), tri_ref[...], preferred_element_type=F32)
    before = incl - both + seen_s[...]
    r_ref[...] = jnp.concatenate([jnp.sum(jnp.where(hit1, before, 0.0), axis=0, keepdims=True),
                                  jnp.sum(jnp.where(hit2, before, 0.0), axis=0, keepdims=True)],
                                 axis=0).astype(jnp.int32)
    seen_s[...] = seen_s[...] + incl[:, tm - 1:tm]
    cnt_ref[...] = seen_s[...].astype(jnp.int32)


def route(h, router_w, router_b, tm=1024):
    n, d = h.shape
    tri = (jnp.arange(tm)[:, None] <= jnp.arange(tm)[None, :]).astype(BF16)
    wt = router_w.astype(F32).T
    wt_hi = wt.astype(BF16)
    wt_lo = (wt - wt_hi.astype(F32)).astype(BF16)
    tile = pl.BlockSpec((TOP_K, tm), lambda i: (0, i))
    return pl.pallas_call(
        _router_kernel,
        grid=(n // tm,),
        in_specs=[pl.BlockSpec((tm, d), lambda i: (i, 0)),
                  pl.BlockSpec((2, N_EXPERTS, d), lambda i: (0, 0, 0)),
                  pl.BlockSpec((N_GROUPS, EXPERTS_PER_GROUP, 1), lambda i: (0, 0, 0)),
                  pl.BlockSpec((tm, tm), lambda i: (0, 0))],
        out_specs=[tile, tile, tile, pl.BlockSpec((N_EXPERTS, 1), lambda i: (0, 0))],
        out_shape=[jax.ShapeDtypeStruct((TOP_K, n), jnp.int32), jax.ShapeDtypeStruct((TOP_K, n), F32),
                   jax.ShapeDtypeStruct((TOP_K, n), jnp.int32), jax.ShapeDtypeStruct((N_EXPERTS, 1), jnp.int32)],
        scratch_shapes=[pltpu.VMEM((N_EXPERTS, 1), F32)],
        compiler_params=_params("arbitrary"),
        name="router",
    )(h, jnp.stack([wt_hi, wt_lo]), router_b.astype(F32).reshape(N_GROUPS, EXPERTS_PER_GROUP, 1), tri)


def _experts_kernel(blk_exp_ref, n_used_ref, x_ref, wg_ref, wu_ref, wd_ref, o_ref, wg_s, wu_s, wd_s):
    i = pl.program_id(0)
    expert = blk_exp_ref[i]
    prev = blk_exp_ref[jnp.maximum(i - 1, 0)]

    @pl.when((i == 0) | (expert != prev))
    def _():
        wg_s[...] = wg_ref[0, 0].astype(BF16)
        wu_s[...] = wu_ref[0, 0].astype(BF16)
        wd_s[...] = wd_ref[0, 0].astype(BF16)

    @pl.when(i < n_used_ref[0])
    def _():
        x = _unpack_halves(x_ref[...])
        hg = jnp.dot(x, wg_s[...], preferred_element_type=F32)
        hu = jnp.dot(x, wu_s[...], preferred_element_type=F32)
        hidden = (jax.nn.silu(hg) * hu).astype(BF16)
        o_ref[...] = jnp.dot(hidden, wd_s[...], preferred_element_type=F32).astype(o_ref.dtype)

    @pl.when(i >= n_used_ref[0])
    def _():
        o_ref[...] = jnp.zeros(o_ref.shape, o_ref.dtype)


def experts(xs, blk_exp, n_used, w_gate, w_up, w_down, layer):
    p = xs.shape[0]
    d = w_gate.shape[2]
    f = w_gate.shape[3]
    grid_spec = pltpu.PrefetchScalarGridSpec(
        num_scalar_prefetch=2,
        grid=(p // MOE_ROWS,),
        in_specs=[pl.BlockSpec((MOE_ROWS, d // 2), lambda i, be, nu: (i, 0)),
                  pl.BlockSpec((1, 1, d, f), lambda i, be, nu: (layer, be[i], 0, 0)),
                  pl.BlockSpec((1, 1, d, f), lambda i, be, nu: (layer, be[i], 0, 0)),
                  pl.BlockSpec((1, 1, f, d), lambda i, be, nu: (layer, be[i], 0, 0))],
        out_specs=pl.BlockSpec((MOE_ROWS, d), lambda i, be, nu: (i, 0)),
        scratch_shapes=[pltpu.VMEM((d, f), BF16), pltpu.VMEM((d, f), BF16), pltpu.VMEM((f, d), BF16)],
    )
    return pl.pallas_call(
        _experts_kernel,
        grid_spec=grid_spec,
        out_shape=jax.ShapeDtypeStruct((p, d), BF16),
        compiler_params=_params("arbitrary"),
        name="experts",
    )(blk_exp, n_used, xs, w_gate, w_up, w_down)


def _dispatch_kernel(dest_ref, fill_ref, x_ref, xs_ref, zero_s, fill_sem, row_sem, *, tm, n_blocks):
    base = pl.program_id(0) * tm

    @pl.when(pl.program_id(0) == 0)
    def _():
        zero_s[...] = jnp.zeros(zero_s.shape, zero_s.dtype)

        def zero_block(blk):
            return pltpu.make_async_copy(zero_s, xs_ref.at[pl.ds(pl.multiple_of(blk * MOE_ROWS, MOE_ROWS),
                                                                 MOE_ROWS)], fill_sem)

        def start(blk, carry):
            @pl.when(fill_ref[blk] != 0)
            def _():
                zero_block(blk).start()
            return carry

        def wait(blk, carry):
            @pl.when(fill_ref[blk] != 0)
            def _():
                zero_block(blk).wait()
            return carry

        lax.fori_loop(0, n_blocks, start, 0)
        lax.fori_loop(0, n_blocks, wait, 0)

    def issue(r, carry):
        for k in range(TOP_K):
            pltpu.make_async_copy(x_ref.at[pl.ds(r, 1)], xs_ref.at[pl.ds(dest_ref[k, base + r], 1)],
                                  row_sem).start()
        return carry

    lax.fori_loop(0, tm, issue, 0, unroll=4)
    for k in range(TOP_K):
        pltpu.make_async_copy(x_ref, xs_ref.at[pl.ds(0, tm)], row_sem).wait()


def dispatch_rows(x, dest, fill, tm=512):
    n, d = x.shape
    n_blocks = fill.shape[0]
    grid_spec = pltpu.PrefetchScalarGridSpec(
        num_scalar_prefetch=2,
        grid=(n // tm,),
        in_specs=[pl.BlockSpec((tm, d), lambda i, dst, fl: (i, 0))],
        out_specs=pl.BlockSpec(memory_space=pl.ANY),
        scratch_shapes=[pltpu.VMEM((MOE_ROWS, d), x.dtype), pltpu.SemaphoreType.DMA(()),
                        pltpu.SemaphoreType.DMA(())],
    )
    return pl.pallas_call(
        functools.partial(_dispatch_kernel, tm=tm, n_blocks=n_blocks),
        grid_spec=grid_spec,
        out_shape=jax.ShapeDtypeStruct((n_blocks * MOE_ROWS, d), x.dtype),
        compiler_params=_params("arbitrary"),
        name="dispatch_rows",
    )(dest, fill, x)


def moe_ffn(h, hp, router_w, router_b, w_gate, w_up, w_down, layer, ln_g, ln_b):
    n, d = h.shape
    a = n * TOP_K
    e_idx, gate, rank, counts = route(h, router_w, router_b)
    counts = counts[:, 0]
    padded = (counts + MOE_ROWS - 1) // MOE_ROWS * MOE_ROWS
    pad_end = jnp.cumsum(padded)
    pad_start = pad_end - padded
    experts_iota = jnp.arange(N_EXPERTS, dtype=jnp.int32)[:, None, None]
    dest = jnp.sum(jnp.where(e_idx[None] == experts_iota, pad_start[:, None, None], 0), axis=0) + rank
    n_blocks = a // MOE_ROWS + N_EXPERTS
    blk_start = jnp.arange(n_blocks, dtype=jnp.int32)[:, None] * MOE_ROWS
    blk_exp = jnp.minimum(jnp.sum((pad_end[None, :] <= blk_start).astype(jnp.int32), axis=1), N_EXPERTS - 1)
    n_used = (pad_end[-1:] // MOE_ROWS).astype(jnp.int32)
    ends_expert = jnp.any((pad_end[None, :] == blk_start + MOE_ROWS) & (padded[None, :] > 0), axis=1)
    fill = (ends_expert | (blk_start[:, 0] >= pad_end[-1])).astype(jnp.int32)
    xs = dispatch_rows(hp, dest, fill)
    yb = experts(xs, blk_exp, n_used, w_gate, w_up, w_down, layer)
    y0 = yb.at[dest[0]].get(mode="promise_in_bounds")
    y1 = yb.at[dest[1]].get(mode="promise_in_bounds")
    return combine_ln(h, y0, y1, gate.T, ln_g, ln_b)


FFT_R = 128
HALF_R = FFT_R // 2
HY_GROUP = 4


def _proj_t_kernel(w_ref, x_ref, o_ref, *, scale):
    acc = lax.dot_general(w_ref[...].astype(BF16), x_ref[...], (((0,), (1,)), ((), ())),
                          preferred_element_type=F32)
    o_ref[...] = (acc * scale).astype(o_ref.dtype)


def proj_t(w, col0, ncols, x, out_dtype, scale=1.0, tm=512, tn=1024):
    k = w.shape[0]
    n = x.shape[0]
    first = col0 // tm
    return pl.pallas_call(
        functools.partial(_proj_t_kernel, scale=scale),
        grid=(ncols // tm, n // tn),
        in_specs=[pl.BlockSpec((k, tm), lambda i, j: (0, first + i)),
                  pl.BlockSpec((tn, k), lambda i, j: (j, 0))],
        out_specs=pl.BlockSpec((tm, tn), lambda i, j: (i, j)),
        out_shape=jax.ShapeDtypeStruct((ncols, n), out_dtype),
        compiler_params=_params("arbitrary", "arbitrary"),
        name="proj_t",
    )(w, x)


def _dft_constants():
    idx = jnp.arange(FFT_R, dtype=jnp.int32)
    prod = idx[:, None] * idx[None, :]
    ang = (prod % FFT_R).astype(F32) * (2.0 * math.pi / FFT_R)
    f_r, f_i = jnp.cos(ang), -jnp.sin(ang)
    ang_t = prod.astype(F32) * (2.0 * math.pi / (FFT_R * FFT_R))
    g_r, g_i = f_r[:HALF_R], -f_i[:HALF_R]
    return dict(
        la=jnp.block([[f_r[:, :HALF_R], -f_i[:, :HALF_R]], [f_i[:, :HALF_R], f_r[:, :HALF_R]]]).astype(BF16),
        la_real=jnp.concatenate([f_r, f_i], axis=0).astype(BF16),
        rb=jnp.block([[f_r, f_i], [-f_i, f_r]]).astype(BF16),
        rc=jnp.block([[f_r, -f_i], [f_i, f_r]]).astype(BF16),
        ld=(jnp.block([[g_r, -g_i], [g_i, g_r]]) / (FFT_R * FFT_R)).astype(BF16),
        t_r=jnp.cos(ang_t), t_i=-jnp.sin(ang_t))


def _rows_to_lanes(x):
    return jnp.concatenate([x[:FFT_R], x[FFT_R:]], axis=1)


def _fwd_lane_stage(a, t_r, t_i, rb):
    a_r, a_i = a[:FFT_R], a[FFT_R:]
    t2_r = jnp.concatenate([t_r, t_r], axis=1)
    t2_i = jnp.concatenate([t_i, t_i], axis=1)
    b_r = a_r * t2_r - a_i * t2_i
    b_i = a_r * t2_i + a_i * t2_r
    lhs = jnp.concatenate([jnp.concatenate([b_r[:, :FFT_R], b_i[:, :FFT_R]], axis=1),
                           jnp.concatenate([b_r[:, FFT_R:], b_i[:, FFT_R:]], axis=1)], axis=0)
    return jnp.dot(lhs.astype(BF16), rb, preferred_element_type=F32)


def _conv_pairs(ms, kfs, la, rb, rc, ld, t_r, t_i):
    a = [jnp.dot(la, jnp.concatenate([m_r, m_i], axis=0).astype(BF16), preferred_element_type=F32)
         for m_r, m_i in ms]
    z = [_fwd_lane_stage(ai, t_r, t_i, rb) for ai in a]
    c = []
    for zi, kf in zip(z, kfs):
        z_r, z_i, k_r, k_i = zi[:, :FFT_R], zi[:, FFT_R:], kf[:, :FFT_R], kf[:, FFT_R:]
        y = jnp.concatenate([z_r * k_r - z_i * k_i, z_r * k_i + z_i * k_r], axis=1)
        c.append(jnp.dot(y.astype(BF16), rc, preferred_element_type=F32))
    t2_r = jnp.concatenate([t_r, t_r], axis=0)
    t2_i = jnp.concatenate([t_i, t_i], axis=0)
    out = []
    for ci in c:
        c_r, c_i = ci[:, :FFT_R], ci[:, FFT_R:]
        d_r = c_r * t2_r + c_i * t2_i
        d_i = c_i * t2_r - c_r * t2_i
        rhs = jnp.concatenate([_rows_to_lanes(d_r), _rows_to_lanes(d_i)], axis=0)
        x = jnp.dot(ld, rhs.astype(BF16), preferred_element_type=F32)
        out.append((x[:HALF_R], x[HALF_R:]))
    return out


def _short_conv_tile(x, w0, w1, w2, b):
    rows = x.shape[0]
    lane = lax.broadcasted_iota(jnp.int32, x.shape, 1)
    row = lax.broadcasted_iota(jnp.int32, x.shape, 0)
    r = pltpu.roll(x, 1, 1)
    prev = jnp.where(lane == 0, jnp.where(row == 0, 0.0, pltpu.roll(r, 1, 0)), r)
    r = pltpu.roll(x, FFT_R - 1, 1)
    nxt = jnp.where(lane == FFT_R - 1, jnp.where(row == rows - 1, 0.0, pltpu.roll(r, rows - 1, 0)), r)
    return w0 * prev + w1 * x + w2 * nxt + b


def _hyena_kernel(cw_ref, cb_ref, skip_ref, v_ref, x1_ref, x2_ref, kf_ref,
                  la_ref, rb_ref, rc_ref, ld_ref, tr_ref, ti_ref, o_ref, *, tc):
    c_base = pl.program_id(0) * tc
    lane = lax.broadcasted_iota(jnp.int32, (1, 2 * FFT_R), 1)

    def conv_in(ref, part, b, c0):
        tiles = []
        for cc in range(2):
            ch = part * HY_CH + c_base + c0 + cc
            tiles.append(_short_conv_tile(ref[c0 + cc, b], cw_ref[0, ch], cw_ref[1, ch], cw_ref[2, ch], cb_ref[ch]))
        return jnp.concatenate(tiles, axis=1)

    def group(g, carry):
        starts = [2 * (HY_GROUP * g + j) for j in range(HY_GROUP)]
        consts = (la_ref[...], rb_ref[...], rc_ref[...], ld_ref[...], tr_ref[...], ti_ref[...])
        zs = [(conv_in(v_ref, 0, 0, c0), conv_in(v_ref, 0, 1, c0)) for c0 in starts]
        for o, g_ref in enumerate((x1_ref, x2_ref)):
            kfs = [kf_ref[o, pl.ds(c0, 2)].reshape(2 * FFT_R, 2 * FFT_R) for c0 in starts]
            ys = _conv_pairs(zs, kfs, *consts)
            nxt = []
            for c0, (z_r, z_i), (y_r, y_i) in zip(starts, zs, ys):
                sk = jnp.where(lane < FFT_R, skip_ref[o, c_base + c0], skip_ref[o, c_base + c0 + 1])
                nxt.append((conv_in(g_ref, o + 1, 0, c0) * (y_r + sk * z_r),
                            conv_in(g_ref, o + 1, 1, c0) * (y_i + sk * z_i)))
            zs = nxt
        for c0, (z_r, z_i) in zip(starts, zs):
            o_ref[c0, 0] = z_r[:, :FFT_R].astype(o_ref.dtype)
            o_ref[c0 + 1, 0] = z_r[:, FFT_R:].astype(o_ref.dtype)
            o_ref[c0, 1] = z_i[:, :FFT_R].astype(o_ref.dtype)
            o_ref[c0 + 1, 1] = z_i[:, FFT_R:].astype(o_ref.dtype)
        return carry

    lax.fori_loop(0, tc // (2 * HY_GROUP), group, 0)


def hyena_mix(u_t, kf, conv_w, conv_b, skip, consts, tc=16):
    rows, n = u_t.shape
    ch = rows // 3
    u5 = u_t.reshape(rows, 2, HALF_R, FFT_R)
    nct = ch // tc
    smem = pl.BlockSpec(memory_space=pltpu.SMEM)
    part = lambda k: pl.BlockSpec((tc, 2, HALF_R, FFT_R), lambda i: (k * nct + i, 0, 0, 0))
    full2 = lambda a: pl.BlockSpec(a.shape, lambda i: (0, 0))
    mats = [consts[k] for k in ("la", "rb", "rc", "ld", "t_r", "t_i")]
    out = pl.pallas_call(
        functools.partial(_hyena_kernel, tc=tc),
        grid=(nct,),
        in_specs=[smem, smem, smem, part(0), part(1), part(2),
                  pl.BlockSpec((2, tc, FFT_R, 2 * FFT_R), lambda i: (0, i, 0, 0))] + [full2(a) for a in mats],
        out_specs=pl.BlockSpec((tc, 2, HALF_R, FFT_R), lambda i: (i, 0, 0, 0)),
        out_shape=jax.ShapeDtypeStruct((ch, 2, HALF_R, FFT_R), BF16),
        compiler_params=_params("arbitrary"),
        name="hyena_mix",
    )(conv_w.astype(F32), conv_b.astype(F32), skip.astype(F32), u5, u5, u5, kf, *mats)
    return out.reshape(ch, n)


def _hyena_filter_kernel(delta_ref, hf_ref, hb_ref, tpos_ref, la_ref, rb_ref, tr_ref, ti_ref, kf_ref, *, tc):
    c_base = pl.program_id(1) * tc

    def taps_of(c0):
        taps = []
        for cc in range(2):
            k = jnp.concatenate([hf_ref[0, 0, c0 + cc], hb_ref[0, 0, c0 + cc]], axis=0)
            k = k * jnp.exp(-tpos_ref[...] * delta_ref[c_base + c0 + cc])
            taps.append(k * lax.rsqrt(jnp.sum(k * k, keepdims=True)))
        return jnp.concatenate(taps, axis=1).astype(BF16)

    def group(g, carry):
        starts = [2 * (HY_GROUP * g + j) for j in range(HY_GROUP)]
        a = [jnp.dot(la_ref[...], taps_of(c0), preferred_element_type=F32) for c0 in starts]
        z = [_fwd_lane_stage(ai, tr_ref[...], ti_ref[...], rb_ref[...]) for ai in a]
        for c0, zi in zip(starts, z):
            kf_ref[0, pl.ds(c0, 2)] = zi.reshape(2, FFT_R, 2 * FFT_R)
        return carry

    lax.fori_loop(0, tc // (2 * HY_GROUP), group, 0)


def hyena_filter_spectra(seq, w1, b1, w2, b2, w3, b3, freq, w4, consts, tc=16):
    t = jnp.linspace(0.0, 1.0, seq, dtype=F32)[:, None]
    omega = 2.0 * math.pi * jnp.arange(seq, dtype=F32)[:, None] / seq
    bands = jnp.linspace(1e-4, HF_BANDS - 1, HF_BANDS, dtype=F32)[None, :]
    ang = omega * bands
    z = jnp.concatenate([t, jnp.cos(ang), -jnp.sin(ang)], -1)
    fr = freq.astype(F32)
    hid = jnp.sin(fr * (z @ w1.astype(F32) + b1.astype(F32)))
    hid = jnp.sin(fr * (hid @ w2.astype(F32) + b2.astype(F32)))
    hid = jnp.sin(fr * (hid @ w3.astype(F32) + b3.astype(F32)))
    back = lambda a: jnp.concatenate([jnp.zeros_like(a[:1]), a[:0:-1]], axis=0)
    w4t = w4.astype(BF16).T
    half = HY_ORDER * HY_CH
    h_fwd = matmul(w4t[:half], hid.T.astype(BF16), F32)
    h_bwd = matmul(w4t[half:], back(hid).T.astype(BF16), F32)
    shape5 = (HY_ORDER, HY_CH, HALF_R, FFT_R)
    tpos = jnp.concatenate([t, back(t)], axis=0).reshape(FFT_R, FFT_R)
    max_decay = math.log(HF_TARGET) / HF_FAST
    min_decay = math.log(HF_TARGET) / HF_SLOW
    deltas = jnp.abs(jnp.linspace(min_decay, max_decay, HY_CH, dtype=F32))
    taps = pl.BlockSpec((1, 1, tc, HALF_R, FFT_R), lambda o, i: (0, o, i, 0, 0))
    full2 = lambda a: pl.BlockSpec(a.shape, lambda o, i: (0, 0))
    mats = [consts[k] for k in ("la_real", "rb", "t_r", "t_i")]
    return pl.pallas_call(
        functools.partial(_hyena_filter_kernel, tc=tc),
        grid=(HY_ORDER, HY_CH // tc),
        in_specs=[pl.BlockSpec(memory_space=pltpu.SMEM), taps, taps, full2(tpos)] + [full2(a) for a in mats],
        out_specs=pl.BlockSpec((1, tc, FFT_R, 2 * FFT_R), lambda o, i: (o, i, 0, 0)),
        out_shape=jax.ShapeDtypeStruct((HY_ORDER, HY_CH, FFT_R, 2 * FFT_R), F32),
        compiler_params=_params("arbitrary", "arbitrary"),
        name="hyena_filter_spectra",
    )(deltas, h_fwd.reshape((1,) + shape5), h_bwd.reshape((1,) + shape5), tpos, *mats)


def kernel(x, ev_w_in, ev_hy_conv_w, ev_hy_conv_b, ev_hf_w1, ev_hf_b1, ev_hf_w2, ev_hf_b2, ev_hf_w3, ev_hf_b3,
           ev_hf_freq, ev_hf_w4, ev_hy_skip, ev_lam_q1, ev_lam_k1, ev_lam_q2, ev_lam_k2, ev_subln_g, ev_w_out,
           od_w_in, od_conv_w, od_conv_b, od_wa, od_ba, od_wx, od_bx, od_lam, od_w_out, ln1_g, ln1_b, ln2_g,
           ln2_b, router_w, router_b, ex_w_gate, ex_w_up, ex_w_down):
    bsz, seq, d = x.shape
    n = bsz * seq
    h = x.reshape(n, d).astype(F32)
    hb = h.astype(BF16)
    for layer in range(DEPTH):
        i = layer // 2
        if layer % 2 == 0:
            w_in = ev_w_in[i]
            c0 = 3 * HY_CH
            u_t = proj_t(w_in, 0, c0, hb, F32)
            qt = proj_t(w_in, c0, DA_QK, hb, BF16, scale=DA_HEAD_DIM ** -0.5)
            k = matmul(hb, w_in, BF16, col0=c0 + DA_QK, ncols=DA_QK)
            vt = proj_t(w_in, c0 + 2 * DA_QK, DA_WIDTH, hb, BF16)
            consts = _dft_constants()
            kf = hyena_filter_spectra(seq, ev_hf_w1[i], ev_hf_b1[i], ev_hf_w2[i], ev_hf_b2[i],
                                      ev_hf_w3[i], ev_hf_b3[i], ev_hf_freq[i], ev_hf_w4[i], consts)
            y_hy = hyena_mix(u_t, kf, ev_hy_conv_w[i], ev_hy_conv_b[i], ev_hy_skip[i], consts)
            lambda_init = 0.8 - 0.6 * math.exp(-0.3 * layer)
            lam = (jnp.exp(jnp.sum(ev_lam_q1[i].astype(F32) * ev_lam_k1[i].astype(F32)))
                   - jnp.exp(jnp.sum(ev_lam_q2[i].astype(F32) * ev_lam_k2[i].astype(F32))) + lambda_init)
            y_da = diff_attention(qt, k.reshape(bsz, seq, DA_QK), vt,
                                  lam, ev_subln_g[i], lambda_init).reshape(n, DA_WIDTH)
            h, hp = even_out_proj(y_hy, y_da, ev_w_out[i].astype(BF16), h, ln1_g[layer], ln1_b[layer])
        else:
            u = matmul(hb, od_w_in[i], BF16)
            yd = rglru_bidir(u.reshape(bsz, seq, 2 * RG_WIDTH), od_conv_w[i], od_conv_b[i], od_wa[i], od_ba[i],
                             od_wx[i], od_bx[i], od_lam[i]).reshape(2, n, RG_WIDTH)
            h, hp = odd_out_proj(u, yd, od_w_out[i].astype(BF16), h, ln1_g[layer], ln1_b[layer])
        h, hb = moe_ffn(h, hp, router_w, router_b, ex_w_gate, ex_w_up, ex_w_down, layer,
                        ln2_g[layer], ln2_b[layer])
    return h.reshape(bsz, seq, d).astype(x.dtype)
```

```python
import functools
import math

import jax
import jax.numpy as jnp
from jax import lax
from jax.experimental import pallas as pl
from jax.experimental.pallas import tpu as pltpu

F32 = jnp.float32
BF16 = jnp.bfloat16

D_MODEL = 2048
DEPTH = 2
HY_CH = D_MODEL // 2
HY_ORDER = 2
HF_BANDS = 16
HF_TARGET = 1e-2
HF_FAST = 0.3
HF_SLOW = 1.5
DA_HEADS = 8
DA_HEAD_DIM = 64
DA_V_DIM = 2 * DA_HEAD_DIM
DA_QK = DA_HEADS * 2 * DA_HEAD_DIM
DA_WIDTH = DA_HEADS * DA_V_DIM
RG_WIDTH = D_MODEL
RG_BLOCKS = 8
RG_BLOCK_W = RG_WIDTH // RG_BLOCKS
RG_C = 8.0
RG_CONV = 4
N_EXPERTS = 32
N_GROUPS = 4
EXPERTS_PER_GROUP = N_EXPERTS // N_GROUPS
TOP_K = 2
D_FF = 512
DN_ALPHA = (2 * DEPTH) ** 0.25
LN_EPS = 1e-5

VMEM_LIMIT_BYTES = 56 * 1024 * 1024
MOE_ROWS = 256


def _params(*sem):
    return pltpu.CompilerParams(dimension_semantics=sem, vmem_limit_bytes=VMEM_LIMIT_BYTES)


def _mm_kernel(x_ref, w_ref, o_ref):
    o_ref[...] = jnp.dot(x_ref[...], w_ref[...].astype(BF16), preferred_element_type=F32).astype(o_ref.dtype)


def matmul(x, w, out_dtype, col0=0, ncols=None, tm=512, tn=1024):
    m, k = x.shape
    n = w.shape[1] if ncols is None else ncols
    tm, tn = min(tm, m), min(tn, n)
    first = col0 // tn
    return pl.pallas_call(
        _mm_kernel,
        grid=(n // tn, m // tm),
        in_specs=[pl.BlockSpec((tm, k), lambda j, i: (i, 0)),
                  pl.BlockSpec((k, tn), lambda j, i: (0, first + j))],
        out_specs=pl.BlockSpec((tm, tn), lambda j, i: (i, j)),
        out_shape=jax.ShapeDtypeStruct((m, n), out_dtype),
        compiler_params=_params("arbitrary", "arbitrary"),
        name="matmul",
    )(x, w)


def _ln_rows(z, g, b):
    mu = jnp.mean(z, axis=-1, keepdims=True)
    zc = z - mu
    var = jnp.mean(zc * zc, axis=-1, keepdims=True)
    return zc * lax.rsqrt(var + LN_EPS) * g + b


def _pack_halves(y):
    half = y.shape[1] // 2
    hi = pltpu.bitcast(y[:, :half].astype(BF16).astype(F32), jnp.uint32)
    lo = pltpu.bitcast(y[:, half:].astype(BF16).astype(F32), jnp.uint32)
    return hi | (lo >> 16)


def _unpack_halves(p):
    hi = pltpu.bitcast(p & jnp.uint32(0xFFFF0000), F32)
    lo = pltpu.bitcast(p << 16, F32)
    return jnp.concatenate([hi, lo], axis=1).astype(BF16)


def _even_out_kernel(xa_ref, xb_ref, wa_ref, wb_ref, r_ref, g_ref, b_ref, o_ref, op_ref):
    acc = lax.dot_general(xa_ref[...], wa_ref[...], (((0,), (0,)), ((), ())), preferred_element_type=F32)
    acc += jnp.dot(xb_ref[...], wb_ref[...], preferred_element_type=F32)
    y = _ln_rows(DN_ALPHA * r_ref[...] + acc, g_ref[...], b_ref[...])
    o_ref[...] = y
    op_ref[...] = _pack_halves(y)


def even_out_proj(y_hy_t, y_da, w_out, resid, g, b, tm=512):
    ka, m = y_hy_t.shape
    kb = y_da.shape[1]
    d = w_out.shape[1]
    row = lambda i: (i, 0)
    fixed = lambda i: (0, 0)
    return pl.pallas_call(
        _even_out_kernel,
        grid=(m // tm,),
        in_specs=[pl.BlockSpec((ka, tm), lambda i: (0, i)), pl.BlockSpec((tm, kb), row),
                  pl.BlockSpec((ka, d), fixed), pl.BlockSpec((kb, d), fixed),
                  pl.BlockSpec((tm, d), row), pl.BlockSpec((1, d), fixed), pl.BlockSpec((1, d), fixed)],
        out_specs=[pl.BlockSpec((tm, d), row), pl.BlockSpec((tm, d // 2), row)],
        out_shape=[jax.ShapeDtypeStruct((m, d), F32), jax.ShapeDtypeStruct((m, d // 2), jnp.uint32)],
        compiler_params=_params("arbitrary"),
        name="even_out_proj",
    )(y_hy_t, y_da, w_out[:ka], w_out[ka:], resid, g.reshape(1, d), b.reshape(1, d))


def _odd_out_kernel(gate_ref, y0_ref, y1_ref, w_ref, r_ref, g_ref, b_ref, o_ref, op_ref):
    x = jax.nn.gelu(gate_ref[...].astype(F32)) * (y0_ref[0].astype(F32) + y1_ref[0].astype(F32))
    acc = jnp.dot(x.astype(BF16), w_ref[...], preferred_element_type=F32)
    y = _ln_rows(DN_ALPHA * r_ref[...] + acc, g_ref[...], b_ref[...])
    o_ref[...] = y
    op_ref[...] = _pack_halves(y)


def odd_out_proj(u, yd, w_out, resid, g, b, tm=256):
    m = u.shape[0]
    k, d = w_out.shape
    row = lambda i: (i, 0)
    fixed = lambda i: (0, 0)
    return pl.pallas_call(
        _odd_out_kernel,
        grid=(m // tm,),
        in_specs=[pl.BlockSpec((tm, k), row),
                  pl.BlockSpec((1, tm, k), lambda i: (0, i, 0)),
                  pl.BlockSpec((1, tm, k), lambda i: (1, i, 0)),
                  pl.BlockSpec((k, d), fixed),
                  pl.BlockSpec((tm, d), row), pl.BlockSpec((1, d), fixed), pl.BlockSpec((1, d), fixed)],
        out_specs=[pl.BlockSpec((tm, d), row), pl.BlockSpec((tm, d // 2), row)],
        out_shape=[jax.ShapeDtypeStruct((m, d), F32), jax.ShapeDtypeStruct((m, d // 2), jnp.uint32)],
        compiler_params=_params("arbitrary"),
        name="odd_out_proj",
    )(u, yd, yd, w_out, resid, g.reshape(1, d), b.reshape(1, d))


def _combine_ln_kernel(h_ref, y0_ref, y1_ref, gate_ref, g_ref, b_ref, o_ref, ob_ref):
    gate = gate_ref[...]
    ffn = gate[:, 0:1] * y0_ref[...].astype(F32) + gate[:, 1:2] * y1_ref[...].astype(F32)
    y = _ln_rows(DN_ALPHA * h_ref[...] + ffn, g_ref[...], b_ref[...])
    o_ref[...] = y
    ob_ref[...] = y.astype(BF16)


def combine_ln(h, y0, y1, gate, g, b, tm=512):
    m, d = h.shape
    row = lambda i: (i, 0)
    fixed = lambda i: (0, 0)
    return pl.pallas_call(
        _combine_ln_kernel,
        grid=(m // tm,),
        in_specs=[pl.BlockSpec((tm, d), row), pl.BlockSpec((tm, d), row), pl.BlockSpec((tm, d), row),
                  pl.BlockSpec((tm, TOP_K), row), pl.BlockSpec((1, d), fixed), pl.BlockSpec((1, d), fixed)],
        out_specs=[pl.BlockSpec((tm, d), row), pl.BlockSpec((tm, d), row)],
        out_shape=[jax.ShapeDtypeStruct((m, d), F32), jax.ShapeDtypeStruct((m, d), BF16)],
        compiler_params=_params("arbitrary"),
        name="combine_ln",
    )(h, y0, y1, gate, g.reshape(1, d), b.reshape(1, d))


ONES_ROWS = 16
POS_SPLIT = 16


def _attn_kernel(slopes_ref, lam_ref, qt_ref, k_ref, vt_ref, fq_ref, fk_ref, g_ref, o_ref, m_s, acc_s, sa_s, sb_s,
                 *, tq, tk, seq, out_scale):
    head = pl.program_id(1)
    i0 = pl.program_id(2) * tq
    slope = slopes_ref[head]
    lam = lam_ref[0]
    qt = qt_ref[...]
    row = lax.broadcasted_iota(jnp.int32, qt.shape, 0)
    zero = jnp.zeros_like(qt)
    q2 = jnp.concatenate([jnp.where(row < DA_HEAD_DIM, qt, zero),
                          jnp.where(row >= DA_HEAD_DIM, qt, zero)], axis=1)
    fq = fq_ref[0]
    fq2 = jnp.concatenate([fq, fq], axis=1)
    q_aug = jnp.concatenate([q2, fq2], axis=0)
    fk = fk_ref[0]
    fk_neg = -fk
    ones = jnp.ones((ONES_ROWS, tk), BF16)
    m_s[...] = jnp.full(m_s.shape, -jnp.inf, F32)
    acc_s[...] = jnp.zeros(acc_s.shape, F32)
    j_diag = i0 // tk

    def update(j, s, shift):
        off = pl.multiple_of(j * tk, tk)
        va = jnp.concatenate([vt_ref[:, pl.ds(off, tk)], ones], axis=0)
        m_prev = m_s[...]
        m_new = jnp.maximum(m_prev, jnp.max(s, axis=0, keepdims=True) - shift)
        p = jnp.exp(s - (m_new + shift))
        alpha = jnp.exp(m_prev - m_new)
        acc_s[...] = alpha * acc_s[...] + jnp.dot(va, p.astype(BF16), preferred_element_type=F32)
        m_s[...] = m_new

    def chunk_of(t):
        return t + (t >= j_diag).astype(jnp.int32)

    def scores(t):
        j = chunk_of(t)
        off = pl.multiple_of(j * tk, tk)
        ka = jnp.concatenate([k_ref[0, pl.ds(off, tk), :], jnp.where(j < j_diag, fk, fk_neg)], axis=1)
        return jnp.dot(ka, q_aug, preferred_element_type=F32)

    def consume(s_ref, t):
        j = chunk_of(t)
        update(j, s_ref[...], slope * jnp.abs(i0 - j * tk).astype(F32))

    j0 = pl.multiple_of(j_diag * tk, tk)
    dist = jnp.abs(lax.broadcasted_iota(jnp.int32, (tk, tq), 0) + (j0 - i0)
                   - lax.broadcasted_iota(jnp.int32, (tk, tq), 1)).astype(F32)
    bias = slope * dist
    s_diag = jnp.dot(k_ref[0, pl.ds(j0, tk), :], q2, preferred_element_type=F32)
    sa_s[...] = s_diag - jnp.concatenate([bias, bias], axis=1)
    sb_s[...] = scores(jnp.int32(0))
    update(j_diag, sa_s[...], 0.0)
    n_off = seq // tk - 1

    def pair(t, carry):
        sa_s[...] = scores(2 * t + 1)
        consume(sb_s, 2 * t)
        sb_s[...] = scores(2 * t + 2)
        consume(sa_s, 2 * t + 1)
        return carry

    lax.fori_loop(0, n_off // 2, pair, 0)
    consume(sb_s, jnp.int32(n_off - 1))

    acc = acc_s[...]
    o1 = acc[:DA_V_DIM, :tq] / acc[DA_V_DIM:DA_V_DIM + 1, :tq]
    o2 = acc[:DA_V_DIM, tq:] / acc[DA_V_DIM:DA_V_DIM + 1, tq:]
    o = o1 - lam * o2
    o = o * lax.rsqrt(jnp.mean(o * o, axis=0, keepdims=True) + LN_EPS) * g_ref[...]
    o_ref[0] = (o * out_scale).T.astype(o_ref.dtype)


def diff_attention(qt, k, vt, lam, subln_g, lambda_init, tq=512, tk=512):
    bsz, seq, width = k.shape
    nq = seq // tq
    slopes = 2.0 ** (-(8.0 / DA_HEADS) * jnp.arange(1, DA_HEADS + 1, dtype=F32))

    def split(n):
        pos = jnp.arange(n, dtype=jnp.int32)
        return (pos // POS_SPLIT * POS_SPLIT).astype(F32), (pos % POS_SPLIT).astype(F32)

    qhi, qlo = split(tq)
    khi, klo = split(tk)
    sl = slopes[:, None]
    fq = jnp.zeros((DA_HEADS, DA_V_DIM, tq), F32)
    fq = fq.at[:, 0].set(1.0).at[:, 1].set(1.0).at[:, 2].set(-sl * qhi).at[:, 3].set(-sl * qlo)
    fk = jnp.zeros((DA_HEADS, tk, DA_V_DIM), F32)
    fk = fk.at[:, :, 0].set(sl * khi).at[:, :, 1].set(sl * klo).at[:, :, 2].set(1.0).at[:, :, 3].set(1.0)
    kern = functools.partial(_attn_kernel, tq=tq, tk=tk, seq=seq, out_scale=1.0 - lambda_init)
    smem = pl.BlockSpec(memory_space=pltpu.SMEM)
    return pl.pallas_call(
        kern,
        grid=(bsz, DA_HEADS, seq // tq),
        in_specs=[smem, smem,
                  pl.BlockSpec((DA_V_DIM, tq), lambda b, h, i: (h, b * nq + i)),
                  pl.BlockSpec((1, seq, DA_V_DIM), lambda b, h, i: (b, 0, h)),
                  pl.BlockSpec((DA_V_DIM, seq), lambda b, h, i: (h, b)),
                  pl.BlockSpec((1, DA_V_DIM, tq), lambda b, h, i: (h, 0, 0)),
                  pl.BlockSpec((1, tk, DA_V_DIM), lambda b, h, i: (h, 0, 0)),
                  pl.BlockSpec((DA_V_DIM, 1), lambda b, h, i: (0, 0))],
        out_specs=pl.BlockSpec((1, tq, DA_V_DIM), lambda b, h, i: (b, i, h)),
        out_shape=jax.ShapeDtypeStruct((bsz, seq, width), BF16),
        scratch_shapes=[pltpu.VMEM((1, 2 * tq), F32), pltpu.VMEM((DA_V_DIM + ONES_ROWS, 2 * tq), F32),
                        pltpu.VMEM((tk, 2 * tq), F32), pltpu.VMEM((tk, 2 * tq), F32)],
        compiler_params=_params("arbitrary", "arbitrary", "arbitrary"),
        name="diff_attention",
    )(slopes, lam.reshape(1).astype(F32), qt, k, vt, fq.astype(BF16), fk.astype(BF16),
      subln_g.reshape(DA_V_DIM, 1).astype(F32))


HALO = 16


def _rglru_kernel(xr_ref, prev_ref, next_ref, cw_ref, cb_ref, wa_ref, wx_ref, ba_ref, bx_ref, kk_ref, y_ref,
                  a_s, b_s, y_s, h_s, *, ts, nt):
    direction = pl.program_id(0)
    t = pl.program_id(2)
    chunk = t + direction * (nt - 1 - 2 * t)

    @pl.when(t == 0)
    def _():
        h_s[...] = jnp.zeros(h_s.shape, F32)

    xf = jnp.concatenate([jnp.where(chunk == 0, 0.0, prev_ref[0].astype(F32)), xr_ref[0].astype(F32),
                          jnp.where(chunk == nt - 1, 0.0, next_ref[0].astype(F32))], axis=0)
    rows = ts + 2 * HALO

    def tap(j):
        shifted = xf if j == 2 else pltpu.roll(xf, (2 - j) % rows, 0)
        return cw_ref[j:j + 1, :] * shifted[HALO:HALO + ts]

    x = cb_ref[...] + sum(tap(j) for j in range(RG_CONV))
    xb = x.astype(BF16)
    xh = 0.5 * x
    for n in range(RG_BLOCKS):
        cols = slice(n * RG_BLOCK_W, (n + 1) * RG_BLOCK_W)
        xs = xb[:, cols]
        tr = jnp.tanh(jnp.dot(xs, wa_ref[0, n], preferred_element_type=F32) + ba_ref[0, :, cols])
        ti = jnp.tanh(jnp.dot(xs, wx_ref[0, n], preferred_element_type=F32) + bx_ref[0, :, cols])
        kk = kk_ref[0, :, cols]
        a = jnp.exp2(kk * tr + kk)
        a_s[:, cols] = a
        s = 1.0 - a * a
        root = jnp.where(s > 0.0, s * lax.rsqrt(s), 0.0)
        b_s[:, cols] = root * (ti * xh[:, cols] + xh[:, cols])

    def step(k, h):
        row = jnp.where(direction == 0, k, ts - 1 - k)
        h = a_s[pl.ds(row, 1), :] * h + b_s[pl.ds(row, 1), :]
        y_s[pl.ds(row, 1), :] = h
        return h

    h_s[...] = lax.fori_loop(0, ts, step, h_s[...], unroll=8)
    y_ref[0, 0] = y_s[...].astype(y_ref.dtype)


def rglru_bidir(u, conv_w, conv_b, wa, ba, wx, bx, lam, ts=256):
    bsz, seq, w2 = u.shape
    w = w2 // 2
    nt = seq // ts
    per = ts // HALO
    kk = (-0.5 * RG_C * math.log2(math.e)) * jax.nn.softplus(-lam.astype(F32)).reshape(2, 1, w)
    chunk = lambda d, t: t + d * (nt - 1 - 2 * t)
    dmap4 = lambda d, b, t: (d, 0, 0, 0)
    dmap3 = lambda d, b, t: (d, 0, 0)
    fixed = lambda d, b, t: (0, 0)
    return pl.pallas_call(
        functools.partial(_rglru_kernel, ts=ts, nt=nt),
        grid=(2, bsz, nt),
        in_specs=[pl.BlockSpec((1, ts, w), lambda d, b, t: (b, chunk(d, t), 1)),
                  pl.BlockSpec((1, HALO, w), lambda d, b, t: (b, jnp.maximum(chunk(d, t) * per - 1, 0), 1)),
                  pl.BlockSpec((1, HALO, w),
                               lambda d, b, t: (b, jnp.minimum((chunk(d, t) + 1) * per, seq // HALO - 1), 1)),
                  pl.BlockSpec((RG_CONV, w), fixed), pl.BlockSpec((1, w), fixed),
                  pl.BlockSpec((1, RG_BLOCKS, RG_BLOCK_W, RG_BLOCK_W), dmap4),
                  pl.BlockSpec((1, RG_BLOCKS, RG_BLOCK_W, RG_BLOCK_W), dmap4),
                  pl.BlockSpec((1, 1, w), dmap3), pl.BlockSpec((1, 1, w), dmap3), pl.BlockSpec((1, 1, w), dmap3)],
        out_specs=pl.BlockSpec((1, 1, ts, w), lambda d, b, t: (d, b, chunk(d, t), 0)),
        out_shape=jax.ShapeDtypeStruct((2, bsz, seq, w), BF16),
        scratch_shapes=[pltpu.VMEM((ts, w), F32), pltpu.VMEM((ts, w), F32), pltpu.VMEM((ts, w), F32),
                        pltpu.VMEM((1, w), F32)],
        compiler_params=_params("arbitrary", "arbitrary", "arbitrary"),
        name="rglru_bidir",
    )(u, u, u, conv_w.astype(F32), conv_b.astype(F32).reshape(1, w), (0.5 * wa).astype(BF16),
      (0.5 * wx).astype(BF16), 0.5 * ba.astype(F32).reshape(2, 1, w), 0.5 * bx.astype(F32).reshape(2, 1, w), kk)


def _router_kernel(h_ref, wt_ref, b_ref, tri_ref, e_ref, g_ref, r_ref, cnt_ref, seen_s):
    h = h_ref[...]
    h_hi = h.astype(BF16)
    h_lo = (h - h_hi.astype(F32)).astype(BF16)
    nt_dot = lambda a, b: lax.dot_general(a, b, (((1,), (1,)), ((), ())), preferred_element_type=F32)
    logits = nt_dot(wt_ref[0], h_hi) + (nt_dot(wt_ref[0], h_lo) + nt_dot(wt_ref[1], h_hi))
    tm = logits.shape[1]
    s = jax.nn.sigmoid(logits).reshape(N_GROUPS, EXPERTS_PER_GROUP, tm)
    sel = s + b_ref[...]
    idx = lax.broadcasted_iota(jnp.int32, sel.shape, 1)
    big = jnp.int32(EXPERTS_PER_GROUP)
    v1 = jnp.max(sel, axis=1, keepdims=True)
    i1 = jnp.min(jnp.where(sel == v1, idx, big), axis=1, keepdims=True)
    rest = jnp.where(idx == i1, -jnp.inf, sel)
    v2 = jnp.max(rest, axis=1, keepdims=True)
    i2 = jnp.min(jnp.where(rest == v2, idx, big), axis=1, keepdims=True)
    score = v1 + v2
    gidx = lax.broadcasted_iota(jnp.int32, score.shape, 0)
    best = jnp.max(score, axis=0, keepdims=True)
    grp = jnp.min(jnp.where(score == best, gidx, jnp.int32(N_GROUPS)), axis=0, keepdims=True)
    pick = gidx == grp
    l1 = jnp.sum(jnp.where(pick, i1, 0), axis=0)
    l2 = jnp.sum(jnp.where(pick, i2, 0), axis=0)
    s_g = jnp.sum(jnp.where(pick, s, 0.0), axis=0)
    eidx = lax.broadcasted_iota(jnp.int32, s_g.shape, 0)
    w1 = jnp.sum(jnp.where(eidx == l1, s_g, 0.0), axis=0, keepdims=True)
    w2 = jnp.sum(jnp.where(eidx == l2, s_g, 0.0), axis=0, keepdims=True)
    tot = w1 + w2
    base = grp[0] * EXPERTS_PER_GROUP
    e1, e2 = base + l1, base + l2
    e_ref[...] = jnp.concatenate([e1, e2], axis=0)
    g_ref[...] = jnp.concatenate([w1 / tot, w2 / tot], axis=0)

    @pl.when(pl.program_id(0) == 0)
    def _():
        seen_s[...] = jnp.zeros(seen_s.shape, F32)

    eall = lax.broadcasted_iota(jnp.int32, (N_EXPERTS, tm), 0)
    hit1, hit2 = eall == e1, eall == e2
    both = jnp.where(hit1, 1.0, jnp.where(hit2, 1.0, 0.0))
    incl = jnp.dot(both.astype(BF16), tri_ref[...], preferred_element_type=F32)
    before = incl - both + seen_s[...]
    r_ref[...] = jnp.concatenate([jnp.sum(jnp.where(hit1, before, 0.0), axis=0, keepdims=True),
                                  jnp.sum(jnp.where(hit2, before, 0.0), axis=0, keepdims=True)],
                                 axis=0).astype(jnp.int32)
    seen_s[...] = seen_s[...] + incl[:, tm - 1:tm]
    cnt_ref[...] = seen_s[...].astype(jnp.int32)


def route(h, router_w, router_b, tm=1024):
    n, d = h.shape
    tri = (jnp.arange(tm)[:, None] <= jnp.arange(tm)[None, :]).astype(BF16)
    wt = router_w.astype(F32).T
    wt_hi = wt.astype(BF16)
    wt_lo = (wt - wt_hi.astype(F32)).astype(BF16)
    tile = pl.BlockSpec((TOP_K, tm), lambda i: (0, i))
    return pl.pallas_call(
        _router_kernel,
        grid=(n // tm,),
        in_specs=[pl.BlockSpec((tm, d), lambda i: (i, 0)),
                  pl.BlockSpec((2, N_EXPERTS, d), lambda i: (0, 0, 0)),
                  pl.BlockSpec((N_GROUPS, EXPERTS_PER_GROUP, 1), lambda i: (0, 0, 0)),
                  pl.BlockSpec((tm, tm), lambda i: (0, 0))],
        out_specs=[tile, tile, tile, pl.BlockSpec((N_EXPERTS, 1), lambda i: (0, 0))],
        out_shape=[jax.ShapeDtypeStruct((TOP_K, n), jnp.int32), jax.ShapeDtypeStruct((TOP_K, n), F32),
                   jax.ShapeDtypeStruct((TOP_K, n), jnp.int32), jax.ShapeDtypeStruct((N_EXPERTS, 1), jnp.int32)],
        scratch_shapes=[pltpu.VMEM((N_EXPERTS, 1), F32)],
        compiler_params=_params("arbitrary"),
        name="router",
    )(h, jnp.stack([wt_hi, wt_lo]), router_b.astype(F32).reshape(N_GROUPS, EXPERTS_PER_GROUP, 1), tri)


def _experts_kernel(blk_exp_ref, nxt_exp_ref, slot_ref, n_used_ref, x_ref, wg_hbm, wu_hbm, wd_hbm, o_ref,
                    wg_f, wu_f, wd_f, wg_s, wu_s, wd_s, sem, *, layer):
    i = pl.program_id(0)
    expert = blk_exp_ref[i]
    prev = blk_exp_ref[jnp.maximum(i - 1, 0)]
    slot = slot_ref[i]

    def fetch(e, s):
        return [pltpu.make_async_copy(src.at[layer, e], dst.at[s], sem.at[s])
                for src, dst in ((wg_hbm, wg_f), (wu_hbm, wu_f), (wd_hbm, wd_f))]

    @pl.when(i == 0)
    def _():
        for copy in fetch(expert, slot):
            copy.start()

    @pl.when((i == 0) | (expert != prev))
    def _():
        for copy in fetch(expert, slot):
            copy.wait()
        wg_s[...] = wg_f[slot].astype(BF16)
        wu_s[...] = wu_f[slot].astype(BF16)
        wd_s[...] = wd_f[slot].astype(BF16)

        @pl.when(nxt_exp_ref[i] >= 0)
        def _():
            for copy in fetch(nxt_exp_ref[i], 1 - slot):
                copy.start()

    @pl.when(i < n_used_ref[0])
    def _():
        x = _unpack_halves(x_ref[...])
        hg = jnp.dot(x, wg_s[...], preferred_element_type=F32)
        hu = jnp.dot(x, wu_s[...], preferred_element_type=F32)
        hidden = (jax.nn.silu(hg) * hu).astype(BF16)
        o_ref[...] = jnp.dot(hidden, wd_s[...], preferred_element_type=F32).astype(o_ref.dtype)

    @pl.when(i >= n_used_ref[0])
    def _():
        o_ref[...] = jnp.zeros(o_ref.shape, o_ref.dtype)


def experts(xs, blk_exp, n_used, w_gate, w_up, w_down, layer):
    p = xs.shape[0]
    n_blocks = p // MOE_ROWS
    d = w_gate.shape[2]
    f = w_gate.shape[3]
    idx = jnp.arange(n_blocks, dtype=jnp.int32)
    starts = jnp.concatenate([jnp.ones((1,), bool), blk_exp[1:] != blk_exp[:-1]])
    slot = (jnp.cumsum(starts.astype(jnp.int32)) - 1) % 2
    later = starts[None, :] & (idx[None, :] > idx[:, None])
    first_later = jnp.min(jnp.where(later, idx[None, :], n_blocks), axis=1)
    nxt_exp = jnp.sum(jnp.where(idx[None, :] == first_later[:, None], blk_exp[None, :], 0), axis=1)
    nxt_exp = jnp.where(first_later < n_blocks, nxt_exp, -1).astype(jnp.int32)
    hbm = pl.BlockSpec(memory_space=pl.ANY)
    grid_spec = pltpu.PrefetchScalarGridSpec(
        num_scalar_prefetch=4,
        grid=(n_blocks,),
        in_specs=[pl.BlockSpec((MOE_ROWS, d // 2), lambda i, *_: (i, 0)), hbm, hbm, hbm],
        out_specs=pl.BlockSpec((MOE_ROWS, d), lambda i, *_: (i, 0)),
        scratch_shapes=[pltpu.VMEM((2, d, f), F32), pltpu.VMEM((2, d, f), F32), pltpu.VMEM((2, f, d), F32),
                        pltpu.VMEM((d, f), BF16), pltpu.VMEM((d, f), BF16), pltpu.VMEM((f, d), BF16),
                        pltpu.SemaphoreType.DMA((2,))],
    )
    return pl.pallas_call(
        functools.partial(_experts_kernel, layer=layer),
        grid_spec=grid_spec,
        out_shape=jax.ShapeDtypeStruct((p, d), BF16),
        compiler_params=_params("arbitrary"),
        name="experts",
    )(blk_exp, nxt_exp, slot.astype(jnp.int32), n_used, xs, w_gate, w_up, w_down)


def _dispatch_kernel(dest_ref, fill_ref, x_ref, xs_ref, zero_s, fill_sem, row_sem, *, tm, n_blocks):
    base = pl.program_id(0) * tm

    @pl.when(pl.program_id(0) == 0)
    def _():
        zero_s[...] = jnp.zeros(zero_s.shape, zero_s.dtype)

        def zero_block(blk):
            return pltpu.make_async_copy(zero_s, xs_ref.at[pl.ds(pl.multiple_of(blk * MOE_ROWS, MOE_ROWS),
                                                                 MOE_ROWS)], fill_sem)

        def start(blk, carry):
            @pl.when(fill_ref[blk] != 0)
            def _():
                zero_block(blk).start()
            return carry

        def wait(blk, carry):
            @pl.when(fill_ref[blk] != 0)
            def _():
                zero_block(blk).wait()
            return carry

        lax.fori_loop(0, n_blocks, start, 0)
        lax.fori_loop(0, n_blocks, wait, 0)

    def issue(r, carry):
        for k in range(TOP_K):
            pltpu.make_async_copy(x_ref.at[pl.ds(r, 1)], xs_ref.at[pl.ds(dest_ref[k, base + r], 1)],
                                  row_sem).start()
        return carry

    lax.fori_loop(0, tm, issue, 0, unroll=4)
    for k in range(TOP_K):
        pltpu.make_async_copy(x_ref, xs_ref.at[pl.ds(0, tm)], row_sem).wait()


def dispatch_rows(x, dest, fill, tm=512):
    n, d = x.shape
    n_blocks = fill.shape[0]
    grid_spec = pltpu.PrefetchScalarGridSpec(
        num_scalar_prefetch=2,
        grid=(n // tm,),
        in_specs=[pl.BlockSpec((tm, d), lambda i, dst, fl: (i, 0))],
        out_specs=pl.BlockSpec(memory_space=pl.ANY),
        scratch_shapes=[pltpu.VMEM((MOE_ROWS, d), x.dtype), pltpu.SemaphoreType.DMA(()),
                        pltpu.SemaphoreType.DMA(())],
    )
    return pl.pallas_call(
        functools.partial(_dispatch_kernel, tm=tm, n_blocks=n_blocks),
        grid_spec=grid_spec,
        out_shape=jax.ShapeDtypeStruct((n_blocks * MOE_ROWS, d), x.dtype),
        compiler_params=_params("arbitrary"),
        name="dispatch_rows",
    )(dest, fill, x)


def moe_ffn(h, hp, router_w, router_b, w_gate, w_up, w_down, layer, ln_g, ln_b):
    n, d = h.shape
    a = n * TOP_K
    e_idx, gate, rank, counts = route(h, router_w, router_b)
    counts = counts[:, 0]
    padded = (counts + MOE_ROWS - 1) // MOE_ROWS * MOE_ROWS
    pad_end = jnp.cumsum(padded)
    pad_start = pad_end - padded
    experts_iota = jnp.arange(N_EXPERTS, dtype=jnp.int32)[:, None, None]
    dest = jnp.sum(jnp.where(e_idx[None] == experts_iota, pad_start[:, None, None], 0), axis=0) + rank
    n_blocks = a // MOE_ROWS + N_EXPERTS
    blk_start = jnp.arange(n_blocks, dtype=jnp.int32)[:, None] * MOE_ROWS
    blk_exp = jnp.minimum(jnp.sum((pad_end[None, :] <= blk_start).astype(jnp.int32), axis=1), N_EXPERTS - 1)
    n_used = (pad_end[-1:] // MOE_ROWS).astype(jnp.int32)
    ends_expert = jnp.any((pad_end[None, :] == blk_start + MOE_ROWS) & (padded[None, :] > 0), axis=1)
    fill = (ends_expert | (blk_start[:, 0] >= pad_end[-1])).astype(jnp.int32)
    xs = dispatch_rows(hp, dest, fill)
    yb = experts(xs, blk_exp, n_used, w_gate, w_up, w_down, layer)
    y0 = yb.at[dest[0]].get(mode="promise_in_bounds")
    y1 = yb.at[dest[1]].get(mode="promise_in_bounds")
    return combine_ln(h, y0, y1, gate.T, ln_g, ln_b)


FFT_R = 128
HALF_R = FFT_R // 2
HY_GROUP = 4


def _proj_t_kernel(w_ref, x_ref, o_ref, *, scale):
    acc = lax.dot_general(w_ref[...].astype(BF16), x_ref[...], (((0,), (1,)), ((), ())),
                          preferred_element_type=F32)
    o_ref[...] = (acc * scale).astype(o_ref.dtype)


def proj_t(w, col0, ncols, x, out_dtype, scale=1.0, tm=512, tn=1024):
    k = w.shape[0]
    n = x.shape[0]
    first = col0 // tm
    return pl.pallas_call(
        functools.partial(_proj_t_kernel, scale=scale),
        grid=(ncols // tm, n // tn),
        in_specs=[pl.BlockSpec((k, tm), lambda i, j: (0, first + i)),
                  pl.BlockSpec((tn, k), lambda i, j: (j, 0))],
        out_specs=pl.BlockSpec((tm, tn), lambda i, j: (i, j)),
        out_shape=jax.ShapeDtypeStruct((ncols, n), out_dtype),
        compiler_params=_params("arbitrary", "arbitrary"),
        name="proj_t",
    )(w, x)


def _dft_constants():
    idx = jnp.arange(FFT_R, dtype=jnp.int32)
    prod = idx[:, None] * idx[None, :]
    ang = (prod % FFT_R).astype(F32) * (2.0 * math.pi / FFT_R)
    f_r, f_i = jnp.cos(ang), -jnp.sin(ang)
    ang_t = prod.astype(F32) * (2.0 * math.pi / (FFT_R * FFT_R))
    g_r, g_i = f_r[:HALF_R], -f_i[:HALF_R]
    return dict(
        la=jnp.block([[f_r[:, :HALF_R], -f_i[:, :HALF_R]], [f_i[:, :HALF_R], f_r[:, :HALF_R]]]).astype(BF16),
        la_real=jnp.concatenate([f_r, f_i], axis=0).astype(BF16),
        rb=jnp.block([[f_r, f_i], [-f_i, f_r]]).astype(BF16),
        rc=jnp.block([[f_r, -f_i], [f_i, f_r]]).astype(BF16),
        ld=(jnp.block([[g_r, -g_i], [g_i, g_r]]) / (FFT_R * FFT_R)).astype(BF16),
        t_r=jnp.cos(ang_t), t_i=-jnp.sin(ang_t))


def _rows_to_lanes(x):
    return jnp.concatenate([x[:FFT_R], x[FFT_R:]], axis=1)


def _fwd_lane_stage(a, t_r, t_i, rb):
    a_r, a_i = a[:FFT_R], a[FFT_R:]
    t2_r = jnp.concatenate([t_r, t_r], axis=1)
    t2_i = jnp.concatenate([t_i, t_i], axis=1)
    b_r = a_r * t2_r - a_i * t2_i
    b_i = a_r * t2_i + a_i * t2_r
    lhs = jnp.concatenate([jnp.concatenate([b_r[:, :FFT_R], b_i[:, :FFT_R]], axis=1),
                           jnp.concatenate([b_r[:, FFT_R:], b_i[:, FFT_R:]], axis=1)], axis=0)
    return jnp.dot(lhs.astype(BF16), rb, preferred_element_type=F32)


def _conv_pairs(ms, kfs, la, rb, rc, ld, t_r, t_i):
    a = [jnp.dot(la, jnp.concatenate([m_r, m_i], axis=0).astype(BF16), preferred_element_type=F32)
         for m_r, m_i in ms]
    z = [_fwd_lane_stage(ai, t_r, t_i, rb) for ai in a]
    c = []
    for zi, kf in zip(z, kfs):
        z_r, z_i, k_r, k_i = zi[:, :FFT_R], zi[:, FFT_R:], kf[:, :FFT_R], kf[:, FFT_R:]
        y = jnp.concatenate([z_r * k_r - z_i * k_i, z_r * k_i + z_i * k_r], axis=1)
        c.append(jnp.dot(y.astype(BF16), rc, preferred_element_type=F32))
    t2_r = jnp.concatenate([t_r, t_r], axis=0)
    t2_i = jnp.concatenate([t_i, t_i], axis=0)
    out = []
    for ci in c:
        c_r, c_i = ci[:, :FFT_R], ci[:, FFT_R:]
        d_r = c_r * t2_r + c_i * t2_i
        d_i = c_i * t2_r - c_r * t2_i
        rhs = jnp.concatenate([_rows_to_lanes(d_r), _rows_to_lanes(d_i)], axis=0)
        x = jnp.dot(ld, rhs.astype(BF16), preferred_element_type=F32)
        out.append((x[:HALF_R], x[HALF_R:]))
    return out


def _short_conv_tile(x, w0, w1, w2, b):
    rows = x.shape[0]
    lane = lax.broadcasted_iota(jnp.int32, x.shape, 1)
    row = lax.broadcasted_iota(jnp.int32, x.shape, 0)
    r = pltpu.roll(x, 1, 1)
    prev = jnp.where(lane == 0, jnp.where(row == 0, 0.0, pltpu.roll(r, 1, 0)), r)
    r = pltpu.roll(x, FFT_R - 1, 1)
    nxt = jnp.where(lane == FFT_R - 1, jnp.where(row == rows - 1, 0.0, pltpu.roll(r, rows - 1, 0)), r)
    return w0 * prev + w1 * x + w2 * nxt + b


def _hyena_kernel(cw_ref, cb_ref, skip_ref, v_ref, x1_ref, x2_ref, kf_ref,
                  la_ref, rb_ref, rc_ref, ld_ref, tr_ref, ti_ref, o_ref, *, tc):
    c_base = pl.program_id(0) * tc
    lane = lax.broadcasted_iota(jnp.int32, (1, 2 * FFT_R), 1)

    def conv_in(ref, part, b, c0):
        tiles = []
        for cc in range(2):
            ch = part * HY_CH + c_base + c0 + cc
            tiles.append(_short_conv_tile(ref[c0 + cc, b], cw_ref[0, ch], cw_ref[1, ch], cw_ref[2, ch], cb_ref[ch]))
        return jnp.concatenate(tiles, axis=1)

    def group(g, carry):
        starts = [2 * (HY_GROUP * g + j) for j in range(HY_GROUP)]
        consts = (la_ref[...], rb_ref[...], rc_ref[...], ld_ref[...], tr_ref[...], ti_ref[...])
        zs = [(conv_in(v_ref, 0, 0, c0), conv_in(v_ref, 0, 1, c0)) for c0 in starts]
        for o, g_ref in enumerate((x1_ref, x2_ref)):
            kfs = [kf_ref[o, pl.ds(c0, 2)].reshape(2 * FFT_R, 2 * FFT_R) for c0 in starts]
            ys = _conv_pairs(zs, kfs, *consts)
            nxt = []
            for c0, (z_r, z_i), (y_r, y_i) in zip(starts, zs, ys):
                sk = jnp.where(lane < FFT_R, skip_ref[o, c_base + c0], skip_ref[o, c_base + c0 + 1])
                nxt.append((conv_in(g_ref, o + 1, 0, c0) * (y_r + sk * z_r),
                            conv_in(g_ref, o + 1, 1, c0) * (y_i + sk * z_i)))
            zs = nxt
        for c0, (z_r, z_i) in zip(starts, zs):
            o_ref[c0, 0] = z_r[:, :FFT_R].astype(o_ref.dtype)
            o_ref[c0 + 1, 0] = z_r[:, FFT_R:].astype(o_ref.dtype)
            o_ref[c0, 1] = z_i[:, :FFT_R].astype(o_ref.dtype)
            o_ref[c0 + 1, 1] = z_i[:, FFT_R:].astype(o_ref.dtype)
        return carry

    lax.fori_loop(0, tc // (2 * HY_GROUP), group, 0)


def hyena_mix(u_t, kf, conv_w, conv_b, skip, consts, tc=16):
    rows, n = u_t.shape
    ch = rows // 3
    u5 = u_t.reshape(rows, 2, HALF_R, FFT_R)
    nct = ch // tc
    smem = pl.BlockSpec(memory_space=pltpu.SMEM)
    part = lambda k: pl.BlockSpec((tc, 2, HALF_R, FFT_R), lambda i: (k * nct + i, 0, 0, 0))
    full2 = lambda a: pl.BlockSpec(a.shape, lambda i: (0, 0))
    mats = [consts[k] for k in ("la", "rb", "rc", "ld", "t_r", "t_i")]
    out = pl.pallas_call(
        functools.partial(_hyena_kernel, tc=tc),
        grid=(nct,),
        in_specs=[smem, smem, smem, part(0), part(1), part(2),
                  pl.BlockSpec((2, tc, FFT_R, 2 * FFT_R), lambda i: (0, i, 0, 0))] + [full2(a) for a in mats],
        out_specs=pl.BlockSpec((tc, 2, HALF_R, FFT_R), lambda i: (i, 0, 0, 0)),
        out_shape=jax.ShapeDtypeStruct((ch, 2, HALF_R, FFT_R), BF16),
        compiler_params=_params("arbitrary"),
        name="hyena_mix",
    )(conv_w.astype(F32), conv_b.astype(F32), skip.astype(F32), u5, u5, u5, kf, *mats)
    return out.reshape(ch, n)


def _hyena_filter_kernel(delta_ref, hf_ref, hb_ref, tpos_ref, la_ref, rb_ref, tr_ref, ti_ref, kf_ref, *, tc):
    c_base = pl.program_id(1) * tc

    def taps_of(c0):
        taps = []
        for cc in range(2):
            k = jnp.concatenate([hf_ref[0, 0, c0 + cc], hb_ref[0, 0, c0 + cc]], axis=0)
            k = k * jnp.exp(-tpos_ref[...] * delta_ref[c_base + c0 + cc])
            taps.append(k * lax.rsqrt(jnp.sum(k * k, keepdims=True)))
        return jnp.concatenate(taps, axis=1).astype(BF16)

    def group(g, carry):
        starts = [2 * (HY_GROUP * g + j) for j in range(HY_GROUP)]
        a = [jnp.dot(la_ref[...], taps_of(c0), preferred_element_type=F32) for c0 in starts]
        z = [_fwd_lane_stage(ai, tr_ref[...], ti_ref[...], rb_ref[...]) for ai in a]
        for c0, zi in zip(starts, z):
            kf_ref[0, pl.ds(c0, 2)] = zi.reshape(2, FFT_R, 2 * FFT_R)
        return carry

    lax.fori_loop(0, tc // (2 * HY_GROUP), group, 0)


def hyena_filter_spectra(seq, w1, b1, w2, b2, w3, b3, freq, w4, consts, tc=16):
    t = jnp.linspace(0.0, 1.0, seq, dtype=F32)[:, None]
    omega = 2.0 * math.pi * jnp.arange(seq, dtype=F32)[:, None] / seq
    bands = jnp.linspace(1e-4, HF_BANDS - 1, HF_BANDS, dtype=F32)[None, :]
    ang = omega * bands
    z = jnp.concatenate([t, jnp.cos(ang), -jnp.sin(ang)], -1)
    fr = freq.astype(F32)
    hid = jnp.sin(fr * (z @ w1.astype(F32) + b1.astype(F32)))
    hid = jnp.sin(fr * (hid @ w2.astype(F32) + b2.astype(F32)))
    hid = jnp.sin(fr * (hid @ w3.astype(F32) + b3.astype(F32)))
    back = lambda a: jnp.concatenate([jnp.zeros_like(a[:1]), a[:0:-1]], axis=0)
    w4t = w4.astype(BF16).T
    half = HY_ORDER * HY_CH
    h_fwd = matmul(w4t[:half], hid.T.astype(BF16), F32)
    h_bwd = matmul(w4t[half:], back(hid).T.astype(BF16), F32)
    shape5 = (HY_ORDER, HY_CH, HALF_R, FFT_R)
    tpos = jnp.concatenate([t, back(t)], axis=0).reshape(FFT_R, FFT_R)
    max_decay = math.log(HF_TARGET) / HF_FAST
    min_decay = math.log(HF_TARGET) / HF_SLOW
    deltas = jnp.abs(jnp.linspace(min_decay, max_decay, HY_CH, dtype=F32))
    taps = pl.BlockSpec((1, 1, tc, HALF_R, FFT_R), lambda o, i: (0, o, i, 0, 0))
    full2 = lambda a: pl.BlockSpec(a.shape, lambda o, i: (0, 0))
    mats = [consts[k] for k in ("la_real", "rb", "t_r", "t_i")]
    return pl.pallas_call(
        functools.partial(_hyena_filter_kernel, tc=tc),
        grid=(HY_ORDER, HY_CH // tc),
        in_specs=[pl.BlockSpec(memory_space=pltpu.SMEM), taps, taps, full2(tpos)] + [full2(a) for a in mats],
        out_specs=pl.BlockSpec((1, tc, FFT_R, 2 * FFT_R), lambda o, i: (o, i, 0, 0)),
        out_shape=jax.ShapeDtypeStruct((HY_ORDER, HY_CH, FFT_R, 2 * FFT_R), F32),
        compiler_params=_params("arbitrary", "arbitrary"),
        name="hyena_filter_spectra",
    )(deltas, h_fwd.reshape((1,) + shape5), h_bwd.reshape((1,) + shape5), tpos, *mats)


def kernel(x, ev_w_in, ev_hy_conv_w, ev_hy_conv_b, ev_hf_w1, ev_hf_b1, ev_hf_w2, ev_hf_b2, ev_hf_w3, ev_hf_b3,
           ev_hf_freq, ev_hf_w4, ev_hy_skip, ev_lam_q1, ev_lam_k1, ev_lam_q2, ev_lam_k2, ev_subln_g, ev_w_out,
           od_w_in, od_conv_w, od_conv_b, od_wa, od_ba, od_wx, od_bx, od_lam, od_w_out, ln1_g, ln1_b, ln2_g,
           ln2_b, router_w, router_b, ex_w_gate, ex_w_up, ex_w_down):
    bsz, seq, d = x.shape
    n = bsz * seq
    h = x.reshape(n, d).astype(F32)
    hb = h.astype(BF16)
    for layer in range(DEPTH):
        i = layer // 2
        if layer % 2 == 0:
            w_in = ev_w_in[i]
            c0 = 3 * HY_CH
            u_t = proj_t(w_in, 0, c0, hb, F32)
            qt = proj_t(w_in, c0, DA_QK, hb, BF16, scale=DA_HEAD_DIM ** -0.5)
            k = matmul(hb, w_in, BF16, col0=c0 + DA_QK, ncols=DA_QK)
            vt = proj_t(w_in, c0 + 2 * DA_QK, DA_WIDTH, hb, BF16)
            consts = _dft_constants()
            kf = hyena_filter_spectra(seq, ev_hf_w1[i], ev_hf_b1[i], ev_hf_w2[i], ev_hf_b2[i],
                                      ev_hf_w3[i], ev_hf_b3[i], ev_hf_freq[i], ev_hf_w4[i], consts)
            y_hy = hyena_mix(u_t, kf, ev_hy_conv_w[i], ev_hy_conv_b[i], ev_hy_skip[i], consts)
            lambda_init = 0.8 - 0.6 * math.exp(-0.3 * layer)
            lam = (jnp.exp(jnp.sum(ev_lam_q1[i].astype(F32) * ev_lam_k1[i].astype(F32)))
                   - jnp.exp(jnp.sum(ev_lam_q2[i].astype(F32) * ev_lam_k2[i].astype(F32))) + lambda_init)
            y_da = diff_attention(qt, k.reshape(bsz, seq, DA_QK), vt,
                                  lam, ev_subln_g[i], lambda_init).reshape(n, DA_WIDTH)
            h, hp = even_out_proj(y_hy, y_da, ev_w_out[i].astype(BF16), h, ln1_g[layer], ln1_b[layer])
        else:
            u = matmul(hb, od_w_in[i], BF16)
            yd = rglru_bidir(u.reshape(bsz, seq, 2 * RG_WIDTH), od_conv_w[i], od_conv_b[i], od_wa[i], od_ba[i],
                             od_wx[i], od_bx[i], od_lam[i]).reshape(2, n, RG_WIDTH)
            h, hp = odd_out_proj(u, yd, od_w_out[i].astype(BF16), h, ln1_g[layer], ln1_b[layer])
        h, hb = moe_ffn(h, hp, router_w, router_b, ex_w_gate, ex_w_up, ex_w_down, layer,
                        ln2_g[layer], ln2_b[layer])
    return h.reshape(bsz, seq, d).astype(x.dtype)
```

```python
import functools
import math

import jax
import jax.numpy as jnp
from jax import lax
from jax.experimental import pallas as pl
from jax.experimental.pallas import tpu as pltpu

F32 = jnp.float32
BF16 = jnp.bfloat16

D_MODEL = 2048
DEPTH = 2
HY_CH = D_MODEL // 2
HY_ORDER = 2
HF_BANDS = 16
HF_TARGET = 1e-2
HF_FAST = 0.3
HF_SLOW = 1.5
DA_HEADS = 8
DA_HEAD_DIM = 64
DA_V_DIM = 2 * DA_HEAD_DIM
DA_QK = DA_HEADS * 2 * DA_HEAD_DIM
DA_WIDTH = DA_HEADS * DA_V_DIM
RG_WIDTH = D_MODEL
RG_BLOCKS = 8
RG_BLOCK_W = RG_WIDTH // RG_BLOCKS
RG_C = 8.0
RG_CONV = 4
N_EXPERTS = 32
N_GROUPS = 4
EXPERTS_PER_GROUP = N_EXPERTS // N_GROUPS
TOP_K = 2
D_FF = 512
DN_ALPHA = (2 * DEPTH) ** 0.25
LN_EPS = 1e-5

VMEM_LIMIT_BYTES = 56 * 1024 * 1024
MOE_ROWS = 256


def _params(*sem):
    return pltpu.CompilerParams(dimension_semantics=sem, vmem_limit_bytes=VMEM_LIMIT_BYTES)


def _mm_kernel(x_ref, w_ref, o_ref):
    o_ref[...] = jnp.dot(x_ref[...], w_ref[...].astype(BF16), preferred_element_type=F32).astype(o_ref.dtype)


def matmul(x, w, out_dtype, col0=0, ncols=None, tm=512, tn=1024):
    m, k = x.shape
    n = w.shape[1] if ncols is None else ncols
    tm, tn = min(tm, m), min(tn, n)
    first = col0 // tn
    return pl.pallas_call(
        _mm_kernel,
        grid=(n // tn, m // tm),
        in_specs=[pl.BlockSpec((tm, k), lambda j, i: (i, 0)),
                  pl.BlockSpec((k, tn), lambda j, i: (0, first + j))],
        out_specs=pl.BlockSpec((tm, tn), lambda j, i: (i, j)),
        out_shape=jax.ShapeDtypeStruct((m, n), out_dtype),
        compiler_params=_params("arbitrary", "arbitrary"),
        name="matmul",
    )(x, w)


def _ln_rows(z, g, b):
    mu = jnp.mean(z, axis=-1, keepdims=True)
    zc = z - mu
    var = jnp.mean(zc * zc, axis=-1, keepdims=True)
    return zc * lax.rsqrt(var + LN_EPS) * g + b


def _pack_halves(y):
    half = y.shape[1] // 2
    hi = pltpu.bitcast(y[:, :half].astype(BF16).astype(F32), jnp.uint32)
    lo = pltpu.bitcast(y[:, half:].astype(BF16).astype(F32), jnp.uint32)
    return hi | (lo >> 16)


def _unpack_halves(p):
    hi = pltpu.bitcast(p & jnp.uint32(0xFFFF0000), F32)
    lo = pltpu.bitcast(p << 16, F32)
    return jnp.concatenate([hi, lo], axis=1).astype(BF16)


def _even_out_kernel(xa_ref, xb_ref, wa_ref, wb_ref, r_ref, g_ref, b_ref, o_ref, op_ref):
    acc = lax.dot_general(xa_ref[...], wa_ref[...], (((0,), (0,)), ((), ())), preferred_element_type=F32)
    acc += jnp.dot(xb_ref[...], wb_ref[...], preferred_element_type=F32)
    y = _ln_rows(DN_ALPHA * r_ref[...] + acc, g_ref[...], b_ref[...])
    o_ref[...] = y
    op_ref[...] = _pack_halves(y)


def even_out_proj(y_hy_t, y_da, w_out, resid, g, b, tm=512):
    ka, m = y_hy_t.shape
    kb = y_da.shape[1]
    d = w_out.shape[1]
    row = lambda i: (i, 0)
    fixed = lambda i: (0, 0)
    return pl.pallas_call(
        _even_out_kernel,
        grid=(m // tm,),
        in_specs=[pl.BlockSpec((ka, tm), lambda i: (0, i)), pl.BlockSpec((tm, kb), row),
                  pl.BlockSpec((ka, d), fixed), pl.BlockSpec((kb, d), fixed),
                  pl.BlockSpec((tm, d), row), pl.BlockSpec((1, d), fixed), pl.BlockSpec((1, d), fixed)],
        out_specs=[pl.BlockSpec((tm, d), row), pl.BlockSpec((tm, d // 2), row)],
        out_shape=[jax.ShapeDtypeStruct((m, d), F32), jax.ShapeDtypeStruct((m, d // 2), jnp.uint32)],
        compiler_params=_params("arbitrary"),
        name="even_out_proj",
    )(y_hy_t, y_da, w_out[:ka], w_out[ka:], resid, g.reshape(1, d), b.reshape(1, d))


def _odd_out_kernel(gate_ref, y0_ref, y1_ref, w_ref, r_ref, g_ref, b_ref, o_ref, op_ref):
    x = jax.nn.gelu(gate_ref[...].astype(F32)) * (y0_ref[0].astype(F32) + y1_ref[0].astype(F32))
    acc = jnp.dot(x.astype(BF16), w_ref[...], preferred_element_type=F32)
    y = _ln_rows(DN_ALPHA * r_ref[...] + acc, g_ref[...], b_ref[...])
    o_ref[...] = y
    op_ref[...] = _pack_halves(y)


def odd_out_proj(u, yd, w_out, resid, g, b, tm=256):
    m = u.shape[0]
    k, d = w_out.shape
    row = lambda i: (i, 0)
    fixed = lambda i: (0, 0)
    return pl.pallas_call(
        _odd_out_kernel,
        grid=(m // tm,),
        in_specs=[pl.BlockSpec((tm, k), row),
                  pl.BlockSpec((1, tm, k), lambda i: (0, i, 0)),
                  pl.BlockSpec((1, tm, k), lambda i: (1, i, 0)),
                  pl.BlockSpec((k, d), fixed),
                  pl.BlockSpec((tm, d), row), pl.BlockSpec((1, d), fixed), pl.BlockSpec((1, d), fixed)],
        out_specs=[pl.BlockSpec((tm, d), row), pl.BlockSpec((tm, d // 2), row)],
        out_shape=[jax.ShapeDtypeStruct((m, d), F32), jax.ShapeDtypeStruct((m, d // 2), jnp.uint32)],
        compiler_params=_params("arbitrary"),
        name="odd_out_proj",
    )(u, yd, yd, w_out, resid, g.reshape(1, d), b.reshape(1, d))


def _combine_ln_kernel(h_ref, y0_ref, y1_ref, gate_ref, g_ref, b_ref, o_ref, ob_ref):
    gate = gate_ref[...]
    ffn = gate[:, 0:1] * y0_ref[...].astype(F32) + gate[:, 1:2] * y1_ref[...].astype(F32)
    y = _ln_rows(DN_ALPHA * h_ref[...] + ffn, g_ref[...], b_ref[...])
    o_ref[...] = y
    ob_ref[...] = y.astype(BF16)


def combine_ln(h, y, gate, g, b, tm=512):
    m, d = h.shape
    row = lambda i: (i, 0)
    fixed = lambda i: (0, 0)
    return pl.pallas_call(
        _combine_ln_kernel,
        grid=(m // tm,),
        in_specs=[pl.BlockSpec((tm, d), row), pl.BlockSpec((tm, d), row),
                  pl.BlockSpec((tm, d), lambda i: (m // tm + i, 0)),
                  pl.BlockSpec((tm, TOP_K), row), pl.BlockSpec((1, d), fixed), pl.BlockSpec((1, d), fixed)],
        out_specs=[pl.BlockSpec((tm, d), row), pl.BlockSpec((tm, d), row)],
        out_shape=[jax.ShapeDtypeStruct((m, d), F32), jax.ShapeDtypeStruct((m, d), BF16)],
        compiler_params=_params("arbitrary"),
        name="combine_ln",
    )(h, y, y, gate, g.reshape(1, d), b.reshape(1, d))


ONES_ROWS = 16
POS_SPLIT = 16


def _attn_kernel(slopes_ref, lam_ref, qt_ref, k_ref, vt_ref, fq_ref, fk_ref, bd_ref, g_ref, o_ref, m_s, acc_s, sa_s, sb_s,
                 *, tq, tk, seq, out_scale):
    head = pl.program_id(1)
    i0 = pl.program_id(2) * tq
    slope = slopes_ref[head]
    lam = lam_ref[0]
    qt = qt_ref[...]
    row = lax.broadcasted_iota(jnp.int32, qt.shape, 0)
    zero = jnp.zeros_like(qt)
    q2 = jnp.concatenate([jnp.where(row < DA_HEAD_DIM, qt, zero),
                          jnp.where(row >= DA_HEAD_DIM, qt, zero)], axis=1)
    fq = fq_ref[0]
    fq2 = jnp.concatenate([fq, fq], axis=1)
    q_aug = jnp.concatenate([q2, fq2], axis=0)
    fk = fk_ref[0]
    fk_neg = -fk
    ones = jnp.ones((ONES_ROWS, tk), BF16)
    m_s[...] = jnp.full(m_s.shape, -jnp.inf, F32)
    acc_s[...] = jnp.zeros(acc_s.shape, F32)
    j_diag = i0 // tk

    def update(j, s, shift):
        off = pl.multiple_of(j * tk, tk)
        va = jnp.concatenate([vt_ref[:, pl.ds(off, tk)], ones], axis=0)
        m_prev = m_s[...]
        m_new = jnp.maximum(m_prev, jnp.max(s, axis=0, keepdims=True) - shift)
        p = jnp.exp(s - (m_new + shift))
        alpha = jnp.exp(m_prev - m_new)
        acc_s[...] = alpha * acc_s[...] + jnp.dot(va, p.astype(BF16), preferred_element_type=F32)
        m_s[...] = m_new

    def chunk_of(t):
        return t + (t >= j_diag).astype(jnp.int32)

    def scores(t):
        j = chunk_of(t)
        off = pl.multiple_of(j * tk, tk)
        ka = jnp.concatenate([k_ref[0, pl.ds(off, tk), :], jnp.where(j < j_diag, fk, fk_neg)], axis=1)
        return jnp.dot(ka, q_aug, preferred_element_type=F32)

    def consume(s_ref, t):
        j = chunk_of(t)
        update(j, s_ref[...], slope * jnp.abs(i0 - j * tk).astype(F32))

    j0 = pl.multiple_of(j_diag * tk, tk)
    bias = bd_ref[0]
    s_diag = jnp.dot(k_ref[0, pl.ds(j0, tk), :], q2, preferred_element_type=F32)
    sa_s[...] = s_diag - jnp.concatenate([bias, bias], axis=1)
    sb_s[...] = scores(jnp.int32(0))
    update(j_diag, sa_s[...], 0.0)
    n_off = seq // tk - 1

    def pair(t, carry):
        sa_s[...] = scores(2 * t + 1)
        consume(sb_s, 2 * t)
        sb_s[...] = scores(2 * t + 2)
        consume(sa_s, 2 * t + 1)
        return carry

    lax.fori_loop(0, n_off // 2, pair, 0)
    consume(sb_s, jnp.int32(n_off - 1))

    acc = acc_s[...]
    o1 = acc[:DA_V_DIM, :tq] / acc[DA_V_DIM:DA_V_DIM + 1, :tq]
    o2 = acc[:DA_V_DIM, tq:] / acc[DA_V_DIM:DA_V_DIM + 1, tq:]
    o = o1 - lam * o2
    o = o * lax.rsqrt(jnp.mean(o * o, axis=0, keepdims=True) + LN_EPS) * g_ref[...]
    o_ref[0] = (o * out_scale).T.astype(o_ref.dtype)


def diff_attention(qt, k, vt, lam, subln_g, lambda_init, tq=512, tk=512):
    bsz, seq, width = k.shape
    nq = seq // tq
    slopes = 2.0 ** (-(8.0 / DA_HEADS) * jnp.arange(1, DA_HEADS + 1, dtype=F32))

    def split(n):
        pos = jnp.arange(n, dtype=jnp.int32)
        return (pos // POS_SPLIT * POS_SPLIT).astype(F32), (pos % POS_SPLIT).astype(F32)

    qhi, qlo = split(tq)
    khi, klo = split(tk)
    sl = slopes[:, None]
    fq = jnp.zeros((DA_HEADS, DA_V_DIM, tq), F32)
    fq = fq.at[:, 0].set(1.0).at[:, 1].set(1.0).at[:, 2].set(-sl * qhi).at[:, 3].set(-sl * qlo)
    fk = jnp.zeros((DA_HEADS, tk, DA_V_DIM), F32)
    fk = fk.at[:, :, 0].set(sl * khi).at[:, :, 1].set(sl * klo).at[:, :, 2].set(1.0).at[:, :, 3].set(1.0)
    assert tq == tk, "the chunk holding a query tile's diagonal must be the tile itself"
    pos = jnp.arange(tk, dtype=jnp.int32)
    bias_diag = slopes[:, None, None] * jnp.abs(pos[:, None] - pos[None, :]).astype(F32)
    kern = functools.partial(_attn_kernel, tq=tq, tk=tk, seq=seq, out_scale=1.0 - lambda_init)
    smem = pl.BlockSpec(memory_space=pltpu.SMEM)
    return pl.pallas_call(
        kern,
        grid=(bsz, DA_HEADS, seq // tq),
        in_specs=[smem, smem,
                  pl.BlockSpec((DA_V_DIM, tq), lambda b, h, i: (h, b * nq + i)),
                  pl.BlockSpec((1, seq, DA_V_DIM), lambda b, h, i: (b, 0, h)),
                  pl.BlockSpec((DA_V_DIM, seq), lambda b, h, i: (h, b)),
                  pl.BlockSpec((1, DA_V_DIM, tq), lambda b, h, i: (h, 0, 0)),
                  pl.BlockSpec((1, tk, DA_V_DIM), lambda b, h, i: (h, 0, 0)),
                  pl.BlockSpec((1, tk, tq), lambda b, h, i: (h, 0, 0)),
                  pl.BlockSpec((DA_V_DIM, 1), lambda b, h, i: (0, 0))],
        out_specs=pl.BlockSpec((1, tq, DA_V_DIM), lambda b, h, i: (b, i, h)),
        out_shape=jax.ShapeDtypeStruct((bsz, seq, width), BF16),
        scratch_shapes=[pltpu.VMEM((1, 2 * tq), F32), pltpu.VMEM((DA_V_DIM + ONES_ROWS, 2 * tq), F32),
                        pltpu.VMEM((tk, 2 * tq), F32), pltpu.VMEM((tk, 2 * tq), F32)],
        compiler_params=_params("arbitrary", "arbitrary", "arbitrary"),
        name="diff_attention",
    )(slopes, lam.reshape(1).astype(F32), qt, k, vt, fq.astype(BF16), fk.astype(BF16), bias_diag,
      subln_g.reshape(DA_V_DIM, 1).astype(F32))


HALO = 16


def _rglru_kernel(xr_ref, prev_ref, next_ref, cw_ref, cb_ref, wa_ref, wx_ref, ba_ref, bx_ref, kk_ref, y_ref,
                  a_s, b_s, y_s, h_s, *, ts, nt):
    direction = pl.program_id(0)
    t = pl.program_id(2)
    chunk = t + direction * (nt - 1 - 2 * t)

    @pl.when(t == 0)
    def _():
        h_s[...] = jnp.zeros(h_s.shape, F32)

    xf = jnp.concatenate([jnp.where(chunk == 0, 0.0, prev_ref[0].astype(F32)), xr_ref[0].astype(F32),
                          jnp.where(chunk == nt - 1, 0.0, next_ref[0].astype(F32))], axis=0)
    rows = ts + 2 * HALO

    def tap(j):
        shifted = xf if j == 2 else pltpu.roll(xf, (2 - j) % rows, 0)
        return cw_ref[j:j + 1, :] * shifted[HALO:HALO + ts]

    x = cb_ref[...] + sum(tap(j) for j in range(RG_CONV))
    xb = x.astype(BF16)
    xh = 0.5 * x
    for n in range(RG_BLOCKS):
        cols = slice(n * RG_BLOCK_W, (n + 1) * RG_BLOCK_W)
        xs = xb[:, cols]
        tr = jnp.tanh(jnp.dot(xs, wa_ref[0, n], preferred_element_type=F32) + ba_ref[0, :, cols])
        ti = jnp.tanh(jnp.dot(xs, wx_ref[0, n], preferred_element_type=F32) + bx_ref[0, :, cols])
        kk = kk_ref[0, :, cols]
        a = jnp.exp2(kk * tr + kk)
        a_s[:, cols] = a
        s = 1.0 - a * a
        root = jnp.where(s > 0.0, s * lax.rsqrt(s), 0.0)
        b_s[:, cols] = root * (ti * xh[:, cols] + xh[:, cols])

    def step(k, h):
        row = jnp.where(direction == 0, k, ts - 1 - k)
        h = a_s[pl.ds(row, 1), :] * h + b_s[pl.ds(row, 1), :]
        y_s[pl.ds(row, 1), :] = h
        return h

    h_s[...] = lax.fori_loop(0, ts, step, h_s[...], unroll=8)
    y_ref[0, 0] = y_s[...].astype(y_ref.dtype)


def rglru_bidir(u, conv_w, conv_b, wa, ba, wx, bx, lam, ts=256):
    bsz, seq, w2 = u.shape
    w = w2 // 2
    nt = seq // ts
    per = ts // HALO
    kk = (-0.5 * RG_C * math.log2(math.e)) * jax.nn.softplus(-lam.astype(F32)).reshape(2, 1, w)
    chunk = lambda d, t: t + d * (nt - 1 - 2 * t)
    dmap4 = lambda d, b, t: (d, 0, 0, 0)
    dmap3 = lambda d, b, t: (d, 0, 0)
    fixed = lambda d, b, t: (0, 0)
    return pl.pallas_call(
        functools.partial(_rglru_kernel, ts=ts, nt=nt),
        grid=(2, bsz, nt),
        in_specs=[pl.BlockSpec((1, ts, w), lambda d, b, t: (b, chunk(d, t), 1)),
                  pl.BlockSpec((1, HALO, w), lambda d, b, t: (b, jnp.maximum(chunk(d, t) * per - 1, 0), 1)),
                  pl.BlockSpec((1, HALO, w),
                               lambda d, b, t: (b, jnp.minimum((chunk(d, t) + 1) * per, seq // HALO - 1), 1)),
                  pl.BlockSpec((RG_CONV, w), fixed), pl.BlockSpec((1, w), fixed),
                  pl.BlockSpec((1, RG_BLOCKS, RG_BLOCK_W, RG_BLOCK_W), dmap4),
                  pl.BlockSpec((1, RG_BLOCKS, RG_BLOCK_W, RG_BLOCK_W), dmap4),
                  pl.BlockSpec((1, 1, w), dmap3), pl.BlockSpec((1, 1, w), dmap3), pl.BlockSpec((1, 1, w), dmap3)],
        out_specs=pl.BlockSpec((1, 1, ts, w), lambda d, b, t: (d, b, chunk(d, t), 0)),
        out_shape=jax.ShapeDtypeStruct((2, bsz, seq, w), BF16),
        scratch_shapes=[pltpu.VMEM((ts, w), F32), pltpu.VMEM((ts, w), F32), pltpu.VMEM((ts, w), F32),
                        pltpu.VMEM((1, w), F32)],
        compiler_params=_params("arbitrary", "arbitrary", "arbitrary"),
        name="rglru_bidir",
    )(u, u, u, conv_w.astype(F32), conv_b.astype(F32).reshape(1, w), (0.5 * wa).astype(BF16),
      (0.5 * wx).astype(BF16), 0.5 * ba.astype(F32).reshape(2, 1, w), 0.5 * bx.astype(F32).reshape(2, 1, w), kk)


def _router_kernel(h_ref, wt_ref, b_ref, tri_ref, e_ref, g_ref, r_ref, cnt_ref, seen_s):
    h = h_ref[...]
    h_hi = h.astype(BF16)
    h_lo = (h - h_hi.astype(F32)).astype(BF16)
    nt_dot = lambda a, b: lax.dot_general(a, b, (((1,), (1,)), ((), ())), preferred_element_type=F32)
    logits = nt_dot(wt_ref[0], h_hi) + (nt_dot(wt_ref[0], h_lo) + nt_dot(wt_ref[1], h_hi))
    tm = logits.shape[1]
    s = jax.nn.sigmoid(logits).reshape(N_GROUPS, EXPERTS_PER_GROUP, tm)
    sel = s + b_ref[...]
    idx = lax.broadcasted_iota(jnp.int32, sel.shape, 1)
    big = jnp.int32(EXPERTS_PER_GROUP)
    v1 = jnp.max(sel, axis=1, keepdims=True)
    i1 = jnp.min(jnp.where(sel == v1, idx, big), axis=1, keepdims=True)
    rest = jnp.where(idx == i1, -jnp.inf, sel)
    v2 = jnp.max(rest, axis=1, keepdims=True)
    i2 = jnp.min(jnp.where(rest == v2, idx, big), axis=1, keepdims=True)
    score = v1 + v2
    gidx = lax.broadcasted_iota(jnp.int32, score.shape, 0)
    best = jnp.max(score, axis=0, keepdims=True)
    grp = jnp.min(jnp.where(score == best, gidx, jnp.int32(N_GROUPS)), axis=0, keepdims=True)
    pick = gidx == grp
    l1 = jnp.sum(jnp.where(pick, i1, 0), axis=0)
    l2 = jnp.sum(jnp.where(pick, i2, 0), axis=0)
    s_g = jnp.sum(jnp.where(pick, s, 0.0), axis=0)
    eidx = lax.broadcasted_iota(jnp.int32, s_g.shape, 0)
    w1 = jnp.sum(jnp.where(eidx == l1, s_g, 0.0), axis=0, keepdims=True)
    w2 = jnp.sum(jnp.where(eidx == l2, s_g, 0.0), axis=0, keepdims=True)
    tot = w1 + w2
    base = grp[0] * EXPERTS_PER_GROUP
    e1, e2 = base + l1, base + l2
    e_ref[...] = jnp.concatenate([e1, e2], axis=0)
    g_ref[...] = jnp.concatenate([w1 / tot, w2 / tot], axis=0)

    @pl.when(pl.program_id(0) == 0)
    def _():
        seen_s[...] = jnp.zeros(seen_s.shape, F32)

    eall = lax.broadcasted_iota(jnp.int32, (N_EXPERTS, tm), 0)
    hit1, hit2 = eall == e1, eall == e2
    both = jnp.where(hit1, 1.0, jnp.where(hit2, 1.0, 0.0))
    incl = jnp.dot(both.astype(BF16), tri_ref[...], preferred_element_type=F32)
    before = incl - both + seen_s[...]
    r_ref[...] = jnp.concatenate([jnp.sum(jnp.where(hit1, before, 0.0), axis=0, keepdims=True),
                                  jnp.sum(jnp.where(hit2, before, 0.0), axis=0, keepdims=True)],
                                 axis=0).astype(jnp.int32)
    seen_s[...] = seen_s[...] + incl[:, tm - 1:tm]
    cnt_ref[...] = seen_s[...].astype(jnp.int32)


def route(h, router_w, router_b, tm=1024):
    n, d = h.shape
    tri = (jnp.arange(tm)[:, None] <= jnp.arange(tm)[None, :]).astype(BF16)
    wt = router_w.astype(F32).T
    wt_hi = wt.astype(BF16)
    wt_lo = (wt - wt_hi.astype(F32)).astype(BF16)
    tile = pl.BlockSpec((TOP_K, tm), lambda i: (0, i))
    return pl.pallas_call(
        _router_kernel,
        grid=(n // tm,),
        in_specs=[pl.BlockSpec((tm, d), lambda i: (i, 0)),
                  pl.BlockSpec((2, N_EXPERTS, d), lambda i: (0, 0, 0)),
                  pl.BlockSpec((N_GROUPS, EXPERTS_PER_GROUP, 1), lambda i: (0, 0, 0)),
                  pl.BlockSpec((tm, tm), lambda i: (0, 0))],
        out_specs=[tile, tile, tile, pl.BlockSpec((N_EXPERTS, 1), lambda i: (0, 0))],
        out_shape=[jax.ShapeDtypeStruct((TOP_K, n), jnp.int32), jax.ShapeDtypeStruct((TOP_K, n), F32),
                   jax.ShapeDtypeStruct((TOP_K, n), jnp.int32), jax.ShapeDtypeStruct((N_EXPERTS, 1), jnp.int32)],
        scratch_shapes=[pltpu.VMEM((N_EXPERTS, 1), F32)],
        compiler_params=_params("arbitrary"),
        name="router",
    )(h, jnp.stack([wt_hi, wt_lo]), router_b.astype(F32).reshape(N_GROUPS, EXPERTS_PER_GROUP, 1), tri)


def _experts_kernel(blk_exp_ref, nxt_exp_ref, slot_ref, n_used_ref, x_ref, wg_hbm, wu_hbm, wd_hbm, o_ref,
                    wg_f, wu_f, wd_f, wg_s, wu_s, wd_s, sem, *, layer):
    i = pl.program_id(0)
    expert = blk_exp_ref[i]
    prev = blk_exp_ref[jnp.maximum(i - 1, 0)]
    slot = slot_ref[i]

    def fetch(e, s):
        return [pltpu.make_async_copy(src.at[layer, e], dst.at[s], sem.at[s])
                for src, dst in ((wg_hbm, wg_f), (wu_hbm, wu_f), (wd_hbm, wd_f))]

    @pl.when(i == 0)
    def _():
        for copy in fetch(expert, slot):
            copy.start()

    @pl.when((i == 0) | (expert != prev))
    def _():
        for copy in fetch(expert, slot):
            copy.wait()
        wg_s[...] = wg_f[slot].astype(BF16)
        wu_s[...] = wu_f[slot].astype(BF16)
        wd_s[...] = wd_f[slot].astype(BF16)

        @pl.when(nxt_exp_ref[i] >= 0)
        def _():
            for copy in fetch(nxt_exp_ref[i], 1 - slot):
                copy.start()

    @pl.when(i < n_used_ref[0])
    def _():
        x = _unpack_halves(x_ref[...])
        hg = jnp.dot(x, wg_s[...], preferred_element_type=F32)
        hu = jnp.dot(x, wu_s[...], preferred_element_type=F32)
        hidden = (jax.nn.silu(hg) * hu).astype(BF16)
        o_ref[...] = jnp.dot(hidden, wd_s[...], preferred_element_type=F32).astype(o_ref.dtype)

    @pl.when(i >= n_used_ref[0])
    def _():
        o_ref[...] = jnp.zeros(o_ref.shape, o_ref.dtype)


def experts(xs, blk_exp, n_used, w_gate, w_up, w_down, layer):
    p = xs.shape[0]
    n_blocks = p // MOE_ROWS
    d = w_gate.shape[2]
    f = w_gate.shape[3]
    idx = jnp.arange(n_blocks, dtype=jnp.int32)
    starts = jnp.concatenate([jnp.ones((1,), bool), blk_exp[1:] != blk_exp[:-1]])
    slot = (jnp.cumsum(starts.astype(jnp.int32)) - 1) % 2
    later = starts[None, :] & (idx[None, :] > idx[:, None])
    first_later = jnp.min(jnp.where(later, idx[None, :], n_blocks), axis=1)
    nxt_exp = jnp.sum(jnp.where(idx[None, :] == first_later[:, None], blk_exp[None, :], 0), axis=1)
    nxt_exp = jnp.where(first_later < n_blocks, nxt_exp, -1).astype(jnp.int32)
    hbm = pl.BlockSpec(memory_space=pl.ANY)
    grid_spec = pltpu.PrefetchScalarGridSpec(
        num_scalar_prefetch=4,
        grid=(n_blocks,),
        in_specs=[pl.BlockSpec((MOE_ROWS, d // 2), lambda i, *_: (i, 0)), hbm, hbm, hbm],
        out_specs=pl.BlockSpec((MOE_ROWS, d), lambda i, *_: (i, 0)),
        scratch_shapes=[pltpu.VMEM((2, d, f), F32), pltpu.VMEM((2, d, f), F32), pltpu.VMEM((2, f, d), F32),
                        pltpu.VMEM((d, f), BF16), pltpu.VMEM((d, f), BF16), pltpu.VMEM((f, d), BF16),
                        pltpu.SemaphoreType.DMA((2,))],
    )
    return pl.pallas_call(
        functools.partial(_experts_kernel, layer=layer),
        grid_spec=grid_spec,
        out_shape=jax.ShapeDtypeStruct((p, d), BF16),
        compiler_params=_params("arbitrary"),
        name="experts",
    )(blk_exp, nxt_exp, slot.astype(jnp.int32), n_used, xs, w_gate, w_up, w_down)


def _dispatch_kernel(dest_ref, fill_ref, x_ref, xs_ref, zero_s, fill_sem, row_sem, *, tm, n_blocks):
    base = pl.program_id(0) * tm

    @pl.when(pl.program_id(0) == 0)
    def _():
        zero_s[...] = jnp.zeros(zero_s.shape, zero_s.dtype)

        def zero_block(blk):
            return pltpu.make_async_copy(zero_s, xs_ref.at[pl.ds(pl.multiple_of(blk * MOE_ROWS, MOE_ROWS),
                                                                 MOE_ROWS)], fill_sem)

        def start(blk, carry):
            @pl.when(fill_ref[blk] != 0)
            def _():
                zero_block(blk).start()
            return carry

        def wait(blk, carry):
            @pl.when(fill_ref[blk] != 0)
            def _():
                zero_block(blk).wait()
            return carry

        lax.fori_loop(0, n_blocks, start, 0)
        lax.fori_loop(0, n_blocks, wait, 0)

    def issue(r, carry):
        for k in range(TOP_K):
            pltpu.make_async_copy(x_ref.at[pl.ds(r, 1)], xs_ref.at[pl.ds(dest_ref[k, base + r], 1)],
                                  row_sem).start()
        return carry

    lax.fori_loop(0, tm, issue, 0, unroll=4)
    for k in range(TOP_K):
        pltpu.make_async_copy(x_ref, xs_ref.at[pl.ds(0, tm)], row_sem).wait()


def dispatch_rows(x, dest, fill, tm=512):
    n, d = x.shape
    n_blocks = fill.shape[0]
    grid_spec = pltpu.PrefetchScalarGridSpec(
        num_scalar_prefetch=2,
        grid=(n // tm,),
        in_specs=[pl.BlockSpec((tm, d), lambda i, dst, fl: (i, 0))],
        out_specs=pl.BlockSpec(memory_space=pl.ANY),
        scratch_shapes=[pltpu.VMEM((MOE_ROWS, d), x.dtype), pltpu.SemaphoreType.DMA(()),
                        pltpu.SemaphoreType.DMA(())],
    )
    return pl.pallas_call(
        functools.partial(_dispatch_kernel, tm=tm, n_blocks=n_blocks),
        grid_spec=grid_spec,
        out_shape=jax.ShapeDtypeStruct((n_blocks * MOE_ROWS, d), x.dtype),
        compiler_params=_params("arbitrary"),
        name="dispatch_rows",
    )(dest, fill, x)


def moe_ffn(h, hp, router_w, router_b, w_gate, w_up, w_down, layer, ln_g, ln_b):
    n, d = h.shape
    a = n * TOP_K
    e_idx, gate, rank, counts = route(h, router_w, router_b)
    counts = counts[:, 0]
    padded = (counts + MOE_ROWS - 1) // MOE_ROWS * MOE_ROWS
    pad_end = jnp.cumsum(padded)
    pad_start = pad_end - padded
    experts_iota = jnp.arange(N_EXPERTS, dtype=jnp.int32)[:, None, None]
    dest = jnp.sum(jnp.where(e_idx[None] == experts_iota, pad_start[:, None, None], 0), axis=0) + rank
    n_blocks = a // MOE_ROWS + N_EXPERTS
    blk_start = jnp.arange(n_blocks, dtype=jnp.int32)[:, None] * MOE_ROWS
    blk_exp = jnp.minimum(jnp.sum((pad_end[None, :] <= blk_start).astype(jnp.int32), axis=1), N_EXPERTS - 1)
    n_used = (pad_end[-1:] // MOE_ROWS).astype(jnp.int32)
    ends_expert = jnp.any((pad_end[None, :] == blk_start + MOE_ROWS) & (padded[None, :] > 0), axis=1)
    fill = (ends_expert | (blk_start[:, 0] >= pad_end[-1])).astype(jnp.int32)
    xs = dispatch_rows(hp, dest, fill)
    yb = experts(xs, blk_exp, n_used, w_gate, w_up, w_down, layer)
    y = yb.at[dest.reshape(a)].get(mode="promise_in_bounds")
    return combine_ln(h, y, gate.T, ln_g, ln_b)


FFT_R = 128
HALF_R = FFT_R // 2
HY_GROUP = 4


def _proj_t_kernel(w_ref, x_ref, o_ref, *, scale):
    acc = lax.dot_general(w_ref[...].astype(BF16), x_ref[...], (((0,), (1,)), ((), ())),
                          preferred_element_type=F32)
    o_ref[...] = (acc * scale).astype(o_ref.dtype)


def proj_t(w, col0, ncols, x, out_dtype, scale=1.0, tm=512, tn=1024):
    k = w.shape[0]
    n = x.shape[0]
    first = col0 // tm
    return pl.pallas_call(
        functools.partial(_proj_t_kernel, scale=scale),
        grid=(ncols // tm, n // tn),
        in_specs=[pl.BlockSpec((k, tm), lambda i, j: (0, first + i)),
                  pl.BlockSpec((tn, k), lambda i, j: (j, 0))],
        out_specs=pl.BlockSpec((tm, tn), lambda i, j: (i, j)),
        out_shape=jax.ShapeDtypeStruct((ncols, n), out_dtype),
        compiler_params=_params("arbitrary", "arbitrary"),
        name="proj_t",
    )(w, x)


def _dft_constants():
    idx = jnp.arange(FFT_R, dtype=jnp.int32)
    prod = idx[:, None] * idx[None, :]
    ang = (prod % FFT_R).astype(F32) * (2.0 * math.pi / FFT_R)
    f_r, f_i = jnp.cos(ang), -jnp.sin(ang)
    ang_t = prod.astype(F32) * (2.0 * math.pi / (FFT_R * FFT_R))
    g_r, g_i = f_r[:HALF_R], -f_i[:HALF_R]
    return dict(
        la=jnp.block([[f_r[:, :HALF_R], -f_i[:, :HALF_R]], [f_i[:, :HALF_R], f_r[:, :HALF_R]]]).astype(BF16),
        la_real=jnp.concatenate([f_r, f_i], axis=0).astype(BF16),
        rb=jnp.block([[f_r, f_i], [-f_i, f_r]]).astype(BF16),
        rc=jnp.block([[f_r, -f_i], [f_i, f_r]]).astype(BF16),
        ld=(jnp.block([[g_r, -g_i], [g_i, g_r]]) / (FFT_R * FFT_R)).astype(BF16),
        t_r=jnp.cos(ang_t), t_i=-jnp.sin(ang_t))


def _rows_to_lanes(x):
    return jnp.concatenate([x[:FFT_R], x[FFT_R:]], axis=1)


def _fwd_lane_stage(a, t_r, t_i, rb):
    a_r, a_i = a[:FFT_R], a[FFT_R:]
    t2_r = jnp.concatenate([t_r, t_r], axis=1)
    t2_i = jnp.concatenate([t_i, t_i], axis=1)
    b_r = a_r * t2_r - a_i * t2_i
    b_i = a_r * t2_i + a_i * t2_r
    lhs = jnp.concatenate([jnp.concatenate([b_r[:, :FFT_R], b_i[:, :FFT_R]], axis=1),
                           jnp.concatenate([b_r[:, FFT_R:], b_i[:, FFT_R:]], axis=1)], axis=0)
    return jnp.dot(lhs.astype(BF16), rb, preferred_element_type=F32)


def _conv_pairs(ms, kfs, la, rb, rc, ld, t_r, t_i):
    a = [jnp.dot(la, jnp.concatenate([m_r, m_i], axis=0).astype(BF16), preferred_element_type=F32)
         for m_r, m_i in ms]
    z = [_fwd_lane_stage(ai, t_r, t_i, rb) for ai in a]
    c = []
    for zi, kf in zip(z, kfs):
        z_r, z_i, k_r, k_i = zi[:, :FFT_R], zi[:, FFT_R:], kf[:, :FFT_R], kf[:, FFT_R:]
        y = jnp.concatenate([z_r * k_r - z_i * k_i, z_r * k_i + z_i * k_r], axis=1)
        c.append(jnp.dot(y.astype(BF16), rc, preferred_element_type=F32))
    t2_r = jnp.concatenate([t_r, t_r], axis=0)
    t2_i = jnp.concatenate([t_i, t_i], axis=0)
    out = []
    for ci in c:
        c_r, c_i = ci[:, :FFT_R], ci[:, FFT_R:]
        d_r = c_r * t2_r + c_i * t2_i
        d_i = c_i * t2_r - c_r * t2_i
        rhs = jnp.concatenate([_rows_to_lanes(d_r), _rows_to_lanes(d_i)], axis=0)
        x = jnp.dot(ld, rhs.astype(BF16), preferred_element_type=F32)
        out.append((x[:HALF_R], x[HALF_R:]))
    return out


def _short_conv_tile(x, w0, w1, w2, b):
    rows = x.shape[0]
    lane = lax.broadcasted_iota(jnp.int32, x.shape, 1)
    row = lax.broadcasted_iota(jnp.int32, x.shape, 0)
    r = pltpu.roll(x, 1, 1)
    prev = jnp.where(lane == 0, jnp.where(row == 0, 0.0, pltpu.roll(r, 1, 0)), r)
    r = pltpu.roll(x, FFT_R - 1, 1)
    nxt = jnp.where(lane == FFT_R - 1, jnp.where(row == rows - 1, 0.0, pltpu.roll(r, rows - 1, 0)), r)
    return w0 * prev + w1 * x + w2 * nxt + b


def _hyena_kernel(cw_ref, cb_ref, skip_ref, v_ref, x1_ref, x2_ref, kf_ref,
                  la_ref, rb_ref, rc_ref, ld_ref, tr_ref, ti_ref, o_ref, *, tc):
    c_base = pl.program_id(0) * tc
    lane = lax.broadcasted_iota(jnp.int32, (1, 2 * FFT_R), 1)

    def conv_in(ref, part, b, c0):
        tiles = []
        for cc in range(2):
            ch = part * HY_CH + c_base + c0 + cc
            tiles.append(_short_conv_tile(ref[c0 + cc, b], cw_ref[0, ch], cw_ref[1, ch], cw_ref[2, ch], cb_ref[ch]))
        return jnp.concatenate(tiles, axis=1)

    def group(g, carry):
        starts = [2 * (HY_GROUP * g + j) for j in range(HY_GROUP)]
        consts = (la_ref[...], rb_ref[...], rc_ref[...], ld_ref[...], tr_ref[...], ti_ref[...])
        zs = [(conv_in(v_ref, 0, 0, c0), conv_in(v_ref, 0, 1, c0)) for c0 in starts]
        for o, g_ref in enumerate((x1_ref, x2_ref)):
            kfs = [kf_ref[o, pl.ds(c0, 2)].reshape(2 * FFT_R, 2 * FFT_R) for c0 in starts]
            ys = _conv_pairs(zs, kfs, *consts)
            nxt = []
            for c0, (z_r, z_i), (y_r, y_i) in zip(starts, zs, ys):
                sk = jnp.where(lane < FFT_R, skip_ref[o, c_base + c0], skip_ref[o, c_base + c0 + 1])
                nxt.append((conv_in(g_ref, o + 1, 0, c0) * (y_r + sk * z_r),
                            conv_in(g_ref, o + 1, 1, c0) * (y_i + sk * z_i)))
            zs = nxt
        for c0, (z_r, z_i) in zip(starts, zs):
            o_ref[c0, 0] = z_r[:, :FFT_R].astype(o_ref.dtype)
            o_ref[c0 + 1, 0] = z_r[:, FFT_R:].astype(o_ref.dtype)
            o_ref[c0, 1] = z_i[:, :FFT_R].astype(o_ref.dtype)
            o_ref[c0 + 1, 1] = z_i[:, FFT_R:].astype(o_ref.dtype)
        return carry

    lax.fori_loop(0, tc // (2 * HY_GROUP), group, 0)


def hyena_mix(u_t, kf, conv_w, conv_b, skip, consts, tc=16):
    rows, n = u_t.shape
    ch = rows // 3
    u5 = u_t.reshape(rows, 2, HALF_R, FFT_R)
    nct = ch // tc
    smem = pl.BlockSpec(memory_space=pltpu.SMEM)
    part = lambda k: pl.BlockSpec((tc, 2, HALF_R, FFT_R), lambda i: (k * nct + i, 0, 0, 0))
    full2 = lambda a: pl.BlockSpec(a.shape, lambda i: (0, 0))
    mats = [consts[k] for k in ("la", "rb", "rc", "ld", "t_r", "t_i")]
    out = pl.pallas_call(
        functools.partial(_hyena_kernel, tc=tc),
        grid=(nct,),
        in_specs=[smem, smem, smem, part(0), part(1), part(2),
                  pl.BlockSpec((2, tc, FFT_R, 2 * FFT_R), lambda i: (0, i, 0, 0))] + [full2(a) for a in mats],
        out_specs=pl.BlockSpec((tc, 2, HALF_R, FFT_R), lambda i: (i, 0, 0, 0)),
        out_shape=jax.ShapeDtypeStruct((ch, 2, HALF_R, FFT_R), BF16),
        compiler_params=_params("arbitrary"),
        name="hyena_mix",
    )(conv_w.astype(F32), conv_b.astype(F32), skip.astype(F32), u5, u5, u5, kf, *mats)
    return out.reshape(ch, n)


def _hyena_filter_kernel(delta_ref, hf_ref, hb_ref, tpos_ref, la_ref, rb_ref, tr_ref, ti_ref, kf_ref, *, tc):
    c_base = pl.program_id(1) * tc

    def taps_of(c0):
        taps = []
        for cc in range(2):
            k = jnp.concatenate([hf_ref[0, 0, c0 + cc], hb_ref[0, 0, c0 + cc]], axis=0)
            k = k * jnp.exp(-tpos_ref[...] * delta_ref[c_base + c0 + cc])
            taps.append(k * lax.rsqrt(jnp.sum(k * k, keepdims=True)))
        return jnp.concatenate(taps, axis=1).astype(BF16)

    def group(g, carry):
        starts = [2 * (HY_GROUP * g + j) for j in range(HY_GROUP)]
        a = [jnp.dot(la_ref[...], taps_of(c0), preferred_element_type=F32) for c0 in starts]
        z = [_fwd_lane_stage(ai, tr_ref[...], ti_ref[...], rb_ref[...]) for ai in a]
        for c0, zi in zip(starts, z):
            kf_ref[0, pl.ds(c0, 2)] = zi.reshape(2, FFT_R, 2 * FFT_R)
        return carry

    lax.fori_loop(0, tc // (2 * HY_GROUP), group, 0)


def hyena_filter_spectra(seq, w1, b1, w2, b2, w3, b3, freq, w4, consts, tc=16):
    t = jnp.linspace(0.0, 1.0, seq, dtype=F32)[:, None]
    omega = 2.0 * math.pi * jnp.arange(seq, dtype=F32)[:, None] / seq
    bands = jnp.linspace(1e-4, HF_BANDS - 1, HF_BANDS, dtype=F32)[None, :]
    ang = omega * bands
    z = jnp.concatenate([t, jnp.cos(ang), -jnp.sin(ang)], -1)
    fr = freq.astype(F32)
    hid = jnp.sin(fr * (z @ w1.astype(F32) + b1.astype(F32)))
    hid = jnp.sin(fr * (hid @ w2.astype(F32) + b2.astype(F32)))
    hid = jnp.sin(fr * (hid @ w3.astype(F32) + b3.astype(F32)))
    back = lambda a: jnp.concatenate([jnp.zeros_like(a[:1]), a[:0:-1]], axis=0)
    w4t = w4.astype(BF16).T
    half = HY_ORDER * HY_CH
    h_fwd = matmul(w4t[:half], hid.T.astype(BF16), F32)
    h_bwd = matmul(w4t[half:], back(hid).T.astype(BF16), F32)
    shape5 = (HY_ORDER, HY_CH, HALF_R, FFT_R)
    tpos = jnp.concatenate([t, back(t)], axis=0).reshape(FFT_R, FFT_R)
    max_decay = math.log(HF_TARGET) / HF_FAST
    min_decay = math.log(HF_TARGET) / HF_SLOW
    deltas = jnp.abs(jnp.linspace(min_decay, max_decay, HY_CH, dtype=F32))
    taps = pl.BlockSpec((1, 1, tc, HALF_R, FFT_R), lambda o, i: (0, o, i, 0, 0))
    full2 = lambda a: pl.BlockSpec(a.shape, lambda o, i: (0, 0))
    mats = [consts[k] for k in ("la_real", "rb", "t_r", "t_i")]
    return pl.pallas_call(
        functools.partial(_hyena_filter_kernel, tc=tc),
        grid=(HY_ORDER, HY_CH // tc),
        in_specs=[pl.BlockSpec(memory_space=pltpu.SMEM), taps, taps, full2(tpos)] + [full2(a) for a in mats],
        out_specs=pl.BlockSpec((1, tc, FFT_R, 2 * FFT_R), lambda o, i: (o, i, 0, 0)),
        out_shape=jax.ShapeDtypeStruct((HY_ORDER, HY_CH, FFT_R, 2 * FFT_R), F32),
        compiler_params=_params("arbitrary", "arbitrary"),
        name="hyena_filter_spectra",
    )(deltas, h_fwd.reshape((1,) + shape5), h_bwd.reshape((1,) + shape5), tpos, *mats)


def kernel(x, ev_w_in, ev_hy_conv_w, ev_hy_conv_b, ev_hf_w1, ev_hf_b1, ev_hf_w2, ev_hf_b2, ev_hf_w3, ev_hf_b3,
           ev_hf_freq, ev_hf_w4, ev_hy_skip, ev_lam_q1, ev_lam_k1, ev_lam_q2, ev_lam_k2, ev_subln_g, ev_w_out,
           od_w_in, od_conv_w, od_conv_b, od_wa, od_ba, od_wx, od_bx, od_lam, od_w_out, ln1_g, ln1_b, ln2_g,
           ln2_b, router_w, router_b, ex_w_gate, ex_w_up, ex_w_down):
    bsz, seq, d = x.shape
    n = bsz * seq
    h = x.reshape(n, d).astype(F32)
    hb = h.astype(BF16)
    for layer in range(DEPTH):
        i = layer // 2
        if layer % 2 == 0:
            w_in = ev_w_in[i]
            c0 = 3 * HY_CH
            u_t = proj_t(w_in, 0, c0, hb, F32)
            qt = proj_t(w_in, c0, DA_QK, hb, BF16, scale=DA_HEAD_DIM ** -0.5)
            k = matmul(hb, w_in, BF16, col0=c0 + DA_QK, ncols=DA_QK)
            vt = proj_t(w_in, c0 + 2 * DA_QK, DA_WIDTH, hb, BF16)
            consts = _dft_constants()
            kf = hyena_filter_spectra(seq, ev_hf_w1[i], ev_hf_b1[i], ev_hf_w2[i], ev_hf_b2[i],
                                      ev_hf_w3[i], ev_hf_b3[i], ev_hf_freq[i], ev_hf_w4[i], consts)
            y_hy = hyena_mix(u_t, kf, ev_hy_conv_w[i], ev_hy_conv_b[i], ev_hy_skip[i], consts)
            lambda_init = 0.8 - 0.6 * math.exp(-0.3 * layer)
            lam = (jnp.exp(jnp.sum(ev_lam_q1[i].astype(F32) * ev_lam_k1[i].astype(F32)))
                   - jnp.exp(jnp.sum(ev_lam_q2[i].astype(F32) * ev_lam_k2[i].astype(F32))) + lambda_init)
            y_da = diff_attention(qt, k.reshape(bsz, seq, DA_QK), vt,
                                  lam, ev_subln_g[i], lambda_init).reshape(n, DA_WIDTH)
            h, hp = even_out_proj(y_hy, y_da, ev_w_out[i].astype(BF16), h, ln1_g[layer], ln1_b[layer])
        else:
            u = matmul(hb, od_w_in[i], BF16)
            yd = rglru_bidir(u.reshape(bsz, seq, 2 * RG_WIDTH), od_conv_w[i], od_conv_b[i], od_wa[i], od_ba[i],
                             od_wx[i], od_bx[i], od_lam[i]).reshape(2, n, RG_WIDTH)
            h, hp = odd_out_proj(u, yd, od_w_out[i].astype(BF16), h, ln1_g[layer], ln1_b[layer])
        h, hb = moe_ffn(h, hp, router_w, router_b, ex_w_gate, ex_w_up, ex_w_down, layer,
                        ln2_g[layer], ln2_b[layer])
    return h.reshape(bsz, seq, d).astype(x.dtype)
```

```python
import functools
import math

import jax
import jax.numpy as jnp
from jax import lax
from jax.experimental import pallas as pl
from jax.experimental.pallas import tpu as pltpu

F32 = jnp.float32
BF16 = jnp.bfloat16

D_MODEL = 2048
DEPTH = 2
HY_CH = D_MODEL // 2
HY_ORDER = 2
HF_BANDS = 16
HF_TARGET = 1e-2
HF_FAST = 0.3
HF_SLOW = 1.5
DA_HEADS = 8
DA_HEAD_DIM = 64
DA_V_DIM = 2 * DA_HEAD_DIM
DA_QK = DA_HEADS * 2 * DA_HEAD_DIM
DA_WIDTH = DA_HEADS * DA_V_DIM
RG_WIDTH = D_MODEL
RG_BLOCKS = 8
RG_BLOCK_W = RG_WIDTH // RG_BLOCKS
RG_C = 8.0
RG_CONV = 4
N_EXPERTS = 32
N_GROUPS = 4
EXPERTS_PER_GROUP = N_EXPERTS // N_GROUPS
TOP_K = 2
D_FF = 512
DN_ALPHA = (2 * DEPTH) ** 0.25
LN_EPS = 1e-5

VMEM_LIMIT_BYTES = 56 * 1024 * 1024
MOE_ROWS = 256


def _params(*sem):
    return pltpu.CompilerParams(dimension_semantics=sem, vmem_limit_bytes=VMEM_LIMIT_BYTES)


def _mm_kernel(x_ref, w_ref, o_ref):
    o_ref[...] = jnp.dot(x_ref[...], w_ref[...].astype(BF16), preferred_element_type=F32).astype(o_ref.dtype)


def matmul(x, w, out_dtype, col0=0, ncols=None, tm=512, tn=1024):
    m, k = x.shape
    n = w.shape[1] if ncols is None else ncols
    tm, tn = min(tm, m), min(tn, n)
    first = col0 // tn
    return pl.pallas_call(
        _mm_kernel,
        grid=(n // tn, m // tm),
        in_specs=[pl.BlockSpec((tm, k), lambda j, i: (i, 0)),
                  pl.BlockSpec((k, tn), lambda j, i: (0, first + j))],
        out_specs=pl.BlockSpec((tm, tn), lambda j, i: (i, j)),
        out_shape=jax.ShapeDtypeStruct((m, n), out_dtype),
        compiler_params=_params("arbitrary", "arbitrary"),
        name="matmul",
    )(x, w)


def _ln_rows(z, g, b):
    mu = jnp.mean(z, axis=-1, keepdims=True)
    zc = z - mu
    var = jnp.mean(zc * zc, axis=-1, keepdims=True)
    return zc * lax.rsqrt(var + LN_EPS) * g + b


def _pack_halves(y):
    half = y.shape[1] // 2
    hi = pltpu.bitcast(y[:, :half].astype(BF16).astype(F32), jnp.uint32)
    lo = pltpu.bitcast(y[:, half:].astype(BF16).astype(F32), jnp.uint32)
    return hi | (lo >> 16)


def _unpack_halves(p):
    hi = pltpu.bitcast(p & jnp.uint32(0xFFFF0000), F32)
    lo = pltpu.bitcast(p << 16, F32)
    return jnp.concatenate([hi, lo], axis=1).astype(BF16)


def _even_out_kernel(xa_ref, xb_ref, wa_ref, wb_ref, r_ref, g_ref, b_ref, o_ref, op_ref):
    acc = lax.dot_general(xa_ref[...], wa_ref[...], (((0,), (0,)), ((), ())), preferred_element_type=F32)
    acc += jnp.dot(xb_ref[...], wb_ref[...], preferred_element_type=F32)
    y = _ln_rows(DN_ALPHA * r_ref[...] + acc, g_ref[...], b_ref[...])
    o_ref[...] = y
    op_ref[...] = _pack_halves(y)


def even_out_proj(y_hy_t, y_da, w_out, resid, g, b, tm=512):
    ka, m = y_hy_t.shape
    kb = y_da.shape[1]
    d = w_out.shape[1]
    row = lambda i: (i, 0)
    fixed = lambda i: (0, 0)
    return pl.pallas_call(
        _even_out_kernel,
        grid=(m // tm,),
        in_specs=[pl.BlockSpec((ka, tm), lambda i: (0, i)), pl.BlockSpec((tm, kb), row),
                  pl.BlockSpec((ka, d), fixed), pl.BlockSpec((kb, d), fixed),
                  pl.BlockSpec((tm, d), row), pl.BlockSpec((1, d), fixed), pl.BlockSpec((1, d), fixed)],
        out_specs=[pl.BlockSpec((tm, d), row), pl.BlockSpec((tm, d // 2), row)],
        out_shape=[jax.ShapeDtypeStruct((m, d), F32), jax.ShapeDtypeStruct((m, d // 2), jnp.uint32)],
        compiler_params=_params("arbitrary"),
        name="even_out_proj",
    )(y_hy_t, y_da, w_out[:ka], w_out[ka:], resid, g.reshape(1, d), b.reshape(1, d))


def _odd_out_kernel(gate_ref, y0_ref, y1_ref, w_ref, r_ref, g_ref, b_ref, o_ref, op_ref):
    x = jax.nn.gelu(gate_ref[...].astype(F32)) * (y0_ref[0].astype(F32) + y1_ref[0].astype(F32))
    acc = jnp.dot(x.astype(BF16), w_ref[...], preferred_element_type=F32)
    y = _ln_rows(DN_ALPHA * r_ref[...] + acc, g_ref[...], b_ref[...])
    o_ref[...] = y
    op_ref[...] = _pack_halves(y)


def odd_out_proj(u, yd, w_out, resid, g, b, tm=512):
    m = u.shape[0]
    k, d = w_out.shape
    row = lambda i: (i, 0)
    fixed = lambda i: (0, 0)
    return pl.pallas_call(
        _odd_out_kernel,
        grid=(m // tm,),
        in_specs=[pl.BlockSpec((tm, k), row),
                  pl.BlockSpec((1, tm, k), lambda i: (0, i, 0)),
                  pl.BlockSpec((1, tm, k), lambda i: (1, i, 0)),
                  pl.BlockSpec((k, d), fixed),
                  pl.BlockSpec((tm, d), row), pl.BlockSpec((1, d), fixed), pl.BlockSpec((1, d), fixed)],
        out_specs=[pl.BlockSpec((tm, d), row), pl.BlockSpec((tm, d // 2), row)],
        out_shape=[jax.ShapeDtypeStruct((m, d), F32), jax.ShapeDtypeStruct((m, d // 2), jnp.uint32)],
        compiler_params=_params("arbitrary"),
        name="odd_out_proj",
    )(u, yd, yd, w_out, resid, g.reshape(1, d), b.reshape(1, d))


def _combine_ln_kernel(h_ref, y0_ref, y1_ref, gate_ref, g_ref, b_ref, o_ref, *maybe_ob_ref):
    gate = gate_ref[...]
    ffn = gate[:, 0:1] * y0_ref[...].astype(F32) + gate[:, 1:2] * y1_ref[...].astype(F32)
    y = _ln_rows(DN_ALPHA * h_ref[...] + ffn, g_ref[...], b_ref[...])
    o_ref[...] = y
    for ob_ref in maybe_ob_ref:
        ob_ref[...] = y.astype(BF16)


def combine_ln(h, y, gate, g, b, with_bf16, tm=512):
    m, d = h.shape
    row = lambda i: (i, 0)
    fixed = lambda i: (0, 0)
    n_out = 2 if with_bf16 else 1
    out = pl.pallas_call(
        _combine_ln_kernel,
        grid=(m // tm,),
        in_specs=[pl.BlockSpec((tm, d), row), pl.BlockSpec((tm, d), row),
                  pl.BlockSpec((tm, d), lambda i: (m // tm + i, 0)),
                  pl.BlockSpec((tm, TOP_K), row), pl.BlockSpec((1, d), fixed), pl.BlockSpec((1, d), fixed)],
        out_specs=[pl.BlockSpec((tm, d), row)] * n_out,
        out_shape=[jax.ShapeDtypeStruct((m, d), F32), jax.ShapeDtypeStruct((m, d), BF16)][:n_out],
        compiler_params=_params("arbitrary"),
        name="combine_ln",
    )(h, y, y, gate, g.reshape(1, d), b.reshape(1, d))
    return (out[0], out[1]) if with_bf16 else (out[0], None)


ONES_ROWS = 16
POS_SPLIT = 16


def _attn_kernel(slopes_ref, lam_ref, qt_ref, k_ref, vt_ref, fq_ref, fk_ref, bd_ref, g_ref, o_ref, m_s, acc_s, sa_s, sb_s,
                 *, tq, tk, seq, out_scale):
    head = pl.program_id(1)
    i0 = pl.program_id(2) * tq
    slope = slopes_ref[head]
    lam = lam_ref[0]
    qt = qt_ref[...]
    row = lax.broadcasted_iota(jnp.int32, qt.shape, 0)
    zero = jnp.zeros_like(qt)
    q2 = jnp.concatenate([jnp.where(row < DA_HEAD_DIM, qt, zero),
                          jnp.where(row >= DA_HEAD_DIM, qt, zero)], axis=1)
    fq = fq_ref[0]
    fq2 = jnp.concatenate([fq, fq], axis=1)
    q_aug = jnp.concatenate([q2, fq2], axis=0)
    fk = fk_ref[0]
    fk_neg = -fk
    ones = jnp.ones((ONES_ROWS, tk), BF16)
    m_s[...] = jnp.full(m_s.shape, -jnp.inf, F32)
    acc_s[...] = jnp.zeros(acc_s.shape, F32)
    j_diag = i0 // tk

    def update(j, s, shift):
        off = pl.multiple_of(j * tk, tk)
        va = jnp.concatenate([vt_ref[:, pl.ds(off, tk)], ones], axis=0)
        m_prev = m_s[...]
        m_new = jnp.maximum(m_prev, jnp.max(s, axis=0, keepdims=True) - shift)
        p = jnp.exp(s - (m_new + shift))
        alpha = jnp.exp(m_prev - m_new)
        acc_s[...] = alpha * acc_s[...] + jnp.dot(va, p.astype(BF16), preferred_element_type=F32)
        m_s[...] = m_new

    def chunk_of(t):
        return t + (t >= j_diag).astype(jnp.int32)

    def scores(t):
        j = chunk_of(t)
        off = pl.multiple_of(j * tk, tk)
        ka = jnp.concatenate([k_ref[0, pl.ds(off, tk), :], jnp.where(j < j_diag, fk, fk_neg)], axis=1)
        return jnp.dot(ka, q_aug, preferred_element_type=F32)

    def consume(s_ref, t):
        j = chunk_of(t)
        update(j, s_ref[...], slope * jnp.abs(i0 - j * tk).astype(F32))

    j0 = pl.multiple_of(j_diag * tk, tk)
    bias = bd_ref[0]
    s_diag = jnp.dot(k_ref[0, pl.ds(j0, tk), :], q2, preferred_element_type=F32)
    sa_s[...] = s_diag - jnp.concatenate([bias, bias], axis=1)
    sb_s[...] = scores(jnp.int32(0))
    update(j_diag, sa_s[...], 0.0)
    n_off = seq // tk - 1

    def pair(t, carry):
        sa_s[...] = scores(2 * t + 1)
        consume(sb_s, 2 * t)
        sb_s[...] = scores(2 * t + 2)
        consume(sa_s, 2 * t + 1)
        return carry

    lax.fori_loop(0, n_off // 2, pair, 0)
    consume(sb_s, jnp.int32(n_off - 1))

    acc = acc_s[...]
    o1 = acc[:DA_V_DIM, :tq] / acc[DA_V_DIM:DA_V_DIM + 1, :tq]
    o2 = acc[:DA_V_DIM, tq:] / acc[DA_V_DIM:DA_V_DIM + 1, tq:]
    o = o1 - lam * o2
    o = o * lax.rsqrt(jnp.mean(o * o, axis=0, keepdims=True) + LN_EPS) * g_ref[...]
    o_ref[0] = (o * out_scale).T.astype(o_ref.dtype)


def diff_attention(qt, k, vt, lam, subln_g, lambda_init, tq=512, tk=512):
    bsz, seq, width = k.shape
    nq = seq // tq
    slopes = 2.0 ** (-(8.0 / DA_HEADS) * jnp.arange(1, DA_HEADS + 1, dtype=F32))

    def split(n):
        pos = jnp.arange(n, dtype=jnp.int32)
        return (pos // POS_SPLIT * POS_SPLIT).astype(F32), (pos % POS_SPLIT).astype(F32)

    qhi, qlo = split(tq)
    khi, klo = split(tk)
    sl = slopes[:, None]
    fq = jnp.zeros((DA_HEADS, DA_V_DIM, tq), F32)
    fq = fq.at[:, 0].set(1.0).at[:, 1].set(1.0).at[:, 2].set(-sl * qhi).at[:, 3].set(-sl * qlo)
    fk = jnp.zeros((DA_HEADS, tk, DA_V_DIM), F32)
    fk = fk.at[:, :, 0].set(sl * khi).at[:, :, 1].set(sl * klo).at[:, :, 2].set(1.0).at[:, :, 3].set(1.0)
    assert tq == tk, "the chunk holding a query tile's diagonal must be the tile itself"
    pos = jnp.arange(tk, dtype=jnp.int32)
    bias_diag = slopes[:, None, None] * jnp.abs(pos[:, None] - pos[None, :]).astype(F32)
    kern = functools.partial(_attn_kernel, tq=tq, tk=tk, seq=seq, out_scale=1.0 - lambda_init)
    smem = pl.BlockSpec(memory_space=pltpu.SMEM)
    return pl.pallas_call(
        kern,
        grid=(bsz, DA_HEADS, seq // tq),
        in_specs=[smem, smem,
                  pl.BlockSpec((DA_V_DIM, tq), lambda b, h, i: (h, b * nq + i)),
                  pl.BlockSpec((1, seq, DA_V_DIM), lambda b, h, i: (b, 0, h)),
                  pl.BlockSpec((DA_V_DIM, seq), lambda b, h, i: (h, b)),
                  pl.BlockSpec((1, DA_V_DIM, tq), lambda b, h, i: (h, 0, 0)),
                  pl.BlockSpec((1, tk, DA_V_DIM), lambda b, h, i: (h, 0, 0)),
                  pl.BlockSpec((1, tk, tq), lambda b, h, i: (h, 0, 0)),
                  pl.BlockSpec((DA_V_DIM, 1), lambda b, h, i: (0, 0))],
        out_specs=pl.BlockSpec((1, tq, DA_V_DIM), lambda b, h, i: (b, i, h)),
        out_shape=jax.ShapeDtypeStruct((bsz, seq, width), BF16),
        scratch_shapes=[pltpu.VMEM((1, 2 * tq), F32), pltpu.VMEM((DA_V_DIM + ONES_ROWS, 2 * tq), F32),
                        pltpu.VMEM((tk, 2 * tq), F32), pltpu.VMEM((tk, 2 * tq), F32)],
        compiler_params=_params("arbitrary", "arbitrary", "arbitrary"),
        name="diff_attention",
    )(slopes, lam.reshape(1).astype(F32), qt, k, vt, fq.astype(BF16), fk.astype(BF16), bias_diag,
      subln_g.reshape(DA_V_DIM, 1).astype(F32))


HALO = 16


def _rglru_kernel(xr_ref, prev_ref, next_ref, cw_ref, cb_ref, wa_ref, wx_ref, ba_ref, bx_ref, kk_ref, y_ref,
                  a_s, b_s, y_s, h_s, *, ts, nt):
    direction = pl.program_id(0)
    t = pl.program_id(2)
    chunk = t + direction * (nt - 1 - 2 * t)

    @pl.when(t == 0)
    def _():
        h_s[...] = jnp.zeros(h_s.shape, F32)

    xf = jnp.concatenate([jnp.where(chunk == 0, 0.0, prev_ref[0].astype(F32)), xr_ref[0].astype(F32),
                          jnp.where(chunk == nt - 1, 0.0, next_ref[0].astype(F32))], axis=0)
    rows = ts + 2 * HALO

    def tap(j):
        shifted = xf if j == 2 else pltpu.roll(xf, (2 - j) % rows, 0)
        return cw_ref[j:j + 1, :] * shifted[HALO:HALO + ts]

    x = cb_ref[...] + sum(tap(j) for j in range(RG_CONV))
    xb = x.astype(BF16)
    xh = 0.5 * x
    for n in range(RG_BLOCKS):
        cols = slice(n * RG_BLOCK_W, (n + 1) * RG_BLOCK_W)
        xs = xb[:, cols]
        tr = jnp.tanh(jnp.dot(xs, wa_ref[0, n], preferred_element_type=F32) + ba_ref[0, :, cols])
        ti = jnp.tanh(jnp.dot(xs, wx_ref[0, n], preferred_element_type=F32) + bx_ref[0, :, cols])
        kk = kk_ref[0, :, cols]
        a = jnp.exp2(kk * tr + kk)
        a_s[:, cols] = a
        s = 1.0 - a * a
        root = jnp.where(s > 0.0, s * lax.rsqrt(s), 0.0)
        b_s[:, cols] = root * (ti * xh[:, cols] + xh[:, cols])

    def step(k, h):
        row = jnp.where(direction == 0, k, ts - 1 - k)
        h = a_s[pl.ds(row, 1), :] * h + b_s[pl.ds(row, 1), :]
        y_s[pl.ds(row, 1), :] = h
        return h

    h_s[...] = lax.fori_loop(0, ts, step, h_s[...], unroll=8)
    y_ref[0, 0] = y_s[...].astype(y_ref.dtype)


def rglru_bidir(u, conv_w, conv_b, wa, ba, wx, bx, lam, ts=512):
    bsz, seq, w2 = u.shape
    w = w2 // 2
    nt = seq // ts
    per = ts // HALO
    kk = (-0.5 * RG_C * math.log2(math.e)) * jax.nn.softplus(-lam.astype(F32)).reshape(2, 1, w)
    chunk = lambda d, t: t + d * (nt - 1 - 2 * t)
    dmap4 = lambda d, b, t: (d, 0, 0, 0)
    dmap3 = lambda d, b, t: (d, 0, 0)
    fixed = lambda d, b, t: (0, 0)
    return pl.pallas_call(
        functools.partial(_rglru_kernel, ts=ts, nt=nt),
        grid=(2, bsz, nt),
        in_specs=[pl.BlockSpec((1, ts, w), lambda d, b, t: (b, chunk(d, t), 1)),
                  pl.BlockSpec((1, HALO, w), lambda d, b, t: (b, jnp.maximum(chunk(d, t) * per - 1, 0), 1)),
                  pl.BlockSpec((1, HALO, w),
                               lambda d, b, t: (b, jnp.minimum((chunk(d, t) + 1) * per, seq // HALO - 1), 1)),
                  pl.BlockSpec((RG_CONV, w), fixed), pl.BlockSpec((1, w), fixed),
                  pl.BlockSpec((1, RG_BLOCKS, RG_BLOCK_W, RG_BLOCK_W), dmap4),
                  pl.BlockSpec((1, RG_BLOCKS, RG_BLOCK_W, RG_BLOCK_W), dmap4),
                  pl.BlockSpec((1, 1, w), dmap3), pl.BlockSpec((1, 1, w), dmap3), pl.BlockSpec((1, 1, w), dmap3)],
        out_specs=pl.BlockSpec((1, 1, ts, w), lambda d, b, t: (d, b, chunk(d, t), 0)),
        out_shape=jax.ShapeDtypeStruct((2, bsz, seq, w), BF16),
        scratch_shapes=[pltpu.VMEM((ts, w), F32), pltpu.VMEM((ts, w), F32), pltpu.VMEM((ts, w), F32),
                        pltpu.VMEM((1, w), F32)],
        compiler_params=_params("arbitrary", "arbitrary", "arbitrary"),
        name="rglru_bidir",
    )(u, u, u, conv_w.astype(F32), conv_b.astype(F32).reshape(1, w), (0.5 * wa).astype(BF16),
      (0.5 * wx).astype(BF16), 0.5 * ba.astype(F32).reshape(2, 1, w), 0.5 * bx.astype(F32).reshape(2, 1, w), kk)


def _router_kernel(h_ref, wt_ref, b_ref, tri_ref, e_ref, g_ref, r_ref, cnt_ref, seen_s):
    h = h_ref[...]
    h_hi = h.astype(BF16)
    h_lo = (h - h_hi.astype(F32)).astype(BF16)
    nt_dot = lambda a, b: lax.dot_general(a, b, (((1,), (1,)), ((), ())), preferred_element_type=F32)
    logits = nt_dot(wt_ref[0], h_hi) + (nt_dot(wt_ref[0], h_lo) + nt_dot(wt_ref[1], h_hi))
    tm = logits.shape[1]
    s = jax.nn.sigmoid(logits).reshape(N_GROUPS, EXPERTS_PER_GROUP, tm)
    sel = s + b_ref[...]
    idx = lax.broadcasted_iota(jnp.int32, sel.shape, 1)
    big = jnp.int32(EXPERTS_PER_GROUP)
    v1 = jnp.max(sel, axis=1, keepdims=True)
    i1 = jnp.min(jnp.where(sel == v1, idx, big), axis=1, keepdims=True)
    rest = jnp.where(idx == i1, -jnp.inf, sel)
    v2 = jnp.max(rest, axis=1, keepdims=True)
    i2 = jnp.min(jnp.where(rest == v2, idx, big), axis=1, keepdims=True)
    score = v1 + v2
    gidx = lax.broadcasted_iota(jnp.int32, score.shape, 0)
    best = jnp.max(score, axis=0, keepdims=True)
    grp = jnp.min(jnp.where(score == best, gidx, jnp.int32(N_GROUPS)), axis=0, keepdims=True)
    pick = gidx == grp
    l1 = jnp.sum(jnp.where(pick, i1, 0), axis=0)
    l2 = jnp.sum(jnp.where(pick, i2, 0), axis=0)
    s_g = jnp.sum(jnp.where(pick, s, 0.0), axis=0)
    eidx = lax.broadcasted_iota(jnp.int32, s_g.shape, 0)
    w1 = jnp.sum(jnp.where(eidx == l1, s_g, 0.0), axis=0, keepdims=True)
    w2 = jnp.sum(jnp.where(eidx == l2, s_g, 0.0), axis=0, keepdims=True)
    tot = w1 + w2
    base = grp[0] * EXPERTS_PER_GROUP
    e1, e2 = base + l1, base + l2
    e_ref[...] = jnp.concatenate([e1, e2], axis=0)
    g_ref[...] = jnp.concatenate([w1 / tot, w2 / tot], axis=0)

    @pl.when(pl.program_id(0) == 0)
    def _():
        seen_s[...] = jnp.zeros(seen_s.shape, F32)

    eall = lax.broadcasted_iota(jnp.int32, (N_EXPERTS, tm), 0)
    hit1, hit2 = eall == e1, eall == e2
    both = jnp.where(hit1, 1.0, jnp.where(hit2, 1.0, 0.0))
    incl = jnp.dot(both.astype(BF16), tri_ref[...], preferred_element_type=F32)
    before = incl - both + seen_s[...]
    r_ref[...] = jnp.concatenate([jnp.sum(jnp.where(hit1, before, 0.0), axis=0, keepdims=True),
                                  jnp.sum(jnp.where(hit2, before, 0.0), axis=0, keepdims=True)],
                                 axis=0).astype(jnp.int32)
    seen_s[...] = seen_s[...] + incl[:, tm - 1:tm]
    cnt_ref[...] = seen_s[...].astype(jnp.int32)


def route(h, router_w, router_b, tm=1024):
    n, d = h.shape
    tri = (jnp.arange(tm)[:, None] <= jnp.arange(tm)[None, :]).astype(BF16)
    wt = router_w.astype(F32).T
    wt_hi = wt.astype(BF16)
    wt_lo = (wt - wt_hi.astype(F32)).astype(BF16)
    tile = pl.BlockSpec((TOP_K, tm), lambda i: (0, i))
    return pl.pallas_call(
        _router_kernel,
        grid=(n // tm,),
        in_specs=[pl.BlockSpec((tm, d), lambda i: (i, 0)),
                  pl.BlockSpec((2, N_EXPERTS, d), lambda i: (0, 0, 0)),
                  pl.BlockSpec((N_GROUPS, EXPERTS_PER_GROUP, 1), lambda i: (0, 0, 0)),
                  pl.BlockSpec((tm, tm), lambda i: (0, 0))],
        out_specs=[tile, tile, tile, pl.BlockSpec((N_EXPERTS, 1), lambda i: (0, 0))],
        out_shape=[jax.ShapeDtypeStruct((TOP_K, n), jnp.int32), jax.ShapeDtypeStruct((TOP_K, n), F32),
                   jax.ShapeDtypeStruct((TOP_K, n), jnp.int32), jax.ShapeDtypeStruct((N_EXPERTS, 1), jnp.int32)],
        scratch_shapes=[pltpu.VMEM((N_EXPERTS, 1), F32)],
        compiler_params=_params("arbitrary"),
        name="router",
    )(h, jnp.stack([wt_hi, wt_lo]), router_b.astype(F32).reshape(N_GROUPS, EXPERTS_PER_GROUP, 1), tri)


def _experts_kernel(blk_exp_ref, nxt_exp_ref, slot_ref, n_used_ref, x_ref, wg_hbm, wu_hbm, wd_hbm, o_ref,
                    wg_f, wu_f, wd_f, wg_s, wu_s, wd_s, sem, *, layer):
    i = pl.program_id(0)
    expert = blk_exp_ref[i]
    prev = blk_exp_ref[jnp.maximum(i - 1, 0)]
    slot = slot_ref[i]

    def fetch(e, s):
        return [pltpu.make_async_copy(src.at[layer, e], dst.at[s], sem.at[s])
                for src, dst in ((wg_hbm, wg_f), (wu_hbm, wu_f), (wd_hbm, wd_f))]

    @pl.when(i == 0)
    def _():
        for copy in fetch(expert, slot):
            copy.start()

    @pl.when((i == 0) | (expert != prev))
    def _():
        for copy in fetch(expert, slot):
            copy.wait()
        wg_s[...] = wg_f[slot].astype(BF16)
        wu_s[...] = wu_f[slot].astype(BF16)
        wd_s[...] = wd_f[slot].astype(BF16)

        @pl.when(nxt_exp_ref[i] >= 0)
        def _():
            for copy in fetch(nxt_exp_ref[i], 1 - slot):
                copy.start()

    @pl.when(i < n_used_ref[0])
    def _():
        x = _unpack_halves(x_ref[...])
        hg = jnp.dot(x, wg_s[...], preferred_element_type=F32)
        hu = jnp.dot(x, wu_s[...], preferred_element_type=F32)
        hidden = (jax.nn.silu(hg) * hu).astype(BF16)
        o_ref[...] = jnp.dot(hidden, wd_s[...], preferred_element_type=F32).astype(o_ref.dtype)

    @pl.when(i >= n_used_ref[0])
    def _():
        o_ref[...] = jnp.zeros(o_ref.shape, o_ref.dtype)


def experts(xs, blk_exp, n_used, w_gate, w_up, w_down, layer):
    p = xs.shape[0]
    n_blocks = p // MOE_ROWS
    d = w_gate.shape[2]
    f = w_gate.shape[3]
    idx = jnp.arange(n_blocks, dtype=jnp.int32)
    starts = jnp.concatenate([jnp.ones((1,), bool), blk_exp[1:] != blk_exp[:-1]])
    slot = (jnp.cumsum(starts.astype(jnp.int32)) - 1) % 2
    later = starts[None, :] & (idx[None, :] > idx[:, None])
    first_later = jnp.min(jnp.where(later, idx[None, :], n_blocks), axis=1)
    nxt_exp = jnp.sum(jnp.where(idx[None, :] == first_later[:, None], blk_exp[None, :], 0), axis=1)
    nxt_exp = jnp.where(first_later < n_blocks, nxt_exp, -1).astype(jnp.int32)
    hbm = pl.BlockSpec(memory_space=pl.ANY)
    grid_spec = pltpu.PrefetchScalarGridSpec(
        num_scalar_prefetch=4,
        grid=(n_blocks,),
        in_specs=[pl.BlockSpec((MOE_ROWS, d // 2), lambda i, *_: (i, 0)), hbm, hbm, hbm],
        out_specs=pl.BlockSpec((MOE_ROWS, d), lambda i, *_: (i, 0)),
        scratch_shapes=[pltpu.VMEM((2, d, f), F32), pltpu.VMEM((2, d, f), F32), pltpu.VMEM((2, f, d), F32),
                        pltpu.VMEM((d, f), BF16), pltpu.VMEM((d, f), BF16), pltpu.VMEM((f, d), BF16),
                        pltpu.SemaphoreType.DMA((2,))],
    )
    return pl.pallas_call(
        functools.partial(_experts_kernel, layer=layer),
        grid_spec=grid_spec,
        out_shape=jax.ShapeDtypeStruct((p, d), BF16),
        compiler_params=_params("arbitrary"),
        name="experts",
    )(blk_exp, nxt_exp, slot.astype(jnp.int32), n_used, xs, w_gate, w_up, w_down)


def _dispatch_kernel(dest_ref, fill_ref, x_ref, xs_ref, zero_s, fill_sem, row_sem, *, tm, n_blocks):
    base = pl.program_id(0) * tm

    @pl.when(pl.program_id(0) == 0)
    def _():
        zero_s[...] = jnp.zeros(zero_s.shape, zero_s.dtype)

        def zero_block(blk):
            return pltpu.make_async_copy(zero_s, xs_ref.at[pl.ds(pl.multiple_of(blk * MOE_ROWS, MOE_ROWS),
                                                                 MOE_ROWS)], fill_sem)

        def start(blk, carry):
            @pl.when(fill_ref[blk] != 0)
            def _():
                zero_block(blk).start()
            return carry

        def wait(blk, carry):
            @pl.when(fill_ref[blk] != 0)
            def _():
                zero_block(blk).wait()
            return carry

        lax.fori_loop(0, n_blocks, start, 0)
        lax.fori_loop(0, n_blocks, wait, 0)

    def issue(r, carry):
        for k in range(TOP_K):
            pltpu.make_async_copy(x_ref.at[pl.ds(r, 1)], xs_ref.at[pl.ds(dest_ref[k, base + r], 1)],
                                  row_sem).start()
        return carry

    lax.fori_loop(0, tm, issue, 0, unroll=4)
    for k in range(TOP_K):
        pltpu.make_async_copy(x_ref, xs_ref.at[pl.ds(0, tm)], row_sem).wait()


def dispatch_rows(x, dest, fill, tm=512):
    n, d = x.shape
    n_blocks = fill.shape[0]
    grid_spec = pltpu.PrefetchScalarGridSpec(
        num_scalar_prefetch=2,
        grid=(n // tm,),
        in_specs=[pl.BlockSpec((tm, d), lambda i, dst, fl: (i, 0))],
        out_specs=pl.BlockSpec(memory_space=pl.ANY),
        scratch_shapes=[pltpu.VMEM((MOE_ROWS, d), x.dtype), pltpu.SemaphoreType.DMA(()),
                        pltpu.SemaphoreType.DMA(())],
    )
    return pl.pallas_call(
        functools.partial(_dispatch_kernel, tm=tm, n_blocks=n_blocks),
        grid_spec=grid_spec,
        out_shape=jax.ShapeDtypeStruct((n_blocks * MOE_ROWS, d), x.dtype),
        compiler_params=_params("arbitrary"),
        name="dispatch_rows",
    )(dest, fill, x)


def moe_ffn(h, hp, router_w, router_b, w_gate, w_up, w_down, layer, ln_g, ln_b):
    n, d = h.shape
    a = n * TOP_K
    e_idx, gate, rank, counts = route(h, router_w, router_b)
    counts = counts[:, 0]
    padded = (counts + MOE_ROWS - 1) // MOE_ROWS * MOE_ROWS
    pad_end = jnp.cumsum(padded)
    pad_start = pad_end - padded
    experts_iota = jnp.arange(N_EXPERTS, dtype=jnp.int32)[:, None, None]
    dest = jnp.sum(jnp.where(e_idx[None] == experts_iota, pad_start[:, None, None], 0), axis=0) + rank
    n_blocks = a // MOE_ROWS + N_EXPERTS
    blk_start = jnp.arange(n_blocks, dtype=jnp.int32)[:, None] * MOE_ROWS
    blk_exp = jnp.minimum(jnp.sum((pad_end[None, :] <= blk_start).astype(jnp.int32), axis=1), N_EXPERTS - 1)
    n_used = (pad_end[-1:] // MOE_ROWS).astype(jnp.int32)
    ends_expert = jnp.any((pad_end[None, :] == blk_start + MOE_ROWS) & (padded[None, :] > 0), axis=1)
    fill = (ends_expert | (blk_start[:, 0] >= pad_end[-1])).astype(jnp.int32)
    xs = dispatch_rows(hp, dest, fill)
    yb = experts(xs, blk_exp, n_used, w_gate, w_up, w_down, layer)
    y = yb.at[dest.reshape(a)].get(mode="promise_in_bounds")
    return combine_ln(h, y, gate.T, ln_g, ln_b, with_bf16=layer + 1 < DEPTH)


FFT_R = 128
HALF_R = FFT_R // 2
HY_GROUP = 4


def _proj_t_kernel(w_ref, x_ref, o_ref, *, scale):
    acc = lax.dot_general(w_ref[...].astype(BF16), x_ref[...], (((0,), (1,)), ((), ())),
                          preferred_element_type=F32)
    o_ref[...] = (acc * scale).astype(o_ref.dtype)


def proj_t(w, col0, ncols, x, out_dtype, scale=1.0, tm=512, tn=1024):
    k = w.shape[0]
    n = x.shape[0]
    first = col0 // tm
    return pl.pallas_call(
        functools.partial(_proj_t_kernel, scale=scale),
        grid=(ncols // tm, n // tn),
        in_specs=[pl.BlockSpec((k, tm), lambda i, j: (0, first + i)),
                  pl.BlockSpec((tn, k), lambda i, j: (j, 0))],
        out_specs=pl.BlockSpec((tm, tn), lambda i, j: (i, j)),
        out_shape=jax.ShapeDtypeStruct((ncols, n), out_dtype),
        compiler_params=_params("arbitrary", "arbitrary"),
        name="proj_t",
    )(w, x)


def _dft_constants():
    idx = jnp.arange(FFT_R, dtype=jnp.int32)
    prod = idx[:, None] * idx[None, :]
    ang = (prod % FFT_R).astype(F32) * (2.0 * math.pi / FFT_R)
    f_r, f_i = jnp.cos(ang), -jnp.sin(ang)
    ang_t = prod.astype(F32) * (2.0 * math.pi / (FFT_R * FFT_R))
    g_r, g_i = f_r[:HALF_R], -f_i[:HALF_R]
    return dict(
        la=jnp.block([[f_r[:, :HALF_R], -f_i[:, :HALF_R]], [f_i[:, :HALF_R], f_r[:, :HALF_R]]]).astype(BF16),
        la_real=jnp.concatenate([f_r, f_i], axis=0).astype(BF16),
        rb=jnp.block([[f_r, f_i], [-f_i, f_r]]).astype(BF16),
        rc=jnp.block([[f_r, -f_i], [f_i, f_r]]).astype(BF16),
        ld=(jnp.block([[g_r, -g_i], [g_i, g_r]]) / (FFT_R * FFT_R)).astype(BF16),
        t_r=jnp.cos(ang_t), t_i=-jnp.sin(ang_t))


def _rows_to_lanes(x):
    return jnp.concatenate([x[:FFT_R], x[FFT_R:]], axis=1)


def _fwd_lane_stage(a, t_r, t_i, rb):
    a_r, a_i = a[:FFT_R], a[FFT_R:]
    t2_r = jnp.concatenate([t_r, t_r], axis=1)
    t2_i = jnp.concatenate([t_i, t_i], axis=1)
    b_r = a_r * t2_r - a_i * t2_i
    b_i = a_r * t2_i + a_i * t2_r
    lhs = jnp.concatenate([jnp.concatenate([b_r[:, :FFT_R], b_i[:, :FFT_R]], axis=1),
                           jnp.concatenate([b_r[:, FFT_R:], b_i[:, FFT_R:]], axis=1)], axis=0)
    return jnp.dot(lhs.astype(BF16), rb, preferred_element_type=F32)


def _conv_pairs(ms, kfs, la, rb, rc, ld, t_r, t_i):
    a = [jnp.dot(la, jnp.concatenate([m_r, m_i], axis=0).astype(BF16), preferred_element_type=F32)
         for m_r, m_i in ms]
    z = [_fwd_lane_stage(ai, t_r, t_i, rb) for ai in a]
    c = []
    for zi, kf in zip(z, kfs):
        z_r, z_i, k_r, k_i = zi[:, :FFT_R], zi[:, FFT_R:], kf[:, :FFT_R], kf[:, FFT_R:]
        y = jnp.concatenate([z_r * k_r - z_i * k_i, z_r * k_i + z_i * k_r], axis=1)
        c.append(jnp.dot(y.astype(BF16), rc, preferred_element_type=F32))
    t2_r = jnp.concatenate([t_r, t_r], axis=0)
    t2_i = jnp.concatenate([t_i, t_i], axis=0)
    out = []
    for ci in c:
        c_r, c_i = ci[:, :FFT_R], ci[:, FFT_R:]
        d_r = c_r * t2_r + c_i * t2_i
        d_i = c_i * t2_r - c_r * t2_i
        rhs = jnp.concatenate([_rows_to_lanes(d_r), _rows_to_lanes(d_i)], axis=0)
        x = jnp.dot(ld, rhs.astype(BF16), preferred_element_type=F32)
        out.append((x[:HALF_R], x[HALF_R:]))
    return out


def _short_conv_tile(x, w0, w1, w2, b):
    rows = x.shape[0]
    lane = lax.broadcasted_iota(jnp.int32, x.shape, 1)
    row = lax.broadcasted_iota(jnp.int32, x.shape, 0)
    r = pltpu.roll(x, 1, 1)
    prev = jnp.where(lane == 0, jnp.where(row == 0, 0.0, pltpu.roll(r, 1, 0)), r)
    r = pltpu.roll(x, FFT_R - 1, 1)
    nxt = jnp.where(lane == FFT_R - 1, jnp.where(row == rows - 1, 0.0, pltpu.roll(r, rows - 1, 0)), r)
    return w0 * prev + w1 * x + w2 * nxt + b


def _hyena_kernel(cw_ref, cb_ref, skip_ref, v_ref, x1_ref, x2_ref, kf_ref,
                  la_ref, rb_ref, rc_ref, ld_ref, tr_ref, ti_ref, o_ref, *, tc):
    c_base = pl.program_id(0) * tc
    lane = lax.broadcasted_iota(jnp.int32, (1, 2 * FFT_R), 1)

    def conv_in(ref, part, b, c0):
        tiles = []
        for cc in range(2):
            ch = part * HY_CH + c_base + c0 + cc
            tiles.append(_short_conv_tile(ref[c0 + cc, b], cw_ref[0, ch], cw_ref[1, ch], cw_ref[2, ch], cb_ref[ch]))
        return jnp.concatenate(tiles, axis=1)

    def group(g, carry):
        starts = [2 * (HY_GROUP * g + j) for j in range(HY_GROUP)]
        consts = (la_ref[...], rb_ref[...], rc_ref[...], ld_ref[...], tr_ref[...], ti_ref[...])
        zs = [(conv_in(v_ref, 0, 0, c0), conv_in(v_ref, 0, 1, c0)) for c0 in starts]
        for o, g_ref in enumerate((x1_ref, x2_ref)):
            kfs = [kf_ref[o, pl.ds(c0, 2)].reshape(2 * FFT_R, 2 * FFT_R) for c0 in starts]
            ys = _conv_pairs(zs, kfs, *consts)
            nxt = []
            for c0, (z_r, z_i), (y_r, y_i) in zip(starts, zs, ys):
                sk = jnp.where(lane < FFT_R, skip_ref[o, c_base + c0], skip_ref[o, c_base + c0 + 1])
                nxt.append((conv_in(g_ref, o + 1, 0, c0) * (y_r + sk * z_r),
                            conv_in(g_ref, o + 1, 1, c0) * (y_i + sk * z_i)))
            zs = nxt
        for c0, (z_r, z_i) in zip(starts, zs):
            o_ref[c0, 0] = z_r[:, :FFT_R].astype(o_ref.dtype)
            o_ref[c0 + 1, 0] = z_r[:, FFT_R:].astype(o_ref.dtype)
            o_ref[c0, 1] = z_i[:, :FFT_R].astype(o_ref.dtype)
            o_ref[c0 + 1, 1] = z_i[:, FFT_R:].astype(o_ref.dtype)
        return carry

    lax.fori_loop(0, tc // (2 * HY_GROUP), group, 0)


def hyena_mix(u_t, kf, conv_w, conv_b, skip, consts, tc=16):
    rows, n = u_t.shape
    ch = rows // 3
    u5 = u_t.reshape(rows, 2, HALF_R, FFT_R)
    nct = ch // tc
    smem = pl.BlockSpec(memory_space=pltpu.SMEM)
    part = lambda k: pl.BlockSpec((tc, 2, HALF_R, FFT_R), lambda i: (k * nct + i, 0, 0, 0))
    full2 = lambda a: pl.BlockSpec(a.shape, lambda i: (0, 0))
    mats = [consts[k] for k in ("la", "rb", "rc", "ld", "t_r", "t_i")]
    out = pl.pallas_call(
        functools.partial(_hyena_kernel, tc=tc),
        grid=(nct,),
        in_specs=[smem, smem, smem, part(0), part(1), part(2),
                  pl.BlockSpec((2, tc, FFT_R, 2 * FFT_R), lambda i: (0, i, 0, 0))] + [full2(a) for a in mats],
        out_specs=pl.BlockSpec((tc, 2, HALF_R, FFT_R), lambda i: (i, 0, 0, 0)),
        out_shape=jax.ShapeDtypeStruct((ch, 2, HALF_R, FFT_R), BF16),
        compiler_params=_params("arbitrary"),
        name="hyena_mix",
    )(conv_w.astype(F32), conv_b.astype(F32), skip.astype(F32), u5, u5, u5, kf, *mats)
    return out.reshape(ch, n)


def _hyena_filter_kernel(delta_ref, hf_ref, hb_ref, tpos_ref, la_ref, rb_ref, tr_ref, ti_ref, kf_ref, *, tc):
    c_base = pl.program_id(1) * tc

    def taps_of(c0):
        taps = []
        for cc in range(2):
            k = jnp.concatenate([hf_ref[0, 0, c0 + cc], hb_ref[0, 0, c0 + cc]], axis=0)
            k = k * jnp.exp(-tpos_ref[...] * delta_ref[c_base + c0 + cc])
            taps.append(k * lax.rsqrt(jnp.sum(k * k, keepdims=True)))
        return jnp.concatenate(taps, axis=1).astype(BF16)

    def group(g, carry):
        starts = [2 * (HY_GROUP * g + j) for j in range(HY_GROUP)]
        a = [jnp.dot(la_ref[...], taps_of(c0), preferred_element_type=F32) for c0 in starts]
        z = [_fwd_lane_stage(ai, tr_ref[...], ti_ref[...], rb_ref[...]) for ai in a]
        for c0, zi in zip(starts, z):
            kf_ref[0, pl.ds(c0, 2)] = zi.reshape(2, FFT_R, 2 * FFT_R)
        return carry

    lax.fori_loop(0, tc // (2 * HY_GROUP), group, 0)


def hyena_filter_spectra(seq, w1, b1, w2, b2, w3, b3, freq, w4, consts, tc=16):
    t = jnp.linspace(0.0, 1.0, seq, dtype=F32)[:, None]
    omega = 2.0 * math.pi * jnp.arange(seq, dtype=F32)[:, None] / seq
    bands = jnp.linspace(1e-4, HF_BANDS - 1, HF_BANDS, dtype=F32)[None, :]
    ang = omega * bands
    z = jnp.concatenate([t, jnp.cos(ang), -jnp.sin(ang)], -1)
    fr = freq.astype(F32)
    hid = jnp.sin(fr * (z @ w1.astype(F32) + b1.astype(F32)))
    hid = jnp.sin(fr * (hid @ w2.astype(F32) + b2.astype(F32)))
    hid = jnp.sin(fr * (hid @ w3.astype(F32) + b3.astype(F32)))
    back = lambda a: jnp.concatenate([jnp.zeros_like(a[:1]), a[:0:-1]], axis=0)
    w4t = w4.astype(BF16).T
    half = HY_ORDER * HY_CH
    h_fwd = matmul(w4t[:half], hid.T.astype(BF16), F32)
    h_bwd = matmul(w4t[half:], back(hid).T.astype(BF16), F32)
    shape5 = (HY_ORDER, HY_CH, HALF_R, FFT_R)
    tpos = jnp.concatenate([t, back(t)], axis=0).reshape(FFT_R, FFT_R)
    max_decay = math.log(HF_TARGET) / HF_FAST
    min_decay = math.log(HF_TARGET) / HF_SLOW
    deltas = jnp.abs(jnp.linspace(min_decay, max_decay, HY_CH, dtype=F32))
    taps = pl.BlockSpec((1, 1, tc, HALF_R, FFT_R), lambda o, i: (0, o, i, 0, 0))
    full2 = lambda a: pl.BlockSpec(a.shape, lambda o, i: (0, 0))
    mats = [consts[k] for k in ("la_real", "rb", "t_r", "t_i")]
    return pl.pallas_call(
        functools.partial(_hyena_filter_kernel, tc=tc),
        grid=(HY_ORDER, HY_CH // tc),
        in_specs=[pl.BlockSpec(memory_space=pltpu.SMEM), taps, taps, full2(tpos)] + [full2(a) for a in mats],
        out_specs=pl.BlockSpec((1, tc, FFT_R, 2 * FFT_R), lambda o, i: (o, i, 0, 0)),
        out_shape=jax.ShapeDtypeStruct((HY_ORDER, HY_CH, FFT_R, 2 * FFT_R), F32),
        compiler_params=_params("arbitrary", "arbitrary"),
        name="hyena_filter_spectra",
    )(deltas, h_fwd.reshape((1,) + shape5), h_bwd.reshape((1,) + shape5), tpos, *mats)


def kernel(x, ev_w_in, ev_hy_conv_w, ev_hy_conv_b, ev_hf_w1, ev_hf_b1, ev_hf_w2, ev_hf_b2, ev_hf_w3, ev_hf_b3,
           ev_hf_freq, ev_hf_w4, ev_hy_skip, ev_lam_q1, ev_lam_k1, ev_lam_q2, ev_lam_k2, ev_subln_g, ev_w_out,
           od_w_in, od_conv_w, od_conv_b, od_wa, od_ba, od_wx, od_bx, od_lam, od_w_out, ln1_g, ln1_b, ln2_g,
           ln2_b, router_w, router_b, ex_w_gate, ex_w_up, ex_w_down):
    bsz, seq, d = x.shape
    n = bsz * seq
    h = x.reshape(n, d).astype(F32)
    hb = h.astype(BF16)
    for layer in range(DEPTH):
        i = layer // 2
        if layer % 2 == 0:
            w_in = ev_w_in[i]
            c0 = 3 * HY_CH
            u_t = proj_t(w_in, 0, c0, hb, F32)
            qt = proj_t(w_in, c0, DA_QK, hb, BF16, scale=DA_HEAD_DIM ** -0.5)
            k = matmul(hb, w_in, BF16, col0=c0 + DA_QK, ncols=DA_QK)
            vt = proj_t(w_in, c0 + 2 * DA_QK, DA_WIDTH, hb, BF16)
            consts = _dft_constants()
            kf = hyena_filter_spectra(seq, ev_hf_w1[i], ev_hf_b1[i], ev_hf_w2[i], ev_hf_b2[i],
                                      ev_hf_w3[i], ev_hf_b3[i], ev_hf_freq[i], ev_hf_w4[i], consts)
            y_hy = hyena_mix(u_t, kf, ev_hy_conv_w[i], ev_hy_conv_b[i], ev_hy_skip[i], consts)
            lambda_init = 0.8 - 0.6 * math.exp(-0.3 * layer)
            lam = (jnp.exp(jnp.sum(ev_lam_q1[i].astype(F32) * ev_lam_k1[i].astype(F32)))
                   - jnp.exp(jnp.sum(ev_lam_q2[i].astype(F32) * ev_lam_k2[i].astype(F32))) + lambda_init)
            y_da = diff_attention(qt, k.reshape(bsz, seq, DA_QK), vt,
                                  lam, ev_subln_g[i], lambda_init).reshape(n, DA_WIDTH)
            h, hp = even_out_proj(y_hy, y_da, ev_w_out[i].astype(BF16), h, ln1_g[layer], ln1_b[layer])
        else:
            u = matmul(hb, od_w_in[i], BF16)
            yd = rglru_bidir(u.reshape(bsz, seq, 2 * RG_WIDTH), od_conv_w[i], od_conv_b[i], od_wa[i], od_ba[i],
                             od_wx[i], od_bx[i], od_lam[i]).reshape(2, n, RG_WIDTH)
            h, hp = odd_out_proj(u, yd, od_w_out[i].astype(BF16), h, ln1_g[layer], ln1_b[layer])
        h, hb = moe_ffn(h, hp, router_w, router_b, ex_w_gate, ex_w_up, ex_w_down, layer,
                        ln2_g[layer], ln2_b[layer])
    return h.reshape(bsz, seq, d).astype(x.dtype)
```

```python
import functools
import math

import jax
import jax.numpy as jnp
from jax import lax
from jax.experimental import pallas as pl
from jax.experimental.pallas import tpu as pltpu

F32 = jnp.float32
BF16 = jnp.bfloat16

D_MODEL = 2048
DEPTH = 2
HY_CH = D_MODEL // 2
HY_ORDER = 2
HF_BANDS = 16
HF_TARGET = 1e-2
HF_FAST = 0.3
HF_SLOW = 1.5
DA_HEADS = 8
DA_HEAD_DIM = 64
DA_V_DIM = 2 * DA_HEAD_DIM
DA_QK = DA_HEADS * 2 * DA_HEAD_DIM
DA_WIDTH = DA_HEADS * DA_V_DIM
RG_WIDTH = D_MODEL
RG_BLOCKS = 8
RG_BLOCK_W = RG_WIDTH // RG_BLOCKS
RG_C = 8.0
RG_CONV = 4
N_EXPERTS = 32
N_GROUPS = 4
EXPERTS_PER_GROUP = N_EXPERTS // N_GROUPS
TOP_K = 2
D_FF = 512
DN_ALPHA = (2 * DEPTH) ** 0.25
LN_EPS = 1e-5

VMEM_LIMIT_BYTES = 56 * 1024 * 1024
MOE_ROWS = 256


def _params(*sem):
    return pltpu.CompilerParams(dimension_semantics=sem, vmem_limit_bytes=VMEM_LIMIT_BYTES)


def _mm_kernel(x_ref, w_ref, o_ref):
    o_ref[...] = jnp.dot(x_ref[...], w_ref[...].astype(BF16), preferred_element_type=F32).astype(o_ref.dtype)


def matmul(x, w, out_dtype, col0=0, ncols=None, tm=512, tn=2048):
    m, k = x.shape
    n = w.shape[1] if ncols is None else ncols
    tm, tn = min(tm, m), min(tn, n)
    first = col0 // tn
    return pl.pallas_call(
        _mm_kernel,
        grid=(n // tn, m // tm),
        in_specs=[pl.BlockSpec((tm, k), lambda j, i: (i, 0)),
                  pl.BlockSpec((k, tn), lambda j, i: (0, first + j))],
        out_specs=pl.BlockSpec((tm, tn), lambda j, i: (i, j)),
        out_shape=jax.ShapeDtypeStruct((m, n), out_dtype),
        compiler_params=_params("arbitrary", "arbitrary"),
        name="matmul",
    )(x, w)


def _ln_rows(z, g, b):
    mu = jnp.mean(z, axis=-1, keepdims=True)
    zc = z - mu
    var = jnp.mean(zc * zc, axis=-1, keepdims=True)
    return zc * lax.rsqrt(var + LN_EPS) * g + b


def _pack_halves(y):
    half = y.shape[1] // 2
    hi = pltpu.bitcast(y[:, :half].astype(BF16).astype(F32), jnp.uint32)
    lo = pltpu.bitcast(y[:, half:].astype(BF16).astype(F32), jnp.uint32)
    return hi | (lo >> 16)


def _unpack_halves(p):
    hi = pltpu.bitcast(p & jnp.uint32(0xFFFF0000), F32)
    lo = pltpu.bitcast(p << 16, F32)
    return jnp.concatenate([hi, lo], axis=1).astype(BF16)


def _even_out_kernel(xa_ref, xb_ref, wa_ref, wb_ref, r_ref, g_ref, b_ref, o_ref, op_ref):
    acc = lax.dot_general(xa_ref[...], wa_ref[...], (((0,), (0,)), ((), ())), preferred_element_type=F32)
    acc += jnp.dot(xb_ref[...], wb_ref[...], preferred_element_type=F32)
    y = _ln_rows(DN_ALPHA * r_ref[...] + acc, g_ref[...], b_ref[...])
    o_ref[...] = y
    op_ref[...] = _pack_halves(y)


def even_out_proj(y_hy_t, y_da, w_out, resid, g, b, tm=512):
    ka, m = y_hy_t.shape
    kb = y_da.shape[1]
    d = w_out.shape[1]
    row = lambda i: (i, 0)
    fixed = lambda i: (0, 0)
    return pl.pallas_call(
        _even_out_kernel,
        grid=(m // tm,),
        in_specs=[pl.BlockSpec((ka, tm), lambda i: (0, i)), pl.BlockSpec((tm, kb), row),
                  pl.BlockSpec((ka, d), fixed), pl.BlockSpec((kb, d), fixed),
                  pl.BlockSpec((tm, d), row), pl.BlockSpec((1, d), fixed), pl.BlockSpec((1, d), fixed)],
        out_specs=[pl.BlockSpec((tm, d), row), pl.BlockSpec((tm, d // 2), row)],
        out_shape=[jax.ShapeDtypeStruct((m, d), F32), jax.ShapeDtypeStruct((m, d // 2), jnp.uint32)],
        compiler_params=_params("arbitrary"),
        name="even_out_proj",
    )(y_hy_t, y_da, w_out[:ka], w_out[ka:], resid, g.reshape(1, d), b.reshape(1, d))


def _odd_out_kernel(gate_ref, y0_ref, y1_ref, w_ref, r_ref, g_ref, b_ref, o_ref, op_ref):
    x = jax.nn.gelu(gate_ref[...].astype(F32)) * (y0_ref[0].astype(F32) + y1_ref[0].astype(F32))
    acc = jnp.dot(x.astype(BF16), w_ref[...], preferred_element_type=F32)
    y = _ln_rows(DN_ALPHA * r_ref[...] + acc, g_ref[...], b_ref[...])
    o_ref[...] = y
    op_ref[...] = _pack_halves(y)


def odd_out_proj(u, yd, w_out, resid, g, b, tm=512):
    m = u.shape[0]
    k, d = w_out.shape
    row = lambda i: (i, 0)
    fixed = lambda i: (0, 0)
    return pl.pallas_call(
        _odd_out_kernel,
        grid=(m // tm,),
        in_specs=[pl.BlockSpec((tm, k), row),
                  pl.BlockSpec((1, tm, k), lambda i: (0, i, 0)),
                  pl.BlockSpec((1, tm, k), lambda i: (1, i, 0)),
                  pl.BlockSpec((k, d), fixed),
                  pl.BlockSpec((tm, d), row), pl.BlockSpec((1, d), fixed), pl.BlockSpec((1, d), fixed)],
        out_specs=[pl.BlockSpec((tm, d), row), pl.BlockSpec((tm, d // 2), row)],
        out_shape=[jax.ShapeDtypeStruct((m, d), F32), jax.ShapeDtypeStruct((m, d // 2), jnp.uint32)],
        compiler_params=_params("arbitrary"),
        name="odd_out_proj",
    )(u, yd, yd, w_out, resid, g.reshape(1, d), b.reshape(1, d))


def _combine_ln_kernel(h_ref, y0_ref, y1_ref, gate_ref, g_ref, b_ref, o_ref, *maybe_ob_ref):
    gate = gate_ref[...]
    ffn = gate[:, 0:1] * y0_ref[...].astype(F32) + gate[:, 1:2] * y1_ref[...].astype(F32)
    y = _ln_rows(DN_ALPHA * h_ref[...] + ffn, g_ref[...], b_ref[...])
    o_ref[...] = y
    for ob_ref in maybe_ob_ref:
        ob_ref[...] = y.astype(BF16)


def combine_ln(h, y, gate, g, b, with_bf16, tm=512):
    m, d = h.shape
    row = lambda i: (i, 0)
    fixed = lambda i: (0, 0)
    n_out = 2 if with_bf16 else 1
    out = pl.pallas_call(
        _combine_ln_kernel,
        grid=(m // tm,),
        in_specs=[pl.BlockSpec((tm, d), row), pl.BlockSpec((tm, d), row),
                  pl.BlockSpec((tm, d), lambda i: (m // tm + i, 0)),
                  pl.BlockSpec((tm, TOP_K), row), pl.BlockSpec((1, d), fixed), pl.BlockSpec((1, d), fixed)],
        out_specs=[pl.BlockSpec((tm, d), row)] * n_out,
        out_shape=[jax.ShapeDtypeStruct((m, d), F32), jax.ShapeDtypeStruct((m, d), BF16)][:n_out],
        compiler_params=_params("arbitrary"),
        name="combine_ln",
    )(h, y, y, gate, g.reshape(1, d), b.reshape(1, d))
    return (out[0], out[1]) if with_bf16 else (out[0], None)


ONES_ROWS = 16
POS_SPLIT = 16


def _attn_kernel(slopes_ref, lam_ref, qt_ref, k_ref, vt_ref, fq_ref, fk_ref, bd_ref, g_ref, o_ref, m_s, acc_s, sa_s, sb_s,
                 *, tq, tk, seq, out_scale):
    head = pl.program_id(1)
    i0 = pl.program_id(2) * tq
    slope = slopes_ref[head]
    lam = lam_ref[0]
    qt = qt_ref[...]
    row = lax.broadcasted_iota(jnp.int32, qt.shape, 0)
    zero = jnp.zeros_like(qt)
    q2 = jnp.concatenate([jnp.where(row < DA_HEAD_DIM, qt, zero),
                          jnp.where(row >= DA_HEAD_DIM, qt, zero)], axis=1)
    fq = fq_ref[0]
    fq2 = jnp.concatenate([fq, fq], axis=1)
    q_aug = jnp.concatenate([q2, fq2], axis=0)
    fk = fk_ref[0]
    fk_neg = -fk
    ones = jnp.ones((ONES_ROWS, tk), BF16)
    m_s[...] = jnp.full(m_s.shape, -jnp.inf, F32)
    acc_s[...] = jnp.zeros(acc_s.shape, F32)
    j_diag = i0 // tk

    def update(j, s, shift):
        off = pl.multiple_of(j * tk, tk)
        va = jnp.concatenate([vt_ref[:, pl.ds(off, tk)], ones], axis=0)
        m_prev = m_s[...]
        m_new = jnp.maximum(m_prev, jnp.max(s, axis=0, keepdims=True) - shift)
        p = jnp.exp(s - (m_new + shift))
        alpha = jnp.exp(m_prev - m_new)
        acc_s[...] = alpha * acc_s[...] + jnp.dot(va, p.astype(BF16), preferred_element_type=F32)
        m_s[...] = m_new

    def chunk_of(t):
        return t + (t >= j_diag).astype(jnp.int32)

    def scores(t):
        j = chunk_of(t)
        off = pl.multiple_of(j * tk, tk)
        ka = jnp.concatenate([k_ref[0, pl.ds(off, tk), :], jnp.where(j < j_diag, fk, fk_neg)], axis=1)
        return jnp.dot(ka, q_aug, preferred_element_type=F32)

    def consume(s_ref, t):
        j = chunk_of(t)
        update(j, s_ref[...], slope * jnp.abs(i0 - j * tk).astype(F32))

    j0 = pl.multiple_of(j_diag * tk, tk)
    bias = bd_ref[0]
    s_diag = jnp.dot(k_ref[0, pl.ds(j0, tk), :], q2, preferred_element_type=F32)
    sa_s[...] = s_diag - jnp.concatenate([bias, bias], axis=1)
    sb_s[...] = scores(jnp.int32(0))
    update(j_diag, sa_s[...], 0.0)
    n_off = seq // tk - 1

    def pair(t, carry):
        sa_s[...] = scores(2 * t + 1)
        consume(sb_s, 2 * t)
        sb_s[...] = scores(2 * t + 2)
        consume(sa_s, 2 * t + 1)
        return carry

    lax.fori_loop(0, n_off // 2, pair, 0)
    consume(sb_s, jnp.int32(n_off - 1))

    acc = acc_s[...]
    o1 = acc[:DA_V_DIM, :tq] / acc[DA_V_DIM:DA_V_DIM + 1, :tq]
    o2 = acc[:DA_V_DIM, tq:] / acc[DA_V_DIM:DA_V_DIM + 1, tq:]
    o = o1 - lam * o2
    o = o * lax.rsqrt(jnp.mean(o * o, axis=0, keepdims=True) + LN_EPS) * g_ref[...]
    o_ref[0] = (o * out_scale).T.astype(o_ref.dtype)


def diff_attention(qt, k, vt, lam, subln_g, lambda_init, tq=512, tk=512):
    bsz, seq, width = k.shape
    nq = seq // tq
    slopes = 2.0 ** (-(8.0 / DA_HEADS) * jnp.arange(1, DA_HEADS + 1, dtype=F32))

    def split(n):
        pos = jnp.arange(n, dtype=jnp.int32)
        return (pos // POS_SPLIT * POS_SPLIT).astype(F32), (pos % POS_SPLIT).astype(F32)

    qhi, qlo = split(tq)
    khi, klo = split(tk)
    sl = slopes[:, None]
    fq = jnp.zeros((DA_HEADS, DA_V_DIM, tq), F32)
    fq = fq.at[:, 0].set(1.0).at[:, 1].set(1.0).at[:, 2].set(-sl * qhi).at[:, 3].set(-sl * qlo)
    fk = jnp.zeros((DA_HEADS, tk, DA_V_DIM), F32)
    fk = fk.at[:, :, 0].set(sl * khi).at[:, :, 1].set(sl * klo).at[:, :, 2].set(1.0).at[:, :, 3].set(1.0)
    assert tq == tk, "the chunk holding a query tile's diagonal must be the tile itself"
    pos = jnp.arange(tk, dtype=jnp.int32)
    bias_diag = slopes[:, None, None] * jnp.abs(pos[:, None] - pos[None, :]).astype(F32)
    kern = functools.partial(_attn_kernel, tq=tq, tk=tk, seq=seq, out_scale=1.0 - lambda_init)
    smem = pl.BlockSpec(memory_space=pltpu.SMEM)
    return pl.pallas_call(
        kern,
        grid=(bsz, DA_HEADS, seq // tq),
        in_specs=[smem, smem,
                  pl.BlockSpec((DA_V_DIM, tq), lambda b, h, i: (h, b * nq + i)),
                  pl.BlockSpec((1, seq, DA_V_DIM), lambda b, h, i: (b, 0, h)),
                  pl.BlockSpec((DA_V_DIM, seq), lambda b, h, i: (h, b)),
                  pl.BlockSpec((1, DA_V_DIM, tq), lambda b, h, i: (h, 0, 0)),
                  pl.BlockSpec((1, tk, DA_V_DIM), lambda b, h, i: (h, 0, 0)),
                  pl.BlockSpec((1, tk, tq), lambda b, h, i: (h, 0, 0)),
                  pl.BlockSpec((DA_V_DIM, 1), lambda b, h, i: (0, 0))],
        out_specs=pl.BlockSpec((1, tq, DA_V_DIM), lambda b, h, i: (b, i, h)),
        out_shape=jax.ShapeDtypeStruct((bsz, seq, width), BF16),
        scratch_shapes=[pltpu.VMEM((1, 2 * tq), F32), pltpu.VMEM((DA_V_DIM + ONES_ROWS, 2 * tq), F32),
                        pltpu.VMEM((tk, 2 * tq), F32), pltpu.VMEM((tk, 2 * tq), F32)],
        compiler_params=_params("arbitrary", "arbitrary", "arbitrary"),
        name="diff_attention",
    )(slopes, lam.reshape(1).astype(F32), qt, k, vt, fq.astype(BF16), fk.astype(BF16), bias_diag,
      subln_g.reshape(DA_V_DIM, 1).astype(F32))


HALO = 16


def _rglru_kernel(xr_ref, prev_ref, next_ref, cw_ref, cb_ref, wa_ref, wx_ref, ba_ref, bx_ref, kk_ref, y_ref,
                  a_s, b_s, y_s, h_s, *, ts, nt):
    direction = pl.program_id(0)
    t = pl.program_id(2)
    chunk = t + direction * (nt - 1 - 2 * t)

    @pl.when(t == 0)
    def _():
        h_s[...] = jnp.zeros(h_s.shape, F32)

    xf = jnp.concatenate([jnp.where(chunk == 0, 0.0, prev_ref[0].astype(F32)), xr_ref[0].astype(F32),
                          jnp.where(chunk == nt - 1, 0.0, next_ref[0].astype(F32))], axis=0)
    rows = ts + 2 * HALO

    def tap(j):
        shifted = xf if j == 2 else pltpu.roll(xf, (2 - j) % rows, 0)
        return cw_ref[j:j + 1, :] * shifted[HALO:HALO + ts]

    x = cb_ref[...] + sum(tap(j) for j in range(RG_CONV))
    xb = x.astype(BF16)
    xh = 0.5 * x
    for n in range(RG_BLOCKS):
        cols = slice(n * RG_BLOCK_W, (n + 1) * RG_BLOCK_W)
        xs = xb[:, cols]
        tr = jnp.tanh(jnp.dot(xs, wa_ref[0, n], preferred_element_type=F32) + ba_ref[0, :, cols])
        ti = jnp.tanh(jnp.dot(xs, wx_ref[0, n], preferred_element_type=F32) + bx_ref[0, :, cols])
        kk = kk_ref[0, :, cols]
        a = jnp.exp2(kk * tr + kk)
        a_s[:, cols] = a
        s = 1.0 - a * a
        root = jnp.where(s > 0.0, s * lax.rsqrt(s), 0.0)
        b_s[:, cols] = root * (ti * xh[:, cols] + xh[:, cols])

    def step(k, h):
        row = jnp.where(direction == 0, k, ts - 1 - k)
        h = a_s[pl.ds(row, 1), :] * h + b_s[pl.ds(row, 1), :]
        y_s[pl.ds(row, 1), :] = h
        return h

    h_s[...] = lax.fori_loop(0, ts, step, h_s[...], unroll=8)
    y_ref[0, 0] = y_s[...].astype(y_ref.dtype)


def rglru_bidir(u, conv_w, conv_b, wa, ba, wx, bx, lam, ts=512):
    bsz, seq, w2 = u.shape
    w = w2 // 2
    nt = seq // ts
    per = ts // HALO
    kk = (-0.5 * RG_C * math.log2(math.e)) * jax.nn.softplus(-lam.astype(F32)).reshape(2, 1, w)
    chunk = lambda d, t: t + d * (nt - 1 - 2 * t)
    dmap4 = lambda d, b, t: (d, 0, 0, 0)
    dmap3 = lambda d, b, t: (d, 0, 0)
    fixed = lambda d, b, t: (0, 0)
    return pl.pallas_call(
        functools.partial(_rglru_kernel, ts=ts, nt=nt),
        grid=(2, bsz, nt),
        in_specs=[pl.BlockSpec((1, ts, w), lambda d, b, t: (b, chunk(d, t), 1)),
                  pl.BlockSpec((1, HALO, w), lambda d, b, t: (b, jnp.maximum(chunk(d, t) * per - 1, 0), 1)),
                  pl.BlockSpec((1, HALO, w),
                               lambda d, b, t: (b, jnp.minimum((chunk(d, t) + 1) * per, seq // HALO - 1), 1)),
                  pl.BlockSpec((RG_CONV, w), fixed), pl.BlockSpec((1, w), fixed),
                  pl.BlockSpec((1, RG_BLOCKS, RG_BLOCK_W, RG_BLOCK_W), dmap4),
                  pl.BlockSpec((1, RG_BLOCKS, RG_BLOCK_W, RG_BLOCK_W), dmap4),
                  pl.BlockSpec((1, 1, w), dmap3), pl.BlockSpec((1, 1, w), dmap3), pl.BlockSpec((1, 1, w), dmap3)],
        out_specs=pl.BlockSpec((1, 1, ts, w), lambda d, b, t: (d, b, chunk(d, t), 0)),
        out_shape=jax.ShapeDtypeStruct((2, bsz, seq, w), BF16),
        scratch_shapes=[pltpu.VMEM((ts, w), F32), pltpu.VMEM((ts, w), F32), pltpu.VMEM((ts, w), F32),
                        pltpu.VMEM((1, w), F32)],
        compiler_params=_params("arbitrary", "arbitrary", "arbitrary"),
        name="rglru_bidir",
    )(u, u, u, conv_w.astype(F32), conv_b.astype(F32).reshape(1, w), (0.5 * wa).astype(BF16),
      (0.5 * wx).astype(BF16), 0.5 * ba.astype(F32).reshape(2, 1, w), 0.5 * bx.astype(F32).reshape(2, 1, w), kk)


def _router_kernel(h_ref, wt_ref, b_ref, tri_ref, e_ref, g_ref, r_ref, cnt_ref, seen_s):
    h = h_ref[...]
    h_hi = h.astype(BF16)
    h_lo = (h - h_hi.astype(F32)).astype(BF16)
    nt_dot = lambda a, b: lax.dot_general(a, b, (((1,), (1,)), ((), ())), preferred_element_type=F32)
    logits = nt_dot(wt_ref[0], h_hi) + (nt_dot(wt_ref[0], h_lo) + nt_dot(wt_ref[1], h_hi))
    tm = logits.shape[1]
    s = jax.nn.sigmoid(logits).reshape(N_GROUPS, EXPERTS_PER_GROUP, tm)
    sel = s + b_ref[...]
    idx = lax.broadcasted_iota(jnp.int32, sel.shape, 1)
    big = jnp.int32(EXPERTS_PER_GROUP)
    v1 = jnp.max(sel, axis=1, keepdims=True)
    i1 = jnp.min(jnp.where(sel == v1, idx, big), axis=1, keepdims=True)
    rest = jnp.where(idx == i1, -jnp.inf, sel)
    v2 = jnp.max(rest, axis=1, keepdims=True)
    i2 = jnp.min(jnp.where(rest == v2, idx, big), axis=1, keepdims=True)
    score = v1 + v2
    gidx = lax.broadcasted_iota(jnp.int32, score.shape, 0)
    best = jnp.max(score, axis=0, keepdims=True)
    grp = jnp.min(jnp.where(score == best, gidx, jnp.int32(N_GROUPS)), axis=0, keepdims=True)
    pick = gidx == grp
    l1 = jnp.sum(jnp.where(pick, i1, 0), axis=0)
    l2 = jnp.sum(jnp.where(pick, i2, 0), axis=0)
    s_g = jnp.sum(jnp.where(pick, s, 0.0), axis=0)
    eidx = lax.broadcasted_iota(jnp.int32, s_g.shape, 0)
    w1 = jnp.sum(jnp.where(eidx == l1, s_g, 0.0), axis=0, keepdims=True)
    w2 = jnp.sum(jnp.where(eidx == l2, s_g, 0.0), axis=0, keepdims=True)
    tot = w1 + w2
    base = grp[0] * EXPERTS_PER_GROUP
    e1, e2 = base + l1, base + l2
    e_ref[...] = jnp.concatenate([e1, e2], axis=0)
    g_ref[...] = jnp.concatenate([w1 / tot, w2 / tot], axis=0)

    @pl.when(pl.program_id(0) == 0)
    def _():
        seen_s[...] = jnp.zeros(seen_s.shape, F32)

    eall = lax.broadcasted_iota(jnp.int32, (N_EXPERTS, tm), 0)
    hit1, hit2 = eall == e1, eall == e2
    both = jnp.where(hit1, 1.0, jnp.where(hit2, 1.0, 0.0))
    incl = jnp.dot(both.astype(BF16), tri_ref[...], preferred_element_type=F32)
    before = incl - both + seen_s[...]
    r_ref[...] = jnp.concatenate([jnp.sum(jnp.where(hit1, before, 0.0), axis=0, keepdims=True),
                                  jnp.sum(jnp.where(hit2, before, 0.0), axis=0, keepdims=True)],
                                 axis=0).astype(jnp.int32)
    seen_s[...] = seen_s[...] + incl[:, tm - 1:tm]
    cnt_ref[...] = seen_s[...].astype(jnp.int32)


def route(h, router_w, router_b, tm=1024):
    n, d = h.shape
    tri = (jnp.arange(tm)[:, None] <= jnp.arange(tm)[None, :]).astype(BF16)
    wt = router_w.astype(F32).T
    wt_hi = wt.astype(BF16)
    wt_lo = (wt - wt_hi.astype(F32)).astype(BF16)
    tile = pl.BlockSpec((TOP_K, tm), lambda i: (0, i))
    return pl.pallas_call(
        _router_kernel,
        grid=(n // tm,),
        in_specs=[pl.BlockSpec((tm, d), lambda i: (i, 0)),
                  pl.BlockSpec((2, N_EXPERTS, d), lambda i: (0, 0, 0)),
                  pl.BlockSpec((N_GROUPS, EXPERTS_PER_GROUP, 1), lambda i: (0, 0, 0)),
                  pl.BlockSpec((tm, tm), lambda i: (0, 0))],
        out_specs=[tile, tile, tile, pl.BlockSpec((N_EXPERTS, 1), lambda i: (0, 0))],
        out_shape=[jax.ShapeDtypeStruct((TOP_K, n), jnp.int32), jax.ShapeDtypeStruct((TOP_K, n), F32),
                   jax.ShapeDtypeStruct((TOP_K, n), jnp.int32), jax.ShapeDtypeStruct((N_EXPERTS, 1), jnp.int32)],
        scratch_shapes=[pltpu.VMEM((N_EXPERTS, 1), F32)],
        compiler_params=_params("arbitrary"),
        name="router",
    )(h, jnp.stack([wt_hi, wt_lo]), router_b.astype(F32).reshape(N_GROUPS, EXPERTS_PER_GROUP, 1), tri)


def _experts_kernel(blk_exp_ref, nxt_exp_ref, slot_ref, n_used_ref, x_ref, wg_hbm, wu_hbm, wd_hbm, o_ref,
                    wg_f, wu_f, wd_f, wg_s, wu_s, wd_s, sem, *, layer):
    i = pl.program_id(0)
    expert = blk_exp_ref[i]
    prev = blk_exp_ref[jnp.maximum(i - 1, 0)]
    slot = slot_ref[i]

    def fetch(e, s):
        return [pltpu.make_async_copy(src.at[layer, e], dst.at[s], sem.at[s])
                for src, dst in ((wg_hbm, wg_f), (wu_hbm, wu_f), (wd_hbm, wd_f))]

    @pl.when(i == 0)
    def _():
        for copy in fetch(expert, slot):
            copy.start()

    @pl.when((i == 0) | (expert != prev))
    def _():
        for copy in fetch(expert, slot):
            copy.wait()
        wg_s[...] = wg_f[slot].astype(BF16)
        wu_s[...] = wu_f[slot].astype(BF16)
        wd_s[...] = wd_f[slot].astype(BF16)

        @pl.when(nxt_exp_ref[i] >= 0)
        def _():
            for copy in fetch(nxt_exp_ref[i], 1 - slot):
                copy.start()

    @pl.when(i < n_used_ref[0])
    def _():
        x = _unpack_halves(x_ref[...])
        hg = jnp.dot(x, wg_s[...], preferred_element_type=F32)
        hu = jnp.dot(x, wu_s[...], preferred_element_type=F32)
        hidden = (jax.nn.silu(hg) * hu).astype(BF16)
        o_ref[...] = jnp.dot(hidden, wd_s[...], preferred_element_type=F32).astype(o_ref.dtype)

    @pl.when(i >= n_used_ref[0])
    def _():
        o_ref[...] = jnp.zeros(o_ref.shape, o_ref.dtype)


def experts(xs, blk_exp, n_used, w_gate, w_up, w_down, layer):
    p = xs.shape[0]
    n_blocks = p // MOE_ROWS
    d = w_gate.shape[2]
    f = w_gate.shape[3]
    idx = jnp.arange(n_blocks, dtype=jnp.int32)
    starts = jnp.concatenate([jnp.ones((1,), bool), blk_exp[1:] != blk_exp[:-1]])
    slot = (jnp.cumsum(starts.astype(jnp.int32)) - 1) % 2
    later = starts[None, :] & (idx[None, :] > idx[:, None])
    first_later = jnp.min(jnp.where(later, idx[None, :], n_blocks), axis=1)
    nxt_exp = jnp.sum(jnp.where(idx[None, :] == first_later[:, None], blk_exp[None, :], 0), axis=1)
    nxt_exp = jnp.where(first_later < n_blocks, nxt_exp, -1).astype(jnp.int32)
    hbm = pl.BlockSpec(memory_space=pl.ANY)
    grid_spec = pltpu.PrefetchScalarGridSpec(
        num_scalar_prefetch=4,
        grid=(n_blocks,),
        in_specs=[pl.BlockSpec((MOE_ROWS, d // 2), lambda i, *_: (i, 0)), hbm, hbm, hbm],
        out_specs=pl.BlockSpec((MOE_ROWS, d), lambda i, *_: (i, 0)),
        scratch_shapes=[pltpu.VMEM((2, d, f), F32), pltpu.VMEM((2, d, f), F32), pltpu.VMEM((2, f, d), F32),
                        pltpu.VMEM((d, f), BF16), pltpu.VMEM((d, f), BF16), pltpu.VMEM((f, d), BF16),
                        pltpu.SemaphoreType.DMA((2,))],
    )
    return pl.pallas_call(
        functools.partial(_experts_kernel, layer=layer),
        grid_spec=grid_spec,
        out_shape=jax.ShapeDtypeStruct((p, d), BF16),
        compiler_params=_params("arbitrary"),
        name="experts",
    )(blk_exp, nxt_exp, slot.astype(jnp.int32), n_used, xs, w_gate, w_up, w_down)


def _dispatch_kernel(dest_ref, fill_ref, x_ref, xs_ref, zero_s, fill_sem, row_sem, *, tm, n_blocks):
    base = pl.program_id(0) * tm

    @pl.when(pl.program_id(0) == 0)
    def _():
        zero_s[...] = jnp.zeros(zero_s.shape, zero_s.dtype)

        def zero_block(blk):
            return pltpu.make_async_copy(zero_s, xs_ref.at[pl.ds(pl.multiple_of(blk * MOE_ROWS, MOE_ROWS),
                                                                 MOE_ROWS)], fill_sem)

        def start(blk, carry):
            @pl.when(fill_ref[blk] != 0)
            def _():
                zero_block(blk).start()
            return carry

        def wait(blk, carry):
            @pl.when(fill_ref[blk] != 0)
            def _():
                zero_block(blk).wait()
            return carry

        lax.fori_loop(0, n_blocks, start, 0)
        lax.fori_loop(0, n_blocks, wait, 0)

    def issue(r, carry):
        for k in range(TOP_K):
            pltpu.make_async_copy(x_ref.at[pl.ds(r, 1)], xs_ref.at[pl.ds(dest_ref[k, base + r], 1)],
                                  row_sem).start()
        return carry

    lax.fori_loop(0, tm, issue, 0, unroll=4)
    for k in range(TOP_K):
        pltpu.make_async_copy(x_ref, xs_ref.at[pl.ds(0, tm)], row_sem).wait()


def dispatch_rows(x, dest, fill, tm=512):
    n, d = x.shape
    n_blocks = fill.shape[0]
    grid_spec = pltpu.PrefetchScalarGridSpec(
        num_scalar_prefetch=2,
        grid=(n // tm,),
        in_specs=[pl.BlockSpec((tm, d), lambda i, dst, fl: (i, 0))],
        out_specs=pl.BlockSpec(memory_space=pl.ANY),
        scratch_shapes=[pltpu.VMEM((MOE_ROWS, d), x.dtype), pltpu.SemaphoreType.DMA(()),
                        pltpu.SemaphoreType.DMA(())],
    )
    return pl.pallas_call(
        functools.partial(_dispatch_kernel, tm=tm, n_blocks=n_blocks),
        grid_spec=grid_spec,
        out_shape=jax.ShapeDtypeStruct((n_blocks * MOE_ROWS, d), x.dtype),
        compiler_params=_params("arbitrary"),
        name="dispatch_rows",
    )(dest, fill, x)


def moe_ffn(h, hp, router_w, router_b, w_gate, w_up, w_down, layer, ln_g, ln_b):
    n, d = h.shape
    a = n * TOP_K
    e_idx, gate, rank, counts = route(h, router_w, router_b)
    counts = counts[:, 0]
    padded = (counts + MOE_ROWS - 1) // MOE_ROWS * MOE_ROWS
    pad_end = jnp.cumsum(padded)
    pad_start = pad_end - padded
    experts_iota = jnp.arange(N_EXPERTS, dtype=jnp.int32)[:, None, None]
    dest = jnp.sum(jnp.where(e_idx[None] == experts_iota, pad_start[:, None, None], 0), axis=0) + rank
    n_blocks = a // MOE_ROWS + N_EXPERTS
    blk_start = jnp.arange(n_blocks, dtype=jnp.int32)[:, None] * MOE_ROWS
    blk_exp = jnp.minimum(jnp.sum((pad_end[None, :] <= blk_start).astype(jnp.int32), axis=1), N_EXPERTS - 1)
    n_used = (pad_end[-1:] // MOE_ROWS).astype(jnp.int32)
    ends_expert = jnp.any((pad_end[None, :] == blk_start + MOE_ROWS) & (padded[None, :] > 0), axis=1)
    fill = (ends_expert | (blk_start[:, 0] >= pad_end[-1])).astype(jnp.int32)
    xs = dispatch_rows(hp, dest, fill)
    yb = experts(xs, blk_exp, n_used, w_gate, w_up, w_down, layer)
    y = yb.at[dest.reshape(a)].get(mode="promise_in_bounds")
    return combine_ln(h, y, gate.T, ln_g, ln_b, with_bf16=layer + 1 < DEPTH)


FFT_R = 128
HALF_R = FFT_R // 2
HY_GROUP = 4


def _proj_t_kernel(w_ref, x_ref, o_ref, *, scale):
    acc = lax.dot_general(w_ref[...].astype(BF16), x_ref[...], (((0,), (1,)), ((), ())),
                          preferred_element_type=F32)
    o_ref[...] = (acc * scale).astype(o_ref.dtype)


def proj_t(w, col0, ncols, x, out_dtype, scale=1.0, tm=1024, tn=1024):
    k = w.shape[0]
    n = x.shape[0]
    first = col0 // tm
    return pl.pallas_call(
        functools.partial(_proj_t_kernel, scale=scale),
        grid=(ncols // tm, n // tn),
        in_specs=[pl.BlockSpec((k, tm), lambda i, j: (0, first + i)),
                  pl.BlockSpec((tn, k), lambda i, j: (j, 0))],
        out_specs=pl.BlockSpec((tm, tn), lambda i, j: (i, j)),
        out_shape=jax.ShapeDtypeStruct((ncols, n), out_dtype),
        compiler_params=_params("arbitrary", "arbitrary"),
        name="proj_t",
    )(w, x)


def _dft_constants():
    idx = jnp.arange(FFT_R, dtype=jnp.int32)
    prod = idx[:, None] * idx[None, :]
    ang = (prod % FFT_R).astype(F32) * (2.0 * math.pi / FFT_R)
    f_r, f_i = jnp.cos(ang), -jnp.sin(ang)
    ang_t = prod.astype(F32) * (2.0 * math.pi / (FFT_R * FFT_R))
    g_r, g_i = f_r[:HALF_R], -f_i[:HALF_R]
    return dict(
        la=jnp.block([[f_r[:, :HALF_R], -f_i[:, :HALF_R]], [f_i[:, :HALF_R], f_r[:, :HALF_R]]]).astype(BF16),
        la_real=jnp.concatenate([f_r, f_i], axis=0).astype(BF16),
        rb=jnp.block([[f_r, f_i], [-f_i, f_r]]).astype(BF16),
        rc=jnp.block([[f_r, -f_i], [f_i, f_r]]).astype(BF16),
        ld=(jnp.block([[g_r, -g_i], [g_i, g_r]]) / (FFT_R * FFT_R)).astype(BF16),
        t_r=jnp.cos(ang_t), t_i=-jnp.sin(ang_t))


def _rows_to_lanes(x):
    return jnp.concatenate([x[:FFT_R], x[FFT_R:]], axis=1)


def _fwd_lane_stage(a, t_r, t_i, rb):
    a_r, a_i = a[:FFT_R], a[FFT_R:]
    t2_r = jnp.concatenate([t_r, t_r], axis=1)
    t2_i = jnp.concatenate([t_i, t_i], axis=1)
    b_r = a_r * t2_r - a_i * t2_i
    b_i = a_r * t2_i + a_i * t2_r
    lhs = jnp.concatenate([jnp.concatenate([b_r[:, :FFT_R], b_i[:, :FFT_R]], axis=1),
                           jnp.concatenate([b_r[:, FFT_R:], b_i[:, FFT_R:]], axis=1)], axis=0)
    return jnp.dot(lhs.astype(BF16), rb, preferred_element_type=F32)


def _conv_pairs(ms, kfs, la, rb, rc, ld, t_r, t_i):
    a = [jnp.dot(la, jnp.concatenate([m_r, m_i], axis=0).astype(BF16), preferred_element_type=F32)
         for m_r, m_i in ms]
    z = [_fwd_lane_stage(ai, t_r, t_i, rb) for ai in a]
    c = []
    for zi, kf in zip(z, kfs):
        z_r, z_i, k_r, k_i = zi[:, :FFT_R], zi[:, FFT_R:], kf[:, :FFT_R], kf[:, FFT_R:]
        y = jnp.concatenate([z_r * k_r - z_i * k_i, z_r * k_i + z_i * k_r], axis=1)
        c.append(jnp.dot(y.astype(BF16), rc, preferred_element_type=F32))
    t2_r = jnp.concatenate([t_r, t_r], axis=0)
    t2_i = jnp.concatenate([t_i, t_i], axis=0)
    out = []
    for ci in c:
        c_r, c_i = ci[:, :FFT_R], ci[:, FFT_R:]
        d_r = c_r * t2_r + c_i * t2_i
        d_i = c_i * t2_r - c_r * t2_i
        rhs = jnp.concatenate([_rows_to_lanes(d_r), _rows_to_lanes(d_i)], axis=0)
        x = jnp.dot(ld, rhs.astype(BF16), preferred_element_type=F32)
        out.append((x[:HALF_R], x[HALF_R:]))
    return out


def _short_conv_tile(x, w0, w1, w2, b):
    rows = x.shape[0]
    lane = lax.broadcasted_iota(jnp.int32, x.shape, 1)
    row = lax.broadcasted_iota(jnp.int32, x.shape, 0)
    r = pltpu.roll(x, 1, 1)
    prev = jnp.where(lane == 0, jnp.where(row == 0, 0.0, pltpu.roll(r, 1, 0)), r)
    r = pltpu.roll(x, FFT_R - 1, 1)
    nxt = jnp.where(lane == FFT_R - 1, jnp.where(row == rows - 1, 0.0, pltpu.roll(r, rows - 1, 0)), r)
    return w0 * prev + w1 * x + w2 * nxt + b


def _hyena_kernel(cw_ref, cb_ref, skip_ref, v_ref, x1_ref, x2_ref, kf_ref,
                  la_ref, rb_ref, rc_ref, ld_ref, tr_ref, ti_ref, o_ref, *, tc):
    c_base = pl.program_id(0) * tc
    lane = lax.broadcasted_iota(jnp.int32, (1, 2 * FFT_R), 1)

    def conv_in(ref, part, b, c0):
        tiles = []
        for cc in range(2):
            ch = part * HY_CH + c_base + c0 + cc
            tiles.append(_short_conv_tile(ref[c0 + cc, b], cw_ref[0, ch], cw_ref[1, ch], cw_ref[2, ch], cb_ref[ch]))
        return jnp.concatenate(tiles, axis=1)

    def group(g, carry):
        starts = [2 * (HY_GROUP * g + j) for j in range(HY_GROUP)]
        consts = (la_ref[...], rb_ref[...], rc_ref[...], ld_ref[...], tr_ref[...], ti_ref[...])
        zs = [(conv_in(v_ref, 0, 0, c0), conv_in(v_ref, 0, 1, c0)) for c0 in starts]
        for o, g_ref in enumerate((x1_ref, x2_ref)):
            kfs = [kf_ref[o, pl.ds(c0, 2)].reshape(2 * FFT_R, 2 * FFT_R) for c0 in starts]
            ys = _conv_pairs(zs, kfs, *consts)
            nxt = []
            for c0, (z_r, z_i), (y_r, y_i) in zip(starts, zs, ys):
                sk = jnp.where(lane < FFT_R, skip_ref[o, c_base + c0], skip_ref[o, c_base + c0 + 1])
                nxt.append((conv_in(g_ref, o + 1, 0, c0) * (y_r + sk * z_r),
                            conv_in(g_ref, o + 1, 1, c0) * (y_i + sk * z_i)))
            zs = nxt
        for c0, (z_r, z_i) in zip(starts, zs):
            o_ref[c0, 0] = z_r[:, :FFT_R].astype(o_ref.dtype)
            o_ref[c0 + 1, 0] = z_r[:, FFT_R:].astype(o_ref.dtype)
            o_ref[c0, 1] = z_i[:, :FFT_R].astype(o_ref.dtype)
            o_ref[c0 + 1, 1] = z_i[:, FFT_R:].astype(o_ref.dtype)
        return carry

    lax.fori_loop(0, tc // (2 * HY_GROUP), group, 0)


def hyena_mix(u_t, kf, conv_w, conv_b, skip, consts, tc=16):
    rows, n = u_t.shape
    ch = rows // 3
    u5 = u_t.reshape(rows, 2, HALF_R, FFT_R)
    nct = ch // tc
    smem = pl.BlockSpec(memory_space=pltpu.SMEM)
    part = lambda k: pl.BlockSpec((tc, 2, HALF_R, FFT_R), lambda i: (k * nct + i, 0, 0, 0))
    full2 = lambda a: pl.BlockSpec(a.shape, lambda i: (0, 0))
    mats = [consts[k] for k in ("la", "rb", "rc", "ld", "t_r", "t_i")]
    out = pl.pallas_call(
        functools.partial(_hyena_kernel, tc=tc),
        grid=(nct,),
        in_specs=[smem, smem, smem, part(0), part(1), part(2),
                  pl.BlockSpec((2, tc, FFT_R, 2 * FFT_R), lambda i: (0, i, 0, 0))] + [full2(a) for a in mats],
        out_specs=pl.BlockSpec((tc, 2, HALF_R, FFT_R), lambda i: (i, 0, 0, 0)),
        out_shape=jax.ShapeDtypeStruct((ch, 2, HALF_R, FFT_R), BF16),
        compiler_params=_params("arbitrary"),
        name="hyena_mix",
    )(conv_w.astype(F32), conv_b.astype(F32), skip.astype(F32), u5, u5, u5, kf, *mats)
    return out.reshape(ch, n)


def _hyena_filter_kernel(delta_ref, hf_ref, hb_ref, tpos_ref, la_ref, rb_ref, tr_ref, ti_ref, kf_ref, *, tc):
    c_base = pl.program_id(1) * tc

    def taps_of(c0):
        taps = []
        for cc in range(2):
            k = jnp.concatenate([hf_ref[0, 0, c0 + cc], hb_ref[0, 0, c0 + cc]], axis=0)
            k = k * jnp.exp(-tpos_ref[...] * delta_ref[c_base + c0 + cc])
            taps.append(k * lax.rsqrt(jnp.sum(k * k, keepdims=True)))
        return jnp.concatenate(taps, axis=1).astype(BF16)

    def group(g, carry):
        starts = [2 * (HY_GROUP * g + j) for j in range(HY_GROUP)]
        a = [jnp.dot(la_ref[...], taps_of(c0), preferred_element_type=F32) for c0 in starts]
        z = [_fwd_lane_stage(ai, tr_ref[...], ti_ref[...], rb_ref[...]) for ai in a]
        for c0, zi in zip(starts, z):
            kf_ref[0, pl.ds(c0, 2)] = zi.reshape(2, FFT_R, 2 * FFT_R)
        return carry

    lax.fori_loop(0, tc // (2 * HY_GROUP), group, 0)


def hyena_filter_spectra(seq, w1, b1, w2, b2, w3, b3, freq, w4, consts, tc=16):
    t = jnp.linspace(0.0, 1.0, seq, dtype=F32)[:, None]
    omega = 2.0 * math.pi * jnp.arange(seq, dtype=F32)[:, None] / seq
    bands = jnp.linspace(1e-4, HF_BANDS - 1, HF_BANDS, dtype=F32)[None, :]
    ang = omega * bands
    z = jnp.concatenate([t, jnp.cos(ang), -jnp.sin(ang)], -1)
    fr = freq.astype(F32)
    hid = jnp.sin(fr * (z @ w1.astype(F32) + b1.astype(F32)))
    hid = jnp.sin(fr * (hid @ w2.astype(F32) + b2.astype(F32)))
    hid = jnp.sin(fr * (hid @ w3.astype(F32) + b3.astype(F32)))
    back = lambda a: jnp.concatenate([jnp.zeros_like(a[:1]), a[:0:-1]], axis=0)
    w4t = w4.astype(BF16).T
    half = HY_ORDER * HY_CH
    h_fwd = matmul(w4t[:half], hid.T.astype(BF16), F32)
    h_bwd = matmul(w4t[half:], back(hid).T.astype(BF16), F32)
    shape5 = (HY_ORDER, HY_CH, HALF_R, FFT_R)
    tpos = jnp.concatenate([t, back(t)], axis=0).reshape(FFT_R, FFT_R)
    max_decay = math.log(HF_TARGET) / HF_FAST
    min_decay = math.log(HF_TARGET) / HF_SLOW
    deltas = jnp.abs(jnp.linspace(min_decay, max_decay, HY_CH, dtype=F32))
    taps = pl.BlockSpec((1, 1, tc, HALF_R, FFT_R), lambda o, i: (0, o, i, 0, 0))
    full2 = lambda a: pl.BlockSpec(a.shape, lambda o, i: (0, 0))
    mats = [consts[k] for k in ("la_real", "rb", "t_r", "t_i")]
    return pl.pallas_call(
        functools.partial(_hyena_filter_kernel, tc=tc),
        grid=(HY_ORDER, HY_CH // tc),
        in_specs=[pl.BlockSpec(memory_space=pltpu.SMEM), taps, taps, full2(tpos)] + [full2(a) for a in mats],
        out_specs=pl.BlockSpec((1, tc, FFT_R, 2 * FFT_R), lambda o, i: (o, i, 0, 0)),
        out_shape=jax.ShapeDtypeStruct((HY_ORDER, HY_CH, FFT_R, 2 * FFT_R), F32),
        compiler_params=_params("arbitrary", "arbitrary"),
        name="hyena_filter_spectra",
    )(deltas, h_fwd.reshape((1,) + shape5), h_bwd.reshape((1,) + shape5), tpos, *mats)


def kernel(x, ev_w_in, ev_hy_conv_w, ev_hy_conv_b, ev_hf_w1, ev_hf_b1, ev_hf_w2, ev_hf_b2, ev_hf_w3, ev_hf_b3,
           ev_hf_freq, ev_hf_w4, ev_hy_skip, ev_lam_q1, ev_lam_k1, ev_lam_q2, ev_lam_k2, ev_subln_g, ev_w_out,
           od_w_in, od_conv_w, od_conv_b, od_wa, od_ba, od_wx, od_bx, od_lam, od_w_out, ln1_g, ln1_b, ln2_g,
           ln2_b, router_w, router_b, ex_w_gate, ex_w_up, ex_w_down):
    bsz, seq, d = x.shape
    n = bsz * seq
    h = x.reshape(n, d).astype(F32)
    hb = h.astype(BF16)
    for layer in range(DEPTH):
        i = layer // 2
        if layer % 2 == 0:
            w_in = ev_w_in[i]
            c0 = 3 * HY_CH
            u_t = proj_t(w_in, 0, c0, hb, F32)
            qt = proj_t(w_in, c0, DA_QK, hb, BF16, scale=DA_HEAD_DIM ** -0.5)
            k = matmul(hb, w_in, BF16, col0=c0 + DA_QK, ncols=DA_QK)
            vt = proj_t(w_in, c0 + 2 * DA_QK, DA_WIDTH, hb, BF16)
            consts = _dft_constants()
            kf = hyena_filter_spectra(seq, ev_hf_w1[i], ev_hf_b1[i], ev_hf_w2[i], ev_hf_b2[i],
                                      ev_hf_w3[i], ev_hf_b3[i], ev_hf_freq[i], ev_hf_w4[i], consts)
            y_hy = hyena_mix(u_t, kf, ev_hy_conv_w[i], ev_hy_conv_b[i], ev_hy_skip[i], consts)
            lambda_init = 0.8 - 0.6 * math.exp(-0.3 * layer)
            lam = (jnp.exp(jnp.sum(ev_lam_q1[i].astype(F32) * ev_lam_k1[i].astype(F32)))
                   - jnp.exp(jnp.sum(ev_lam_q2[i].astype(F32) * ev_lam_k2[i].astype(F32))) + lambda_init)
            y_da = diff_attention(qt, k.reshape(bsz, seq, DA_QK), vt,
                                  lam, ev_subln_g[i], lambda_init).reshape(n, DA_WIDTH)
            h, hp = even_out_proj(y_hy, y_da, ev_w_out[i].astype(BF16), h, ln1_g[layer], ln1_b[layer])
        else:
            u = matmul(hb, od_w_in[i], BF16)
            yd = rglru_bidir(u.reshape(bsz, seq, 2 * RG_WIDTH), od_conv_w[i], od_conv_b[i], od_wa[i], od_ba[i],
                             od_wx[i], od_bx[i], od_lam[i]).reshape(2, n, RG_WIDTH)
            h, hp = odd_out_proj(u, yd, od_w_out[i].astype(BF16), h, ln1_g[layer], ln1_b[layer])
        h, hb = moe_ffn(h, hp, router_w, router_b, ex_w_gate, ex_w_up, ex_w_down, layer,
                        ln2_g[layer], ln2_b[layer])
    return h.reshape(bsz, seq, d).astype(x.dtype)
```

```python
import functools
import math

import jax
import jax.numpy as jnp
from jax import lax
from jax.experimental import pallas as pl
from jax.experimental.pallas import tpu as pltpu

F32 = jnp.float32
BF16 = jnp.bfloat16

D_MODEL = 2048
DEPTH = 2
HY_CH = D_MODEL // 2
HY_ORDER = 2
HF_BANDS = 16
HF_TARGET = 1e-2
HF_FAST = 0.3
HF_SLOW = 1.5
DA_HEADS = 8
DA_HEAD_DIM = 64
DA_V_DIM = 2 * DA_HEAD_DIM
DA_QK = DA_HEADS * 2 * DA_HEAD_DIM
DA_WIDTH = DA_HEADS * DA_V_DIM
RG_WIDTH = D_MODEL
RG_BLOCKS = 8
RG_BLOCK_W = RG_WIDTH // RG_BLOCKS
RG_C = 8.0
RG_CONV = 4
N_EXPERTS = 32
N_GROUPS = 4
EXPERTS_PER_GROUP = N_EXPERTS // N_GROUPS
TOP_K = 2
D_FF = 512
DN_ALPHA = (2 * DEPTH) ** 0.25
LN_EPS = 1e-5

VMEM_LIMIT_BYTES = 56 * 1024 * 1024
MOE_ROWS = 256


def _params(*sem):
    return pltpu.CompilerParams(dimension_semantics=sem, vmem_limit_bytes=VMEM_LIMIT_BYTES)


def _mm_kernel(x_ref, w_ref, o_ref):
    o_ref[...] = jnp.dot(x_ref[...], w_ref[...].astype(BF16), preferred_element_type=F32).astype(o_ref.dtype)


def matmul(x, w, out_dtype, col0=0, ncols=None, tm=1024, tn=2048):
    m, k = x.shape
    n = w.shape[1] if ncols is None else ncols
    tm, tn = min(tm, m), min(tn, n)
    first = col0 // tn
    return pl.pallas_call(
        _mm_kernel,
        grid=(n // tn, m // tm),
        in_specs=[pl.BlockSpec((tm, k), lambda j, i: (i, 0)),
                  pl.BlockSpec((k, tn), lambda j, i: (0, first + j))],
        out_specs=pl.BlockSpec((tm, tn), lambda j, i: (i, j)),
        out_shape=jax.ShapeDtypeStruct((m, n), out_dtype),
        compiler_params=_params("arbitrary", "arbitrary"),
        name="matmul",
    )(x, w)


def _ln_rows(z, g, b):
    mu = jnp.mean(z, axis=-1, keepdims=True)
    zc = z - mu
    var = jnp.mean(zc * zc, axis=-1, keepdims=True)
    return zc * lax.rsqrt(var + LN_EPS) * g + b


def _pack_halves(y):
    half = y.shape[1] // 2
    hi = pltpu.bitcast(y[:, :half].astype(BF16).astype(F32), jnp.uint32)
    lo = pltpu.bitcast(y[:, half:].astype(BF16).astype(F32), jnp.uint32)
    return hi | (lo >> 16)


def _unpack_halves(p):
    hi = pltpu.bitcast(p & jnp.uint32(0xFFFF0000), F32)
    lo = pltpu.bitcast(p << 16, F32)
    return jnp.concatenate([hi, lo], axis=1).astype(BF16)


def _even_out_kernel(xa_ref, xb_ref, wa_ref, wb_ref, r_ref, g_ref, b_ref, o_ref, op_ref):
    acc = lax.dot_general(xa_ref[...], wa_ref[...], (((0,), (0,)), ((), ())), preferred_element_type=F32)
    acc += jnp.dot(xb_ref[...], wb_ref[...], preferred_element_type=F32)
    y = _ln_rows(DN_ALPHA * r_ref[...] + acc, g_ref[...], b_ref[...])
    o_ref[...] = y
    op_ref[...] = _pack_halves(y)


def even_out_proj(y_hy_t, y_da, w_out, resid, g, b, tm=512):
    ka, m = y_hy_t.shape
    kb = y_da.shape[1]
    d = w_out.shape[1]
    row = lambda i: (i, 0)
    fixed = lambda i: (0, 0)
    return pl.pallas_call(
        _even_out_kernel,
        grid=(m // tm,),
        in_specs=[pl.BlockSpec((ka, tm), lambda i: (0, i)), pl.BlockSpec((tm, kb), row),
                  pl.BlockSpec((ka, d), fixed), pl.BlockSpec((kb, d), fixed),
                  pl.BlockSpec((tm, d), row), pl.BlockSpec((1, d), fixed), pl.BlockSpec((1, d), fixed)],
        out_specs=[pl.BlockSpec((tm, d), row), pl.BlockSpec((tm, d // 2), row)],
        out_shape=[jax.ShapeDtypeStruct((m, d), F32), jax.ShapeDtypeStruct((m, d // 2), jnp.uint32)],
        compiler_params=_params("arbitrary"),
        name="even_out_proj",
    )(y_hy_t, y_da, w_out[:ka], w_out[ka:], resid, g.reshape(1, d), b.reshape(1, d))


def _odd_out_kernel(gate_ref, y0_ref, y1_ref, w_ref, r_ref, g_ref, b_ref, o_ref, op_ref):
    x = jax.nn.gelu(gate_ref[...].astype(F32)) * (y0_ref[0].astype(F32) + y1_ref[0].astype(F32))
    acc = jnp.dot(x.astype(BF16), w_ref[...], preferred_element_type=F32)
    y = _ln_rows(DN_ALPHA * r_ref[...] + acc, g_ref[...], b_ref[...])
    o_ref[...] = y
    op_ref[...] = _pack_halves(y)


def odd_out_proj(u, yd, w_out, resid, g, b, tm=512):
    m = u.shape[0]
    k, d = w_out.shape
    row = lambda i: (i, 0)
    fixed = lambda i: (0, 0)
    return pl.pallas_call(
        _odd_out_kernel,
        grid=(m // tm,),
        in_specs=[pl.BlockSpec((tm, k), row),
                  pl.BlockSpec((1, tm, k), lambda i: (0, i, 0)),
                  pl.BlockSpec((1, tm, k), lambda i: (1, i, 0)),
                  pl.BlockSpec((k, d), fixed),
                  pl.BlockSpec((tm, d), row), pl.BlockSpec((1, d), fixed), pl.BlockSpec((1, d), fixed)],
        out_specs=[pl.BlockSpec((tm, d), row), pl.BlockSpec((tm, d // 2), row)],
        out_shape=[jax.ShapeDtypeStruct((m, d), F32), jax.ShapeDtypeStruct((m, d // 2), jnp.uint32)],
        compiler_params=_params("arbitrary"),
        name="odd_out_proj",
    )(u, yd, yd, w_out, resid, g.reshape(1, d), b.reshape(1, d))


def _combine_ln_kernel(h_ref, y0_ref, y1_ref, gate_ref, g_ref, b_ref, o_ref, *maybe_ob_ref):
    gate = gate_ref[...]
    ffn = gate[:, 0:1] * y0_ref[...].astype(F32) + gate[:, 1:2] * y1_ref[...].astype(F32)
    y = _ln_rows(DN_ALPHA * h_ref[...] + ffn, g_ref[...], b_ref[...])
    o_ref[...] = y
    for ob_ref in maybe_ob_ref:
        ob_ref[...] = y.astype(BF16)


def combine_ln(h, y, gate, g, b, with_bf16, tm=512):
    m, d = h.shape
    row = lambda i: (i, 0)
    fixed = lambda i: (0, 0)
    n_out = 2 if with_bf16 else 1
    out = pl.pallas_call(
        _combine_ln_kernel,
        grid=(m // tm,),
        in_specs=[pl.BlockSpec((tm, d), row), pl.BlockSpec((tm, d), row),
                  pl.BlockSpec((tm, d), lambda i: (m // tm + i, 0)),
                  pl.BlockSpec((tm, TOP_K), row), pl.BlockSpec((1, d), fixed), pl.BlockSpec((1, d), fixed)],
        out_specs=[pl.BlockSpec((tm, d), row)] * n_out,
        out_shape=[jax.ShapeDtypeStruct((m, d), F32), jax.ShapeDtypeStruct((m, d), BF16)][:n_out],
        compiler_params=_params("arbitrary"),
        name="combine_ln",
    )(h, y, y, gate, g.reshape(1, d), b.reshape(1, d))
    return (out[0], out[1]) if with_bf16 else (out[0], None)


ONES_ROWS = 16
POS_SPLIT = 16


def _attn_kernel(slopes_ref, lam_ref, qt_ref, k_ref, vt_ref, fq_ref, fk_ref, bd_ref, g_ref, o_ref, m_s, acc_s, sa_s, sb_s,
                 *, tq, tk, seq, out_scale):
    head = pl.program_id(1)
    i0 = pl.program_id(2) * tq
    slope = slopes_ref[head]
    lam = lam_ref[0]
    qt = qt_ref[...]
    row = lax.broadcasted_iota(jnp.int32, qt.shape, 0)
    zero = jnp.zeros_like(qt)
    q2 = jnp.concatenate([jnp.where(row < DA_HEAD_DIM, qt, zero),
                          jnp.where(row >= DA_HEAD_DIM, qt, zero)], axis=1)
    fq = fq_ref[0]
    fq2 = jnp.concatenate([fq, fq], axis=1)
    q_aug = jnp.concatenate([q2, fq2], axis=0)
    fk = fk_ref[0]
    fk_neg = -fk
    ones = jnp.ones((ONES_ROWS, tk), BF16)
    m_s[...] = jnp.full(m_s.shape, -jnp.inf, F32)
    acc_s[...] = jnp.zeros(acc_s.shape, F32)
    j_diag = i0 // tk

    def update(j, s, shift):
        off = pl.multiple_of(j * tk, tk)
        va = jnp.concatenate([vt_ref[:, pl.ds(off, tk)], ones], axis=0)
        m_prev = m_s[...]
        m_new = jnp.maximum(m_prev, jnp.max(s, axis=0, keepdims=True) - shift)
        p = jnp.exp(s - (m_new + shift))
        alpha = jnp.exp(m_prev - m_new)
        acc_s[...] = alpha * acc_s[...] + jnp.dot(va, p.astype(BF16), preferred_element_type=F32)
        m_s[...] = m_new

    def chunk_of(t):
        return t + (t >= j_diag).astype(jnp.int32)

    def scores(t):
        j = chunk_of(t)
        off = pl.multiple_of(j * tk, tk)
        ka = jnp.concatenate([k_ref[0, pl.ds(off, tk), :], jnp.where(j < j_diag, fk, fk_neg)], axis=1)
        return jnp.dot(ka, q_aug, preferred_element_type=F32)

    def consume(s_ref, t):
        j = chunk_of(t)
        update(j, s_ref[...], slope * jnp.abs(i0 - j * tk).astype(F32))

    j0 = pl.multiple_of(j_diag * tk, tk)
    bias = bd_ref[0]
    s_diag = jnp.dot(k_ref[0, pl.ds(j0, tk), :], q2, preferred_element_type=F32)
    sa_s[...] = s_diag - jnp.concatenate([bias, bias], axis=1)
    sb_s[...] = scores(jnp.int32(0))
    update(j_diag, sa_s[...], 0.0)
    n_off = seq // tk - 1

    def pair(t, carry):
        sa_s[...] = scores(2 * t + 1)
        consume(sb_s, 2 * t)
        sb_s[...] = scores(2 * t + 2)
        consume(sa_s, 2 * t + 1)
        return carry

    lax.fori_loop(0, n_off // 2, pair, 0)
    consume(sb_s, jnp.int32(n_off - 1))

    acc = acc_s[...]
    o1 = acc[:DA_V_DIM, :tq] / acc[DA_V_DIM:DA_V_DIM + 1, :tq]
    o2 = acc[:DA_V_DIM, tq:] / acc[DA_V_DIM:DA_V_DIM + 1, tq:]
    o = o1 - lam * o2
    o = o * lax.rsqrt(jnp.mean(o * o, axis=0, keepdims=True) + LN_EPS) * g_ref[...]
    o_ref[0] = (o * out_scale).T.astype(o_ref.dtype)


def diff_attention(qt, k, vt, lam, subln_g, lambda_init, tq=512, tk=512):
    bsz, seq, width = k.shape
    nq = seq // tq
    slopes = 2.0 ** (-(8.0 / DA_HEADS) * jnp.arange(1, DA_HEADS + 1, dtype=F32))

    def split(n):
        pos = jnp.arange(n, dtype=jnp.int32)
        return (pos // POS_SPLIT * POS_SPLIT).astype(F32), (pos % POS_SPLIT).astype(F32)

    qhi, qlo = split(tq)
    khi, klo = split(tk)
    sl = slopes[:, None]
    fq = jnp.zeros((DA_HEADS, DA_V_DIM, tq), F32)
    fq = fq.at[:, 0].set(1.0).at[:, 1].set(1.0).at[:, 2].set(-sl * qhi).at[:, 3].set(-sl * qlo)
    fk = jnp.zeros((DA_HEADS, tk, DA_V_DIM), F32)
    fk = fk.at[:, :, 0].set(sl * khi).at[:, :, 1].set(sl * klo).at[:, :, 2].set(1.0).at[:, :, 3].set(1.0)
    assert tq == tk, "the chunk holding a query tile's diagonal must be the tile itself"
    pos = jnp.arange(tk, dtype=jnp.int32)
    bias_diag = slopes[:, None, None] * jnp.abs(pos[:, None] - pos[None, :]).astype(F32)
    kern = functools.partial(_attn_kernel, tq=tq, tk=tk, seq=seq, out_scale=1.0 - lambda_init)
    smem = pl.BlockSpec(memory_space=pltpu.SMEM)
    return pl.pallas_call(
        kern,
        grid=(bsz, DA_HEADS, seq // tq),
        in_specs=[smem, smem,
                  pl.BlockSpec((DA_V_DIM, tq), lambda b, h, i: (h, b * nq + i)),
                  pl.BlockSpec((1, seq, DA_V_DIM), lambda b, h, i: (b, 0, h)),
                  pl.BlockSpec((DA_V_DIM, seq), lambda b, h, i: (h, b)),
                  pl.BlockSpec((1, DA_V_DIM, tq), lambda b, h, i: (h, 0, 0)),
                  pl.BlockSpec((1, tk, DA_V_DIM), lambda b, h, i: (h, 0, 0)),
                  pl.BlockSpec((1, tk, tq), lambda b, h, i: (h, 0, 0)),
                  pl.BlockSpec((DA_V_DIM, 1), lambda b, h, i: (0, 0))],
        out_specs=pl.BlockSpec((1, tq, DA_V_DIM), lambda b, h, i: (b, i, h)),
        out_shape=jax.ShapeDtypeStruct((bsz, seq, width), BF16),
        scratch_shapes=[pltpu.VMEM((1, 2 * tq), F32), pltpu.VMEM((DA_V_DIM + ONES_ROWS, 2 * tq), F32),
                        pltpu.VMEM((tk, 2 * tq), F32), pltpu.VMEM((tk, 2 * tq), F32)],
        compiler_params=_params("arbitrary", "arbitrary", "arbitrary"),
        name="diff_attention",
    )(slopes, lam.reshape(1).astype(F32), qt, k, vt, fq.astype(BF16), fk.astype(BF16), bias_diag,
      subln_g.reshape(DA_V_DIM, 1).astype(F32))


HALO = 16


def _rglru_kernel(xr_ref, prev_ref, next_ref, cw_ref, cb_ref, wa_ref, wx_ref, ba_ref, bx_ref, kk_ref, y_ref,
                  a_s, b_s, y_s, h_s, *, ts, nt):
    direction = pl.program_id(0)
    t = pl.program_id(2)
    chunk = t + direction * (nt - 1 - 2 * t)

    @pl.when(t == 0)
    def _():
        h_s[...] = jnp.zeros(h_s.shape, F32)

    xf = jnp.concatenate([jnp.where(chunk == 0, 0.0, prev_ref[0].astype(F32)), xr_ref[0].astype(F32),
                          jnp.where(chunk == nt - 1, 0.0, next_ref[0].astype(F32))], axis=0)
    rows = ts + 2 * HALO

    def tap(j):
        shifted = xf if j == 2 else pltpu.roll(xf, (2 - j) % rows, 0)
        return cw_ref[j:j + 1, :] * shifted[HALO:HALO + ts]

    x = cb_ref[...] + sum(tap(j) for j in range(RG_CONV))
    xb = x.astype(BF16)
    xh = 0.5 * x
    for n in range(RG_BLOCKS):
        cols = slice(n * RG_BLOCK_W, (n + 1) * RG_BLOCK_W)
        xs = xb[:, cols]
        tr = jnp.tanh(jnp.dot(xs, wa_ref[0, n], preferred_element_type=F32) + ba_ref[0, :, cols])
        ti = jnp.tanh(jnp.dot(xs, wx_ref[0, n], preferred_element_type=F32) + bx_ref[0, :, cols])
        kk = kk_ref[0, :, cols]
        a = jnp.exp2(kk * tr + kk)
        a_s[:, cols] = a
        s = 1.0 - a * a
        root = jnp.where(s > 0.0, s * lax.rsqrt(s), 0.0)
        b_s[:, cols] = root * (ti * xh[:, cols] + xh[:, cols])

    def step(k, h):
        row = jnp.where(direction == 0, k, ts - 1 - k)
        h = a_s[pl.ds(row, 1), :] * h + b_s[pl.ds(row, 1), :]
        y_s[pl.ds(row, 1), :] = h
        return h

    h_s[...] = lax.fori_loop(0, ts, step, h_s[...], unroll=8)
    y_ref[0, 0] = y_s[...].astype(y_ref.dtype)


def rglru_bidir(u, conv_w, conv_b, wa, ba, wx, bx, lam, ts=512):
    bsz, seq, w2 = u.shape
    w = w2 // 2
    nt = seq // ts
    per = ts // HALO
    kk = (-0.5 * RG_C * math.log2(math.e)) * jax.nn.softplus(-lam.astype(F32)).reshape(2, 1, w)
    chunk = lambda d, t: t + d * (nt - 1 - 2 * t)
    dmap4 = lambda d, b, t: (d, 0, 0, 0)
    dmap3 = lambda d, b, t: (d, 0, 0)
    fixed = lambda d, b, t: (0, 0)
    return pl.pallas_call(
        functools.partial(_rglru_kernel, ts=ts, nt=nt),
        grid=(2, bsz, nt),
        in_specs=[pl.BlockSpec((1, ts, w), lambda d, b, t: (b, chunk(d, t), 1)),
                  pl.BlockSpec((1, HALO, w), lambda d, b, t: (b, jnp.maximum(chunk(d, t) * per - 1, 0), 1)),
                  pl.BlockSpec((1, HALO, w),
                               lambda d, b, t: (b, jnp.minimum((chunk(d, t) + 1) * per, seq // HALO - 1), 1)),
                  pl.BlockSpec((RG_CONV, w), fixed), pl.BlockSpec((1, w), fixed),
                  pl.BlockSpec((1, RG_BLOCKS, RG_BLOCK_W, RG_BLOCK_W), dmap4),
                  pl.BlockSpec((1, RG_BLOCKS, RG_BLOCK_W, RG_BLOCK_W), dmap4),
                  pl.BlockSpec((1, 1, w), dmap3), pl.BlockSpec((1, 1, w), dmap3), pl.BlockSpec((1, 1, w), dmap3)],
        out_specs=pl.BlockSpec((1, 1, ts, w), lambda d, b, t: (d, b, chunk(d, t), 0)),
        out_shape=jax.ShapeDtypeStruct((2, bsz, seq, w), BF16),
        scratch_shapes=[pltpu.VMEM((ts, w), F32), pltpu.VMEM((ts, w), F32), pltpu.VMEM((ts, w), F32),
                        pltpu.VMEM((1, w), F32)],
        compiler_params=_params("arbitrary", "arbitrary", "arbitrary"),
        name="rglru_bidir",
    )(u, u, u, conv_w.astype(F32), conv_b.astype(F32).reshape(1, w), (0.5 * wa).astype(BF16),
      (0.5 * wx).astype(BF16), 0.5 * ba.astype(F32).reshape(2, 1, w), 0.5 * bx.astype(F32).reshape(2, 1, w), kk)


def _router_kernel(h_ref, wt_ref, b_ref, tri_ref, e_ref, g_ref, r_ref, cnt_ref, seen_s):
    h = h_ref[...]
    h_hi = h.astype(BF16)
    h_lo = (h - h_hi.astype(F32)).astype(BF16)
    nt_dot = lambda a, b: lax.dot_general(a, b, (((1,), (1,)), ((), ())), preferred_element_type=F32)
    logits = nt_dot(wt_ref[0], h_hi) + (nt_dot(wt_ref[0], h_lo) + nt_dot(wt_ref[1], h_hi))
    tm = logits.shape[1]
    s = jax.nn.sigmoid(logits).reshape(N_GROUPS, EXPERTS_PER_GROUP, tm)
    sel = s + b_ref[...]
    idx = lax.broadcasted_iota(jnp.int32, sel.shape, 1)
    big = jnp.int32(EXPERTS_PER_GROUP)
    v1 = jnp.max(sel, axis=1, keepdims=True)
    i1 = jnp.min(jnp.where(sel == v1, idx, big), axis=1, keepdims=True)
    rest = jnp.where(idx == i1, -jnp.inf, sel)
    v2 = jnp.max(rest, axis=1, keepdims=True)
    i2 = jnp.min(jnp.where(rest == v2, idx, big), axis=1, keepdims=True)
    score = v1 + v2
    gidx = lax.broadcasted_iota(jnp.int32, score.shape, 0)
    best = jnp.max(score, axis=0, keepdims=True)
    grp = jnp.min(jnp.where(score == best, gidx, jnp.int32(N_GROUPS)), axis=0, keepdims=True)
    pick = gidx == grp
    l1 = jnp.sum(jnp.where(pick, i1, 0), axis=0)
    l2 = jnp.sum(jnp.where(pick, i2, 0), axis=0)
    s_g = jnp.sum(jnp.where(pick, s, 0.0), axis=0)
    eidx = lax.broadcasted_iota(jnp.int32, s_g.shape, 0)
    w1 = jnp.sum(jnp.where(eidx == l1, s_g, 0.0), axis=0, keepdims=True)
    w2 = jnp.sum(jnp.where(eidx == l2, s_g, 0.0), axis=0, keepdims=True)
    tot = w1 + w2
    base = grp[0] * EXPERTS_PER_GROUP
    e1, e2 = base + l1, base + l2
    e_ref[...] = jnp.concatenate([e1, e2], axis=0)
    g_ref[...] = jnp.concatenate([w1 / tot, w2 / tot], axis=0)

    @pl.when(pl.program_id(0) == 0)
    def _():
        seen_s[...] = jnp.zeros(seen_s.shape, F32)

    eall = lax.broadcasted_iota(jnp.int32, (N_EXPERTS, tm), 0)
    hit1, hit2 = eall == e1, eall == e2
    both = jnp.where(hit1, 1.0, jnp.where(hit2, 1.0, 0.0))
    incl = jnp.dot(both.astype(BF16), tri_ref[...], preferred_element_type=F32)
    before = incl - both + seen_s[...]
    r_ref[...] = jnp.concatenate([jnp.sum(jnp.where(hit1, before, 0.0), axis=0, keepdims=True),
                                  jnp.sum(jnp.where(hit2, before, 0.0), axis=0, keepdims=True)],
                                 axis=0).astype(jnp.int32)
    seen_s[...] = seen_s[...] + incl[:, tm - 1:tm]
    cnt_ref[...] = seen_s[...].astype(jnp.int32)


def route(h, router_w, router_b, tm=1024):
    n, d = h.shape
    tri = (jnp.arange(tm)[:, None] <= jnp.arange(tm)[None, :]).astype(BF16)
    wt = router_w.astype(F32).T
    wt_hi = wt.astype(BF16)
    wt_lo = (wt - wt_hi.astype(F32)).astype(BF16)
    tile = pl.BlockSpec((TOP_K, tm), lambda i: (0, i))
    return pl.pallas_call(
        _router_kernel,
        grid=(n // tm,),
        in_specs=[pl.BlockSpec((tm, d), lambda i: (i, 0)),
                  pl.BlockSpec((2, N_EXPERTS, d), lambda i: (0, 0, 0)),
                  pl.BlockSpec((N_GROUPS, EXPERTS_PER_GROUP, 1), lambda i: (0, 0, 0)),
                  pl.BlockSpec((tm, tm), lambda i: (0, 0))],
        out_specs=[tile, tile, tile, pl.BlockSpec((N_EXPERTS, 1), lambda i: (0, 0))],
        out_shape=[jax.ShapeDtypeStruct((TOP_K, n), jnp.int32), jax.ShapeDtypeStruct((TOP_K, n), F32),
                   jax.ShapeDtypeStruct((TOP_K, n), jnp.int32), jax.ShapeDtypeStruct((N_EXPERTS, 1), jnp.int32)],
        scratch_shapes=[pltpu.VMEM((N_EXPERTS, 1), F32)],
        compiler_params=_params("arbitrary"),
        name="router",
    )(h, jnp.stack([wt_hi, wt_lo]), router_b.astype(F32).reshape(N_GROUPS, EXPERTS_PER_GROUP, 1), tri)


def _experts_kernel(blk_exp_ref, nxt_exp_ref, slot_ref, n_used_ref, x_ref, wg_hbm, wu_hbm, wd_hbm, o_ref,
                    wg_f, wu_f, wd_f, wg_s, wu_s, wd_s, sem, *, layer):
    i = pl.program_id(0)
    expert = blk_exp_ref[i]
    prev = blk_exp_ref[jnp.maximum(i - 1, 0)]
    slot = slot_ref[i]

    def fetch(e, s):
        return [pltpu.make_async_copy(src.at[layer, e], dst.at[s], sem.at[s])
                for src, dst in ((wg_hbm, wg_f), (wu_hbm, wu_f), (wd_hbm, wd_f))]

    @pl.when(i == 0)
    def _():
        for copy in fetch(expert, slot):
            copy.start()

    @pl.when((i == 0) | (expert != prev))
    def _():
        for copy in fetch(expert, slot):
            copy.wait()
        wg_s[...] = wg_f[slot].astype(BF16)
        wu_s[...] = wu_f[slot].astype(BF16)
        wd_s[...] = wd_f[slot].astype(BF16)

        @pl.when(nxt_exp_ref[i] >= 0)
        def _():
            for copy in fetch(nxt_exp_ref[i], 1 - slot):
                copy.start()

    @pl.when(i < n_used_ref[0])
    def _():
        x = _unpack_halves(x_ref[...])
        hg = jnp.dot(x, wg_s[...], preferred_element_type=F32)
        hu = jnp.dot(x, wu_s[...], preferred_element_type=F32)
        hidden = (jax.nn.silu(hg) * hu).astype(BF16)
        o_ref[...] = jnp.dot(hidden, wd_s[...], preferred_element_type=F32).astype(o_ref.dtype)

    @pl.when(i >= n_used_ref[0])
    def _():
        o_ref[...] = jnp.zeros(o_ref.shape, o_ref.dtype)


def experts(xs, blk_exp, n_used, w_gate, w_up, w_down, layer):
    p = xs.shape[0]
    n_blocks = p // MOE_ROWS
    d = w_gate.shape[2]
    f = w_gate.shape[3]
    idx = jnp.arange(n_blocks, dtype=jnp.int32)
    starts = jnp.concatenate([jnp.ones((1,), bool), blk_exp[1:] != blk_exp[:-1]])
    slot = (jnp.cumsum(starts.astype(jnp.int32)) - 1) % 2
    later = starts[None, :] & (idx[None, :] > idx[:, None])
    first_later = jnp.min(jnp.where(later, idx[None, :], n_blocks), axis=1)
    nxt_exp = jnp.sum(jnp.where(idx[None, :] == first_later[:, None], blk_exp[None, :], 0), axis=1)
    nxt_exp = jnp.where(first_later < n_blocks, nxt_exp, -1).astype(jnp.int32)
    hbm = pl.BlockSpec(memory_space=pl.ANY)
    grid_spec = pltpu.PrefetchScalarGridSpec(
        num_scalar_prefetch=4,
        grid=(n_blocks,),
        in_specs=[pl.BlockSpec((MOE_ROWS, d // 2), lambda i, *_: (i, 0)), hbm, hbm, hbm],
        out_specs=pl.BlockSpec((MOE_ROWS, d), lambda i, *_: (i, 0)),
        scratch_shapes=[pltpu.VMEM((2, d, f), F32), pltpu.VMEM((2, d, f), F32), pltpu.VMEM((2, f, d), F32),
                        pltpu.VMEM((d, f), BF16), pltpu.VMEM((d, f), BF16), pltpu.VMEM((f, d), BF16),
                        pltpu.SemaphoreType.DMA((2,))],
    )
    return pl.pallas_call(
        functools.partial(_experts_kernel, layer=layer),
        grid_spec=grid_spec,
        out_shape=jax.ShapeDtypeStruct((p, d), BF16),
        compiler_params=_params("arbitrary"),
        name="experts",
    )(blk_exp, nxt_exp, slot.astype(jnp.int32), n_used, xs, w_gate, w_up, w_down)


def _dispatch_kernel(dest_ref, fill_ref, x_ref, xs_ref, zero_s, fill_sem, row_sem, *, tm, n_blocks):
    base = pl.program_id(0) * tm

    @pl.when(pl.program_id(0) == 0)
    def _():
        zero_s[...] = jnp.zeros(zero_s.shape, zero_s.dtype)

        def zero_block(blk):
            return pltpu.make_async_copy(zero_s, xs_ref.at[pl.ds(pl.multiple_of(blk * MOE_ROWS, MOE_ROWS),
                                                                 MOE_ROWS)], fill_sem)

        def start(blk, carry):
            @pl.when(fill_ref[blk] != 0)
            def _():
                zero_block(blk).start()
            return carry

        def wait(blk, carry):
            @pl.when(fill_ref[blk] != 0)
            def _():
                zero_block(blk).wait()
            return carry

        lax.fori_loop(0, n_blocks, start, 0)
        lax.fori_loop(0, n_blocks, wait, 0)

    def issue(r, carry):
        for k in range(TOP_K):
            pltpu.make_async_copy(x_ref.at[pl.ds(r, 1)], xs_ref.at[pl.ds(dest_ref[k, base + r], 1)],
                                  row_sem).start()
        return carry

    lax.fori_loop(0, tm, issue, 0, unroll=4)
    for k in range(TOP_K):
        pltpu.make_async_copy(x_ref, xs_ref.at[pl.ds(0, tm)], row_sem).wait()


def dispatch_rows(x, dest, fill, tm=512):
    n, d = x.shape
    n_blocks = fill.shape[0]
    grid_spec = pltpu.PrefetchScalarGridSpec(
        num_scalar_prefetch=2,
        grid=(n // tm,),
        in_specs=[pl.BlockSpec((tm, d), lambda i, dst, fl: (i, 0))],
        out_specs=pl.BlockSpec(memory_space=pl.ANY),
        scratch_shapes=[pltpu.VMEM((MOE_ROWS, d), x.dtype), pltpu.SemaphoreType.DMA(()),
                        pltpu.SemaphoreType.DMA(())],
    )
    return pl.pallas_call(
        functools.partial(_dispatch_kernel, tm=tm, n_blocks=n_blocks),
        grid_spec=grid_spec,
        out_shape=jax.ShapeDtypeStruct((n_blocks * MOE_ROWS, d), x.dtype),
        compiler_params=_params("arbitrary"),
        name="dispatch_rows",
    )(dest, fill, x)


def moe_ffn(h, hp, router_w, router_b, w_gate, w_up, w_down, layer, ln_g, ln_b):
    n, d = h.shape
    a = n * TOP_K
    e_idx, gate, rank, counts = route(h, router_w, router_b)
    counts = counts[:, 0]
    padded = (counts + MOE_ROWS - 1) // MOE_ROWS * MOE_ROWS
    pad_end = jnp.cumsum(padded)
    pad_start = pad_end - padded
    experts_iota = jnp.arange(N_EXPERTS, dtype=jnp.int32)[:, None, None]
    dest = jnp.sum(jnp.where(e_idx[None] == experts_iota, pad_start[:, None, None], 0), axis=0) + rank
    n_blocks = a // MOE_ROWS + N_EXPERTS
    blk_start = jnp.arange(n_blocks, dtype=jnp.int32)[:, None] * MOE_ROWS
    blk_exp = jnp.minimum(jnp.sum((pad_end[None, :] <= blk_start).astype(jnp.int32), axis=1), N_EXPERTS - 1)
    n_used = (pad_end[-1:] // MOE_ROWS).astype(jnp.int32)
    ends_expert = jnp.any((pad_end[None, :] == blk_start + MOE_ROWS) & (padded[None, :] > 0), axis=1)
    fill = (ends_expert | (blk_start[:, 0] >= pad_end[-1])).astype(jnp.int32)
    xs = dispatch_rows(hp, dest, fill)
    yb = experts(xs, blk_exp, n_used, w_gate, w_up, w_down, layer)
    y = yb.at[dest.reshape(a)].get(mode="promise_in_bounds")
    return combine_ln(h, y, gate.T, ln_g, ln_b, with_bf16=layer + 1 < DEPTH)


FFT_R = 128
HALF_R = FFT_R // 2
HY_GROUP = 4


def _proj_t_kernel(w_ref, x_ref, o_ref, *, scale):
    acc = lax.dot_general(w_ref[...].astype(BF16), x_ref[...], (((0,), (1,)), ((), ())),
                          preferred_element_type=F32)
    o_ref[...] = (acc * scale).astype(o_ref.dtype)


def proj_t(w, col0, ncols, x, out_dtype, scale=1.0, tm=1024, tn=1024):
    k = w.shape[0]
    n = x.shape[0]
    first = col0 // tm
    return pl.pallas_call(
        functools.partial(_proj_t_kernel, scale=scale),
        grid=(ncols // tm, n // tn),
        in_specs=[pl.BlockSpec((k, tm), lambda i, j: (0, first + i)),
                  pl.BlockSpec((tn, k), lambda i, j: (j, 0))],
        out_specs=pl.BlockSpec((tm, tn), lambda i, j: (i, j)),
        out_shape=jax.ShapeDtypeStruct((ncols, n), out_dtype),
        compiler_params=_params("arbitrary", "arbitrary"),
        name="proj_t",
    )(w, x)


def _dft_constants():
    idx = jnp.arange(FFT_R, dtype=jnp.int32)
    prod = idx[:, None] * idx[None, :]
    ang = (prod % FFT_R).astype(F32) * (2.0 * math.pi / FFT_R)
    f_r, f_i = jnp.cos(ang), -jnp.sin(ang)
    ang_t = prod.astype(F32) * (2.0 * math.pi / (FFT_R * FFT_R))
    g_r, g_i = f_r[:HALF_R], -f_i[:HALF_R]
    return dict(
        la=jnp.block([[f_r[:, :HALF_R], -f_i[:, :HALF_R]], [f_i[:, :HALF_R], f_r[:, :HALF_R]]]).astype(BF16),
        la_real=jnp.concatenate([f_r, f_i], axis=0).astype(BF16),
        rb=jnp.block([[f_r, f_i], [-f_i, f_r]]).astype(BF16),
        rc=jnp.block([[f_r, -f_i], [f_i, f_r]]).astype(BF16),
        ld=(jnp.block([[g_r, -g_i], [g_i, g_r]]) / (FFT_R * FFT_R)).astype(BF16),
        t_r=jnp.cos(ang_t), t_i=-jnp.sin(ang_t))


def _rows_to_lanes(x):
    return jnp.concatenate([x[:FFT_R], x[FFT_R:]], axis=1)


def _fwd_lane_stage(a, t_r, t_i, rb):
    a_r, a_i = a[:FFT_R], a[FFT_R:]
    t2_r = jnp.concatenate([t_r, t_r], axis=1)
    t2_i = jnp.concatenate([t_i, t_i], axis=1)
    b_r = a_r * t2_r - a_i * t2_i
    b_i = a_r * t2_i + a_i * t2_r
    lhs = jnp.concatenate([jnp.concatenate([b_r[:, :FFT_R], b_i[:, :FFT_R]], axis=1),
                           jnp.concatenate([b_r[:, FFT_R:], b_i[:, FFT_R:]], axis=1)], axis=0)
    return jnp.dot(lhs.astype(BF16), rb, preferred_element_type=F32)


def _conv_pairs(ms, kfs, la, rb, rc, ld, t_r, t_i):
    a = [jnp.dot(la, jnp.concatenate([m_r, m_i], axis=0).astype(BF16), preferred_element_type=F32)
         for m_r, m_i in ms]
    z = [_fwd_lane_stage(ai, t_r, t_i, rb) for ai in a]
    c = []
    for zi, kf in zip(z, kfs):
        z_r, z_i, k_r, k_i = zi[:, :FFT_R], zi[:, FFT_R:], kf[:, :FFT_R], kf[:, FFT_R:]
        y = jnp.concatenate([z_r * k_r - z_i * k_i, z_r * k_i + z_i * k_r], axis=1)
        c.append(jnp.dot(y.astype(BF16), rc, preferred_element_type=F32))
    t2_r = jnp.concatenate([t_r, t_r], axis=0)
    t2_i = jnp.concatenate([t_i, t_i], axis=0)
    out = []
    for ci in c:
        c_r, c_i = ci[:, :FFT_R], ci[:, FFT_R:]
        d_r = c_r * t2_r + c_i * t2_i
        d_i = c_i * t2_r - c_r * t2_i
        rhs = jnp.concatenate([_rows_to_lanes(d_r), _rows_to_lanes(d_i)], axis=0)
        x = jnp.dot(ld, rhs.astype(BF16), preferred_element_type=F32)
        out.append((x[:HALF_R], x[HALF_R:]))
    return out


def _short_conv_tile(x, w0, w1, w2, b):
    rows = x.shape[0]
    lane = lax.broadcasted_iota(jnp.int32, x.shape, 1)
    row = lax.broadcasted_iota(jnp.int32, x.shape, 0)
    r = pltpu.roll(x, 1, 1)
    prev = jnp.where(lane == 0, jnp.where(row == 0, 0.0, pltpu.roll(r, 1, 0)), r)
    r = pltpu.roll(x, FFT_R - 1, 1)
    nxt = jnp.where(lane == FFT_R - 1, jnp.where(row == rows - 1, 0.0, pltpu.roll(r, rows - 1, 0)), r)
    return w0 * prev + w1 * x + w2 * nxt + b


def _hyena_kernel(cw_ref, cb_ref, skip_ref, v_ref, x1_ref, x2_ref, kf_ref,
                  la_ref, rb_ref, rc_ref, ld_ref, tr_ref, ti_ref, o_ref, *, tc):
    c_base = pl.program_id(0) * tc
    lane = lax.broadcasted_iota(jnp.int32, (1, 2 * FFT_R), 1)

    def conv_in(ref, part, b, c0):
        tiles = []
        for cc in range(2):
            ch = part * HY_CH + c_base + c0 + cc
            tiles.append(_short_conv_tile(ref[c0 + cc, b], cw_ref[0, ch], cw_ref[1, ch], cw_ref[2, ch], cb_ref[ch]))
        return jnp.concatenate(tiles, axis=1)

    def group(g, carry):
        starts = [2 * (HY_GROUP * g + j) for j in range(HY_GROUP)]
        consts = (la_ref[...], rb_ref[...], rc_ref[...], ld_ref[...], tr_ref[...], ti_ref[...])
        zs = [(conv_in(v_ref, 0, 0, c0), conv_in(v_ref, 0, 1, c0)) for c0 in starts]
        for o, g_ref in enumerate((x1_ref, x2_ref)):
            kfs = [kf_ref[o, pl.ds(c0, 2)].reshape(2 * FFT_R, 2 * FFT_R) for c0 in starts]
            ys = _conv_pairs(zs, kfs, *consts)
            nxt = []
            for c0, (z_r, z_i), (y_r, y_i) in zip(starts, zs, ys):
                sk = jnp.where(lane < FFT_R, skip_ref[o, c_base + c0], skip_ref[o, c_base + c0 + 1])
                nxt.append((conv_in(g_ref, o + 1, 0, c0) * (y_r + sk * z_r),
                            conv_in(g_ref, o + 1, 1, c0) * (y_i + sk * z_i)))
            zs = nxt
        for c0, (z_r, z_i) in zip(starts, zs):
            o_ref[c0, 0] = z_r[:, :FFT_R].astype(o_ref.dtype)
            o_ref[c0 + 1, 0] = z_r[:, FFT_R:].astype(o_ref.dtype)
            o_ref[c0, 1] = z_i[:, :FFT_R].astype(o_ref.dtype)
            o_ref[c0 + 1, 1] = z_i[:, FFT_R:].astype(o_ref.dtype)
        return carry

    lax.fori_loop(0, tc // (2 * HY_GROUP), group, 0)


def hyena_mix(u_t, kf, conv_w, conv_b, skip, consts, tc=32):
    rows, n = u_t.shape
    ch = rows // 3
    u5 = u_t.reshape(rows, 2, HALF_R, FFT_R)
    nct = ch // tc
    smem = pl.BlockSpec(memory_space=pltpu.SMEM)
    part = lambda k: pl.BlockSpec((tc, 2, HALF_R, FFT_R), lambda i: (k * nct + i, 0, 0, 0))
    full2 = lambda a: pl.BlockSpec(a.shape, lambda i: (0, 0))
    mats = [consts[k] for k in ("la", "rb", "rc", "ld", "t_r", "t_i")]
    out = pl.pallas_call(
        functools.partial(_hyena_kernel, tc=tc),
        grid=(nct,),
        in_specs=[smem, smem, smem, part(0), part(1), part(2),
                  pl.BlockSpec((2, tc, FFT_R, 2 * FFT_R), lambda i: (0, i, 0, 0))] + [full2(a) for a in mats],
        out_specs=pl.BlockSpec((tc, 2, HALF_R, FFT_R), lambda i: (i, 0, 0, 0)),
        out_shape=jax.ShapeDtypeStruct((ch, 2, HALF_R, FFT_R), BF16),
        compiler_params=_params("arbitrary"),
        name="hyena_mix",
    )(conv_w.astype(F32), conv_b.astype(F32), skip.astype(F32), u5, u5, u5, kf, *mats)
    return out.reshape(ch, n)


def _hyena_filter_kernel(delta_ref, hf_ref, hb_ref, tpos_ref, la_ref, rb_ref, tr_ref, ti_ref, kf_ref, *, tc):
    c_base = pl.program_id(1) * tc

    def taps_of(c0):
        taps = []
        for cc in range(2):
            k = jnp.concatenate([hf_ref[0, 0, c0 + cc], hb_ref[0, 0, c0 + cc]], axis=0)
            k = k * jnp.exp(-tpos_ref[...] * delta_ref[c_base + c0 + cc])
            taps.append(k * lax.rsqrt(jnp.sum(k * k, keepdims=True)))
        return jnp.concatenate(taps, axis=1).astype(BF16)

    def group(g, carry):
        starts = [2 * (HY_GROUP * g + j) for j in range(HY_GROUP)]
        a = [jnp.dot(la_ref[...], taps_of(c0), preferred_element_type=F32) for c0 in starts]
        z = [_fwd_lane_stage(ai, tr_ref[...], ti_ref[...], rb_ref[...]) for ai in a]
        for c0, zi in zip(starts, z):
            kf_ref[0, pl.ds(c0, 2)] = zi.reshape(2, FFT_R, 2 * FFT_R)
        return carry

    lax.fori_loop(0, tc // (2 * HY_GROUP), group, 0)


def hyena_filter_spectra(seq, w1, b1, w2, b2, w3, b3, freq, w4, consts, tc=16):
    t = jnp.linspace(0.0, 1.0, seq, dtype=F32)[:, None]
    omega = 2.0 * math.pi * jnp.arange(seq, dtype=F32)[:, None] / seq
    bands = jnp.linspace(1e-4, HF_BANDS - 1, HF_BANDS, dtype=F32)[None, :]
    ang = omega * bands
    z = jnp.concatenate([t, jnp.cos(ang), -jnp.sin(ang)], -1)
    fr = freq.astype(F32)
    hid = jnp.sin(fr * (z @ w1.astype(F32) + b1.astype(F32)))
    hid = jnp.sin(fr * (hid @ w2.astype(F32) + b2.astype(F32)))
    hid = jnp.sin(fr * (hid @ w3.astype(F32) + b3.astype(F32)))
    back = lambda a: jnp.concatenate([jnp.zeros_like(a[:1]), a[:0:-1]], axis=0)
    w4t = w4.astype(BF16).T
    half = HY_ORDER * HY_CH
    h_fwd = matmul(w4t[:half], hid.T.astype(BF16), F32)
    h_bwd = matmul(w4t[half:], back(hid).T.astype(BF16), F32)
    shape5 = (HY_ORDER, HY_CH, HALF_R, FFT_R)
    tpos = jnp.concatenate([t, back(t)], axis=0).reshape(FFT_R, FFT_R)
    max_decay = math.log(HF_TARGET) / HF_FAST
    min_decay = math.log(HF_TARGET) / HF_SLOW
    deltas = jnp.abs(jnp.linspace(min_decay, max_decay, HY_CH, dtype=F32))
    taps = pl.BlockSpec((1, 1, tc, HALF_R, FFT_R), lambda o, i: (0, o, i, 0, 0))
    full2 = lambda a: pl.BlockSpec(a.shape, lambda o, i: (0, 0))
    mats = [consts[k] for k in ("la_real", "rb", "t_r", "t_i")]
    return pl.pallas_call(
        functools.partial(_hyena_filter_kernel, tc=tc),
        grid=(HY_ORDER, HY_CH // tc),
        in_specs=[pl.BlockSpec(memory_space=pltpu.SMEM), taps, taps, full2(tpos)] + [full2(a) for a in mats],
        out_specs=pl.BlockSpec((1, tc, FFT_R, 2 * FFT_R), lambda o, i: (o, i, 0, 0)),
        out_shape=jax.ShapeDtypeStruct((HY_ORDER, HY_CH, FFT_R, 2 * FFT_R), F32),
        compiler_params=_params("arbitrary", "arbitrary"),
        name="hyena_filter_spectra",
    )(deltas, h_fwd.reshape((1,) + shape5), h_bwd.reshape((1,) + shape5), tpos, *mats)


def kernel(x, ev_w_in, ev_hy_conv_w, ev_hy_conv_b, ev_hf_w1, ev_hf_b1, ev_hf_w2, ev_hf_b2, ev_hf_w3, ev_hf_b3,
           ev_hf_freq, ev_hf_w4, ev_hy_skip, ev_lam_q1, ev_lam_k1, ev_lam_q2, ev_lam_k2, ev_subln_g, ev_w_out,
           od_w_in, od_conv_w, od_conv_b, od_wa, od_ba, od_wx, od_bx, od_lam, od_w_out, ln1_g, ln1_b, ln2_g,
           ln2_b, router_w, router_b, ex_w_gate, ex_w_up, ex_w_down):
    bsz, seq, d = x.shape
    n = bsz * seq
    h = x.reshape(n, d).astype(F32)
    hb = h.astype(BF16)
    for layer in range(DEPTH):
        i = layer // 2
        if layer % 2 == 0:
            w_in = ev_w_in[i]
            c0 = 3 * HY_CH
            u_t = proj_t(w_in, 0, c0, hb, F32)
            qt = proj_t(w_in, c0, DA_QK, hb, BF16, scale=DA_HEAD_DIM ** -0.5)
            k = matmul(hb, w_in, BF16, col0=c0 + DA_QK, ncols=DA_QK)
            vt = proj_t(w_in, c0 + 2 * DA_QK, DA_WIDTH, hb, BF16)
            consts = _dft_constants()
            kf = hyena_filter_spectra(seq, ev_hf_w1[i], ev_hf_b1[i], ev_hf_w2[i], ev_hf_b2[i],
                                      ev_hf_w3[i], ev_hf_b3[i], ev_hf_freq[i], ev_hf_w4[i], consts)
            y_hy = hyena_mix(u_t, kf, ev_hy_conv_w[i], ev_hy_conv_b[i], ev_hy_skip[i], consts)
            lambda_init = 0.8 - 0.6 * math.exp(-0.3 * layer)
            lam = (jnp.exp(jnp.sum(ev_lam_q1[i].astype(F32) * ev_lam_k1[i].astype(F32)))
                   - jnp.exp(jnp.sum(ev_lam_q2[i].astype(F32) * ev_lam_k2[i].astype(F32))) + lambda_init)
            y_da = diff_attention(qt, k.reshape(bsz, seq, DA_QK), vt,
                                  lam, ev_subln_g[i], lambda_init).reshape(n, DA_WIDTH)
            h, hp = even_out_proj(y_hy, y_da, ev_w_out[i].astype(BF16), h, ln1_g[layer], ln1_b[layer])
        else:
            u = matmul(hb, od_w_in[i], BF16)
            yd = rglru_bidir(u.reshape(bsz, seq, 2 * RG_WIDTH), od_conv_w[i], od_conv_b[i], od_wa[i], od_ba[i],
                             od_wx[i], od_bx[i], od_lam[i]).reshape(2, n, RG_WIDTH)
            h, hp = odd_out_proj(u, yd, od_w_out[i].astype(BF16), h, ln1_g[layer], ln1_b[layer])
        h, hb = moe_ffn(h, hp, router_w, router_b, ex_w_gate, ex_w_up, ex_w_down, layer,
                        ln2_g[layer], ln2_b[layer])
    return h.reshape(bsz, seq, d).astype(x.dtype)
```

```python
import functools
import math

import jax
import jax.numpy as jnp
from jax import lax
from jax.experimental import pallas as pl
from jax.experimental.pallas import tpu as pltpu

F32 = jnp.float32
BF16 = jnp.bfloat16

D_MODEL = 2048
DEPTH = 2
HY_CH = D_MODEL // 2
HY_ORDER = 2
HF_BANDS = 16
HF_TARGET = 1e-2
HF_FAST = 0.3
HF_SLOW = 1.5
DA_HEADS = 8
DA_HEAD_DIM = 64
DA_V_DIM = 2 * DA_HEAD_DIM
DA_QK = DA_HEADS * 2 * DA_HEAD_DIM
DA_WIDTH = DA_HEADS * DA_V_DIM
RG_WIDTH = D_MODEL
RG_BLOCKS = 8
RG_BLOCK_W = RG_WIDTH // RG_BLOCKS
RG_C = 8.0
RG_CONV = 4
N_EXPERTS = 32
N_GROUPS = 4
EXPERTS_PER_GROUP = N_EXPERTS // N_GROUPS
TOP_K = 2
D_FF = 512
DN_ALPHA = (2 * DEPTH) ** 0.25
LN_EPS = 1e-5

VMEM_LIMIT_BYTES = 56 * 1024 * 1024
MOE_ROWS = 256


def _params(*sem):
    return pltpu.CompilerParams(dimension_semantics=sem, vmem_limit_bytes=VMEM_LIMIT_BYTES)


def _mm_kernel(x_ref, w_ref, o_ref):
    o_ref[...] = jnp.dot(x_ref[...], w_ref[...].astype(BF16), preferred_element_type=F32).astype(o_ref.dtype)


def matmul(x, w, out_dtype, col0=0, ncols=None, tm=1024, tn=2048):
    m, k = x.shape
    n = w.shape[1] if ncols is None else ncols
    tm, tn = min(tm, m), min(tn, n)
    first = col0 // tn
    return pl.pallas_call(
        _mm_kernel,
        grid=(n // tn, m // tm),
        in_specs=[pl.BlockSpec((tm, k), lambda j, i: (i, 0)),
                  pl.BlockSpec((k, tn), lambda j, i: (0, first + j))],
        out_specs=pl.BlockSpec((tm, tn), lambda j, i: (i, j)),
        out_shape=jax.ShapeDtypeStruct((m, n), out_dtype),
        compiler_params=_params("arbitrary", "arbitrary"),
        name="matmul",
    )(x, w)


def _ln_rows(z, g, b):
    mu = jnp.mean(z, axis=-1, keepdims=True)
    zc = z - mu
    var = jnp.mean(zc * zc, axis=-1, keepdims=True)
    return zc * lax.rsqrt(var + LN_EPS) * g + b


def _pack_halves(y):
    half = y.shape[1] // 2
    hi = pltpu.bitcast(y[:, :half].astype(BF16).astype(F32), jnp.uint32)
    lo = pltpu.bitcast(y[:, half:].astype(BF16).astype(F32), jnp.uint32)
    return hi | (lo >> 16)


def _unpack_halves(p):
    hi = pltpu.bitcast(p & jnp.uint32(0xFFFF0000), F32)
    lo = pltpu.bitcast(p << 16, F32)
    return jnp.concatenate([hi, lo], axis=1).astype(BF16)


def _even_out_kernel(xa_ref, xb_ref, wa_ref, wb_ref, r_ref, g_ref, b_ref, o_ref, op_ref):
    acc = lax.dot_general(xa_ref[...], wa_ref[...], (((0,), (0,)), ((), ())), preferred_element_type=F32)
    acc += jnp.dot(xb_ref[...], wb_ref[...], preferred_element_type=F32)
    y = _ln_rows(DN_ALPHA * r_ref[...] + acc, g_ref[...], b_ref[...])
    o_ref[...] = y
    op_ref[...] = _pack_halves(y)


def even_out_proj(y_hy_t, y_da, w_out, resid, g, b, tm=512):
    ka, m = y_hy_t.shape
    kb = y_da.shape[1]
    d = w_out.shape[1]
    row = lambda i: (i, 0)
    fixed = lambda i: (0, 0)
    return pl.pallas_call(
        _even_out_kernel,
        grid=(m // tm,),
        in_specs=[pl.BlockSpec((ka, tm), lambda i: (0, i)), pl.BlockSpec((tm, kb), row),
                  pl.BlockSpec((ka, d), fixed), pl.BlockSpec((kb, d), fixed),
                  pl.BlockSpec((tm, d), row), pl.BlockSpec((1, d), fixed), pl.BlockSpec((1, d), fixed)],
        out_specs=[pl.BlockSpec((tm, d), row), pl.BlockSpec((tm, d // 2), row)],
        out_shape=[jax.ShapeDtypeStruct((m, d), F32), jax.ShapeDtypeStruct((m, d // 2), jnp.uint32)],
        compiler_params=_params("arbitrary"),
        name="even_out_proj",
    )(y_hy_t, y_da, w_out[:ka], w_out[ka:], resid, g.reshape(1, d), b.reshape(1, d))


def _odd_out_kernel(gate_ref, y0_ref, y1_ref, w_ref, r_ref, g_ref, b_ref, o_ref, op_ref):
    x = jax.nn.gelu(gate_ref[...].astype(F32)) * (y0_ref[0].astype(F32) + y1_ref[0].astype(F32))
    acc = jnp.dot(x.astype(BF16), w_ref[...], preferred_element_type=F32)
    y = _ln_rows(DN_ALPHA * r_ref[...] + acc, g_ref[...], b_ref[...])
    o_ref[...] = y
    op_ref[...] = _pack_halves(y)


def odd_out_proj(u, yd, w_out, resid, g, b, tm=512):
    m = u.shape[0]
    k, d = w_out.shape
    row = lambda i: (i, 0)
    fixed = lambda i: (0, 0)
    return pl.pallas_call(
        _odd_out_kernel,
        grid=(m // tm,),
        in_specs=[pl.BlockSpec((tm, k), row),
                  pl.BlockSpec((1, tm, k), lambda i: (0, i, 0)),
                  pl.BlockSpec((1, tm, k), lambda i: (1, i, 0)),
                  pl.BlockSpec((k, d), fixed),
                  pl.BlockSpec((tm, d), row), pl.BlockSpec((1, d), fixed), pl.BlockSpec((1, d), fixed)],
        out_specs=[pl.BlockSpec((tm, d), row), pl.BlockSpec((tm, d // 2), row)],
        out_shape=[jax.ShapeDtypeStruct((m, d), F32), jax.ShapeDtypeStruct((m, d // 2), jnp.uint32)],
        compiler_params=_params("arbitrary"),
        name="odd_out_proj",
    )(u, yd, yd, w_out, resid, g.reshape(1, d), b.reshape(1, d))


def _combine_ln_kernel(h_ref, y0_ref, y1_ref, gate_ref, g_ref, b_ref, o_ref, *maybe_ob_ref):
    gate = gate_ref[...]
    ffn = gate[:, 0:1] * y0_ref[...].astype(F32) + gate[:, 1:2] * y1_ref[...].astype(F32)
    y = _ln_rows(DN_ALPHA * h_ref[...] + ffn, g_ref[...], b_ref[...])
    o_ref[...] = y
    for ob_ref in maybe_ob_ref:
        ob_ref[...] = y.astype(BF16)


def combine_ln(h, y, gate, g, b, with_bf16, tm=512):
    m, d = h.shape
    row = lambda i: (i, 0)
    fixed = lambda i: (0, 0)
    n_out = 2 if with_bf16 else 1
    out = pl.pallas_call(
        _combine_ln_kernel,
        grid=(m // tm,),
        in_specs=[pl.BlockSpec((tm, d), row), pl.BlockSpec((tm, d), row),
                  pl.BlockSpec((tm, d), lambda i: (m // tm + i, 0)),
                  pl.BlockSpec((tm, TOP_K), row), pl.BlockSpec((1, d), fixed), pl.BlockSpec((1, d), fixed)],
        out_specs=[pl.BlockSpec((tm, d), row)] * n_out,
        out_shape=[jax.ShapeDtypeStruct((m, d), F32), jax.ShapeDtypeStruct((m, d), BF16)][:n_out],
        compiler_params=_params("arbitrary"),
        name="combine_ln",
    )(h, y, y, gate, g.reshape(1, d), b.reshape(1, d))
    return (out[0], out[1]) if with_bf16 else (out[0], None)


ONES_ROWS = 16
POS_SPLIT = 16


def _attn_kernel(slopes_ref, lam_ref, qt_ref, k_ref, vt_ref, fq_ref, fk_ref, bd_ref, g_ref, o_ref, m_s, acc_s, sa_s, sb_s,
                 *, tq, tk, seq, out_scale):
    head = pl.program_id(1)
    i0 = pl.program_id(2) * tq
    slope = slopes_ref[head]
    lam = lam_ref[0]
    qt = qt_ref[...]
    row = lax.broadcasted_iota(jnp.int32, qt.shape, 0)
    zero = jnp.zeros_like(qt)
    q2 = jnp.concatenate([jnp.where(row < DA_HEAD_DIM, qt, zero),
                          jnp.where(row >= DA_HEAD_DIM, qt, zero)], axis=1)
    fq = fq_ref[0]
    fq2 = jnp.concatenate([fq, fq], axis=1)
    q_aug = jnp.concatenate([q2, fq2], axis=0)
    fk = fk_ref[0]
    fk_neg = -fk
    ones = jnp.ones((ONES_ROWS, tk), BF16)
    m_s[...] = jnp.full(m_s.shape, -jnp.inf, F32)
    acc_s[...] = jnp.zeros(acc_s.shape, F32)
    j_diag = i0 // tk

    def update(j, s, shift):
        off = pl.multiple_of(j * tk, tk)
        va = jnp.concatenate([vt_ref[:, pl.ds(off, tk)], ones], axis=0)
        m_prev = m_s[...]
        m_new = jnp.maximum(m_prev, jnp.max(s, axis=0, keepdims=True) - shift)
        p = jnp.exp(s - (m_new + shift))
        alpha = jnp.exp(m_prev - m_new)
        acc_s[...] = alpha * acc_s[...] + jnp.dot(va, p.astype(BF16), preferred_element_type=F32)
        m_s[...] = m_new

    def chunk_of(t):
        return t + (t >= j_diag).astype(jnp.int32)

    def scores(t):
        j = chunk_of(t)
        off = pl.multiple_of(j * tk, tk)
        ka = jnp.concatenate([k_ref[0, pl.ds(off, tk), :], jnp.where(j < j_diag, fk, fk_neg)], axis=1)
        return jnp.dot(ka, q_aug, preferred_element_type=F32)

    def consume(s_ref, t):
        j = chunk_of(t)
        update(j, s_ref[...], slope * jnp.abs(i0 - j * tk).astype(F32))

    j0 = pl.multiple_of(j_diag * tk, tk)
    bias = bd_ref[0]
    s_diag = jnp.dot(k_ref[0, pl.ds(j0, tk), :], q2, preferred_element_type=F32)
    sa_s[...] = s_diag - jnp.concatenate([bias, bias], axis=1)
    sb_s[...] = scores(jnp.int32(0))
    update(j_diag, sa_s[...], 0.0)
    n_off = seq // tk - 1

    def pair(t, carry):
        sa_s[...] = scores(2 * t + 1)
        consume(sb_s, 2 * t)
        sb_s[...] = scores(2 * t + 2)
        consume(sa_s, 2 * t + 1)
        return carry

    lax.fori_loop(0, n_off // 2, pair, 0)
    consume(sb_s, jnp.int32(n_off - 1))

    acc = acc_s[...]
    o1 = acc[:DA_V_DIM, :tq] / acc[DA_V_DIM:DA_V_DIM + 1, :tq]
    o2 = acc[:DA_V_DIM, tq:] / acc[DA_V_DIM:DA_V_DIM + 1, tq:]
    o = o1 - lam * o2
    o = o * lax.rsqrt(jnp.mean(o * o, axis=0, keepdims=True) + LN_EPS) * g_ref[...]
    o_ref[0] = (o * out_scale).T.astype(o_ref.dtype)


def diff_attention(qt, k, vt, lam, subln_g, lambda_init, tq=512, tk=512):
    bsz, seq, width = k.shape
    nq = seq // tq
    slopes = 2.0 ** (-(8.0 / DA_HEADS) * jnp.arange(1, DA_HEADS + 1, dtype=F32))

    def split(n):
        pos = jnp.arange(n, dtype=jnp.int32)
        return (pos // POS_SPLIT * POS_SPLIT).astype(F32), (pos % POS_SPLIT).astype(F32)

    qhi, qlo = split(tq)
    khi, klo = split(tk)
    sl = slopes[:, None]
    fq = jnp.zeros((DA_HEADS, DA_V_DIM, tq), F32)
    fq = fq.at[:, 0].set(1.0).at[:, 1].set(1.0).at[:, 2].set(-sl * qhi).at[:, 3].set(-sl * qlo)
    fk = jnp.zeros((DA_HEADS, tk, DA_V_DIM), F32)
    fk = fk.at[:, :, 0].set(sl * khi).at[:, :, 1].set(sl * klo).at[:, :, 2].set(1.0).at[:, :, 3].set(1.0)
    assert tq == tk, "the chunk holding a query tile's diagonal must be the tile itself"
    pos = jnp.arange(tk, dtype=jnp.int32)
    bias_diag = slopes[:, None, None] * jnp.abs(pos[:, None] - pos[None, :]).astype(F32)
    kern = functools.partial(_attn_kernel, tq=tq, tk=tk, seq=seq, out_scale=1.0 - lambda_init)
    smem = pl.BlockSpec(memory_space=pltpu.SMEM)
    return pl.pallas_call(
        kern,
        grid=(bsz, DA_HEADS, seq // tq),
        in_specs=[smem, smem,
                  pl.BlockSpec((DA_V_DIM, tq), lambda b, h, i: (h, b * nq + i)),
                  pl.BlockSpec((1, seq, DA_V_DIM), lambda b, h, i: (b, 0, h)),
                  pl.BlockSpec((DA_V_DIM, seq), lambda b, h, i: (h, b)),
                  pl.BlockSpec((1, DA_V_DIM, tq), lambda b, h, i: (h, 0, 0)),
                  pl.BlockSpec((1, tk, DA_V_DIM), lambda b, h, i: (h, 0, 0)),
                  pl.BlockSpec((1, tk, tq), lambda b, h, i: (h, 0, 0)),
                  pl.BlockSpec((DA_V_DIM, 1), lambda b, h, i: (0, 0))],
        out_specs=pl.BlockSpec((1, tq, DA_V_DIM), lambda b, h, i: (b, i, h)),
        out_shape=jax.ShapeDtypeStruct((bsz, seq, width), BF16),
        scratch_shapes=[pltpu.VMEM((1, 2 * tq), F32), pltpu.VMEM((DA_V_DIM + ONES_ROWS, 2 * tq), F32),
                        pltpu.VMEM((tk, 2 * tq), F32), pltpu.VMEM((tk, 2 * tq), F32)],
        compiler_params=_params("arbitrary", "arbitrary", "arbitrary"),
        name="diff_attention",
    )(slopes, lam.reshape(1).astype(F32), qt, k, vt, fq.astype(BF16), fk.astype(BF16), bias_diag,
      subln_g.reshape(DA_V_DIM, 1).astype(F32))


HALO = 16


def _rglru_kernel(xr_ref, prev_ref, next_ref, cw_ref, cb_ref, wa_ref, wx_ref, ba_ref, bx_ref, kk_ref, y_ref,
                  a_s, b_s, y_s, h_s, *, ts, nt):
    direction = pl.program_id(0)
    t = pl.program_id(2)
    chunk = t + direction * (nt - 1 - 2 * t)

    @pl.when(t == 0)
    def _():
        h_s[...] = jnp.zeros(h_s.shape, F32)

    xf = jnp.concatenate([jnp.where(chunk == 0, 0.0, prev_ref[0].astype(F32)), xr_ref[0].astype(F32),
                          jnp.where(chunk == nt - 1, 0.0, next_ref[0].astype(F32))], axis=0)
    rows = ts + 2 * HALO

    def tap(j):
        shifted = xf if j == 2 else pltpu.roll(xf, (2 - j) % rows, 0)
        return cw_ref[j:j + 1, :] * shifted[HALO:HALO + ts]

    x = cb_ref[...] + sum(tap(j) for j in range(RG_CONV))
    xb = x.astype(BF16)
    xh = 0.5 * x
    for n in range(RG_BLOCKS):
        cols = slice(n * RG_BLOCK_W, (n + 1) * RG_BLOCK_W)
        xs = xb[:, cols]
        tr = jnp.tanh(jnp.dot(xs, wa_ref[0, n], preferred_element_type=F32) + ba_ref[0, :, cols])
        ti = jnp.tanh(jnp.dot(xs, wx_ref[0, n], preferred_element_type=F32) + bx_ref[0, :, cols])
        kk = kk_ref[0, :, cols]
        a = jnp.exp2(kk * tr + kk)
        a_s[:, cols] = a
        s = 1.0 - a * a
        root = jnp.where(s > 0.0, s * lax.rsqrt(s), 0.0)
        b_s[:, cols] = root * (ti * xh[:, cols] + xh[:, cols])

    def step(k, h):
        row = jnp.where(direction == 0, k, ts - 1 - k)
        h = a_s[pl.ds(row, 1), :] * h + b_s[pl.ds(row, 1), :]
        y_s[pl.ds(row, 1), :] = h
        return h

    h_s[...] = lax.fori_loop(0, ts, step, h_s[...], unroll=8)
    y_ref[0, 0] = y_s[...].astype(y_ref.dtype)


def rglru_bidir(u, conv_w, conv_b, wa, ba, wx, bx, lam, ts=512):
    bsz, seq, w2 = u.shape
    w = w2 // 2
    nt = seq // ts
    per = ts // HALO
    kk = (-0.5 * RG_C * math.log2(math.e)) * jax.nn.softplus(-lam.astype(F32)).reshape(2, 1, w)
    chunk = lambda d, t: t + d * (nt - 1 - 2 * t)
    dmap4 = lambda d, b, t: (d, 0, 0, 0)
    dmap3 = lambda d, b, t: (d, 0, 0)
    fixed = lambda d, b, t: (0, 0)
    return pl.pallas_call(
        functools.partial(_rglru_kernel, ts=ts, nt=nt),
        grid=(2, bsz, nt),
        in_specs=[pl.BlockSpec((1, ts, w), lambda d, b, t: (b, chunk(d, t), 1)),
                  pl.BlockSpec((1, HALO, w), lambda d, b, t: (b, jnp.maximum(chunk(d, t) * per - 1, 0), 1)),
                  pl.BlockSpec((1, HALO, w),
                               lambda d, b, t: (b, jnp.minimum((chunk(d, t) + 1) * per, seq // HALO - 1), 1)),
                  pl.BlockSpec((RG_CONV, w), fixed), pl.BlockSpec((1, w), fixed),
                  pl.BlockSpec((1, RG_BLOCKS, RG_BLOCK_W, RG_BLOCK_W), dmap4),
                  pl.BlockSpec((1, RG_BLOCKS, RG_BLOCK_W, RG_BLOCK_W), dmap4),
                  pl.BlockSpec((1, 1, w), dmap3), pl.BlockSpec((1, 1, w), dmap3), pl.BlockSpec((1, 1, w), dmap3)],
        out_specs=pl.BlockSpec((1, 1, ts, w), lambda d, b, t: (d, b, chunk(d, t), 0)),
        out_shape=jax.ShapeDtypeStruct((2, bsz, seq, w), BF16),
        scratch_shapes=[pltpu.VMEM((ts, w), F32), pltpu.VMEM((ts, w), F32), pltpu.VMEM((ts, w), F32),
                        pltpu.VMEM((1, w), F32)],
        compiler_params=_params("arbitrary", "arbitrary", "arbitrary"),
        name="rglru_bidir",
    )(u, u, u, conv_w.astype(F32), conv_b.astype(F32).reshape(1, w), (0.5 * wa).astype(BF16),
      (0.5 * wx).astype(BF16), 0.5 * ba.astype(F32).reshape(2, 1, w), 0.5 * bx.astype(F32).reshape(2, 1, w), kk)


def _router_kernel(h_ref, wt_ref, b_ref, tri_ref, e_ref, g_ref, r_ref, cnt_ref, seen_s):
    h = h_ref[...]
    h_hi = h.astype(BF16)
    h_lo = (h - h_hi.astype(F32)).astype(BF16)
    nt_dot = lambda a, b: lax.dot_general(a, b, (((1,), (1,)), ((), ())), preferred_element_type=F32)
    logits = nt_dot(wt_ref[0], h_hi) + (nt_dot(wt_ref[0], h_lo) + nt_dot(wt_ref[1], h_hi))
    tm = logits.shape[1]
    s = jax.nn.sigmoid(logits).reshape(N_GROUPS, EXPERTS_PER_GROUP, tm)
    sel = s + b_ref[...]
    idx = lax.broadcasted_iota(jnp.int32, sel.shape, 1)
    big = jnp.int32(EXPERTS_PER_GROUP)
    v1 = jnp.max(sel, axis=1, keepdims=True)
    i1 = jnp.min(jnp.where(sel == v1, idx, big), axis=1, keepdims=True)
    rest = jnp.where(idx == i1, -jnp.inf, sel)
    v2 = jnp.max(rest, axis=1, keepdims=True)
    i2 = jnp.min(jnp.where(rest == v2, idx, big), axis=1, keepdims=True)
    score = v1 + v2
    gidx = lax.broadcasted_iota(jnp.int32, score.shape, 0)
    best = jnp.max(score, axis=0, keepdims=True)
    grp = jnp.min(jnp.where(score == best, gidx, jnp.int32(N_GROUPS)), axis=0, keepdims=True)
    pick = gidx == grp
    l1 = jnp.sum(jnp.where(pick, i1, 0), axis=0)
    l2 = jnp.sum(jnp.where(pick, i2, 0), axis=0)
    s_g = jnp.sum(jnp.where(pick, s, 0.0), axis=0)
    eidx = lax.broadcasted_iota(jnp.int32, s_g.shape, 0)
    w1 = jnp.sum(jnp.where(eidx == l1, s_g, 0.0), axis=0, keepdims=True)
    w2 = jnp.sum(jnp.where(eidx == l2, s_g, 0.0), axis=0, keepdims=True)
    tot = w1 + w2
    base = grp[0] * EXPERTS_PER_GROUP
    e1, e2 = base + l1, base + l2
    e_ref[...] = jnp.concatenate([e1, e2], axis=0)
    g_ref[...] = jnp.concatenate([w1 / tot, w2 / tot], axis=0)

    @pl.when(pl.program_id(0) == 0)
    def _():
        seen_s[...] = jnp.zeros(seen_s.shape, F32)

    eall = lax.broadcasted_iota(jnp.int32, (N_EXPERTS, tm), 0)
    hit1, hit2 = eall == e1, eall == e2
    both = jnp.where(hit1, 1.0, jnp.where(hit2, 1.0, 0.0))
    incl = jnp.dot(both.astype(BF16), tri_ref[...], preferred_element_type=F32)
    before = incl - both + seen_s[...]
    r_ref[...] = jnp.concatenate([jnp.sum(jnp.where(hit1, before, 0.0), axis=0, keepdims=True),
                                  jnp.sum(jnp.where(hit2, before, 0.0), axis=0, keepdims=True)],
                                 axis=0).astype(jnp.int32)
    seen_s[...] = seen_s[...] + incl[:, tm - 1:tm]
    cnt_ref[...] = seen_s[...].astype(jnp.int32)


def route(h, router_w, router_b, tm=1024):
    n, d = h.shape
    tri = (jnp.arange(tm)[:, None] <= jnp.arange(tm)[None, :]).astype(BF16)
    wt = router_w.astype(F32).T
    wt_hi = wt.astype(BF16)
    wt_lo = (wt - wt_hi.astype(F32)).astype(BF16)
    tile = pl.BlockSpec((TOP_K, tm), lambda i: (0, i))
    return pl.pallas_call(
        _router_kernel,
        grid=(n // tm,),
        in_specs=[pl.BlockSpec((tm, d), lambda i: (i, 0)),
                  pl.BlockSpec((2, N_EXPERTS, d), lambda i: (0, 0, 0)),
                  pl.BlockSpec((N_GROUPS, EXPERTS_PER_GROUP, 1), lambda i: (0, 0, 0)),
                  pl.BlockSpec((tm, tm), lambda i: (0, 0))],
        out_specs=[tile, tile, tile, pl.BlockSpec((N_EXPERTS, 1), lambda i: (0, 0))],
        out_shape=[jax.ShapeDtypeStruct((TOP_K, n), jnp.int32), jax.ShapeDtypeStruct((TOP_K, n), F32),
                   jax.ShapeDtypeStruct((TOP_K, n), jnp.int32), jax.ShapeDtypeStruct((N_EXPERTS, 1), jnp.int32)],
        scratch_shapes=[pltpu.VMEM((N_EXPERTS, 1), F32)],
        compiler_params=_params("arbitrary"),
        name="router",
    )(h, jnp.stack([wt_hi, wt_lo]), router_b.astype(F32).reshape(N_GROUPS, EXPERTS_PER_GROUP, 1), tri)


def _experts_kernel(blk_exp_ref, nxt_exp_ref, slot_ref, n_used_ref, x_ref, wg_hbm, wu_hbm, wd_hbm, o_ref,
                    wg_f, wu_f, wd_f, wg_s, wu_s, wd_s, sem, *, layer):
    i = pl.program_id(0)
    expert = blk_exp_ref[i]
    prev = blk_exp_ref[jnp.maximum(i - 1, 0)]
    slot = slot_ref[i]

    def fetch(e, s):
        return [pltpu.make_async_copy(src.at[layer, e], dst.at[s], sem.at[s])
                for src, dst in ((wg_hbm, wg_f), (wu_hbm, wu_f), (wd_hbm, wd_f))]

    @pl.when(i == 0)
    def _():
        for copy in fetch(expert, slot):
            copy.start()

    @pl.when((i == 0) | (expert != prev))
    def _():
        for copy in fetch(expert, slot):
            copy.wait()
        wg_s[...] = wg_f[slot].astype(BF16)
        wu_s[...] = wu_f[slot].astype(BF16)
        wd_s[...] = wd_f[slot].astype(BF16)

        @pl.when(nxt_exp_ref[i] >= 0)
        def _():
            for copy in fetch(nxt_exp_ref[i], 1 - slot):
                copy.start()

    @pl.when(i < n_used_ref[0])
    def _():
        x = _unpack_halves(x_ref[...])
        hg = jnp.dot(x, wg_s[...], preferred_element_type=F32)
        hu = jnp.dot(x, wu_s[...], preferred_element_type=F32)
        hidden = (jax.nn.silu(hg) * hu).astype(BF16)
        o_ref[...] = jnp.dot(hidden, wd_s[...], preferred_element_type=F32).astype(o_ref.dtype)

    @pl.when(i >= n_used_ref[0])
    def _():
        o_ref[...] = jnp.zeros(o_ref.shape, o_ref.dtype)


def experts(xs, blk_exp, n_used, w_gate, w_up, w_down, layer):
    p = xs.shape[0]
    n_blocks = p // MOE_ROWS
    d = w_gate.shape[2]
    f = w_gate.shape[3]
    idx = jnp.arange(n_blocks, dtype=jnp.int32)
    starts = jnp.concatenate([jnp.ones((1,), bool), blk_exp[1:] != blk_exp[:-1]])
    slot = (jnp.cumsum(starts.astype(jnp.int32)) - 1) % 2
    later = starts[None, :] & (idx[None, :] > idx[:, None])
    first_later = jnp.min(jnp.where(later, idx[None, :], n_blocks), axis=1)
    nxt_exp = jnp.sum(jnp.where(idx[None, :] == first_later[:, None], blk_exp[None, :], 0), axis=1)
    nxt_exp = jnp.where(first_later < n_blocks, nxt_exp, -1).astype(jnp.int32)
    hbm = pl.BlockSpec(memory_space=pl.ANY)
    grid_spec = pltpu.PrefetchScalarGridSpec(
        num_scalar_prefetch=4,
        grid=(n_blocks,),
        in_specs=[pl.BlockSpec((MOE_ROWS, d // 2), lambda i, *_: (i, 0)), hbm, hbm, hbm],
        out_specs=pl.BlockSpec((MOE_ROWS, d), lambda i, *_: (i, 0)),
        scratch_shapes=[pltpu.VMEM((2, d, f), F32), pltpu.VMEM((2, d, f), F32), pltpu.VMEM((2, f, d), F32),
                        pltpu.VMEM((d, f), BF16), pltpu.VMEM((d, f), BF16), pltpu.VMEM((f, d), BF16),
                        pltpu.SemaphoreType.DMA((2,))],
    )
    return pl.pallas_call(
        functools.partial(_experts_kernel, layer=layer),
        grid_spec=grid_spec,
        out_shape=jax.ShapeDtypeStruct((p, d), BF16),
        compiler_params=_params("arbitrary"),
        name="experts",
    )(blk_exp, nxt_exp, slot.astype(jnp.int32), n_used, xs, w_gate, w_up, w_down)


def _dispatch_kernel(dest_ref, fill_ref, x_ref, xs_ref, zero_s, fill_sem, row_sem, *, tm, n_blocks):
    base = pl.program_id(0) * tm

    @pl.when(pl.program_id(0) == 0)
    def _():
        zero_s[...] = jnp.zeros(zero_s.shape, zero_s.dtype)

        def zero_block(blk):
            return pltpu.make_async_copy(zero_s, xs_ref.at[pl.ds(pl.multiple_of(blk * MOE_ROWS, MOE_ROWS),
                                                                 MOE_ROWS)], fill_sem)

        def start(blk, carry):
            @pl.when(fill_ref[blk] != 0)
            def _():
                zero_block(blk).start()
            return carry

        def wait(blk, carry):
            @pl.when(fill_ref[blk] != 0)
            def _():
                zero_block(blk).wait()
            return carry

        lax.fori_loop(0, n_blocks, start, 0)
        lax.fori_loop(0, n_blocks, wait, 0)

    def issue(r, carry):
        for k in range(TOP_K):
            pltpu.make_async_copy(x_ref.at[pl.ds(r, 1)], xs_ref.at[pl.ds(dest_ref[k, base + r], 1)],
                                  row_sem).start(priority=k % 2)
        return carry

    lax.fori_loop(0, tm, issue, 0, unroll=4)
    for k in range(TOP_K):
        pltpu.make_async_copy(x_ref, xs_ref.at[pl.ds(0, tm)], row_sem).wait()


def dispatch_rows(x, dest, fill, tm=512):
    n, d = x.shape
    n_blocks = fill.shape[0]
    grid_spec = pltpu.PrefetchScalarGridSpec(
        num_scalar_prefetch=2,
        grid=(n // tm,),
        in_specs=[pl.BlockSpec((tm, d), lambda i, dst, fl: (i, 0))],
        out_specs=pl.BlockSpec(memory_space=pl.ANY),
        scratch_shapes=[pltpu.VMEM((MOE_ROWS, d), x.dtype), pltpu.SemaphoreType.DMA(()),
                        pltpu.SemaphoreType.DMA(())],
    )
    return pl.pallas_call(
        functools.partial(_dispatch_kernel, tm=tm, n_blocks=n_blocks),
        grid_spec=grid_spec,
        out_shape=jax.ShapeDtypeStruct((n_blocks * MOE_ROWS, d), x.dtype),
        compiler_params=_params("arbitrary"),
        name="dispatch_rows",
    )(dest, fill, x)


def moe_ffn(h, hp, router_w, router_b, w_gate, w_up, w_down, layer, ln_g, ln_b):
    n, d = h.shape
    a = n * TOP_K
    e_idx, gate, rank, counts = route(h, router_w, router_b)
    counts = counts[:, 0]
    padded = (counts + MOE_ROWS - 1) // MOE_ROWS * MOE_ROWS
    pad_end = jnp.cumsum(padded)
    pad_start = pad_end - padded
    experts_iota = jnp.arange(N_EXPERTS, dtype=jnp.int32)[:, None, None]
    dest = jnp.sum(jnp.where(e_idx[None] == experts_iota, pad_start[:, None, None], 0), axis=0) + rank
    n_blocks = a // MOE_ROWS + N_EXPERTS
    blk_start = jnp.arange(n_blocks, dtype=jnp.int32)[:, None] * MOE_ROWS
    blk_exp = jnp.minimum(jnp.sum((pad_end[None, :] <= blk_start).astype(jnp.int32), axis=1), N_EXPERTS - 1)
    n_used = (pad_end[-1:] // MOE_ROWS).astype(jnp.int32)
    ends_expert = jnp.any((pad_end[None, :] == blk_start + MOE_ROWS) & (padded[None, :] > 0), axis=1)
    fill = (ends_expert | (blk_start[:, 0] >= pad_end[-1])).astype(jnp.int32)
    xs = dispatch_rows(hp, dest, fill)
    yb = experts(xs, blk_exp, n_used, w_gate, w_up, w_down, layer)
    y = yb.at[dest.reshape(a)].get(mode="promise_in_bounds")
    return combine_ln(h, y, gate.T, ln_g, ln_b, with_bf16=layer + 1 < DEPTH)


FFT_R = 128
HALF_R = FFT_R // 2
HY_GROUP = 4


def _proj_t_kernel(w_ref, x_ref, o_ref, *, scale):
    acc = lax.dot_general(w_ref[...].astype(BF16), x_ref[...], (((0,), (1,)), ((), ())),
                          preferred_element_type=F32)
    o_ref[...] = (acc * scale).astype(o_ref.dtype)


def proj_t(w, col0, ncols, x, out_dtype, scale=1.0, tm=1024, tn=1024):
    k = w.shape[0]
    n = x.shape[0]
    first = col0 // tm
    return pl.pallas_call(
        functools.partial(_proj_t_kernel, scale=scale),
        grid=(ncols // tm, n // tn),
        in_specs=[pl.BlockSpec((k, tm), lambda i, j: (0, first + i)),
                  pl.BlockSpec((tn, k), lambda i, j: (j, 0))],
        out_specs=pl.BlockSpec((tm, tn), lambda i, j: (i, j)),
        out_shape=jax.ShapeDtypeStruct((ncols, n), out_dtype),
        compiler_params=_params("arbitrary", "arbitrary"),
        name="proj_t",
    )(w, x)


def _dft_constants():
    idx = jnp.arange(FFT_R, dtype=jnp.int32)
    prod = idx[:, None] * idx[None, :]
    ang = (prod % FFT_R).astype(F32) * (2.0 * math.pi / FFT_R)
    f_r, f_i = jnp.cos(ang), -jnp.sin(ang)
    ang_t = prod.astype(F32) * (2.0 * math.pi / (FFT_R * FFT_R))
    g_r, g_i = f_r[:HALF_R], -f_i[:HALF_R]
    return dict(
        la=jnp.block([[f_r[:, :HALF_R], -f_i[:, :HALF_R]], [f_i[:, :HALF_R], f_r[:, :HALF_R]]]).astype(BF16),
        la_real=jnp.concatenate([f_r, f_i], axis=0).astype(BF16),
        rb=jnp.block([[f_r, f_i], [-f_i, f_r]]).astype(BF16),
        rc=jnp.block([[f_r, -f_i], [f_i, f_r]]).astype(BF16),
        ld=(jnp.block([[g_r, -g_i], [g_i, g_r]]) / (FFT_R * FFT_R)).astype(BF16),
        t_r=jnp.cos(ang_t), t_i=-jnp.sin(ang_t))


def _rows_to_lanes(x):
    return jnp.concatenate([x[:FFT_R], x[FFT_R:]], axis=1)


def _fwd_lane_stage(a, t_r, t_i, rb):
    a_r, a_i = a[:FFT_R], a[FFT_R:]
    t2_r = jnp.concatenate([t_r, t_r], axis=1)
    t2_i = jnp.concatenate([t_i, t_i], axis=1)
    b_r = a_r * t2_r - a_i * t2_i
    b_i = a_r * t2_i + a_i * t2_r
    lhs = jnp.concatenate([jnp.concatenate([b_r[:, :FFT_R], b_i[:, :FFT_R]], axis=1),
                           jnp.concatenate([b_r[:, FFT_R:], b_i[:, FFT_R:]], axis=1)], axis=0)
    return jnp.dot(lhs.astype(BF16), rb, preferred_element_type=F32)


def _conv_pairs(ms, kfs, la, rb, rc, ld, t_r, t_i):
    a = [jnp.dot(la, jnp.concatenate([m_r, m_i], axis=0).astype(BF16), preferred_element_type=F32)
         for m_r, m_i in ms]
    z = [_fwd_lane_stage(ai, t_r, t_i, rb) for ai in a]
    c = []
    for zi, kf in zip(z, kfs):
        z_r, z_i, k_r, k_i = zi[:, :FFT_R], zi[:, FFT_R:], kf[:, :FFT_R], kf[:, FFT_R:]
        y = jnp.concatenate([z_r * k_r - z_i * k_i, z_r * k_i + z_i * k_r], axis=1)
        c.append(jnp.dot(y.astype(BF16), rc, preferred_element_type=F32))
    t2_r = jnp.concatenate([t_r, t_r], axis=0)
    t2_i = jnp.concatenate([t_i, t_i], axis=0)
    out = []
    for ci in c:
        c_r, c_i = ci[:, :FFT_R], ci[:, FFT_R:]
        d_r = c_r * t2_r + c_i * t2_i
        d_i = c_i * t2_r - c_r * t2_i
        rhs = jnp.concatenate([_rows_to_lanes(d_r), _rows_to_lanes(d_i)], axis=0)
        x = jnp.dot(ld, rhs.astype(BF16), preferred_element_type=F32)
        out.append((x[:HALF_R], x[HALF_R:]))
    return out


def _short_conv_tile(x, w0, w1, w2, b):
    rows = x.shape[0]
    lane = lax.broadcasted_iota(jnp.int32, x.shape, 1)
    row = lax.broadcasted_iota(jnp.int32, x.shape, 0)
    r = pltpu.roll(x, 1, 1)
    prev = jnp.where(lane == 0, jnp.where(row == 0, 0.0, pltpu.roll(r, 1, 0)), r)
    r = pltpu.roll(x, FFT_R - 1, 1)
    nxt = jnp.where(lane == FFT_R - 1, jnp.where(row == rows - 1, 0.0, pltpu.roll(r, rows - 1, 0)), r)
    return w0 * prev + w1 * x + w2 * nxt + b


def _hyena_kernel(cw_ref, cb_ref, skip_ref, v_ref, x1_ref, x2_ref, kf_ref,
                  la_ref, rb_ref, rc_ref, ld_ref, tr_ref, ti_ref, o_ref, *, tc):
    c_base = pl.program_id(0) * tc
    lane = lax.broadcasted_iota(jnp.int32, (1, 2 * FFT_R), 1)

    def conv_in(ref, part, b, c0):
        tiles = []
        for cc in range(2):
            ch = part * HY_CH + c_base + c0 + cc
            tiles.append(_short_conv_tile(ref[c0 + cc, b], cw_ref[0, ch], cw_ref[1, ch], cw_ref[2, ch], cb_ref[ch]))
        return jnp.concatenate(tiles, axis=1)

    def group(g, carry):
        starts = [2 * (HY_GROUP * g + j) for j in range(HY_GROUP)]
        consts = (la_ref[...], rb_ref[...], rc_ref[...], ld_ref[...], tr_ref[...], ti_ref[...])
        zs = [(conv_in(v_ref, 0, 0, c0), conv_in(v_ref, 0, 1, c0)) for c0 in starts]
        for o, g_ref in enumerate((x1_ref, x2_ref)):
            kfs = [kf_ref[o, pl.ds(c0, 2)].reshape(2 * FFT_R, 2 * FFT_R) for c0 in starts]
            ys = _conv_pairs(zs, kfs, *consts)
            nxt = []
            for c0, (z_r, z_i), (y_r, y_i) in zip(starts, zs, ys):
                sk = jnp.where(lane < FFT_R, skip_ref[o, c_base + c0], skip_ref[o, c_base + c0 + 1])
                nxt.append((conv_in(g_ref, o + 1, 0, c0) * (y_r + sk * z_r),
                            conv_in(g_ref, o + 1, 1, c0) * (y_i + sk * z_i)))
            zs = nxt
        for c0, (z_r, z_i) in zip(starts, zs):
            o_ref[c0, 0] = z_r[:, :FFT_R].astype(o_ref.dtype)
            o_ref[c0 + 1, 0] = z_r[:, FFT_R:].astype(o_ref.dtype)
            o_ref[c0, 1] = z_i[:, :FFT_R].astype(o_ref.dtype)
            o_ref[c0 + 1, 1] = z_i[:, FFT_R:].astype(o_ref.dtype)
        return carry

    lax.fori_loop(0, tc // (2 * HY_GROUP), group, 0)


def hyena_mix(u_t, kf, conv_w, conv_b, skip, consts, tc=32):
    rows, n = u_t.shape
    ch = rows // 3
    u5 = u_t.reshape(rows, 2, HALF_R, FFT_R)
    nct = ch // tc
    smem = pl.BlockSpec(memory_space=pltpu.SMEM)
    part = lambda k: pl.BlockSpec((tc, 2, HALF_R, FFT_R), lambda i: (k * nct + i, 0, 0, 0))
    full2 = lambda a: pl.BlockSpec(a.shape, lambda i: (0, 0))
    mats = [consts[k] for k in ("la", "rb", "rc", "ld", "t_r", "t_i")]
    out = pl.pallas_call(
        functools.partial(_hyena_kernel, tc=tc),
        grid=(nct,),
        in_specs=[smem, smem, smem, part(0), part(1), part(2),
                  pl.BlockSpec((2, tc, FFT_R, 2 * FFT_R), lambda i: (0, i, 0, 0))] + [full2(a) for a in mats],
        out_specs=pl.BlockSpec((tc, 2, HALF_R, FFT_R), lambda i: (i, 0, 0, 0)),
        out_shape=jax.ShapeDtypeStruct((ch, 2, HALF_R, FFT_R), BF16),
        compiler_params=_params("arbitrary"),
        name="hyena_mix",
    )(conv_w.astype(F32), conv_b.astype(F32), skip.astype(F32), u5, u5, u5, kf, *mats)
    return out.reshape(ch, n)


def _hyena_filter_kernel(delta_ref, hf_ref, hb_ref, tpos_ref, la_ref, rb_ref, tr_ref, ti_ref, kf_ref, *, tc):
    c_base = pl.program_id(1) * tc

    def taps_of(c0):
        taps = []
        for cc in range(2):
            k = jnp.concatenate([hf_ref[0, 0, c0 + cc], hb_ref[0, 0, c0 + cc]], axis=0)
            k = k * jnp.exp(-tpos_ref[...] * delta_ref[c_base + c0 + cc])
            taps.append(k * lax.rsqrt(jnp.sum(k * k, keepdims=True)))
        return jnp.concatenate(taps, axis=1).astype(BF16)

    def group(g, carry):
        starts = [2 * (HY_GROUP * g + j) for j in range(HY_GROUP)]
        a = [jnp.dot(la_ref[...], taps_of(c0), preferred_element_type=F32) for c0 in starts]
        z = [_fwd_lane_stage(ai, tr_ref[...], ti_ref[...], rb_ref[...]) for ai in a]
        for c0, zi in zip(starts, z):
            kf_ref[0, pl.ds(c0, 2)] = zi.reshape(2, FFT_R, 2 * FFT_R)
        return carry

    lax.fori_loop(0, tc // (2 * HY_GROUP), group, 0)


def hyena_filter_spectra(seq, w1, b1, w2, b2, w3, b3, freq, w4, consts, tc=16):
    t = jnp.linspace(0.0, 1.0, seq, dtype=F32)[:, None]
    omega = 2.0 * math.pi * jnp.arange(seq, dtype=F32)[:, None] / seq
    bands = jnp.linspace(1e-4, HF_BANDS - 1, HF_BANDS, dtype=F32)[None, :]
    ang = omega * bands
    z = jnp.concatenate([t, jnp.cos(ang), -jnp.sin(ang)], -1)
    fr = freq.astype(F32)
    hid = jnp.sin(fr * (z @ w1.astype(F32) + b1.astype(F32)))
    hid = jnp.sin(fr * (hid @ w2.astype(F32) + b2.astype(F32)))
    hid = jnp.sin(fr * (hid @ w3.astype(F32) + b3.astype(F32)))
    back = lambda a: jnp.concatenate([jnp.zeros_like(a[:1]), a[:0:-1]], axis=0)
    w4t = w4.astype(BF16).T
    half = HY_ORDER * HY_CH
    h_fwd = matmul(w4t[:half], hid.T.astype(BF16), F32)
    h_bwd = matmul(w4t[half:], back(hid).T.astype(BF16), F32)
    shape5 = (HY_ORDER, HY_CH, HALF_R, FFT_R)
    tpos = jnp.concatenate([t, back(t)], axis=0).reshape(FFT_R, FFT_R)
    max_decay = math.log(HF_TARGET) / HF_FAST
    min_decay = math.log(HF_TARGET) / HF_SLOW
    deltas = jnp.abs(jnp.linspace(min_decay, max_decay, HY_CH, dtype=F32))
    taps = pl.BlockSpec((1, 1, tc, HALF_R, FFT_R), lambda o, i: (0, o, i, 0, 0))
    full2 = lambda a: pl.BlockSpec(a.shape, lambda o, i: (0, 0))
    mats = [consts[k] for k in ("la_real", "rb", "t_r", "t_i")]
    return pl.pallas_call(
        functools.partial(_hyena_filter_kernel, tc=tc),
        grid=(HY_ORDER, HY_CH // tc),
        in_specs=[pl.BlockSpec(memory_space=pltpu.SMEM), taps, taps, full2(tpos)] + [full2(a) for a in mats],
        out_specs=pl.BlockSpec((1, tc, FFT_R, 2 * FFT_R), lambda o, i: (o, i, 0, 0)),
        out_shape=jax.ShapeDtypeStruct((HY_ORDER, HY_CH, FFT_R, 2 * FFT_R), F32),
        compiler_params=_params("arbitrary", "arbitrary"),
        name="hyena_filter_spectra",
    )(deltas, h_fwd.reshape((1,) + shape5), h_bwd.reshape((1,) + shape5), tpos, *mats)


def kernel(x, ev_w_in, ev_hy_conv_w, ev_hy_conv_b, ev_hf_w1, ev_hf_b1, ev_hf_w2, ev_hf_b2, ev_hf_w3, ev_hf_b3,
           ev_hf_freq, ev_hf_w4, ev_hy_skip, ev_lam_q1, ev_lam_k1, ev_lam_q2, ev_lam_k2, ev_subln_g, ev_w_out,
           od_w_in, od_conv_w, od_conv_b, od_wa, od_ba, od_wx, od_bx, od_lam, od_w_out, ln1_g, ln1_b, ln2_g,
           ln2_b, router_w, router_b, ex_w_gate, ex_w_up, ex_w_down):
    bsz, seq, d = x.shape
    n = bsz * seq
    h = x.reshape(n, d).astype(F32)
    hb = h.astype(BF16)
    for layer in range(DEPTH):
        i = layer // 2
        if layer % 2 == 0:
            w_in = ev_w_in[i]
            c0 = 3 * HY_CH
            u_t = proj_t(w_in, 0, c0, hb, F32)
            qt = proj_t(w_in, c0, DA_QK, hb, BF16, scale=DA_HEAD_DIM ** -0.5)
            k = matmul(hb, w_in, BF16, col0=c0 + DA_QK, ncols=DA_QK)
            vt = proj_t(w_in, c0 + 2 * DA_QK, DA_WIDTH, hb, BF16)
            consts = _dft_constants()
            kf = hyena_filter_spectra(seq, ev_hf_w1[i], ev_hf_b1[i], ev_hf_w2[i], ev_hf_b2[i],
                                      ev_hf_w3[i], ev_hf_b3[i], ev_hf_freq[i], ev_hf_w4[i], consts)
            y_hy = hyena_mix(u_t, kf, ev_hy_conv_w[i], ev_hy_conv_b[i], ev_hy_skip[i], consts)
            lambda_init = 0.8 - 0.6 * math.exp(-0.3 * layer)
            lam = (jnp.exp(jnp.sum(ev_lam_q1[i].astype(F32) * ev_lam_k1[i].astype(F32)))
                   - jnp.exp(jnp.sum(ev_lam_q2[i].astype(F32) * ev_lam_k2[i].astype(F32))) + lambda_init)
            y_da = diff_attention(qt, k.reshape(bsz, seq, DA_QK), vt,
                                  lam, ev_subln_g[i], lambda_init).reshape(n, DA_WIDTH)
            h, hp = even_out_proj(y_hy, y_da, ev_w_out[i].astype(BF16), h, ln1_g[layer], ln1_b[layer])
        else:
            u = matmul(hb, od_w_in[i], BF16)
            yd = rglru_bidir(u.reshape(bsz, seq, 2 * RG_WIDTH), od_conv_w[i], od_conv_b[i], od_wa[i], od_ba[i],
                             od_wx[i], od_bx[i], od_lam[i]).reshape(2, n, RG_WIDTH)
            h, hp = odd_out_proj(u, yd, od_w_out[i].astype(BF16), h, ln1_g[layer], ln1_b[layer])
        h, hb = moe_ffn(h, hp, router_w, router_b, ex_w_gate, ex_w_up, ex_w_down, layer,
                        ln2_g[layer], ln2_b[layer])
    return h.reshape(bsz, seq, d).astype(x.dtype)
```
